```python
import math
import jax, jax.numpy as jnp
from jax import lax
import numpy as np

D_MODEL = 1024
BATCH = 32
SEQ = 256
DEPTH = 2
DEC_BATCH = 2
DEC_SEQ = 1024
PAST_LEN = 512

GRID_W = 64
N_AB = (DEPTH + 1) // 2
N_CD = DEPTH // 2
EPS = 1e-6
NEG_INF = -1e30

S5_WIDTH = D_MODEL // 2
S5_GROUP = 16
S5_GROUPS = S5_WIDTH // S5_GROUP
S5_STATE = 64
S5_DIR_PARAMS = ('s5_lam_re', 's5_lam_im', 's5_log_dt', 's5_b_re', 's5_b_im', 's5_c_re', 's5_c_im')
GDN_DK = 128
GDN_DV = 128
GDN_HEADS = D_MODEL // 256
GDN_WIDTH = GDN_HEADS * GDN_DV
GDN_CHUNK = 64
SHORT_CONV = 3
HEAD_DIM = 64
C_HEADS = 8
C_KV_HEADS = 2
C_GROUP = C_HEADS // C_KV_HEADS
WINDOW = 128
Q_BLOCK = 128
D_HEADS = 4
D_VDIM = 2 * HEAD_DIM
ATTN_SCALE = HEAD_DIM ** -0.5
ROPE_THETA = 10000.0
D_FF = 2816
FFN_CONV = 3

MIX_WIDTH = S5_WIDTH + GDN_WIDTH
AB_SPLITS = [S5_WIDTH, S5_WIDTH + 3 * GDN_WIDTH, S5_WIDTH + 4 * GDN_WIDTH, S5_WIDTH + 4 * GDN_WIDTH + 2 * GDN_HEADS]
AB_IN = S5_WIDTH + 4 * GDN_WIDTH + 4 * GDN_HEADS
C_Q = C_HEADS * HEAD_DIM
C_KV = C_KV_HEADS * HEAD_DIM
D_QK = D_HEADS * 2 * HEAD_DIM
CD_SPLITS = [C_Q, C_Q + C_KV, C_Q + 2 * C_KV, C_Q + 2 * C_KV + D_QK, C_Q + 2 * C_KV + 2 * D_QK]
CD_IN = C_Q + 2 * C_KV + 2 * D_QK + D_HEADS * D_VDIM

kernel_name = 'hybrid_diffusion_prefix_trunk_step'


def _rms(x, g):
    xf = x.astype(jnp.float32)
    y = xf * lax.rsqrt(jnp.mean(xf * xf, axis=-1, keepdims=True) + EPS)
    return (y * g.astype(jnp.float32)).astype(x.dtype)


def _l2norm(x):
    return x * lax.rsqrt(jnp.sum(x * x, axis=-1, keepdims=True) + EPS)


def _adaln(cvec, w, b):
    m = (jax.nn.silu(cvec) @ w + b)[:, None, :]
    return jnp.split(m, 6, axis=-1)


def _modulate(x, g, shift, scale):
    return _rms(x, g) * (1 + scale) + shift


def _dwconv(x, w, b=None):
    k = w.shape[0]
    pad = k // 2
    n = x.shape[1]
    xp = jnp.pad(x, ((0, 0), (pad, pad), (0, 0)))
    y = sum(xp[:, i:i + n] * w[i] for i in range(k))
    return y if b is None else y + b


def _rope2d(x):
    n = x.shape[1]
    rows = n // GRID_W
    row = jnp.repeat(jnp.arange(rows), GRID_W).astype(jnp.float32)
    col = jnp.tile(jnp.arange(GRID_W), rows).astype(jnp.float32)
    half = HEAD_DIM // 2
    quarter = half // 2
    inv = ROPE_THETA ** (-jnp.arange(quarter, dtype=jnp.float32) / quarter)
    bshape = (1, n) + (1,) * (x.ndim - 3) + (quarter,)
    xf = x.astype(jnp.float32)

    def rot(xa, pos):
        ang = (pos[:, None] * inv[None, :]).reshape(bshape)
        cos, sin = jnp.cos(ang), jnp.sin(ang)
        x1, x2 = xa[..., :quarter], xa[..., quarter:]
        return jnp.concatenate([x1 * cos - x2 * sin, x2 * cos + x1 * sin], axis=-1)

    out = jnp.concatenate([rot(xf[..., :half], row), rot(xf[..., half:], col)], axis=-1)
    return out.astype(x.dtype)


def _cplx_combine(e1, e2):
    a1r, a1i, b1r, b1i = e1
    a2r, a2i, b2r, b2i = e2
    return (a2r * a1r - a2i * a1i, a2r * a1i + a2i * a1r,
            a2r * b1r - a2i * b1i + b2r, a2r * b1i + a2i * b1r + b2i)


def _s5_scan(u, lam_re, lam_im, log_dt, b_re, b_im, c_re, c_im, h0_re, h0_im):
    dt = jnp.exp(log_dt)[:, None]
    mag = jnp.exp(lam_re * dt)
    ar, ai = mag * jnp.cos(lam_im * dt), mag * jnp.sin(lam_im * dt)
    den = lam_re * lam_re + lam_im * lam_im
    fr = ((ar - 1.0) * lam_re + ai * lam_im) / den
    fi = (ai * lam_re - (ar - 1.0) * lam_im) / den
    bbr = fr[..., None] * b_re - fi[..., None] * b_im
    bbi = fr[..., None] * b_im + fi[..., None] * b_re
    xr = jnp.einsum('blgc,gpc->blgp', u, bbr)
    xi = jnp.einsum('blgc,gpc->blgp', u, bbi)
    xr = xr.at[:, 0].add(ar * h0_re - ai * h0_im)
    xi = xi.at[:, 0].add(ar * h0_im + ai * h0_re)
    elems = (jnp.broadcast_to(ar, xr.shape), jnp.broadcast_to(ai, xr.shape), xr, xi)
    _, _, hr, hi = lax.associative_scan(_cplx_combine, elems, axis=1)
    y = jnp.einsum('blgp,gcp->blgc', hr, c_re) - jnp.einsum('blgp,gcp->blgc', hi, c_im)
    return y, hr[:, -1], hi[:, -1]


def _s5_mixer(u, p, j, h0_re, h0_im):
    bsz, n, _ = u.shape
    uf = u.astype(jnp.float32).reshape(bsz, n, S5_GROUPS, S5_GROUP)

    def direction(ud, dr):
        prm = [p[name][j, dr].astype(jnp.float32) for name in S5_DIR_PARAMS]
        return _s5_scan(ud, *prm, h0_re[:, dr].astype(jnp.float32), h0_im[:, dr].astype(jnp.float32))

    y_f, hr_f, hi_f = direction(uf, 0)
    y_b, hr_b, hi_b = direction(uf[:, ::-1], 1)
    y = (y_f + y_b[:, ::-1]).reshape(bsz, n, S5_WIDTH) + p['s5_d'][j].astype(jnp.float32) * u.astype(jnp.float32)
    g = jax.nn.gelu(y)
    out = g * jax.nn.sigmoid(g @ p['s5_w_glu'][j].astype(jnp.float32) + p['s5_b_glu'][j].astype(jnp.float32))
    return out.astype(u.dtype), jnp.stack([hr_f, hr_b], axis=1), jnp.stack([hi_f, hi_b], axis=1)


def _chunk_gated_delta(q, k, v, g, beta, s0):
    bsz, n, nh, _ = q.shape
    dvv = v.shape[-1]
    cs = GDN_CHUNK
    nc = n // cs

    def to_chunks(t):
        t = t.reshape((bsz, nc, cs, nh) + t.shape[3:])
        return jnp.moveaxis(jnp.moveaxis(t, 1, 0), 2, 3)

    q, k, v, g, beta = (to_chunks(t) for t in (q, k, v, g, beta))
    gc = jnp.cumsum(g, axis=-1)
    idx = jnp.arange(cs)
    incl = idx[:, None] >= idx[None, :]
    strict = idx[:, None] > idx[None, :]
    decay = jnp.exp(jnp.where(incl, gc[..., :, None] - gc[..., None, :], NEG_INF))
    kb = k * beta[..., None]
    nmat = jnp.where(strict, jnp.einsum('nbhid,nbhjd->nbhij', kb, k) * decay, 0.0)
    eye = jnp.eye(cs, dtype=jnp.float32)
    tmat = lax.linalg.triangular_solve(nmat + eye, jnp.broadcast_to(eye, nmat.shape),
                                       left_side=True, lower=True, unit_diagonal=True)
    u = tmat @ (v * beta[..., None])
    w = tmat @ (kb * jnp.exp(gc)[..., None])
    amat = jnp.where(incl, jnp.einsum('nbhid,nbhjd->nbhij', q, k) * decay, 0.0)

    def step(s, xs):
        qi, ki, ui, wi, gi, ai = xs
        vn = ui - wi @ s
        o = (qi * jnp.exp(gi)[..., None]) @ s + ai @ vn
        gl = gi[..., -1:]
        s = s * jnp.exp(gl)[..., None] + jnp.einsum('bhcd,bhce->bhde', ki * jnp.exp(gl - gi)[..., None], vn)
        return s, o

    s_fin, o = lax.scan(step, s0, (q, k, u, w, gc, amat))
    o = jnp.moveaxis(jnp.moveaxis(o, 3, 2), 0, 1).reshape(bsz, n, nh, dvv)
    return o, s_fin


def _gdn_mixer(qkv, z, a, b, p, j, s0):
    bsz, n, _ = qkv.shape
    qkv = jax.nn.silu(_dwconv(qkv, p['gdn_conv_w'][j])).astype(jnp.float32)
    q, k, v = jnp.split(qkv, 3, axis=-1)
    q = _l2norm(q.reshape(bsz, n, GDN_HEADS, GDN_DK)) * (GDN_DK ** -0.5)
    k = _l2norm(k.reshape(bsz, n, GDN_HEADS, GDN_DK))
    v = v.reshape(bsz, n, GDN_HEADS, GDN_DV)
    beta = jax.nn.sigmoid(b.astype(jnp.float32))
    g = -jnp.exp(p['gdn_a_log'][j].astype(jnp.float32)) * jax.nn.softplus(
        a.astype(jnp.float32) + p['gdn_dt_bias'][j].astype(jnp.float32))
    s0 = s0.astype(jnp.float32)
    o_f, s_f = _chunk_gated_delta(q, k, v, g[:, :, 0], beta[:, :, 0], s0[:, 0])
    r = lambda t: t[:, ::-1]
    o_b, s_b = _chunk_gated_delta(r(q), r(k), r(v), r(g[:, :, 1]), r(beta[:, :, 1]), s0[:, 1])
    o = _rms(o_f + r(o_b), p['gdn_norm_g'][j]) * jax.nn.silu(
        z.astype(jnp.float32).reshape(bsz, n, GDN_HEADS, GDN_DV))
    return o.reshape(bsz, n, GDN_WIDTH).astype(z.dtype), jnp.stack([s_f, s_b], axis=1)


def _ab_mix(h, s5_re0, s5_im0, gdn0, p, j):
    bsz, n, _ = h.shape
    proj = h @ p['w_in_ab'][j]
    u, qkv, z, a, b = jnp.split(proj, AB_SPLITS, axis=-1)
    a = a.reshape(bsz, n, 2, GDN_HEADS)
    b = b.reshape(bsz, n, 2, GDN_HEADS)
    ya, s5r, s5i = _s5_mixer(u, p, j, s5_re0, s5_im0)
    yb, sg = _gdn_mixer(qkv, z, a, b, p, j, gdn0)
    out = jnp.concatenate([ya, yb], axis=-1).astype(h.dtype) @ p['w_out_ab'][j]
    return out, s5r, s5i, sg


def _to_blocks(t):
    bsz, n = t.shape[:2]
    return jnp.moveaxis(t.reshape((bsz, n // Q_BLOCK, Q_BLOCK) + t.shape[2:]), 1, 0)


def _from_blocks(t):
    t = jnp.moveaxis(t, 0, 1)
    return t.reshape((t.shape[0], t.shape[1] * t.shape[2]) + t.shape[3:])


def _cd_project(h, p, j):
    bsz, n, _ = h.shape
    proj = h @ p['w_in_cd'][j]
    qc, kc, vc, qd, kd, vd = jnp.split(proj, CD_SPLITS, axis=-1)
    qc = _rms(qc.reshape(bsz, n, C_HEADS, HEAD_DIM), p['c_qn'][j])
    kc = _rms(kc.reshape(bsz, n, C_KV_HEADS, HEAD_DIM), p['c_kn'][j])
    vc = vc.reshape(bsz, n, C_KV_HEADS, HEAD_DIM)
    qd = _rms(qd.reshape(bsz, n, D_HEADS, 2, HEAD_DIM), p['d_qn'][j])
    kd = _rms(kd.reshape(bsz, n, D_HEADS, 2, HEAD_DIM), p['d_kn'][j])
    vd = vd.reshape(bsz, n, D_HEADS, D_VDIM)
    return qc, kc, vc, qd, kd, vd


def _sink_logits(sink, like):
    sk = sink.astype(jnp.float32).reshape(1, C_KV_HEADS, C_GROUP, 1, 1)
    return jnp.broadcast_to(sk, like.shape[:-1] + (1,))


def _gqa_sink_dense(q, k, v, sink):
    bsz, n = q.shape[:2]
    qg = q.reshape(bsz, n, C_KV_HEADS, C_GROUP, HEAD_DIM)
    vf = v.astype(jnp.float32)

    def block(qi):
        s = jnp.einsum('bqgrd,bkgd->bgrqk', qi, k, preferred_element_type=jnp.float32) * ATTN_SCALE
        pr = jax.nn.softmax(jnp.concatenate([s, _sink_logits(sink, s)], axis=-1), axis=-1)[..., :-1]
        return jnp.einsum('bgrqk,bkgd->bqgrd', pr, vf)

    o = _from_blocks(lax.map(block, _to_blocks(qg)))
    return o.reshape(bsz, n, C_HEADS * HEAD_DIM)


def _gqa_sink_window(q, k, v, kc, vc, sink):
    bsz, n = q.shape[:2]
    span = Q_BLOCK + 2 * WINDOW
    n_ctx = kc.shape[1]
    qg = q.reshape(bsz, n, C_KV_HEADS, C_GROUP, HEAD_DIM)
    padw = ((0, 0), (WINDOW, WINDOW), (0, 0), (0, 0))
    kp, vp = jnp.pad(k, padw), jnp.pad(v, padw)
    vcf = vc.astype(jnp.float32)

    def block(bi):
        start = bi * Q_BLOCK
        qi = lax.dynamic_slice_in_dim(qg, start, Q_BLOCK, axis=1)
        ki = lax.dynamic_slice_in_dim(kp, start, span, axis=1)
        vi = lax.dynamic_slice_in_dim(vp, start, span, axis=1).astype(jnp.float32)
        qpos = start + jnp.arange(Q_BLOCK)
        kpos = start - WINDOW + jnp.arange(span)
        ok = (jnp.abs(qpos[:, None] - kpos[None, :]) <= WINDOW) & (kpos >= 0) & (kpos < n)
        s_loc = jnp.where(ok, jnp.einsum('bqgrd,bkgd->bgrqk', qi, ki, preferred_element_type=jnp.float32) * ATTN_SCALE, NEG_INF)
        s_ctx = jnp.einsum('bqgrd,bkgd->bgrqk', qi, kc, preferred_element_type=jnp.float32) * ATTN_SCALE
        pr = jax.nn.softmax(jnp.concatenate([s_ctx, s_loc, _sink_logits(sink, s_loc)], axis=-1), axis=-1)
        return (jnp.einsum('bgrqk,bkgd->bqgrd', pr[..., :n_ctx], vcf)
                + jnp.einsum('bgrqk,bkgd->bqgrd', pr[..., n_ctx:n_ctx + span], vi))

    o = _from_blocks(lax.map(block, jnp.arange(n // Q_BLOCK)))
    return o.reshape(bsz, n, C_HEADS * HEAD_DIM)


def _diff_attn_dense(q, k, v, lam):
    vf = v.astype(jnp.float32)

    def block(qi):
        s = jnp.einsum('bqhcd,bkhcd->bhcqk', qi, k, preferred_element_type=jnp.float32) * ATTN_SCALE
        pr = jax.nn.softmax(s, axis=-1)
        att = pr[:, :, 0] - lam * pr[:, :, 1]
        return jnp.einsum('bhqk,bkhe->bqhe', att, vf)

    return _from_blocks(lax.map(block, _to_blocks(q)))


def _lambda_init(layer):
    return 0.8 - 0.6 * math.exp(-0.3 * layer)


def _diff_lambda(p, j, lam_init):
    f = lambda name: p[name][j].astype(jnp.float32)
    return jnp.exp(jnp.sum(f('d_lq1') * f('d_lk1'))) - jnp.exp(jnp.sum(f('d_lq2') * f('d_lk2'))) + lam_init


def _cd_merge(oc, od, h, p, j, lam_init):
    bsz, n, _ = h.shape
    od = _rms(od, p['d_subln'][j]) * (1.0 - lam_init)
    mix = jnp.concatenate([oc, od.reshape(bsz, n, D_HEADS * D_VDIM)], axis=-1).astype(h.dtype)
    return mix @ p['w_out_cd'][j]


def _ffn(h, p, l):
    up = h @ p['ffn_up'][l]
    a, b = jnp.split(up, 2, axis=-1)
    a = _dwconv(a, p['ffn_conv_w'][l], p['ffn_conv_b'][l])
    return (jax.nn.silu(a) * b) @ p['ffn_down'][l]


def _context_pass(x, c_ctx, p):
    bsz = x.shape[0]
    s5r, s5i, gdn, ck, cv, dk, dv = [], [], [], [], [], [], []
    for l in range(DEPTH):
        j = l // 2
        sh1, sc1, g1, sh2, sc2, g2 = _adaln(c_ctx[None, :], p['w_mod'][l], p['b_mod'][l])
        h = _modulate(x, p['norm1_g'][l], sh1, sc1)
        if l % 2 == 0:
            zs = jnp.zeros((bsz, 2, S5_GROUPS, S5_STATE), jnp.float32)
            zg = jnp.zeros((bsz, 2, GDN_HEADS, GDN_DK, GDN_DV), jnp.float32)
            out, hr, hi, sg = _ab_mix(h, zs, zs, zg, p, j)
            s5r.append(hr)
            s5i.append(hi)
            gdn.append(sg)
        else:
            qc, kc, vc, qd, kd, vd = _cd_project(h, p, j)
            lam_init = _lambda_init(l)
            oc = _gqa_sink_dense(qc, kc, vc, p['c_sink'][j])
            od = _diff_attn_dense(qd, kd, vd, _diff_lambda(p, j, lam_init))
            out = _cd_merge(oc, od, h, p, j, lam_init)
            ck.append(kc)
            cv.append(vc)
            dk.append(kd)
            dv.append(vd)
        x = (x + g1 * out).astype(h.dtype)
        x = (x + g2 * _ffn(_modulate(x, p['norm2_g'][l], sh2, sc2), p, l)).astype(h.dtype)
    st = lambda lst: jnp.stack(lst, axis=1)
    return x, st(s5r), st(s5i), st(gdn), st(ck), st(cv), st(dk), st(dv)


def _latent_pass(x, c, s5_re, s5_im, gdn, ck, cv, dk, dv, p):
    for l in range(DEPTH):
        j = l // 2
        sh1, sc1, g1, sh2, sc2, g2 = _adaln(c, p['w_mod'][l], p['b_mod'][l])
        h = _modulate(x, p['norm1_g'][l], sh1, sc1)
        if l % 2 == 0:
            out = _ab_mix(h, s5_re[:, j], s5_im[:, j], gdn[:, j], p, j)[0]
        else:
            qc, kc, vc, qd, kd, vd = _cd_project(h, p, j)
            qc, kc, qd, kd = _rope2d(qc), _rope2d(kc), _rope2d(qd), _rope2d(kd)
            lam_init = _lambda_init(l)
            oc = _gqa_sink_window(qc, kc, vc, ck[:, j], cv[:, j], p['c_sink'][j])
            od = _diff_attn_dense(qd, jnp.concatenate([dk[:, j], kd], axis=1),
                                  jnp.concatenate([dv[:, j], vd], axis=1), _diff_lambda(p, j, lam_init))
            out = _cd_merge(oc, od, h, p, j, lam_init)
        x = (x + g1 * out).astype(h.dtype)
        x = (x + g2 * _ffn(_modulate(x, p['norm2_g'][l], sh2, sc2), p, l)).astype(h.dtype)
    return x


def setup_inputs(seed: int = 0) -> dict:
    key = jax.random.key(seed)
    ks = iter(jax.random.split(key, 64))
    nrm = lambda shape, scale=1.0: scale * jax.random.normal(next(ks), shape, jnp.float32)
    gain = lambda shape: 1.0 + 0.02 * nrm(shape)
    unif = lambda shape, lo, hi: jax.random.uniform(next(ks), shape, jnp.float32, lo, hi)
    D = D_MODEL
    inp = {}
    inp['x_prompt'] = nrm((BATCH, SEQ, D))
    inp['x_sample'] = nrm((DEC_BATCH, DEC_SEQ, D))
    inp['c'] = nrm((DEC_BATCH, D))
    inp['state_s5_re'] = nrm((DEC_BATCH, N_AB, 2, S5_GROUPS, S5_STATE), 0.1)
    inp['state_s5_im'] = nrm((DEC_BATCH, N_AB, 2, S5_GROUPS, S5_STATE), 0.1)
    inp['state_gdn'] = nrm((DEC_BATCH, N_AB, 2, GDN_HEADS, GDN_DK, GDN_DV), 0.1)
    inp['cache_c_k'] = nrm((DEC_BATCH, N_CD, PAST_LEN, C_KV_HEADS, HEAD_DIM))
    inp['cache_c_v'] = nrm((DEC_BATCH, N_CD, PAST_LEN, C_KV_HEADS, HEAD_DIM))
    inp['cache_d_k'] = nrm((DEC_BATCH, N_CD, PAST_LEN, D_HEADS, 2, HEAD_DIM))
    inp['cache_d_v'] = nrm((DEC_BATCH, N_CD, PAST_LEN, D_HEADS, D_VDIM))
    inp['c_ctx'] = nrm((D,))
    inp['w_mod'] = nrm((DEPTH, D, 6 * D), 0.5 * D ** -0.5)
    inp['b_mod'] = nrm((DEPTH, 6 * D), 0.02)
    inp['norm1_g'] = gain((DEPTH, D))
    inp['norm2_g'] = gain((DEPTH, D))
    inp['w_in_ab'] = nrm((N_AB, D, AB_IN), D ** -0.5)
    inp['w_out_ab'] = nrm((N_AB, MIX_WIDTH, D), MIX_WIDTH ** -0.5)
    inp['s5_lam_re'] = -0.5 + 0.01 * nrm((N_AB, 2, S5_GROUPS, S5_STATE))
    inp['s5_lam_im'] = math.pi * jnp.arange(S5_STATE, dtype=jnp.float32) + 0.01 * nrm((N_AB, 2, S5_GROUPS, S5_STATE))
    inp['s5_log_dt'] = unif((N_AB, 2, S5_GROUPS), math.log(1e-3), math.log(1e-1))
    inp['s5_b_re'] = nrm((N_AB, 2, S5_GROUPS, S5_STATE, S5_GROUP), (2 * S5_GROUP) ** -0.5)
    inp['s5_b_im'] = nrm((N_AB, 2, S5_GROUPS, S5_STATE, S5_GROUP), (2 * S5_GROUP) ** -0.5)
    inp['s5_c_re'] = nrm((N_AB, 2, S5_GROUPS, S5_GROUP, S5_STATE), S5_STATE ** -0.5)
    inp['s5_c_im'] = nrm((N_AB, 2, S5_GROUPS, S5_GROUP, S5_STATE), S5_STATE ** -0.5)
    inp['s5_d'] = nrm((N_AB, S5_WIDTH))
    inp['s5_w_glu'] = nrm((N_AB, S5_WIDTH, S5_WIDTH), S5_WIDTH ** -0.5)
    inp['s5_b_glu'] = nrm((N_AB, S5_WIDTH), 0.02)
    inp['gdn_conv_w'] = nrm((N_AB, SHORT_CONV, 3 * GDN_WIDTH), SHORT_CONV ** -0.5)
    inp['gdn_a_log'] = jnp.log(unif((N_AB, 2, GDN_HEADS), 1.0, 16.0))
    dt = jnp.exp(unif((N_AB, 2, GDN_HEADS), math.log(1e-3), math.log(1e-1)))
    inp['gdn_dt_bias'] = dt + jnp.log(-jnp.expm1(-dt))
    inp['gdn_norm_g'] = gain((N_AB, GDN_DV))
    inp['w_in_cd'] = nrm((N_CD, D, CD_IN), D ** -0.5)
    inp['w_out_cd'] = nrm((N_CD, MIX_WIDTH, D), MIX_WIDTH ** -0.5)
    inp['c_qn'] = gain((N_CD, HEAD_DIM))
    inp['c_kn'] = gain((N_CD, HEAD_DIM))
    inp['c_sink'] = nrm((N_CD, C_HEADS))
    inp['d_qn'] = gain((N_CD, HEAD_DIM))
    inp['d_kn'] = gain((N_CD, HEAD_DIM))
    inp['d_lq1'] = nrm((N_CD, HEAD_DIM), 0.1)
    inp['d_lk1'] = nrm((N_CD, HEAD_DIM), 0.1)
    inp['d_lq2'] = nrm((N_CD, HEAD_DIM), 0.1)
    inp['d_lk2'] = nrm((N_CD, HEAD_DIM), 0.1)
    inp['d_subln'] = gain((N_CD, D_VDIM))
    inp['ffn_up'] = nrm((DEPTH, D, 2 * D_FF), D ** -0.5)
    inp['ffn_conv_w'] = nrm((DEPTH, FFN_CONV, D_FF), FFN_CONV ** -0.5)
    inp['ffn_conv_b'] = nrm((DEPTH, D_FF), 0.02)
    inp['ffn_down'] = nrm((DEPTH, D_FF, D), D_FF ** -0.5)
    return inp


def reference(x_prompt, x_sample, c, state_s5_re, state_s5_im, state_gdn, cache_c_k, cache_c_v, cache_d_k, cache_d_v,
              c_ctx, w_mod, b_mod, norm1_g, norm2_g, w_in_ab, w_out_ab, s5_lam_re, s5_lam_im, s5_log_dt,
              s5_b_re, s5_b_im, s5_c_re, s5_c_im, s5_d, s5_w_glu, s5_b_glu, gdn_conv_w, gdn_a_log, gdn_dt_bias,
              gdn_norm_g, w_in_cd, w_out_cd, c_qn, c_kn, c_sink, d_qn, d_kn, d_lq1, d_lk1, d_lq2, d_lk2, d_subln,
              ffn_up, ffn_conv_w, ffn_conv_b, ffn_down):
    p = dict(w_mod=w_mod, b_mod=b_mod, norm1_g=norm1_g, norm2_g=norm2_g, w_in_ab=w_in_ab, w_out_ab=w_out_ab,
             s5_lam_re=s5_lam_re, s5_lam_im=s5_lam_im, s5_log_dt=s5_log_dt, s5_b_re=s5_b_re, s5_b_im=s5_b_im,
             s5_c_re=s5_c_re, s5_c_im=s5_c_im, s5_d=s5_d, s5_w_glu=s5_w_glu, s5_b_glu=s5_b_glu,
             gdn_conv_w=gdn_conv_w, gdn_a_log=gdn_a_log, gdn_dt_bias=gdn_dt_bias, gdn_norm_g=gdn_norm_g,
             w_in_cd=w_in_cd, w_out_cd=w_out_cd, c_qn=c_qn, c_kn=c_kn, c_sink=c_sink, d_qn=d_qn, d_kn=d_kn,
             d_lq1=d_lq1, d_lk1=d_lk1, d_lq2=d_lq2, d_lk2=d_lk2, d_subln=d_subln,
             ffn_up=ffn_up, ffn_conv_w=ffn_conv_w, ffn_conv_b=ffn_conv_b, ffn_down=ffn_down)
    y_prompt, new_s5_re, new_s5_im, new_gdn, new_c_k, new_c_v, new_d_k, new_d_v = _context_pass(x_prompt, c_ctx, p)
    y_sample = _latent_pass(x_sample, c, state_s5_re, state_s5_im, state_gdn,
                            cache_c_k, cache_c_v, cache_d_k, cache_d_v, p)
    return (y_prompt, y_sample, new_s5_re, new_s5_im, new_gdn, new_c_k, new_c_v, new_d_k, new_d_v)
```

```python
import functools
import math

import jax
import jax.numpy as jnp
from jax import lax
from jax.experimental import pallas as pl
from jax.experimental.pallas import tpu as pltpu

F32 = jnp.float32
BF16 = jnp.bfloat16

D_MODEL = 1024
GRID_W = 64
EPS = 1e-6
NEG_INF = -1e30

S5_WIDTH = 512
S5_GROUP = 16
S5_GROUPS = 32
S5_STATE = 64
S5_TILE_GROUPS = 8
S5_TILE_CH = S5_TILE_GROUPS * S5_GROUP
S5_TILE_ST = S5_TILE_GROUPS * S5_STATE
S5_TILES = S5_GROUPS // S5_TILE_GROUPS
S5_SEG = 256
S5_TCHUNK = 128
S5_ROWS = 8

GDN_DK = 128
GDN_DV = 128
GDN_HEADS = 4
GDN_WIDTH = 512
GDN_CHUNK = 64

HEAD_DIM = 64
C_HEADS = 8
C_KV_HEADS = 2
C_GROUP = 4
WINDOW = 128
Q_BLOCK = 128
D_HEADS = 4
D_VDIM = 128
ATTN_SCALE = HEAD_DIM ** -0.5
ROPE_THETA = 10000.0

D_FF = 2816
FF_CHUNK = 256
FF_CHUNKS = D_FF // FF_CHUNK

AB_MAIN = S5_WIDTH + 4 * GDN_WIDTH
AB_PAD = AB_MAIN + 128
CD_IN = 2304

VMEM_LIMIT = 56 * 1024 * 1024


def _cparams(sem):
    return pltpu.CompilerParams(dimension_semantics=sem, vmem_limit_bytes=VMEM_LIMIT)


def _sigmoid(x):
    return 1.0 / (1.0 + jnp.exp(-x))


def _silu(x):
    return x * _sigmoid(x)


def _softplus(x):
    return jnp.maximum(x, 0.0) + jnp.log(1.0 + jnp.exp(-jnp.abs(x)))


def _gelu_tanh(x):
    return 0.5 * x * (1.0 + jnp.tanh(math.sqrt(2.0 / math.pi) * (x + 0.044715 * (x * x * x))))


def _rms_mod(x, g, shift, scale):
    y = x * lax.rsqrt(jnp.mean(x * x, axis=-1, keepdims=True) + EPS)
    return (y * g) * (1.0 + scale) + shift


def _dot(a, b):
    return jnp.dot(a.astype(BF16), b.astype(BF16), preferred_element_type=F32)


def _dot_nt(a, b):
    return lax.dot_general(a.astype(BF16), b.astype(BF16), (((1,), (1,)), ((), ())),
                           preferred_element_type=F32)


def _dot_tn(a, b):
    return lax.dot_general(a.astype(BF16), b.astype(BF16), (((0,), (0,)), ((), ())),
                           preferred_element_type=F32)


def _dot_f32(a, b):
    return jnp.dot(a, b, preferred_element_type=F32, precision=lax.Precision.HIGHEST)


def _split_dot(a, b_bf16):
    hi = a.astype(BF16)
    lo = (a - hi.astype(F32)).astype(BF16)
    return (jnp.dot(hi, b_bf16, preferred_element_type=F32)
            + jnp.dot(lo, b_bf16, preferred_element_type=F32))


def _mod_kernel(ct_ref, w_ref, b_ref, o_ref, *, n_rows):
    c = ct_ref[...]
    s = _silu(c)
    w = w_ref[0]
    o_ref[0] = jnp.zeros(o_ref.shape[1:], F32)
    for m in range(n_rows):
        o_ref[0, m:m + 1, :] = jnp.sum(w * s[:, m:m + 1], axis=0, keepdims=True) + b_ref[0]


def _modulation(cvecs, w_mod, b_mod):
    n, d = cvecs.shape
    depth, _, n_out = w_mod.shape
    tn = 512
    ct = jnp.zeros((d, 8), F32).at[:, :n].set(cvecs.T)
    return pl.pallas_call(
        functools.partial(_mod_kernel, n_rows=n),
        grid=(depth, n_out // tn),
        in_specs=[pl.BlockSpec((d, 8), lambda l, j: (0, 0)),
                  pl.BlockSpec((1, d, tn), lambda l, j: (l, 0, j)),
                  pl.BlockSpec((1, 1, tn), lambda l, j: (l, 0, j))],
        out_specs=pl.BlockSpec((1, 8, tn), lambda l, j: (l, 0, j)),
        out_shape=jax.ShapeDtypeStruct((depth, 8, n_out), F32),
        compiler_params=_cparams(("parallel", "parallel")),
        name="adaln_mod",
    )(ct, w_mod, b_mod.reshape(depth, 1, n_out))


def _proj_ab_kernel(x_ref, g_ref, sh_ref, sc_ref, w_ref, u_ref, qkv_ref, z_ref, ab_ref):
    h = _rms_mod(x_ref[0], g_ref[...], sh_ref[0], sc_ref[0])
    y = _dot(h, w_ref[...])
    u_ref[...] = y[:, :S5_WIDTH]
    qkv_ref[0] = y[:, S5_WIDTH:S5_WIDTH + 3 * GDN_WIDTH]
    z_ref[0] = y[:, S5_WIDTH + 3 * GDN_WIDTH:AB_MAIN]
    ab_ref[0] = y[:, AB_MAIN:AB_PAD]


def _proj_ab(x, g, shift, scale, w):
    bsz, n, d = x.shape
    tm = S5_SEG
    nseg = n // tm
    per_seq = shift.shape[0] > 1
    midx = (lambda b, i: (b, 0, 0)) if per_seq else (lambda b, i: (0, 0, 0))
    return pl.pallas_call(
        _proj_ab_kernel,
        grid=(bsz, nseg),
        in_specs=[pl.BlockSpec((1, tm, d), lambda b, i: (b, i, 0)),
                  pl.BlockSpec((1, d), lambda b, i: (0, 0)),
                  pl.BlockSpec((1, 1, d), midx),
                  pl.BlockSpec((1, 1, d), midx),
                  pl.BlockSpec((d, AB_PAD), lambda b, i: (0, 0))],
        out_specs=[pl.BlockSpec((tm, S5_WIDTH), lambda b, i: (0, b * nseg + i)),
                   pl.BlockSpec((1, tm, 3 * GDN_WIDTH), lambda b, i: (b, i, 0)),
                   pl.BlockSpec((1, tm, GDN_WIDTH), lambda b, i: (b, i, 0)),
                   pl.BlockSpec((1, tm, 128), lambda b, i: (b, i, 0))],
        out_shape=[jax.ShapeDtypeStruct((tm, bsz * nseg * S5_WIDTH), F32),
                   jax.ShapeDtypeStruct((bsz, n, 3 * GDN_WIDTH), F32),
                   jax.ShapeDtypeStruct((bsz, n, GDN_WIDTH), F32),
                   jax.ShapeDtypeStruct((bsz, n, 128), F32)],
        compiler_params=_cparams(("parallel", "parallel")),
        name="proj_ab",
    )(x, g, shift, scale, w)


def _s5_kernel(u_ref, bm_ref, cm_ref, a_ref, h0r_ref, h0i_ref, y_ref, fr_ref, fi_ref,
               xs_ref, hr_ref, hi_ref):
    d = pl.program_id(0)
    i = pl.program_id(3)
    nt = pl.num_programs(3)
    tc = u_ref.shape[0]

    @pl.when(i == 0)
    def _():
        hr_ref[...] = h0r_ref[0]
        hi_ref[...] = h0i_ref[0]

    u2 = u_ref[...].reshape(tc * S5_ROWS, S5_TILE_CH)
    xs_ref[...] = _dot(u2, bm_ref[0, 0]).reshape(tc, S5_ROWS, 2 * S5_TILE_ST)
    ar = jnp.broadcast_to(a_ref[0, 0, 0:1, :], (S5_ROWS, S5_TILE_ST))
    ai = jnp.broadcast_to(a_ref[0, 0, 1:2, :], (S5_ROWS, S5_TILE_ST))

    def step(t, carry):
        hr, hi = carry
        tt = jnp.where(d == 0, t, tc - 1 - t)
        x = xs_ref[tt]
        nr = ar * hr - ai * hi + x[:, :S5_TILE_ST]
        ni = ar * hi + ai * hr + x[:, S5_TILE_ST:]
        xs_ref[tt] = jnp.concatenate([nr, ni], axis=-1)
        return nr, ni

    hr, hi = lax.fori_loop(0, tc, step, (hr_ref[...], hi_ref[...]), unroll=4)
    hr_ref[...] = hr
    hi_ref[...] = hi
    hs = xs_ref[...].reshape(tc * S5_ROWS, 2 * S5_TILE_ST)
    y_ref[0] = _dot(hs, cm_ref[0, 0]).reshape(tc, S5_ROWS, S5_TILE_CH)

    @pl.when(i == nt - 1)
    def _():
        fr_ref[0] = hr
        fi_ref[0] = hi


def _s5_scan(u_t, bmat, cmat, amat, h0r, h0i):
    n, rows, _ = u_t.shape
    nt = n // S5_TCHUNK
    tmap = lambda d, r, j, i: (i + d * (nt - 1 - 2 * i), r, j)
    return pl.pallas_call(
        _s5_kernel,
        grid=(2, rows // S5_ROWS, S5_TILES, nt),
        in_specs=[pl.BlockSpec((S5_TCHUNK, S5_ROWS, S5_TILE_CH), tmap),
                  pl.BlockSpec((1, 1, S5_TILE_CH, 2 * S5_TILE_ST), lambda d, r, j, i: (d, j, 0, 0)),
                  pl.BlockSpec((1, 1, 2 * S5_TILE_ST, S5_TILE_CH), lambda d, r, j, i: (d, j, 0, 0)),
                  pl.BlockSpec((1, 1, 8, S5_TILE_ST), lambda d, r, j, i: (d, j, 0, 0)),
                  pl.BlockSpec((1, S5_ROWS, S5_TILE_ST), lambda d, r, j, i: (d, r, j)),
                  pl.BlockSpec((1, S5_ROWS, S5_TILE_ST), lambda d, r, j, i: (d, r, j))],
        out_specs=[pl.BlockSpec((1, S5_TCHUNK, S5_ROWS, S5_TILE_CH),
                                lambda d, r, j, i: (d,) + tmap(d, r, j, i)),
                   pl.BlockSpec((1, S5_ROWS, S5_TILE_ST), lambda d, r, j, i: (d, r, j)),
                   pl.BlockSpec((1, S5_ROWS, S5_TILE_ST), lambda d, r, j, i: (d, r, j))],
        out_shape=[jax.ShapeDtypeStruct((2, n, rows, S5_WIDTH), F32),
                   jax.ShapeDtypeStruct((2, rows, S5_GROUPS * S5_STATE), F32),
                   jax.ShapeDtypeStruct((2, rows, S5_GROUPS * S5_STATE), F32)],
        scratch_shapes=[pltpu.VMEM((S5_TCHUNK, S5_ROWS, 2 * S5_TILE_ST), F32),
                        pltpu.VMEM((S5_ROWS, S5_TILE_ST), F32),
                        pltpu.VMEM((S5_ROWS, S5_TILE_ST), F32)],
        compiler_params=_cparams(("parallel", "parallel", "parallel", "arbitrary")),
        name="s5_scan",
    )(u_t, bmat, cmat, amat, h0r, h0i)


def _s5_params(p, j):
    lam_re, lam_im, log_dt = p['s5_lam_re'][j], p['s5_lam_im'][j], p['s5_log_dt'][j]
    dt = jnp.exp(log_dt)[..., None]
    mag = jnp.exp(lam_re * dt)
    ar, ai = mag * jnp.cos(lam_im * dt), mag * jnp.sin(lam_im * dt)
    den = lam_re * lam_re + lam_im * lam_im
    fr = ((ar - 1.0) * lam_re + ai * lam_im) / den
    fi = (ai * lam_re - (ar - 1.0) * lam_im) / den
    b_re, b_im = p['s5_b_re'][j], p['s5_b_im'][j]
    bbr = fr[..., None] * b_re - fi[..., None] * b_im
    bbi = fr[..., None] * b_im + fi[..., None] * b_re
    eye = jnp.eye(S5_TILE_GROUPS, dtype=F32)

    def in_blocks(t):
        t = t.reshape(2, S5_TILES, S5_TILE_GROUPS, S5_STATE, S5_GROUP)
        t = jnp.einsum('dtgpc,gh->dtgchp', t, eye)
        return t.reshape(2, S5_TILES, S5_TILE_CH, S5_TILE_ST)

    def out_blocks(t):
        t = t.reshape(2, S5_TILES, S5_TILE_GROUPS, S5_GROUP, S5_STATE)
        t = jnp.einsum('dtgcp,gh->dtgphc', t, eye)
        return t.reshape(2, S5_TILES, S5_TILE_ST, S5_TILE_CH)

    bmat = jnp.concatenate([in_blocks(bbr), in_blocks(bbi)], axis=-1).astype(BF16)
    cmat = jnp.concatenate([out_blocks(p['s5_c_re'][j]), -out_blocks(p['s5_c_im'][j])], axis=-2).astype(BF16)
    seg_mag = jnp.exp(lam_re * dt * S5_SEG)
    pr, pi = seg_mag * jnp.cos(lam_im * dt * S5_SEG), seg_mag * jnp.sin(lam_im * dt * S5_SEG)
    flat = lambda t: t.reshape(2, S5_TILES, 1, S5_TILE_ST)
    amat = jnp.concatenate([flat(ar), flat(ai), flat(pr), flat(pi),
                            jnp.zeros((2, S5_TILES, 4, S5_TILE_ST), F32)], axis=2)
    return bmat, cmat, amat


def _s5_glu_kernel(yf_ref, yb_ref, u_ref, d_ref, w_ref, b_ref, o_ref):
    y = yf_ref[0] + yb_ref[0] + d_ref[...] * u_ref[...]
    g = _gelu_tanh(y)
    o_ref[0] = g * _sigmoid(_dot(g, w_ref[...]) + b_ref[...])


def _s5_glu(y_t, u_t, s5_d, w_glu, b_glu, bsz, nseg):
    tm = S5_SEG
    return pl.pallas_call(
        _s5_glu_kernel,
        grid=(bsz, nseg),
        in_specs=[pl.BlockSpec((1, tm, S5_WIDTH), lambda b, i: (0, 0, b * nseg + i)),
                  pl.BlockSpec((1, tm, S5_WIDTH), lambda b, i: (1, 0, b * nseg + i)),
                  pl.BlockSpec((tm, S5_WIDTH), lambda b, i: (0, b * nseg + i)),
                  pl.BlockSpec((1, S5_WIDTH), lambda b, i: (0, 0)),
                  pl.BlockSpec((S5_WIDTH, S5_WIDTH), lambda b, i: (0, 0)),
                  pl.BlockSpec((1, S5_WIDTH), lambda b, i: (0, 0))],
        out_specs=pl.BlockSpec((1, tm, S5_WIDTH), lambda b, i: (b, i, 0)),
        out_shape=jax.ShapeDtypeStruct((bsz, nseg * tm, S5_WIDTH), F32),
        compiler_params=_cparams(("parallel", "parallel")),
        name="s5_glu",
    )(y_t, y_t, u_t, s5_d, w_glu, b_glu)


def _s5_mixer(u_t, p, j, h0r, h0i, bsz, nseg):
    rows = bsz * nseg
    bmat, cmat, amat = _s5_params(p, j)
    u3 = u_t.reshape(S5_SEG, rows, S5_WIDTH)
    if nseg == 1:
        y_t, fr, fi = _s5_scan(u3, bmat, cmat, amat, h0r, h0i)
    else:
        zero = jnp.zeros((2, bsz, nseg, S5_GROUPS * S5_STATE), F32)
        first = jnp.array([0, nseg - 1])
        seed = lambda h0: zero.at[jnp.arange(2), :, first].set(h0).reshape(2, rows, -1)
        _, fr, fi = _s5_scan(u3, bmat, cmat, amat, seed(h0r), seed(h0i))
        fr = fr.reshape(2, bsz, nseg, -1)
        fi = fi.reshape(2, bsz, nseg, -1)
        pr = amat[:, :, 2].reshape(2, 1, -1)
        pi = amat[:, :, 3].reshape(2, 1, -1)

        def chain(dr, order):
            hr, hi = (h0r[dr], h0i[dr])
            outs_r, outs_i = {}, {}
            for n_done, k in enumerate(order):
                outs_r[k], outs_i[k] = hr, hi
                if n_done == 0:
                    hr, hi = fr[dr, :, k], fi[dr, :, k]
                else:
                    hr, hi = (pr[dr] * hr - pi[dr] * hi + fr[dr, :, k],
                              pr[dr] * hi + pi[dr] * hr + fi[dr, :, k])
            st = lambda o: jnp.stack([o[k] for k in range(nseg)], axis=1)
            return st(outs_r), st(outs_i), hr, hi

        sr0, si0, er0, ei0 = chain(0, list(range(nseg)))
        sr1, si1, er1, ei1 = chain(1, list(range(nseg - 1, -1, -1)))
        start_r = jnp.stack([sr0, sr1]).reshape(2, rows, -1)
        start_i = jnp.stack([si0, si1]).reshape(2, rows, -1)
        y_t, _, _ = _s5_scan(u3, bmat, cmat, amat, start_r, start_i)
        fr = jnp.stack([er0, er1])
        fi = jnp.stack([ei0, ei1])
    ya = _s5_glu(y_t.reshape(2, S5_SEG, rows * S5_WIDTH), u_t, p['s5_d'][j][None],
                 p['s5_w_glu'][j].astype(BF16), p['s5_b_glu'][j][None], bsz, nseg)
    return ya, fr, fi


def _gdn_kernel(qkv_ref, z_ref, ab_ref, cw_ref, gp_ref, ng_ref, s0_ref, o_ref, sf_ref,
                q_s, k_s, v_s, gate_s, of_s, ob_s, st_s):
    n = qkv_ref.shape[1]
    nc = n // GDN_CHUNK
    row = lax.broadcasted_iota(jnp.int32, (n, 1), 0)

    for blk in range(3 * GDN_HEADS):
        cols = slice(blk * GDN_DK, (blk + 1) * GDN_DK)
        hs = slice((blk % GDN_HEADS) * GDN_DK, (blk % GDN_HEADS + 1) * GDN_DK)
        x = qkv_ref[0, :, cols]
        xm = jnp.where(row == 0, 0.0, pltpu.roll(x, 1, 0))
        xp = jnp.where(row == n - 1, 0.0, pltpu.roll(x, n - 1, 0))
        y = _silu(xm * cw_ref[0:1, cols] + x * cw_ref[1:2, cols] + xp * cw_ref[2:3, cols])
        if blk < GDN_HEADS:
            q_s[:, hs] = y * lax.rsqrt(jnp.sum(y * y, axis=-1, keepdims=True) + EPS) * (GDN_DK ** -0.5)
        elif blk < 2 * GDN_HEADS:
            k_s[:, hs] = y * lax.rsqrt(jnp.sum(y * y, axis=-1, keepdims=True) + EPS)
        else:
            v_s[:, hs] = y

    ab = ab_ref[0]
    beta = _sigmoid(ab)
    g = -jnp.exp(gp_ref[0:1, :]) * _softplus(ab + gp_ref[1:2, :])
    pos = row % GDN_CHUNK
    pre, suf = g, g
    sft = 1
    while sft < GDN_CHUNK:
        pre = pre + jnp.where(pos >= sft, pltpu.roll(pre, sft, 0), 0.0)
        suf = suf + jnp.where(pos < GDN_CHUNK - sft, pltpu.roll(suf, n - sft, 0), 0.0)
        sft *= 2
    gate_s[0] = beta
    gate_s[1] = pre
    gate_s[2] = suf

    st_s[...] = s0_ref[0]
    ri = lax.broadcasted_iota(jnp.int32, (GDN_CHUNK, GDN_CHUNK), 0)
    ci = lax.broadcasted_iota(jnp.int32, (GDN_CHUNK, GDN_CHUNK), 1)
    eye = (ri == ci).astype(F32)

    def chunk_step(c, carry):
        for dr in range(2):
            cidx = c if dr == 0 else nc - 1 - c
            rows = pl.ds(pl.multiple_of(cidx * GDN_CHUNK, GDN_CHUNK), GDN_CHUNK)
            incl = (ri >= ci) if dr == 0 else (ri <= ci)
            strict = (ri > ci) if dr == 0 else (ri < ci)
            gc_blk = gate_s[1 + dr, rows, :]
            gc_t = gc_blk.T
            beta_blk = gate_s[0, rows, :]
            for h in range(GDN_HEADS):
                hs = slice(h * GDN_DK, (h + 1) * GDN_DK)
                lane = dr * GDN_HEADS + h
                q = q_s[rows, hs]
                k = k_s[rows, hs]
                v = v_s[rows, hs]
                bcol = beta_blk[:, 8 + lane:9 + lane]
                gcol = gc_blk[:, lane:lane + 1]
                grow = gc_t[lane:lane + 1, :]
                gl = gcol[GDN_CHUNK - 1:GDN_CHUNK] if dr == 0 else gcol[0:1]
                decay = jnp.exp(jnp.where(incl, gcol - grow, NEG_INF))
                kb = k * bcol
                nmat = jnp.where(strict, _dot_nt(kb, k) * decay, 0.0)
                tmat = eye - nmat
                pw = nmat
                for _ in range(5):
                    pw = _dot_f32(pw, pw)
                    tmat = tmat + _dot_f32(tmat, pw)
                u = _dot(tmat, v * bcol)
                w = _dot(tmat, kb * jnp.exp(gcol))
                amat = jnp.where(incl, _dot_nt(q, k) * decay, 0.0)
                s = st_s[dr, h]
                vn = u - _dot(w, s)
                o = _dot(q * jnp.exp(gcol), s) + _dot(amat, vn)
                st_s[dr, h] = s * jnp.exp(gl) + _dot_tn(k * jnp.exp(gl - gcol), vn)
                if dr == 0:
                    of_s[rows, hs] = o
                else:
                    ob_s[rows, hs] = o
        return carry

    lax.fori_loop(0, nc, chunk_step, 0)
    sf_ref[0] = st_s[...]

    z = z_ref[0]
    for h in range(GDN_HEADS):
        hs = slice(h * GDN_DV, (h + 1) * GDN_DV)
        o = of_s[:, hs] + ob_s[:, hs]
        o = o * lax.rsqrt(jnp.mean(o * o, axis=-1, keepdims=True) + EPS) * ng_ref[...]
        o_ref[0, :, hs] = o * _silu(z[:, hs])


def _gdn_mixer(qkv, z, ab, p, j, s0):
    bsz, n, _ = qkv.shape
    gp = jnp.zeros((8, 128), F32)
    gp = gp.at[0, :8].set(p['gdn_a_log'][j].reshape(8)).at[1, :8].set(p['gdn_dt_bias'][j].reshape(8))
    cw = jnp.zeros((8, 3 * GDN_WIDTH), F32).at[:3].set(p['gdn_conv_w'][j])
    return pl.pallas_call(
        _gdn_kernel,
        grid=(bsz,),
        in_specs=[pl.BlockSpec((1, n, 3 * GDN_WIDTH), lambda b: (b, 0, 0)),
                  pl.BlockSpec((1, n, GDN_WIDTH), lambda b: (b, 0, 0)),
                  pl.BlockSpec((1, n, 128), lambda b: (b, 0, 0)),
                  pl.BlockSpec((8, 3 * GDN_WIDTH), lambda b: (0, 0)),
                  pl.BlockSpec((8, 128), lambda b: (0, 0)),
                  pl.BlockSpec((1, GDN_DV), lambda b: (0, 0)),
                  pl.BlockSpec((1, 2, GDN_HEADS, GDN_DK, GDN_DV), lambda b: (b, 0, 0, 0, 0))],
        out_specs=[pl.BlockSpec((1, n, GDN_WIDTH), lambda b: (b, 0, 0)),
                   pl.BlockSpec((1, 2, GDN_HEADS, GDN_DK, GDN_DV), lambda b: (b, 0, 0, 0, 0))],
        out_shape=[jax.ShapeDtypeStruct((bsz, n, GDN_WIDTH), F32),
                   jax.ShapeDtypeStruct((bsz, 2, GDN_HEADS, GDN_DK, GDN_DV), F32)],
        scratch_shapes=[pltpu.VMEM((n, GDN_WIDTH), F32), pltpu.VMEM((n, GDN_WIDTH), F32),
                        pltpu.VMEM((n, GDN_WIDTH), F32), pltpu.VMEM((3, n, 128), F32),
                        pltpu.VMEM((n, GDN_WIDTH), F32), pltpu.VMEM((n, GDN_WIDTH), F32),
                        pltpu.VMEM((2, GDN_HEADS, GDN_DK, GDN_DV), F32)],
        compiler_params=_cparams(("parallel",)),
        name="gdn_mixer",
    )(qkv, z, ab, cw, gp, p['gdn_norm_g'][j][None], s0)


def _proj_cd_kernel(x_ref, g_ref, sh_ref, sc_ref, w_ref, gm_ref, gain_ref, cos_ref, sin_ref,
                    qc_ref, kc_ref, vc_ref, qd_ref, kd_ref, vd_ref, *, rope):
    h = _rms_mod(x_ref[0], g_ref[...], sh_ref[0], sc_ref[0])
    y = _dot(h, w_ref[...])
    lane = lax.broadcasted_iota(jnp.int32, (1, 512), 1)
    low = (lane % 32) < 16

    def head_norm(t, gain, scale):
        w = t.shape[1]
        ms = _split_dot(t * t, gm_ref[:w, :w])
        t = t * lax.rsqrt(ms + EPS) * gain
        if rope:
            part = jnp.where(low[:, :w], pltpu.roll(t, w - 16, 1), pltpu.roll(t, 16, 1))
            t = t * cos_ref[:, :w] + part * sin_ref[:, :w]
        return t * scale if scale != 1.0 else t

    qc_ref[0] = head_norm(y[:, 0:512], gain_ref[0:1, :], ATTN_SCALE)
    kc_ref[0] = head_norm(y[:, 512:640], gain_ref[1:2, :128], 1.0)
    vc_ref[0] = y[:, 640:768]
    qd_ref[0] = head_norm(y[:, 768:1280], gain_ref[2:3, :], ATTN_SCALE)
    kd_ref[0] = head_norm(y[:, 1280:1792], gain_ref[3:4, :], 1.0)
    vd_ref[0] = y[:, 1792:2304]


def _rope_tables(n):
    rows = n // GRID_W
    row = jnp.repeat(jnp.arange(rows), GRID_W).astype(F32)
    col = jnp.tile(jnp.arange(GRID_W), rows).astype(F32)
    quarter = HEAD_DIM // 4
    inv = ROPE_THETA ** (-jnp.arange(quarter, dtype=F32) / quarter)
    ang_r = row[:, None] * inv[None, :]
    ang_c = col[:, None] * inv[None, :]
    cos = jnp.concatenate([jnp.cos(ang_r), jnp.cos(ang_r), jnp.cos(ang_c), jnp.cos(ang_c)], axis=-1)
    sin = jnp.concatenate([-jnp.sin(ang_r), jnp.sin(ang_r), -jnp.sin(ang_c), jnp.sin(ang_c)], axis=-1)
    return jnp.tile(cos, (1, 8)), jnp.tile(sin, (1, 8))


def _proj_cd(x, g, shift, scale, w, p, j, rope):
    bsz, n, d = x.shape
    tm = 256
    per_seq = shift.shape[0] > 1
    midx = (lambda b, i: (b, 0, 0)) if per_seq else (lambda b, i: (0, 0, 0))
    lane = jnp.arange(512)
    gmat = ((lane[:, None] // HEAD_DIM) == (lane[None, :] // HEAD_DIM)).astype(F32) / HEAD_DIM
    gains = jnp.zeros((8, 512), F32)
    gains = gains.at[0].set(jnp.tile(p['c_qn'][j], 8)).at[1].set(jnp.tile(p['c_kn'][j], 8))
    gains = gains.at[2].set(jnp.tile(p['d_qn'][j], 8)).at[3].set(jnp.tile(p['d_kn'][j], 8))
    if rope:
        cos, sin = _rope_tables(n)
    else:
        cos, sin = jnp.ones((tm, 512), F32), jnp.zeros((tm, 512), F32)
    tidx = (lambda b, i: (i, 0)) if rope else (lambda b, i: (0, 0))
    blk = lambda w_: pl.BlockSpec((1, tm, w_), lambda b, i: (b, i, 0))
    return pl.pallas_call(
        functools.partial(_proj_cd_kernel, rope=rope),
        grid=(bsz, n // tm),
        in_specs=[blk(d),
                  pl.BlockSpec((1, d), lambda b, i: (0, 0)),
                  pl.BlockSpec((1, 1, d), midx),
                  pl.BlockSpec((1, 1, d), midx),
                  pl.BlockSpec((d, CD_IN), lambda b, i: (0, 0)),
                  pl.BlockSpec((512, 512), lambda b, i: (0, 0)),
                  pl.BlockSpec((8, 512), lambda b, i: (0, 0)),
                  pl.BlockSpec((tm, 512), tidx),
                  pl.BlockSpec((tm, 512), tidx)],
        out_specs=[blk(512), blk(128), blk(128), blk(512), blk(512), blk(512)],
        out_shape=[jax.ShapeDtypeStruct((bsz, n, w_), F32) for w_ in (512, 128, 128, 512, 512, 512)],
        compiler_params=_cparams(("parallel", "parallel")),
        name="proj_cd",
    )(x, g, shift, scale, w, gmat.astype(BF16), gains, cos, sin)


def _softmax_parts(scores, extra=None):
    m = scores[0].max(axis=-1, keepdims=True)
    for s in scores[1:]:
        m = jnp.maximum(m, s.max(axis=-1, keepdims=True))
    if extra is not None:
        m = jnp.maximum(m, extra)
    ps = [jnp.exp(s - m) for s in scores]
    den = ps[0].sum(axis=-1, keepdims=True)
    for pp in ps[1:]:
        den = den + pp.sum(axis=-1, keepdims=True)
    if extra is not None:
        den = den + jnp.exp(extra - m)
    return ps, den


def _attn_kernel(qc_ref, kc_ref, vc_ref, qd_ref, kd_ref, vd_ref, *rest, windowed, n_ctx, lam_scale):
    if n_ctx:
        ck_ref, cv_ref, dk_ref, dv_ref, misc_ref, sub_ref, o_ref = rest
    else:
        misc_ref, sub_ref, o_ref = rest
    tq = qc_ref.shape[1]
    n = kc_ref.shape[1]
    start = pl.program_id(1) * tq
    if windowed:
        span = tq + 2 * WINDOW
        k0 = pl.multiple_of(jnp.clip(start - WINDOW, 0, n - span), 128)
        krows = pl.ds(k0, span)
        qpos = start + lax.broadcasted_iota(jnp.int32, (tq, span), 0)
        kpos = k0 + lax.broadcasted_iota(jnp.int32, (tq, span), 1)
        ok = jnp.abs(qpos - kpos) <= WINDOW
    else:
        krows = pl.ds(0, n)

    qc = qc_ref[0]
    kc = kc_ref[0, krows, :]
    vc = vc_ref[0, krows, :]
    for h in range(C_HEADS):
        gsl = slice((h // C_GROUP) * HEAD_DIM, (h // C_GROUP + 1) * HEAD_DIM)
        q = qc[:, h * HEAD_DIM:(h + 1) * HEAD_DIM]
        s_loc = _dot_nt(q, kc[:, gsl])
        if windowed:
            s_loc = jnp.where(ok, s_loc, NEG_INF)
        scores = [s_loc]
        if n_ctx:
            scores.append(_dot_nt(q, ck_ref[0, :, gsl]))
        ps, den = _softmax_parts(scores, misc_ref[0:1, h:h + 1])
        o = _dot(ps[0], vc[:, gsl])
        if n_ctx:
            o = o + _dot(ps[1], cv_ref[0, :, gsl])
        o_ref[0, :, h * HEAD_DIM:(h + 1) * HEAD_DIM] = o / den

    qd = qd_ref[0]
    lam = misc_ref[1:2, 0:1]
    for h in range(D_HEADS):
        atts = []
        for c in range(2):
            sl = slice((2 * h + c) * HEAD_DIM, (2 * h + c + 1) * HEAD_DIM)
            q = qd[:, sl]
            scores = [_dot_nt(q, kd_ref[0, :, sl])]
            if n_ctx:
                scores.append(_dot_nt(q, dk_ref[0, :, sl]))
            ps, den = _softmax_parts(scores)
            atts.append([pp / den for pp in ps])
        vsl = slice(h * D_VDIM, (h + 1) * D_VDIM)
        o = _dot(atts[0][0] - lam * atts[1][0], vd_ref[0, :, vsl])
        if n_ctx:
            o = o + _dot(atts[0][1] - lam * atts[1][1], dv_ref[0, :, vsl])
        o = o * lax.rsqrt(jnp.mean(o * o, axis=-1, keepdims=True) + EPS) * sub_ref[...] * lam_scale
        o_ref[0, :, 512 + h * D_VDIM:512 + (h + 1) * D_VDIM] = o


def _attention(qc, kc, vc, qd, kd, vd, caches, misc, subln, lam_init, tq, windowed):
    bsz, n, _ = qc.shape
    qblk = lambda w_: pl.BlockSpec((1, tq, w_), lambda b, i: (b, i, 0))
    kblk = lambda rows, w_: pl.BlockSpec((1, rows, w_), lambda b, i: (b, 0, 0))
    in_specs = [qblk(512), kblk(n, 128), kblk(n, 128), qblk(512), kblk(n, 512), kblk(n, 512)]
    args = [qc, kc, vc, qd, kd, vd]
    n_ctx = 0
    if caches is not None:
        n_ctx = caches[0].shape[1]
        in_specs += [kblk(n_ctx, 128), kblk(n_ctx, 128), kblk(n_ctx, 512), kblk(n_ctx, 512)]
        args += list(caches)
    in_specs += [pl.BlockSpec((8, 128), lambda b, i: (0, 0)), pl.BlockSpec((1, D_VDIM), lambda b, i: (0, 0))]
    args += [misc, subln]
    return pl.pallas_call(
        functools.partial(_attn_kernel, windowed=windowed, n_ctx=n_ctx, lam_scale=1.0 - lam_init),
        grid=(bsz, n // tq),
        in_specs=in_specs,
        out_specs=pl.BlockSpec((1, tq, 1024), lambda b, i: (b, i, 0)),
        out_shape=jax.ShapeDtypeStruct((bsz, n, 1024), F32),
        compiler_params=_cparams(("parallel", "parallel")),
        name="attn_win" if windowed else "attn_ctx",
    )(*args)


def _post_kernel(x_ref, mix_ref, g1_ref, sh_ref, sc_ref, g2_ref, ng_ref, wo_ref, wup_ref, cw_ref, wdn_ref,
                 o_ref, acc_ref, *, seq_len):
    rows = x_ref.shape[1]
    x1 = x_ref[0] + g1_ref[0] * _dot(mix_ref[0], wo_ref[...])
    h = _rms_mod(x1, ng_ref[...], sh_ref[0], sc_ref[0]).astype(BF16)
    pos = lax.broadcasted_iota(jnp.int32, (rows, 1), 0) % seq_len
    first = pos == 0
    last = pos == seq_len - 1
    acc_ref[...] = jnp.zeros_like(acc_ref)

    def chunk(c, carry):
        up = jnp.dot(h, wup_ref[c], preferred_element_type=F32)
        a = up[:, :FF_CHUNK]
        b = up[:, FF_CHUNK:]
        cw = cw_ref[c]
        am = jnp.where(first, 0.0, pltpu.roll(a, 1, 0))
        ap = jnp.where(last, 0.0, pltpu.roll(a, rows - 1, 0))
        a = am * cw[0:1] + a * cw[1:2] + ap * cw[2:3] + cw[3:4]
        acc_ref[...] += _dot(_silu(a) * b, wdn_ref[c])
        return carry

    lax.fori_loop(0, FF_CHUNKS, chunk, 0)
    o_ref[0] = x1 + g2_ref[0] * acc_ref[...]


def _post(x, mix, g1, sh2, sc2, g2, norm_g, w_out, wup, cw, wdn, seq_len):
    t, rows, d = x.shape
    per_tile = g1.shape[0] > 1
    midx = (lambda b: (b, 0, 0)) if per_tile else (lambda b: (0, 0, 0))
    mspec = pl.BlockSpec((1, 1, d), midx)
    whole = lambda shape: pl.BlockSpec(shape, lambda b: (0,) * len(shape), pipeline_mode=pl.Buffered(1))
    return pl.pallas_call(
        functools.partial(_post_kernel, seq_len=seq_len),
        grid=(t,),
        in_specs=[pl.BlockSpec((1, rows, d), lambda b: (b, 0, 0)),
                  pl.BlockSpec((1, rows, d), lambda b: (b, 0, 0)),
                  mspec, mspec, mspec, mspec,
                  pl.BlockSpec((1, d), lambda b: (0, 0)),
                  whole((d, d)),
                  whole((FF_CHUNKS, d, 2 * FF_CHUNK)),
                  whole((FF_CHUNKS, 8, FF_CHUNK)),
                  whole((FF_CHUNKS, FF_CHUNK, d))],
        out_specs=pl.BlockSpec((1, rows, d), lambda b: (b, 0, 0)),
        out_shape=jax.ShapeDtypeStruct((t, rows, d), F32),
        scratch_shapes=[pltpu.VMEM((rows, d), F32)],
        compiler_params=_cparams(("parallel",)),
        name="post_ffn",
    )(x, mix, g1, sh2, sc2, g2, norm_g, w_out, wup, cw, wdn)


def _ffn_weights(p, l):
    up = p['ffn_up'][l]
    a = up[:, :D_FF].reshape(D_MODEL, FF_CHUNKS, FF_CHUNK)
    b = up[:, D_FF:].reshape(D_MODEL, FF_CHUNKS, FF_CHUNK)
    wup = jnp.concatenate([a, b], axis=-1).transpose(1, 0, 2).astype(BF16)
    cw = jnp.zeros((8, D_FF), F32).at[:3].set(p['ffn_conv_w'][l]).at[3].set(p['ffn_conv_b'][l])
    cw = cw.reshape(8, FF_CHUNKS, FF_CHUNK).transpose(1, 0, 2)
    wdn = p['ffn_down'][l].reshape(FF_CHUNKS, FF_CHUNK, D_MODEL).astype(BF16)
    return wup, cw, wdn


def _lambda_init(layer):
    return 0.8 - 0.6 * math.exp(-0.3 * layer)


def _trunk(x, mods, p, states, caches, seqs_per_tile):
    bsz, n, d = x.shape
    nseg = n // S5_SEG
    depth = p['w_mod'].shape[0]
    news = {k: [] for k in ('s5r', 's5i', 'gdn', 'ck', 'cv', 'dk', 'dv')}
    for l in range(depth):
        j = l // 2
        sh1, sc1, g1, sh2, sc2, g2 = mods[l]
        ng1 = p['norm1_g'][l][None]
        if l % 2 == 0:
            w_in = p['w_in_ab'][j]
            w_pad = jnp.concatenate([w_in[:, :AB_MAIN], w_in[:, AB_MAIN:],
                                     jnp.zeros((d, AB_PAD - w_in.shape[1]), F32)], axis=1).astype(BF16)
            u_t, qkv, z, ab = _proj_ab(x, ng1, sh1, sc1, w_pad)
            if states is None:
                h0r = jnp.zeros((2, bsz, S5_GROUPS * S5_STATE), F32)
                h0i = h0r
                s0 = jnp.zeros((bsz, 2, GDN_HEADS, GDN_DK, GDN_DV), F32)
            else:
                h0r = states[0][:, j].reshape(bsz, 2, -1).transpose(1, 0, 2)
                h0i = states[1][:, j].reshape(bsz, 2, -1).transpose(1, 0, 2)
                s0 = states[2][:, j]
            ya, fr, fi = _s5_mixer(u_t, p, j, h0r, h0i, bsz, nseg)
            yb, sg = _gdn_mixer(qkv, z, ab, p, j, s0)
            mix = jnp.concatenate([ya, yb], axis=-1)
            w_out = p['w_out_ab'][j]
            news['s5r'].append(fr.transpose(1, 0, 2).reshape(bsz, 2, S5_GROUPS, S5_STATE))
            news['s5i'].append(fi.transpose(1, 0, 2).reshape(bsz, 2, S5_GROUPS, S5_STATE))
            news['gdn'].append(sg)
        else:
            lam_init = _lambda_init(l)
            f = lambda name: p[name][j]
            lam = (jnp.exp(jnp.sum(f('d_lq1') * f('d_lk1'))) - jnp.exp(jnp.sum(f('d_lq2') * f('d_lk2')))
                   + lam_init)
            misc = jnp.zeros((8, 128), F32).at[0, :C_HEADS].set(p['c_sink'][j]).at[1, :].set(lam)
            rope = caches is not None
            qc, kc, vc, qd, kd, vd = _proj_cd(x, ng1, sh1, sc1, p['w_in_cd'][j].astype(BF16), p, j, rope)
            if caches is None:
                mix = _attention(qc, kc, vc, qd, kd, vd, None, misc, p['d_subln'][j][None],
                                 lam_init, n, False)
            else:
                n_ctx = caches[0].shape[2]
                cc = (caches[0][:, j].reshape(bsz, n_ctx, 128), caches[1][:, j].reshape(bsz, n_ctx, 128),
                      caches[2][:, j].reshape(bsz, n_ctx, 512), caches[3][:, j].reshape(bsz, n_ctx, 512))
                mix = _attention(qc, kc, vc, qd, kd, vd, cc, misc, p['d_subln'][j][None],
                                 lam_init, Q_BLOCK, True)
            w_out = p['w_out_cd'][j]
            news['ck'].append(kc.reshape(bsz, n, C_KV_HEADS, HEAD_DIM))
            news['cv'].append(vc.reshape(bsz, n, C_KV_HEADS, HEAD_DIM))
            news['dk'].append(kd.reshape(bsz, n, D_HEADS, 2, HEAD_DIM))
            news['dv'].append(vd.reshape(bsz, n, D_HEADS, D_VDIM))
        wup, cw, wdn = _ffn_weights(p, l)
        tiles = bsz // seqs_per_tile
        x = _post(x.reshape(tiles, seqs_per_tile * n, d), mix.reshape(tiles, seqs_per_tile * n, d),
                  g1, sh2, sc2, g2, p['norm2_g'][l][None], w_out.astype(BF16), wup, cw, wdn, n)
        x = x.reshape(bsz, n, d)
    return x, news


def kernel(x_prompt, x_sample, c, state_s5_re, state_s5_im, state_gdn, cache_c_k, cache_c_v, cache_d_k, cache_d_v, c_ctx, w_mod, b_mod, norm1_g, norm2_g, w_in_ab, w_out_ab, s5_lam_re, s5_lam_im, s5_log_dt, s5_b_re, s5_b_im, s5_c_re, s5_c_im, s5_d, s5_w_glu, s5_b_glu, gdn_conv_w, gdn_a_log, gdn_dt_bias, gdn_norm_g, w_in_cd, w_out_cd, c_qn, c_kn, c_sink, d_qn, d_kn, d_lq1, d_lk1, d_lq2, d_lk2, d_subln, ffn_up, ffn_conv_w, ffn_conv_b, ffn_down):
    p = dict(w_mod=w_mod, b_mod=b_mod, norm1_g=norm1_g, norm2_g=norm2_g, w_in_ab=w_in_ab, w_out_ab=w_out_ab,
             s5_lam_re=s5_lam_re, s5_lam_im=s5_lam_im, s5_log_dt=s5_log_dt, s5_b_re=s5_b_re, s5_b_im=s5_b_im,
             s5_c_re=s5_c_re, s5_c_im=s5_c_im, s5_d=s5_d, s5_w_glu=s5_w_glu, s5_b_glu=s5_b_glu,
             gdn_conv_w=gdn_conv_w, gdn_a_log=gdn_a_log, gdn_dt_bias=gdn_dt_bias, gdn_norm_g=gdn_norm_g,
             w_in_cd=w_in_cd, w_out_cd=w_out_cd, c_qn=c_qn, c_kn=c_kn, c_sink=c_sink, d_qn=d_qn, d_kn=d_kn,
             d_lq1=d_lq1, d_lk1=d_lk1, d_lq2=d_lq2, d_lk2=d_lk2, d_subln=d_subln,
             ffn_up=ffn_up, ffn_conv_w=ffn_conv_w, ffn_conv_b=ffn_conv_b, ffn_down=ffn_down)
    depth = w_mod.shape[0]
    n_dec = c.shape[0]
    mod = _modulation(jnp.concatenate([c_ctx[None], c], axis=0), w_mod, b_mod)
    split6 = lambda m: [m[:, None, k * D_MODEL:(k + 1) * D_MODEL] for k in range(6)]
    mods_ctx = [split6(mod[l, 0:1]) for l in range(depth)]
    mods_dec = [split6(mod[l, 1:1 + n_dec]) for l in range(depth)]

    y_prompt, nw = _trunk(x_prompt, mods_ctx, p, None, None, 2)
    y_sample, _ = _trunk(x_sample, mods_dec, p, (state_s5_re, state_s5_im, state_gdn),
                         (cache_c_k, cache_c_v, cache_d_k, cache_d_v), 1)
    st = lambda name: jnp.stack(nw[name], axis=1)
    return (y_prompt, y_sample, st('s5r'), st('s5i'), st('gdn'), st('ck'), st('cv'), st('dk'), st('dv'))
```

```python
import functools
import math

import jax
import jax.numpy as jnp
from jax import lax
from jax.experimental import pallas as pl
from jax.experimental.pallas import tpu as pltpu

F32 = jnp.float32
BF16 = jnp.bfloat16

D_MODEL = 1024
GRID_W = 64
EPS = 1e-6
NEG_INF = -1e30

S5_WIDTH = 512
S5_GROUP = 16
S5_GROUPS = 32
S5_STATE = 64
S5_TILE_GROUPS = 8
S5_TILE_CH = S5_TILE_GROUPS * S5_GROUP
S5_TILE_ST = S5_TILE_GROUPS * S5_STATE
S5_TILES = S5_GROUPS // S5_TILE_GROUPS
S5_SEG = 256
S5_TCHUNK = 128
S5_ROWS = 8

GDN_DK = 128
GDN_DV = 128
GDN_HEADS = 4
GDN_WIDTH = 512
GDN_CHUNK = 64

HEAD_DIM = 64
C_HEADS = 8
C_KV_HEADS = 2
C_GROUP = 4
WINDOW = 128
Q_BLOCK = 128
D_HEADS = 4
D_VDIM = 128
ATTN_SCALE = HEAD_DIM ** -0.5
ROPE_THETA = 10000.0

D_FF = 2816
FF_CHUNK = 256
FF_CHUNKS = D_FF // FF_CHUNK
POST_ROWS = 512
POST_HALO = 8

AB_MAIN = S5_WIDTH + 4 * GDN_WIDTH
AB_PAD = AB_MAIN + 128
CD_IN = 2304

VMEM_LIMIT = 56 * 1024 * 1024


def _cparams(sem):
    return pltpu.CompilerParams(dimension_semantics=sem, vmem_limit_bytes=VMEM_LIMIT)


def _sigmoid(x):
    return 1.0 / (1.0 + jnp.exp(-x))


def _silu(x):
    return x * _sigmoid(x)


def _softplus(x):
    return jnp.maximum(x, 0.0) + jnp.log(1.0 + jnp.exp(-jnp.abs(x)))


def _gelu_tanh(x):
    return 0.5 * x * (1.0 + jnp.tanh(math.sqrt(2.0 / math.pi) * (x + 0.044715 * (x * x * x))))


def _rms_mod(x, g, shift, scale):
    y = x * lax.rsqrt(jnp.mean(x * x, axis=-1, keepdims=True) + EPS)
    return (y * g) * (1.0 + scale) + shift


def _dot(a, b):
    return jnp.dot(a.astype(BF16), b.astype(BF16), preferred_element_type=F32)


def _dot_nt(a, b):
    return lax.dot_general(a.astype(BF16), b.astype(BF16), (((1,), (1,)), ((), ())),
                           preferred_element_type=F32)


def _dot_tn(a, b):
    return lax.dot_general(a.astype(BF16), b.astype(BF16), (((0,), (0,)), ((), ())),
                           preferred_element_type=F32)


def _hi_lo(a):
    hi = a.astype(BF16)
    return hi, (a - hi.astype(F32)).astype(BF16)


def _dot3(a, b):
    a_hi, a_lo = _hi_lo(a)
    b_hi, b_lo = _hi_lo(b)
    mm = functools.partial(jnp.dot, preferred_element_type=F32)
    return mm(a_hi, b_hi) + (mm(a_lo, b_hi) + mm(a_hi, b_lo))


def _split_dot(a, b_bf16):
    hi = a.astype(BF16)
    lo = (a - hi.astype(F32)).astype(BF16)
    return (jnp.dot(hi, b_bf16, preferred_element_type=F32)
            + jnp.dot(lo, b_bf16, preferred_element_type=F32))


def _mod_kernel(ct_ref, w_ref, b_ref, o_ref, *, n_rows):
    c = ct_ref[...]
    s = _silu(c)
    w = w_ref[0]
    o_ref[0] = jnp.zeros(o_ref.shape[1:], F32)
    for m in range(n_rows):
        o_ref[0, m:m + 1, :] = jnp.sum(w * s[:, m:m + 1], axis=0, keepdims=True) + b_ref[0]


def _modulation(cvecs, w_mod, b_mod):
    n, d = cvecs.shape
    depth, _, n_out = w_mod.shape
    tn = 512
    ct = jnp.zeros((d, 8), F32).at[:, :n].set(cvecs.T)
    return pl.pallas_call(
        functools.partial(_mod_kernel, n_rows=n),
        grid=(depth, n_out // tn),
        in_specs=[pl.BlockSpec((d, 8), lambda l, j: (0, 0)),
                  pl.BlockSpec((1, d, tn), lambda l, j: (l, 0, j)),
                  pl.BlockSpec((1, 1, tn), lambda l, j: (l, 0, j))],
        out_specs=pl.BlockSpec((1, 8, tn), lambda l, j: (l, 0, j)),
        out_shape=jax.ShapeDtypeStruct((depth, 8, n_out), F32),
        compiler_params=_cparams(("parallel", "parallel")),
        name="adaln_mod",
    )(ct, w_mod, b_mod.reshape(depth, 1, n_out))


def _proj_ab_kernel(x_ref, g_ref, sh_ref, sc_ref, w_ref, u_ref, qkv_ref, z_ref, ab_ref):
    h = _rms_mod(x_ref[0], g_ref[...], sh_ref[0], sc_ref[0])
    y = _dot(h, w_ref[...])
    u_ref[...] = y[:, :S5_WIDTH]
    qkv_ref[0] = y[:, S5_WIDTH:S5_WIDTH + 3 * GDN_WIDTH]
    z_ref[0] = y[:, S5_WIDTH + 3 * GDN_WIDTH:AB_MAIN]
    ab_ref[0] = y[:, AB_MAIN:AB_PAD]


def _proj_ab(x, g, shift, scale, w):
    bsz, n, d = x.shape
    tm = S5_SEG
    nseg = n // tm
    per_seq = shift.shape[0] > 1
    midx = (lambda b, i: (b, 0, 0)) if per_seq else (lambda b, i: (0, 0, 0))
    return pl.pallas_call(
        _proj_ab_kernel,
        grid=(bsz, nseg),
        in_specs=[pl.BlockSpec((1, tm, d), lambda b, i: (b, i, 0)),
                  pl.BlockSpec((1, d), lambda b, i: (0, 0)),
                  pl.BlockSpec((1, 1, d), midx),
                  pl.BlockSpec((1, 1, d), midx),
                  pl.BlockSpec((d, AB_PAD), lambda b, i: (0, 0))],
        out_specs=[pl.BlockSpec((tm, S5_WIDTH), lambda b, i: (0, b * nseg + i)),
                   pl.BlockSpec((1, tm, 3 * GDN_WIDTH), lambda b, i: (b, i, 0)),
                   pl.BlockSpec((1, tm, GDN_WIDTH), lambda b, i: (b, i, 0)),
                   pl.BlockSpec((1, tm, 128), lambda b, i: (b, i, 0))],
        out_shape=[jax.ShapeDtypeStruct((tm, bsz * nseg * S5_WIDTH), F32),
                   jax.ShapeDtypeStruct((bsz, n, 3 * GDN_WIDTH), F32),
                   jax.ShapeDtypeStruct((bsz, n, GDN_WIDTH), F32),
                   jax.ShapeDtypeStruct((bsz, n, 128), F32)],
        compiler_params=_cparams(("parallel", "parallel")),
        name="proj_ab",
    )(x, g, shift, scale, w)


def _s5_kernel(u_ref, bm_ref, cm_ref, a_ref, h0r_ref, h0i_ref, y_ref, fr_ref, fi_ref,
               xs_ref, hr_ref, hi_ref):
    d = pl.program_id(0)
    i = pl.program_id(3)
    nt = pl.num_programs(3)
    tc = u_ref.shape[0]

    @pl.when(i == 0)
    def _():
        hr_ref[...] = h0r_ref[0]
        hi_ref[...] = h0i_ref[0]

    u2 = u_ref[...].reshape(tc * S5_ROWS, S5_TILE_CH)
    xs_ref[...] = _dot(u2, bm_ref[0, 0]).reshape(tc, S5_ROWS, 2 * S5_TILE_ST)
    ar = jnp.broadcast_to(a_ref[0, 0, 0:1, :], (S5_ROWS, S5_TILE_ST))
    ai = jnp.broadcast_to(a_ref[0, 0, 1:2, :], (S5_ROWS, S5_TILE_ST))

    def step(t, carry):
        hr, hi = carry
        tt = jnp.where(d == 0, t, tc - 1 - t)
        x = xs_ref[tt]
        nr = ar * hr - ai * hi + x[:, :S5_TILE_ST]
        ni = ar * hi + ai * hr + x[:, S5_TILE_ST:]
        xs_ref[tt] = jnp.concatenate([nr, ni], axis=-1)
        return nr, ni

    hr, hi = lax.fori_loop(0, tc, step, (hr_ref[...], hi_ref[...]), unroll=4)
    hr_ref[...] = hr
    hi_ref[...] = hi
    hs = xs_ref[...].reshape(tc * S5_ROWS, 2 * S5_TILE_ST)
    y_ref[0] = _dot(hs, cm_ref[0, 0]).reshape(tc, S5_ROWS, S5_TILE_CH)

    @pl.when(i == nt - 1)
    def _():
        fr_ref[0] = hr
        fi_ref[0] = hi


def _s5_scan(u_t, bmat, cmat, amat, h0r, h0i):
    n, rows, _ = u_t.shape
    nt = n // S5_TCHUNK
    tmap = lambda d, r, j, i: (i + d * (nt - 1 - 2 * i), r, j)
    return pl.pallas_call(
        _s5_kernel,
        grid=(2, rows // S5_ROWS, S5_TILES, nt),
        in_specs=[pl.BlockSpec((S5_TCHUNK, S5_ROWS, S5_TILE_CH), tmap),
                  pl.BlockSpec((1, 1, S5_TILE_CH, 2 * S5_TILE_ST), lambda d, r, j, i: (d, j, 0, 0)),
                  pl.BlockSpec((1, 1, 2 * S5_TILE_ST, S5_TILE_CH), lambda d, r, j, i: (d, j, 0, 0)),
                  pl.BlockSpec((1, 1, 8, S5_TILE_ST), lambda d, r, j, i: (d, j, 0, 0)),
                  pl.BlockSpec((1, S5_ROWS, S5_TILE_ST), lambda d, r, j, i: (d, r, j)),
                  pl.BlockSpec((1, S5_ROWS, S5_TILE_ST), lambda d, r, j, i: (d, r, j))],
        out_specs=[pl.BlockSpec((1, S5_TCHUNK, S5_ROWS, S5_TILE_CH),
                                lambda d, r, j, i: (d,) + tmap(d, r, j, i)),
                   pl.BlockSpec((1, S5_ROWS, S5_TILE_ST), lambda d, r, j, i: (d, r, j)),
                   pl.BlockSpec((1, S5_ROWS, S5_TILE_ST), lambda d, r, j, i: (d, r, j))],
        out_shape=[jax.ShapeDtypeStruct((2, n, rows, S5_WIDTH), F32),
                   jax.ShapeDtypeStruct((2, rows, S5_GROUPS * S5_STATE), F32),
                   jax.ShapeDtypeStruct((2, rows, S5_GROUPS * S5_STATE), F32)],
        scratch_shapes=[pltpu.VMEM((S5_TCHUNK, S5_ROWS, 2 * S5_TILE_ST), F32),
                        pltpu.VMEM((S5_ROWS, S5_TILE_ST), F32),
                        pltpu.VMEM((S5_ROWS, S5_TILE_ST), F32)],
        compiler_params=_cparams(("parallel", "parallel", "parallel", "arbitrary")),
        name="s5_scan",
    )(u_t, bmat, cmat, amat, h0r, h0i)


def _s5_params(p, j):
    lam_re, lam_im, log_dt = p['s5_lam_re'][j], p['s5_lam_im'][j], p['s5_log_dt'][j]
    dt = jnp.exp(log_dt)[..., None]
    mag = jnp.exp(lam_re * dt)
    ar, ai = mag * jnp.cos(lam_im * dt), mag * jnp.sin(lam_im * dt)
    den = lam_re * lam_re + lam_im * lam_im
    fr = ((ar - 1.0) * lam_re + ai * lam_im) / den
    fi = (ai * lam_re - (ar - 1.0) * lam_im) / den
    b_re, b_im = p['s5_b_re'][j], p['s5_b_im'][j]
    bbr = fr[..., None] * b_re - fi[..., None] * b_im
    bbi = fr[..., None] * b_im + fi[..., None] * b_re
    eye = jnp.eye(S5_TILE_GROUPS, dtype=F32)

    def in_blocks(t):
        t = t.reshape(2, S5_TILES, S5_TILE_GROUPS, S5_STATE, S5_GROUP)
        t = jnp.einsum('dtgpc,gh->dtgchp', t, eye)
        return t.reshape(2, S5_TILES, S5_TILE_CH, S5_TILE_ST)

    def out_blocks(t):
        t = t.reshape(2, S5_TILES, S5_TILE_GROUPS, S5_GROUP, S5_STATE)
        t = jnp.einsum('dtgcp,gh->dtgphc', t, eye)
        return t.reshape(2, S5_TILES, S5_TILE_ST, S5_TILE_CH)

    bmat = jnp.concatenate([in_blocks(bbr), in_blocks(bbi)], axis=-1).astype(BF16)
    cmat = jnp.concatenate([out_blocks(p['s5_c_re'][j]), -out_blocks(p['s5_c_im'][j])], axis=-2).astype(BF16)
    seg_mag = jnp.exp(lam_re * dt * S5_SEG)
    pr, pi = seg_mag * jnp.cos(lam_im * dt * S5_SEG), seg_mag * jnp.sin(lam_im * dt * S5_SEG)
    flat = lambda t: t.reshape(2, S5_TILES, 1, S5_TILE_ST)
    amat = jnp.concatenate([flat(ar), flat(ai), flat(pr), flat(pi),
                            jnp.zeros((2, S5_TILES, 4, S5_TILE_ST), F32)], axis=2)
    return bmat, cmat, amat


def _s5_glu_kernel(yf_ref, yb_ref, u_ref, d_ref, w_ref, b_ref, o_ref):
    y = yf_ref[0] + yb_ref[0] + d_ref[...] * u_ref[...]
    g = _gelu_tanh(y)
    o_ref[0] = g * _sigmoid(_dot(g, w_ref[...]) + b_ref[...])


def _s5_glu(y_t, u_t, s5_d, w_glu, b_glu, bsz, nseg):
    tm = S5_SEG
    return pl.pallas_call(
        _s5_glu_kernel,
        grid=(bsz, nseg),
        in_specs=[pl.BlockSpec((1, tm, S5_WIDTH), lambda b, i: (0, 0, b * nseg + i)),
                  pl.BlockSpec((1, tm, S5_WIDTH), lambda b, i: (1, 0, b * nseg + i)),
                  pl.BlockSpec((tm, S5_WIDTH), lambda b, i: (0, b * nseg + i)),
                  pl.BlockSpec((1, S5_WIDTH), lambda b, i: (0, 0)),
                  pl.BlockSpec((S5_WIDTH, S5_WIDTH), lambda b, i: (0, 0)),
                  pl.BlockSpec((1, S5_WIDTH), lambda b, i: (0, 0))],
        out_specs=pl.BlockSpec((1, tm, S5_WIDTH), lambda b, i: (b, i, 0)),
        out_shape=jax.ShapeDtypeStruct((bsz, nseg * tm, S5_WIDTH), F32),
        compiler_params=_cparams(("parallel", "parallel")),
        name="s5_glu",
    )(y_t, y_t, u_t, s5_d, w_glu, b_glu)


def _s5_mixer(u_t, p, j, h0r, h0i, bsz, nseg):
    rows = bsz * nseg
    bmat, cmat, amat = _s5_params(p, j)
    u3 = u_t.reshape(S5_SEG, rows, S5_WIDTH)
    if nseg == 1:
        y_t, fr, fi = _s5_scan(u3, bmat, cmat, amat, h0r, h0i)
    else:
        zero = jnp.zeros((2, bsz, nseg, S5_GROUPS * S5_STATE), F32)
        first = jnp.array([0, nseg - 1])
        seed = lambda h0: zero.at[jnp.arange(2), :, first].set(h0).reshape(2, rows, -1)
        _, fr, fi = _s5_scan(u3, bmat, cmat, amat, seed(h0r), seed(h0i))
        fr = fr.reshape(2, bsz, nseg, -1)
        fi = fi.reshape(2, bsz, nseg, -1)
        pr = amat[:, :, 2].reshape(2, 1, -1)
        pi = amat[:, :, 3].reshape(2, 1, -1)

        def chain(dr, order):
            hr, hi = (h0r[dr], h0i[dr])
            outs_r, outs_i = {}, {}
            for n_done, k in enumerate(order):
                outs_r[k], outs_i[k] = hr, hi
                if n_done == 0:
                    hr, hi = fr[dr, :, k], fi[dr, :, k]
                else:
                    hr, hi = (pr[dr] * hr - pi[dr] * hi + fr[dr, :, k],
                              pr[dr] * hi + pi[dr] * hr + fi[dr, :, k])
            st = lambda o: jnp.stack([o[k] for k in range(nseg)], axis=1)
            return st(outs_r), st(outs_i), hr, hi

        sr0, si0, er0, ei0 = chain(0, list(range(nseg)))
        sr1, si1, er1, ei1 = chain(1, list(range(nseg - 1, -1, -1)))
        start_r = jnp.stack([sr0, sr1]).reshape(2, rows, -1)
        start_i = jnp.stack([si0, si1]).reshape(2, rows, -1)
        y_t, _, _ = _s5_scan(u3, bmat, cmat, amat, start_r, start_i)
        fr = jnp.stack([er0, er1])
        fi = jnp.stack([ei0, ei1])
    ya = _s5_glu(y_t.reshape(2, S5_SEG, rows * S5_WIDTH), u_t, p['s5_d'][j][None],
                 p['s5_w_glu'][j].astype(BF16), p['s5_b_glu'][j][None], bsz, nseg)
    return ya, fr, fi


def _gdn_kernel(qkv_ref, z_ref, ab_ref, cw_ref, gp_ref, ng_ref, s0_ref, o_ref, sf_ref,
                q_s, k_s, v_s, gate_s, of_s, ob_s, st_s):
    n = qkv_ref.shape[1]
    nc = n // GDN_CHUNK
    row = lax.broadcasted_iota(jnp.int32, (n, 1), 0)

    for blk in range(3 * GDN_HEADS):
        cols = slice(blk * GDN_DK, (blk + 1) * GDN_DK)
        hs = slice((blk % GDN_HEADS) * GDN_DK, (blk % GDN_HEADS + 1) * GDN_DK)
        x = qkv_ref[0, :, cols]
        xm = jnp.where(row == 0, 0.0, pltpu.roll(x, 1, 0))
        xp = jnp.where(row == n - 1, 0.0, pltpu.roll(x, n - 1, 0))
        y = _silu(xm * cw_ref[0:1, cols] + x * cw_ref[1:2, cols] + xp * cw_ref[2:3, cols])
        if blk < GDN_HEADS:
            q_s[:, hs] = y * lax.rsqrt(jnp.sum(y * y, axis=-1, keepdims=True) + EPS) * (GDN_DK ** -0.5)
        elif blk < 2 * GDN_HEADS:
            k_s[:, hs] = y * lax.rsqrt(jnp.sum(y * y, axis=-1, keepdims=True) + EPS)
        else:
            v_s[:, hs] = y

    ab = ab_ref[0]
    beta = _sigmoid(ab)
    g = -jnp.exp(gp_ref[0:1, :]) * _softplus(ab + gp_ref[1:2, :])
    pos = row % GDN_CHUNK
    pre, suf = g, g
    sft = 1
    while sft < GDN_CHUNK:
        pre = pre + jnp.where(pos >= sft, pltpu.roll(pre, sft, 0), 0.0)
        suf = suf + jnp.where(pos < GDN_CHUNK - sft, pltpu.roll(suf, n - sft, 0), 0.0)
        sft *= 2
    gate_s[0] = beta
    gate_s[1] = pre
    gate_s[2] = suf

    st_s[...] = s0_ref[0]
    ri = lax.broadcasted_iota(jnp.int32, (GDN_CHUNK, GDN_CHUNK), 0)
    ci = lax.broadcasted_iota(jnp.int32, (GDN_CHUNK, GDN_CHUNK), 1)
    eye = (ri == ci).astype(F32)

    def chunk_step(c, carry):
        ch = []
        for dr in range(2):
            cidx = c if dr == 0 else nc - 1 - c
            rows = pl.ds(pl.multiple_of(cidx * GDN_CHUNK, GDN_CHUNK), GDN_CHUNK)
            incl = (ri >= ci) if dr == 0 else (ri <= ci)
            strict = (ri > ci) if dr == 0 else (ri < ci)
            gc_blk = gate_s[1 + dr, rows, :]
            gc_t = gc_blk.T
            beta_blk = gate_s[0, rows, :]
            for h in range(GDN_HEADS):
                hs = slice(h * GDN_DK, (h + 1) * GDN_DK)
                lane = dr * GDN_HEADS + h
                gcol = gc_blk[:, lane:lane + 1]
                ch.append(dict(dr=dr, h=h, rows=rows, hs=hs, incl=incl, strict=strict,
                               q=q_s[rows, hs], k=k_s[rows, hs], v=v_s[rows, hs],
                               bcol=beta_blk[:, 8 + lane:9 + lane], gcol=gcol,
                               grow=gc_t[lane:lane + 1, :],
                               gl=gcol[GDN_CHUNK - 1:GDN_CHUNK] if dr == 0 else gcol[0:1]))
        for t in ch:
            t['kb'] = t['k'] * t['bcol']
            t['kq'] = _dot_nt(jnp.concatenate([t['kb'], t['q']], axis=0), t['k'])
        for t in ch:
            decay = jnp.exp(jnp.where(t['incl'], t['gcol'] - t['grow'], NEG_INF))
            t['pw'] = jnp.where(t['strict'], t['kq'][:GDN_CHUNK] * decay, 0.0)
            t['amat'] = jnp.where(t['incl'], t['kq'][GDN_CHUNK:] * decay, 0.0)
            t['tm'] = eye - t['pw']
        for t in ch:
            t['pw'] = _dot3(t['pw'], t['pw'])
        for rnd in range(5):
            for t in ch:
                if rnd < 4:
                    r = _dot3(jnp.concatenate([t['pw'], t['tm']], axis=0), t['pw'])
                    t['pw'] = r[:GDN_CHUNK]
                    t['tm'] = t['tm'] + r[GDN_CHUNK:]
                else:
                    t['tm'] = t['tm'] + _dot3(t['tm'], t['pw'])
        for t in ch:
            rhs = jnp.concatenate([t['v'] * t['bcol'], t['kb'] * jnp.exp(t['gcol'])], axis=1)
            t['uw'] = _dot(t['tm'], rhs)
        for t in ch:
            t['s'] = st_s[t['dr'], t['h']]
            lhs = jnp.concatenate([t['uw'][:, GDN_DV:], t['q'] * jnp.exp(t['gcol'])], axis=0)
            t['ws'] = _dot(lhs, t['s'])
        for t in ch:
            vn = t['uw'][:, :GDN_DV] - t['ws'][:GDN_CHUNK]
            o = t['ws'][GDN_CHUNK:] + _dot(t['amat'], vn)
            st_s[t['dr'], t['h']] = (t['s'] * jnp.exp(t['gl'])
                                     + _dot_tn(t['k'] * jnp.exp(t['gl'] - t['gcol']), vn))
            if t['dr'] == 0:
                of_s[t['rows'], t['hs']] = o
            else:
                ob_s[t['rows'], t['hs']] = o
        return carry

    lax.fori_loop(0, nc, chunk_step, 0)
    sf_ref[0] = st_s[...]

    z = z_ref[0]
    for h in range(GDN_HEADS):
        hs = slice(h * GDN_DV, (h + 1) * GDN_DV)
        o = of_s[:, hs] + ob_s[:, hs]
        o = o * lax.rsqrt(jnp.mean(o * o, axis=-1, keepdims=True) + EPS) * ng_ref[...]
        o_ref[0, :, hs] = o * _silu(z[:, hs])


def _gdn_mixer(qkv, z, ab, p, j, s0):
    bsz, n, _ = qkv.shape
    gp = jnp.zeros((8, 128), F32)
    gp = gp.at[0, :8].set(p['gdn_a_log'][j].reshape(8)).at[1, :8].set(p['gdn_dt_bias'][j].reshape(8))
    cw = jnp.zeros((8, 3 * GDN_WIDTH), F32).at[:3].set(p['gdn_conv_w'][j])
    return pl.pallas_call(
        _gdn_kernel,
        grid=(bsz,),
        in_specs=[pl.BlockSpec((1, n, 3 * GDN_WIDTH), lambda b: (b, 0, 0)),
                  pl.BlockSpec((1, n, GDN_WIDTH), lambda b: (b, 0, 0)),
                  pl.BlockSpec((1, n, 128), lambda b: (b, 0, 0)),
                  pl.BlockSpec((8, 3 * GDN_WIDTH), lambda b: (0, 0)),
                  pl.BlockSpec((8, 128), lambda b: (0, 0)),
                  pl.BlockSpec((1, GDN_DV), lambda b: (0, 0)),
                  pl.BlockSpec((1, 2, GDN_HEADS, GDN_DK, GDN_DV), lambda b: (b, 0, 0, 0, 0))],
        out_specs=[pl.BlockSpec((1, n, GDN_WIDTH), lambda b: (b, 0, 0)),
                   pl.BlockSpec((1, 2, GDN_HEADS, GDN_DK, GDN_DV), lambda b: (b, 0, 0, 0, 0))],
        out_shape=[jax.ShapeDtypeStruct((bsz, n, GDN_WIDTH), F32),
                   jax.ShapeDtypeStruct((bsz, 2, GDN_HEADS, GDN_DK, GDN_DV), F32)],
        scratch_shapes=[pltpu.VMEM((n, GDN_WIDTH), F32), pltpu.VMEM((n, GDN_WIDTH), F32),
                        pltpu.VMEM((n, GDN_WIDTH), F32), pltpu.VMEM((3, n, 128), F32),
                        pltpu.VMEM((n, GDN_WIDTH), F32), pltpu.VMEM((n, GDN_WIDTH), F32),
                        pltpu.VMEM((2, GDN_HEADS, GDN_DK, GDN_DV), F32)],
        compiler_params=_cparams(("parallel",)),
        name="gdn_mixer",
    )(qkv, z, ab, cw, gp, p['gdn_norm_g'][j][None], s0)


def _proj_cd_kernel(x_ref, g_ref, sh_ref, sc_ref, w_ref, gm_ref, gain_ref, cos_ref, sin_ref,
                    qc_ref, kc_ref, vc_ref, qd_ref, kd_ref, vd_ref, *, rope):
    h = _rms_mod(x_ref[0], g_ref[...], sh_ref[0], sc_ref[0])
    y = _dot(h, w_ref[...])
    lane = lax.broadcasted_iota(jnp.int32, (1, 512), 1)
    low = (lane % 32) < 16

    def head_norm(t, gain, scale):
        w = t.shape[1]
        ms = _split_dot(t * t, gm_ref[:w, :w])
        t = t * lax.rsqrt(ms + EPS) * gain
        if rope:
            part = jnp.where(low[:, :w], pltpu.roll(t, w - 16, 1), pltpu.roll(t, 16, 1))
            t = t * cos_ref[:, :w] + part * sin_ref[:, :w]
        return t * scale if scale != 1.0 else t

    qc_ref[0] = head_norm(y[:, 0:512], gain_ref[0:1, :], ATTN_SCALE)
    kc_ref[0] = head_norm(y[:, 512:640], gain_ref[1:2, :128], 1.0)
    vc_ref[0] = y[:, 640:768]
    qd_ref[0] = head_norm(y[:, 768:1280], gain_ref[2:3, :], ATTN_SCALE)
    kd_ref[0] = head_norm(y[:, 1280:1792], gain_ref[3:4, :], 1.0)
    vd_ref[0] = y[:, 1792:2304]


def _rope_tables(n):
    rows = n // GRID_W
    row = jnp.repeat(jnp.arange(rows), GRID_W).astype(F32)
    col = jnp.tile(jnp.arange(GRID_W), rows).astype(F32)
    quarter = HEAD_DIM // 4
    inv = ROPE_THETA ** (-jnp.arange(quarter, dtype=F32) / quarter)
    ang_r = row[:, None] * inv[None, :]
    ang_c = col[:, None] * inv[None, :]
    cos = jnp.concatenate([jnp.cos(ang_r), jnp.cos(ang_r), jnp.cos(ang_c), jnp.cos(ang_c)], axis=-1)
    sin = jnp.concatenate([-jnp.sin(ang_r), jnp.sin(ang_r), -jnp.sin(ang_c), jnp.sin(ang_c)], axis=-1)
    return jnp.tile(cos, (1, 8)), jnp.tile(sin, (1, 8))


def _proj_cd(x, g, shift, scale, w, p, j, rope):
    bsz, n, d = x.shape
    tm = 256
    per_seq = shift.shape[0] > 1
    midx = (lambda b, i: (b, 0, 0)) if per_seq else (lambda b, i: (0, 0, 0))
    lane = jnp.arange(512)
    gmat = ((lane[:, None] // HEAD_DIM) == (lane[None, :] // HEAD_DIM)).astype(F32) / HEAD_DIM
    gains = jnp.zeros((8, 512), F32)
    gains = gains.at[0].set(jnp.tile(p['c_qn'][j], 8)).at[1].set(jnp.tile(p['c_kn'][j], 8))
    gains = gains.at[2].set(jnp.tile(p['d_qn'][j], 8)).at[3].set(jnp.tile(p['d_kn'][j], 8))
    if rope:
        cos, sin = _rope_tables(n)
    else:
        cos, sin = jnp.ones((tm, 512), F32), jnp.zeros((tm, 512), F32)
    tidx = (lambda b, i: (i, 0)) if rope else (lambda b, i: (0, 0))
    blk = lambda w_: pl.BlockSpec((1, tm, w_), lambda b, i: (b, i, 0))
    return pl.pallas_call(
        functools.partial(_proj_cd_kernel, rope=rope),
        grid=(bsz, n // tm),
        in_specs=[blk(d),
                  pl.BlockSpec((1, d), lambda b, i: (0, 0)),
                  pl.BlockSpec((1, 1, d), midx),
                  pl.BlockSpec((1, 1, d), midx),
                  pl.BlockSpec((d, CD_IN), lambda b, i: (0, 0)),
                  pl.BlockSpec((512, 512), lambda b, i: (0, 0)),
                  pl.BlockSpec((8, 512), lambda b, i: (0, 0)),
                  pl.BlockSpec((tm, 512), tidx),
                  pl.BlockSpec((tm, 512), tidx)],
        out_specs=[blk(512), blk(128), blk(128), blk(512), blk(512), blk(512)],
        out_shape=[jax.ShapeDtypeStruct((bsz, n, w_), F32) for w_ in (512, 128, 128, 512, 512, 512)],
        compiler_params=_cparams(("parallel", "parallel")),
        name="proj_cd",
    )(x, g, shift, scale, w, gmat.astype(BF16), gains, cos, sin)


def _softmax_parts(scores, extra=None):
    m = scores[0].max(axis=-1, keepdims=True)
    for s in scores[1:]:
        m = jnp.maximum(m, s.max(axis=-1, keepdims=True))
    if extra is not None:
        m = jnp.maximum(m, extra)
    ps = [jnp.exp(s - m) for s in scores]
    den = ps[0].sum(axis=-1, keepdims=True)
    for pp in ps[1:]:
        den = den + pp.sum(axis=-1, keepdims=True)
    if extra is not None:
        den = den + jnp.exp(extra - m)
    return ps, den


def _attn_kernel(qc_ref, kc_ref, vc_ref, qd_ref, kd_ref, vd_ref, *rest, windowed, n_ctx, lam_scale):
    if n_ctx:
        ck_ref, cv_ref, dk_ref, dv_ref, misc_ref, sub_ref, o_ref = rest
    else:
        misc_ref, sub_ref, o_ref = rest
    tq = qc_ref.shape[1]
    n = kc_ref.shape[1]
    start = pl.program_id(1) * tq
    if windowed:
        span = tq + 2 * WINDOW
        k0 = pl.multiple_of(jnp.clip(start - WINDOW, 0, n - span), 128)
        krows = pl.ds(k0, span)
        qpos = start + lax.broadcasted_iota(jnp.int32, (tq, span), 0)
        kpos = k0 + lax.broadcasted_iota(jnp.int32, (tq, span), 1)
        ok = jnp.abs(qpos - kpos) <= WINDOW
    else:
        krows = pl.ds(0, n)

    qc = qc_ref[0].astype(BF16)
    qd = qd_ref[0].astype(BF16)
    kc = kc_ref[0, krows, :].astype(BF16)
    vc = vc_ref[0, krows, :].astype(BF16)
    kd = kd_ref[0].astype(BF16)
    c_scores = []
    for h in range(C_HEADS):
        gsl = slice((h // C_GROUP) * HEAD_DIM, (h // C_GROUP + 1) * HEAD_DIM)
        q = qc[:, h * HEAD_DIM:(h + 1) * HEAD_DIM]
        sc = [_dot_nt(q, kc[:, gsl])]
        if n_ctx:
            sc.append(_dot_nt(q, ck_ref[0, :, gsl]))
        c_scores.append(sc)
    d_scores = []
    for hc in range(2 * D_HEADS):
        sl = slice(hc * HEAD_DIM, (hc + 1) * HEAD_DIM)
        q = qd[:, sl]
        sc = [_dot_nt(q, kd[:, sl])]
        if n_ctx:
            sc.append(_dot_nt(q, dk_ref[0, :, sl]))
        d_scores.append(sc)

    c_probs = []
    for h in range(C_HEADS):
        sc = c_scores[h]
        if windowed:
            sc[0] = jnp.where(ok, sc[0], NEG_INF)
        c_probs.append(_softmax_parts(sc, misc_ref[0:1, h:h + 1]))
    lam = misc_ref[1:2, 0:1]
    d_att = []
    for h in range(D_HEADS):
        ps0, den0 = _softmax_parts(d_scores[2 * h])
        ps1, den1 = _softmax_parts(d_scores[2 * h + 1])
        d_att.append([p0 / den0 - lam * (p1 / den1) for p0, p1 in zip(ps0, ps1)])

    for h in range(C_HEADS):
        gsl = slice((h // C_GROUP) * HEAD_DIM, (h // C_GROUP + 1) * HEAD_DIM)
        ps, den = c_probs[h]
        o = _dot(ps[0], vc[:, gsl])
        if n_ctx:
            o = o + _dot(ps[1], cv_ref[0, :, gsl])
        o_ref[0, :, h * HEAD_DIM:(h + 1) * HEAD_DIM] = o / den
    for h in range(D_HEADS):
        vsl = slice(h * D_VDIM, (h + 1) * D_VDIM)
        o = _dot(d_att[h][0], vd_ref[0, :, vsl])
        if n_ctx:
            o = o + _dot(d_att[h][1], dv_ref[0, :, vsl])
        o = o * lax.rsqrt(jnp.mean(o * o, axis=-1, keepdims=True) + EPS) * sub_ref[...] * lam_scale
        o_ref[0, :, 512 + h * D_VDIM:512 + (h + 1) * D_VDIM] = o


def _attention(qc, kc, vc, qd, kd, vd, caches, misc, subln, lam_init, tq, windowed):
    bsz, n, _ = qc.shape
    qblk = lambda w_: pl.BlockSpec((1, tq, w_), lambda b, i: (b, i, 0))
    kblk = lambda rows, w_: pl.BlockSpec((1, rows, w_), lambda b, i: (b, 0, 0))
    in_specs = [qblk(512), kblk(n, 128), kblk(n, 128), qblk(512), kblk(n, 512), kblk(n, 512)]
    args = [qc, kc, vc, qd, kd, vd]
    n_ctx = 0
    if caches is not None:
        n_ctx = caches[0].shape[1]
        in_specs += [kblk(n_ctx, 128), kblk(n_ctx, 128), kblk(n_ctx, 512), kblk(n_ctx, 512)]
        args += list(caches)
    in_specs += [pl.BlockSpec((8, 128), lambda b, i: (0, 0)), pl.BlockSpec((1, D_VDIM), lambda b, i: (0, 0))]
    args += [misc, subln]
    return pl.pallas_call(
        functools.partial(_attn_kernel, windowed=windowed, n_ctx=n_ctx, lam_scale=1.0 - lam_init),
        grid=(bsz, n // tq),
        in_specs=in_specs,
        out_specs=pl.BlockSpec((1, tq, 1024), lambda b, i: (b, i, 0)),
        out_shape=jax.ShapeDtypeStruct((bsz, n, 1024), F32),
        compiler_params=_cparams(("parallel", "parallel")),
        name="attn_win" if windowed else "attn_ctx",
    )(*args)


def _post_kernel(x_ref, xp_ref, xn_ref, mix_ref, mp_ref, mn_ref, g1_ref, sh_ref, sc_ref, g2_ref, ng_ref,
                 wo_ref, wup_ref, cw_ref, wdn_ref, o_ref, acc_ref, *, seq_len):
    rows = x_ref.shape[0]
    ext = rows + 2 * POST_HALO
    xe = jnp.concatenate([xp_ref[...], x_ref[...], xn_ref[...]], axis=0)
    me = jnp.concatenate([mp_ref[...], mix_ref[...], mn_ref[...]], axis=0)
    x1 = xe + g1_ref[0] * _dot(me, wo_ref[...])
    h = _rms_mod(x1, ng_ref[...], sh_ref[0], sc_ref[0]).astype(BF16)
    x1 = x1[POST_HALO:POST_HALO + rows]
    row0 = pl.program_id(0) * rows - POST_HALO
    pos = (row0 + lax.broadcasted_iota(jnp.int32, (ext, 1), 0)) % seq_len
    first = pos == 0
    last = pos == seq_len - 1
    acc_ref[...] = jnp.zeros_like(acc_ref)

    def up_proj(c):
        return jnp.dot(h, wup_ref[c], preferred_element_type=F32)

    def down(c, up):
        a = up[:, :FF_CHUNK]
        b = up[:, FF_CHUNK:]
        cw = cw_ref[c]
        am = jnp.where(first, 0.0, pltpu.roll(a, 1, 0))
        ap = jnp.where(last, 0.0, pltpu.roll(a, ext - 1, 0))
        a = am * cw[0:1] + a * cw[1:2] + ap * cw[2:3] + cw[3:4]
        act = (_silu(a) * b)[POST_HALO:POST_HALO + rows]
        acc_ref[...] += _dot(act, wdn_ref[c])

    def chunk(c, up):
        up_next = up_proj(c + 1)
        down(c, up)
        return up_next

    up_last = lax.fori_loop(0, FF_CHUNKS - 1, chunk, up_proj(0))
    down(FF_CHUNKS - 1, up_last)
    o_ref[...] = x1 + g2_ref[0] * acc_ref[...]


def _post(x, mix, g1, sh2, sc2, g2, norm_g, w_out, wup, cw, wdn):
    bsz, seq_len, d = x.shape
    rows = POST_ROWS
    total = bsz * seq_len
    nhalo = total // POST_HALO
    per_seq = g1.shape[0] > 1
    midx = (lambda i: ((i * rows) // seq_len, 0, 0)) if per_seq else (lambda i: (0, 0, 0))
    mspec = pl.BlockSpec((1, 1, d), midx)
    main = pl.BlockSpec((rows, d), lambda i: (i, 0))
    prev = pl.BlockSpec((POST_HALO, d), lambda i: (jnp.maximum(i * (rows // POST_HALO) - 1, 0), 0))
    nxt = pl.BlockSpec((POST_HALO, d), lambda i: (jnp.minimum((i + 1) * (rows // POST_HALO), nhalo - 1), 0))
    whole = lambda shape: pl.BlockSpec(shape, lambda i: (0,) * len(shape), pipeline_mode=pl.Buffered(1))
    x2 = x.reshape(total, d)
    m2 = mix.reshape(total, d)
    out = pl.pallas_call(
        functools.partial(_post_kernel, seq_len=seq_len),
        grid=(total // rows,),
        in_specs=[main, prev, nxt, main, prev, nxt,
                  mspec, mspec, mspec, mspec,
                  pl.BlockSpec((1, d), lambda i: (0, 0)),
                  whole((d, d)),
                  whole((FF_CHUNKS, d, 2 * FF_CHUNK)),
                  whole((FF_CHUNKS, 8, FF_CHUNK)),
                  whole((FF_CHUNKS, FF_CHUNK, d))],
        out_specs=main,
        out_shape=jax.ShapeDtypeStruct((total, d), F32),
        scratch_shapes=[pltpu.VMEM((rows, d), F32)],
        compiler_params=_cparams(("parallel",)),
        name="post_ffn",
    )(x2, x2, x2, m2, m2, m2, g1, sh2, sc2, g2, norm_g, w_out, wup, cw, wdn)
    return out.reshape(bsz, seq_len, d)


def _ffn_weights(p, l):
    up = p['ffn_up'][l]
    a = up[:, :D_FF].reshape(D_MODEL, FF_CHUNKS, FF_CHUNK)
    b = up[:, D_FF:].reshape(D_MODEL, FF_CHUNKS, FF_CHUNK)
    wup = jnp.concatenate([a, b], axis=-1).transpose(1, 0, 2).astype(BF16)
    cw = jnp.zeros((8, D_FF), F32).at[:3].set(p['ffn_conv_w'][l]).at[3].set(p['ffn_conv_b'][l])
    cw = cw.reshape(8, FF_CHUNKS, FF_CHUNK).transpose(1, 0, 2)
    wdn = p['ffn_down'][l].reshape(FF_CHUNKS, FF_CHUNK, D_MODEL).astype(BF16)
    return wup, cw, wdn


def _lambda_init(layer):
    return 0.8 - 0.6 * math.exp(-0.3 * layer)


def _trunk(x, mods, p, states, caches):
    bsz, n, d = x.shape
    nseg = n // S5_SEG
    depth = p['w_mod'].shape[0]
    news = {k: [] for k in ('s5r', 's5i', 'gdn', 'ck', 'cv', 'dk', 'dv')}
    for l in range(depth):
        j = l // 2
        sh1, sc1, g1, sh2, sc2, g2 = mods[l]
        ng1 = p['norm1_g'][l][None]
        if l % 2 == 0:
            w_in = p['w_in_ab'][j]
            w_pad = jnp.concatenate([w_in[:, :AB_MAIN], w_in[:, AB_MAIN:],
                                     jnp.zeros((d, AB_PAD - w_in.shape[1]), F32)], axis=1).astype(BF16)
            u_t, qkv, z, ab = _proj_ab(x, ng1, sh1, sc1, w_pad)
            if states is None:
                h0r = jnp.zeros((2, bsz, S5_GROUPS * S5_STATE), F32)
                h0i = h0r
                s0 = jnp.zeros((bsz, 2, GDN_HEADS, GDN_DK, GDN_DV), F32)
            else:
                h0r = states[0][:, j].reshape(bsz, 2, -1).transpose(1, 0, 2)
                h0i = states[1][:, j].reshape(bsz, 2, -1).transpose(1, 0, 2)
                s0 = states[2][:, j]
            ya, fr, fi = _s5_mixer(u_t, p, j, h0r, h0i, bsz, nseg)
            yb, sg = _gdn_mixer(qkv, z, ab, p, j, s0)
            mix = jnp.concatenate([ya, yb], axis=-1)
            w_out = p['w_out_ab'][j]
            news['s5r'].append(fr.transpose(1, 0, 2).reshape(bsz, 2, S5_GROUPS, S5_STATE))
            news['s5i'].append(fi.transpose(1, 0, 2).reshape(bsz, 2, S5_GROUPS, S5_STATE))
            news['gdn'].append(sg)
        else:
            lam_init = _lambda_init(l)
            f = lambda name: p[name][j]
            lam = (jnp.exp(jnp.sum(f('d_lq1') * f('d_lk1'))) - jnp.exp(jnp.sum(f('d_lq2') * f('d_lk2')))
                   + lam_init)
            misc = jnp.zeros((8, 128), F32).at[0, :C_HEADS].set(p['c_sink'][j]).at[1, :].set(lam)
            rope = caches is not None
            qc, kc, vc, qd, kd, vd = _proj_cd(x, ng1, sh1, sc1, p['w_in_cd'][j].astype(BF16), p, j, rope)
            if caches is None:
                mix = _attention(qc, kc, vc, qd, kd, vd, None, misc, p['d_subln'][j][None],
                                 lam_init, n, False)
            else:
                n_ctx = caches[0].shape[2]
                cc = (caches[0][:, j].reshape(bsz, n_ctx, 128), caches[1][:, j].reshape(bsz, n_ctx, 128),
                      caches[2][:, j].reshape(bsz, n_ctx, 512), caches[3][:, j].reshape(bsz, n_ctx, 512))
                mix = _attention(qc, kc, vc, qd, kd, vd, cc, misc, p['d_subln'][j][None],
                                 lam_init, Q_BLOCK, True)
            w_out = p['w_out_cd'][j]
            news['ck'].append(kc.reshape(bsz, n, C_KV_HEADS, HEAD_DIM))
            news['cv'].append(vc.reshape(bsz, n, C_KV_HEADS, HEAD_DIM))
            news['dk'].append(kd.reshape(bsz, n, D_HEADS, 2, HEAD_DIM))
            news['dv'].append(vd.reshape(bsz, n, D_HEADS, D_VDIM))
        wup, cw, wdn = _ffn_weights(p, l)
        x = _post(x, mix, g1, sh2, sc2, g2, p['norm2_g'][l][None], w_out.astype(BF16), wup, cw, wdn)
    return x, news


def kernel(x_prompt, x_sample, c, state_s5_re, state_s5_im, state_gdn, cache_c_k, cache_c_v, cache_d_k, cache_d_v, c_ctx, w_mod, b_mod, norm1_g, norm2_g, w_in_ab, w_out_ab, s5_lam_re, s5_lam_im, s5_log_dt, s5_b_re, s5_b_im, s5_c_re, s5_c_im, s5_d, s5_w_glu, s5_b_glu, gdn_conv_w, gdn_a_log, gdn_dt_bias, gdn_norm_g, w_in_cd, w_out_cd, c_qn, c_kn, c_sink, d_qn, d_kn, d_lq1, d_lk1, d_lq2, d_lk2, d_subln, ffn_up, ffn_conv_w, ffn_conv_b, ffn_down):
    p = dict(w_mod=w_mod, b_mod=b_mod, norm1_g=norm1_g, norm2_g=norm2_g, w_in_ab=w_in_ab, w_out_ab=w_out_ab,
             s5_lam_re=s5_lam_re, s5_lam_im=s5_lam_im, s5_log_dt=s5_log_dt, s5_b_re=s5_b_re, s5_b_im=s5_b_im,
             s5_c_re=s5_c_re, s5_c_im=s5_c_im, s5_d=s5_d, s5_w_glu=s5_w_glu, s5_b_glu=s5_b_glu,
             gdn_conv_w=gdn_conv_w, gdn_a_log=gdn_a_log, gdn_dt_bias=gdn_dt_bias, gdn_norm_g=gdn_norm_g,
             w_in_cd=w_in_cd, w_out_cd=w_out_cd, c_qn=c_qn, c_kn=c_kn, c_sink=c_sink, d_qn=d_qn, d_kn=d_kn,
             d_lq1=d_lq1, d_lk1=d_lk1, d_lq2=d_lq2, d_lk2=d_lk2, d_subln=d_subln,
             ffn_up=ffn_up, ffn_conv_w=ffn_conv_w, ffn_conv_b=ffn_conv_b, ffn_down=ffn_down)
    depth = w_mod.shape[0]
    n_dec = c.shape[0]
    mod = _modulation(jnp.concatenate([c_ctx[None], c], axis=0), w_mod, b_mod)
    split6 = lambda m: [m[:, None, k * D_MODEL:(k + 1) * D_MODEL] for k in range(6)]
    mods_ctx = [split6(mod[l, 0:1]) for l in range(depth)]
    mods_dec = [split6(mod[l, 1:1 + n_dec]) for l in range(depth)]

    y_prompt, nw = _trunk(x_prompt, mods_ctx, p, None, None)
    y_sample, _ = _trunk(x_sample, mods_dec, p, (state_s5_re, state_s5_im, state_gdn),
                         (cache_c_k, cache_c_v, cache_d_k, cache_d_v))
    st = lambda name: jnp.stack(nw[name], axis=1)
    return (y_prompt, y_sample, st('s5r'), st('s5i'), st('gdn'), st('ck'), st('cv'), st('dk'), st('dv'))
```

```python
import functools
import math

import jax
import jax.numpy as jnp
from jax import lax
from jax.experimental import pallas as pl
from jax.experimental.pallas import tpu as pltpu

F32 = jnp.float32
BF16 = jnp.bfloat16

D_MODEL = 1024
GRID_W = 64
EPS = 1e-6
NEG_INF = -1e30

S5_WIDTH = 512
S5_GROUP = 16
S5_GROUPS = 32
S5_STATE = 64
S5_TILE_GROUPS = 8
S5_TILE_CH = S5_TILE_GROUPS * S5_GROUP
S5_TILE_ST = S5_TILE_GROUPS * S5_STATE
S5_TILES = S5_GROUPS // S5_TILE_GROUPS
S5_SEG = 256
S5_TCHUNK = 128
S5_ROWS = 8

GDN_DK = 128
GDN_DV = 128
GDN_HEADS = 4
GDN_WIDTH = 512
GDN_CHUNK = 64

HEAD_DIM = 64
C_HEADS = 8
C_KV_HEADS = 2
C_GROUP = 4
WINDOW = 128
Q_BLOCK = 128
D_HEADS = 4
D_VDIM = 128
ATTN_SCALE = HEAD_DIM ** -0.5
ROPE_THETA = 10000.0

D_FF = 2816
FF_CHUNK = 256
FF_CHUNKS = D_FF // FF_CHUNK
POST_ROWS = 512
POST_HALO = 8

AB_MAIN = S5_WIDTH + 4 * GDN_WIDTH
AB_PAD = AB_MAIN + 128
CD_IN = 2304

VMEM_LIMIT = 56 * 1024 * 1024


def _cparams(sem):
    return pltpu.CompilerParams(dimension_semantics=sem, vmem_limit_bytes=VMEM_LIMIT)


def _sigmoid(x):
    return 1.0 / (1.0 + jnp.exp(-x))


def _silu(x):
    return x * _sigmoid(x)


def _softplus(x):
    return jnp.maximum(x, 0.0) + jnp.log(1.0 + jnp.exp(-jnp.abs(x)))


def _gelu_tanh(x):
    return 0.5 * x * (1.0 + jnp.tanh(math.sqrt(2.0 / math.pi) * (x + 0.044715 * (x * x * x))))


def _rms_mod(x, g, shift, scale):
    y = x * lax.rsqrt(jnp.mean(x * x, axis=-1, keepdims=True) + EPS)
    return (y * g) * (1.0 + scale) + shift


def _dot(a, b):
    return jnp.dot(a.astype(BF16), b.astype(BF16), preferred_element_type=F32)


def _dot_nt(a, b):
    return lax.dot_general(a.astype(BF16), b.astype(BF16), (((1,), (1,)), ((), ())),
                           preferred_element_type=F32)


def _dot_tn(a, b):
    return lax.dot_general(a.astype(BF16), b.astype(BF16), (((0,), (0,)), ((), ())),
                           preferred_element_type=F32)


def _hi_lo(a):
    hi = a.astype(BF16)
    return hi, (a - hi.astype(F32)).astype(BF16)


def _dot3(a, b):
    a_hi, a_lo = _hi_lo(a)
    b_hi, b_lo = _hi_lo(b)
    mm = functools.partial(jnp.dot, preferred_element_type=F32)
    return mm(a_hi, b_hi) + (mm(a_lo, b_hi) + mm(a_hi, b_lo))


def _split_dot(a, b_bf16):
    hi = a.astype(BF16)
    lo = (a - hi.astype(F32)).astype(BF16)
    return (jnp.dot(hi, b_bf16, preferred_element_type=F32)
            + jnp.dot(lo, b_bf16, preferred_element_type=F32))


def _mod_kernel(ct_ref, w_ref, b_ref, o_ref, *, n_rows):
    c = ct_ref[...]
    s = _silu(c)
    w = w_ref[0]
    o_ref[0] = jnp.zeros(o_ref.shape[1:], F32)
    for m in range(n_rows):
        o_ref[0, m:m + 1, :] = jnp.sum(w * s[:, m:m + 1], axis=0, keepdims=True) + b_ref[0]


def _modulation(cvecs, w_mod, b_mod):
    n, d = cvecs.shape
    depth, _, n_out = w_mod.shape
    tn = 512
    ct = jnp.zeros((d, 8), F32).at[:, :n].set(cvecs.T)
    return pl.pallas_call(
        functools.partial(_mod_kernel, n_rows=n),
        grid=(depth, n_out // tn),
        in_specs=[pl.BlockSpec((d, 8), lambda l, j: (0, 0)),
                  pl.BlockSpec((1, d, tn), lambda l, j: (l, 0, j)),
                  pl.BlockSpec((1, 1, tn), lambda l, j: (l, 0, j))],
        out_specs=pl.BlockSpec((1, 8, tn), lambda l, j: (l, 0, j)),
        out_shape=jax.ShapeDtypeStruct((depth, 8, n_out), F32),
        compiler_params=_cparams(("parallel", "parallel")),
        name="adaln_mod",
    )(ct, w_mod, b_mod.reshape(depth, 1, n_out))


def _proj_ab_kernel(x_ref, g_ref, sh_ref, sc_ref, w_ref, wg_ref, u_ref, qkv_ref, z_ref, ab_ref):
    h = _rms_mod(x_ref[0], g_ref[...], sh_ref[0], sc_ref[0]).astype(BF16)
    y = jnp.dot(h, w_ref[...], preferred_element_type=F32)
    u_ref[...] = y[:, :S5_WIDTH]
    qkv_ref[0] = y[:, S5_WIDTH:S5_WIDTH + 3 * GDN_WIDTH]
    z_ref[0] = y[:, S5_WIDTH + 3 * GDN_WIDTH:AB_MAIN]
    ab_ref[0] = jnp.dot(h, wg_ref[...], preferred_element_type=F32)


def _proj_ab(x, g, shift, scale, w, w_gate):
    bsz, n, d = x.shape
    tm = S5_SEG
    nseg = n // tm
    per_seq = shift.shape[0] > 1
    midx = (lambda b, i: (b, 0, 0)) if per_seq else (lambda b, i: (0, 0, 0))
    return pl.pallas_call(
        _proj_ab_kernel,
        grid=(bsz, nseg),
        in_specs=[pl.BlockSpec((1, tm, d), lambda b, i: (b, i, 0)),
                  pl.BlockSpec((1, d), lambda b, i: (0, 0)),
                  pl.BlockSpec((1, 1, d), midx),
                  pl.BlockSpec((1, 1, d), midx),
                  pl.BlockSpec((d, AB_MAIN), lambda b, i: (0, 0)),
                  pl.BlockSpec((d, 128), lambda b, i: (0, 0))],
        out_specs=[pl.BlockSpec((tm, S5_WIDTH), lambda b, i: (0, b * nseg + i)),
                   pl.BlockSpec((1, tm, 3 * GDN_WIDTH), lambda b, i: (b, i, 0)),
                   pl.BlockSpec((1, tm, GDN_WIDTH), lambda b, i: (b, i, 0)),
                   pl.BlockSpec((1, tm, 128), lambda b, i: (b, i, 0))],
        out_shape=[jax.ShapeDtypeStruct((tm, bsz * nseg * S5_WIDTH), F32),
                   jax.ShapeDtypeStruct((bsz, n, 3 * GDN_WIDTH), F32),
                   jax.ShapeDtypeStruct((bsz, n, GDN_WIDTH), F32),
                   jax.ShapeDtypeStruct((bsz, n, 128), F32)],
        compiler_params=_cparams(("parallel", "parallel")),
        name="proj_ab",
    )(x, g, shift, scale, w, w_gate)


def _s5_kernel(u_ref, bm_ref, cm_ref, a_ref, h0r_ref, h0i_ref, y_ref, fr_ref, fi_ref,
               xs_ref, hr_ref, hi_ref):
    d = pl.program_id(0)
    i = pl.program_id(3)
    nt = pl.num_programs(3)
    tc = u_ref.shape[0]

    @pl.when(i == 0)
    def _():
        hr_ref[...] = h0r_ref[0]
        hi_ref[...] = h0i_ref[0]

    u2 = u_ref[...].reshape(tc * S5_ROWS, S5_TILE_CH)
    xs_ref[...] = _dot(u2, bm_ref[0, 0]).reshape(tc, S5_ROWS, 2 * S5_TILE_ST)
    ar = jnp.broadcast_to(a_ref[0, 0, 0:1, :], (S5_ROWS, S5_TILE_ST))
    ai = jnp.broadcast_to(a_ref[0, 0, 1:2, :], (S5_ROWS, S5_TILE_ST))

    def step(t, carry):
        hr, hi = carry
        tt = jnp.where(d == 0, t, tc - 1 - t)
        x = xs_ref[tt]
        nr = ar * hr - ai * hi + x[:, :S5_TILE_ST]
        ni = ar * hi + ai * hr + x[:, S5_TILE_ST:]
        xs_ref[tt] = jnp.concatenate([nr, ni], axis=-1)
        return nr, ni

    hr, hi = lax.fori_loop(0, tc, step, (hr_ref[...], hi_ref[...]), unroll=4)
    hr_ref[...] = hr
    hi_ref[...] = hi
    hs = xs_ref[...].reshape(tc * S5_ROWS, 2 * S5_TILE_ST)
    y_ref[0] = _dot(hs, cm_ref[0, 0]).reshape(tc, S5_ROWS, S5_TILE_CH)

    @pl.when(i == nt - 1)
    def _():
        fr_ref[0] = hr
        fi_ref[0] = hi


def _s5_scan(u_t, bmat, cmat, amat, h0r, h0i):
    n, rows, _ = u_t.shape
    nt = n // S5_TCHUNK
    tmap = lambda d, r, j, i: (i + d * (nt - 1 - 2 * i), r, j)
    return pl.pallas_call(
        _s5_kernel,
        grid=(2, rows // S5_ROWS, S5_TILES, nt),
        in_specs=[pl.BlockSpec((S5_TCHUNK, S5_ROWS, S5_TILE_CH), tmap),
                  pl.BlockSpec((1, 1, S5_TILE_CH, 2 * S5_TILE_ST), lambda d, r, j, i: (d, j, 0, 0)),
                  pl.BlockSpec((1, 1, 2 * S5_TILE_ST, S5_TILE_CH), lambda d, r, j, i: (d, j, 0, 0)),
                  pl.BlockSpec((1, 1, 8, S5_TILE_ST), lambda d, r, j, i: (d, j, 0, 0)),
                  pl.BlockSpec((1, S5_ROWS, S5_TILE_ST), lambda d, r, j, i: (d, r, j)),
                  pl.BlockSpec((1, S5_ROWS, S5_TILE_ST), lambda d, r, j, i: (d, r, j))],
        out_specs=[pl.BlockSpec((1, S5_TCHUNK, S5_ROWS, S5_TILE_CH),
                                lambda d, r, j, i: (d,) + tmap(d, r, j, i)),
                   pl.BlockSpec((1, S5_ROWS, S5_TILE_ST), lambda d, r, j, i: (d, r, j)),
                   pl.BlockSpec((1, S5_ROWS, S5_TILE_ST), lambda d, r, j, i: (d, r, j))],
        out_shape=[jax.ShapeDtypeStruct((2, n, rows, S5_WIDTH), F32),
                   jax.ShapeDtypeStruct((2, rows, S5_GROUPS * S5_STATE), F32),
                   jax.ShapeDtypeStruct((2, rows, S5_GROUPS * S5_STATE), F32)],
        scratch_shapes=[pltpu.VMEM((S5_TCHUNK, S5_ROWS, 2 * S5_TILE_ST), F32),
                        pltpu.VMEM((S5_ROWS, S5_TILE_ST), F32),
                        pltpu.VMEM((S5_ROWS, S5_TILE_ST), F32)],
        compiler_params=_cparams(("parallel", "parallel", "parallel", "arbitrary")),
        name="s5_scan",
    )(u_t, bmat, cmat, amat, h0r, h0i)


def _s5_params(p, j):
    lam_re, lam_im, log_dt = p['s5_lam_re'][j], p['s5_lam_im'][j], p['s5_log_dt'][j]
    dt = jnp.exp(log_dt)[..., None]
    mag = jnp.exp(lam_re * dt)
    ar, ai = mag * jnp.cos(lam_im * dt), mag * jnp.sin(lam_im * dt)
    den = lam_re * lam_re + lam_im * lam_im
    fr = ((ar - 1.0) * lam_re + ai * lam_im) / den
    fi = (ai * lam_re - (ar - 1.0) * lam_im) / den
    b_re, b_im = p['s5_b_re'][j], p['s5_b_im'][j]
    bbr = fr[..., None] * b_re - fi[..., None] * b_im
    bbi = fr[..., None] * b_im + fi[..., None] * b_re
    eye = jnp.eye(S5_TILE_GROUPS, dtype=F32)

    def in_blocks(t):
        t = t.reshape(2, S5_TILES, S5_TILE_GROUPS, S5_STATE, S5_GROUP)
        t = jnp.einsum('dtgpc,gh->dtgchp', t, eye)
        return t.reshape(2, S5_TILES, S5_TILE_CH, S5_TILE_ST)

    def out_blocks(t):
        t = t.reshape(2, S5_TILES, S5_TILE_GROUPS, S5_GROUP, S5_STATE)
        t = jnp.einsum('dtgcp,gh->dtgphc', t, eye)
        return t.reshape(2, S5_TILES, S5_TILE_ST, S5_TILE_CH)

    bmat = jnp.concatenate([in_blocks(bbr), in_blocks(bbi)], axis=-1).astype(BF16)
    cmat = jnp.concatenate([out_blocks(p['s5_c_re'][j]), -out_blocks(p['s5_c_im'][j])], axis=-2).astype(BF16)
    seg_mag = jnp.exp(lam_re * dt * S5_SEG)
    pr, pi = seg_mag * jnp.cos(lam_im * dt * S5_SEG), seg_mag * jnp.sin(lam_im * dt * S5_SEG)
    flat = lambda t: t.reshape(2, S5_TILES, 1, S5_TILE_ST)
    amat = jnp.concatenate([flat(ar), flat(ai), flat(pr), flat(pi),
                            jnp.zeros((2, S5_TILES, 4, S5_TILE_ST), F32)], axis=2)
    return bmat, cmat, amat


def _s5_glu_kernel(yf_ref, yb_ref, u_ref, d_ref, w_ref, b_ref, o_ref):
    y = yf_ref[0] + yb_ref[0] + d_ref[...] * u_ref[...]
    g = _gelu_tanh(y)
    o_ref[0] = g * _sigmoid(_dot(g, w_ref[...]) + b_ref[...])


def _s5_glu(y_t, u_t, s5_d, w_glu, b_glu, bsz, nseg):
    tm = S5_SEG
    return pl.pallas_call(
        _s5_glu_kernel,
        grid=(bsz, nseg),
        in_specs=[pl.BlockSpec((1, tm, S5_WIDTH), lambda b, i: (0, 0, b * nseg + i)),
                  pl.BlockSpec((1, tm, S5_WIDTH), lambda b, i: (1, 0, b * nseg + i)),
                  pl.BlockSpec((tm, S5_WIDTH), lambda b, i: (0, b * nseg + i)),
                  pl.BlockSpec((1, S5_WIDTH), lambda b, i: (0, 0)),
                  pl.BlockSpec((S5_WIDTH, S5_WIDTH), lambda b, i: (0, 0)),
                  pl.BlockSpec((1, S5_WIDTH), lambda b, i: (0, 0))],
        out_specs=pl.BlockSpec((1, tm, S5_WIDTH), lambda b, i: (b, i, 0)),
        out_shape=jax.ShapeDtypeStruct((bsz, nseg * tm, S5_WIDTH), F32),
        compiler_params=_cparams(("parallel", "parallel")),
        name="s5_glu",
    )(y_t, y_t, u_t, s5_d, w_glu, b_glu)


def _s5_mixer(u_t, p, j, h0r, h0i, bsz, nseg):
    rows = bsz * nseg
    bmat, cmat, amat = _s5_params(p, j)
    u3 = u_t.reshape(S5_SEG, rows, S5_WIDTH)
    if nseg == 1:
        y_t, fr, fi = _s5_scan(u3, bmat, cmat, amat, h0r, h0i)
    else:
        zero = jnp.zeros((2, bsz, nseg, S5_GROUPS * S5_STATE), F32)
        first = jnp.array([0, nseg - 1])
        seed = lambda h0: zero.at[jnp.arange(2), :, first].set(h0).reshape(2, rows, -1)
        _, fr, fi = _s5_scan(u3, bmat, cmat, amat, seed(h0r), seed(h0i))
        fr = fr.reshape(2, bsz, nseg, -1)
        fi = fi.reshape(2, bsz, nseg, -1)
        pr = amat[:, :, 2].reshape(2, 1, -1)
        pi = amat[:, :, 3].reshape(2, 1, -1)

        def chain(dr, order):
            hr, hi = (h0r[dr], h0i[dr])
            outs_r, outs_i = {}, {}
            for n_done, k in enumerate(order):
                outs_r[k], outs_i[k] = hr, hi
                if n_done == 0:
                    hr, hi = fr[dr, :, k], fi[dr, :, k]
                else:
                    hr, hi = (pr[dr] * hr - pi[dr] * hi + fr[dr, :, k],
                              pr[dr] * hi + pi[dr] * hr + fi[dr, :, k])
            st = lambda o: jnp.stack([o[k] for k in range(nseg)], axis=1)
            return st(outs_r), st(outs_i), hr, hi

        sr0, si0, er0, ei0 = chain(0, list(range(nseg)))
        sr1, si1, er1, ei1 = chain(1, list(range(nseg - 1, -1, -1)))
        start_r = jnp.stack([sr0, sr1]).reshape(2, rows, -1)
        start_i = jnp.stack([si0, si1]).reshape(2, rows, -1)
        y_t, _, _ = _s5_scan(u3, bmat, cmat, amat, start_r, start_i)
        fr = jnp.stack([er0, er1])
        fi = jnp.stack([ei0, ei1])
    ya = _s5_glu(y_t.reshape(2, S5_SEG, rows * S5_WIDTH), u_t, p['s5_d'][j][None],
                 p['s5_w_glu'][j].astype(BF16), p['s5_b_glu'][j][None], bsz, nseg)
    return ya, fr, fi


def _gdn_kernel(qkv_ref, z_ref, ab_ref, cw_ref, gp_ref, ng_ref, s0_ref, o_ref, sf_ref,
                q_s, k_s, v_s, gate_s, of_s, ob_s, st_s):
    n = qkv_ref.shape[1]
    nc = n // GDN_CHUNK
    row = lax.broadcasted_iota(jnp.int32, (n, 1), 0)

    for blk in range(3 * GDN_HEADS):
        cols = slice(blk * GDN_DK, (blk + 1) * GDN_DK)
        hs = slice((blk % GDN_HEADS) * GDN_DK, (blk % GDN_HEADS + 1) * GDN_DK)
        x = qkv_ref[0, :, cols]
        xm = jnp.where(row == 0, 0.0, pltpu.roll(x, 1, 0))
        xp = jnp.where(row == n - 1, 0.0, pltpu.roll(x, n - 1, 0))
        y = _silu(xm * cw_ref[0:1, cols] + x * cw_ref[1:2, cols] + xp * cw_ref[2:3, cols])
        if blk < GDN_HEADS:
            q_s[:, hs] = y * lax.rsqrt(jnp.sum(y * y, axis=-1, keepdims=True) + EPS) * (GDN_DK ** -0.5)
        elif blk < 2 * GDN_HEADS:
            k_s[:, hs] = y * lax.rsqrt(jnp.sum(y * y, axis=-1, keepdims=True) + EPS)
        else:
            v_s[:, hs] = y

    ab = ab_ref[0]
    beta = _sigmoid(ab)
    g = -jnp.exp(gp_ref[0:1, :]) * _softplus(ab + gp_ref[1:2, :])
    pos = row % GDN_CHUNK
    pre, suf = g, g
    sft = 1
    while sft < GDN_CHUNK:
        pre = pre + jnp.where(pos >= sft, pltpu.roll(pre, sft, 0), 0.0)
        suf = suf + jnp.where(pos < GDN_CHUNK - sft, pltpu.roll(suf, n - sft, 0), 0.0)
        sft *= 2
    gate_s[0] = beta
    gate_s[1] = pre
    gate_s[2] = suf

    st_s[...] = s0_ref[0]
    ri = lax.broadcasted_iota(jnp.int32, (GDN_CHUNK, GDN_CHUNK), 0)
    ci = lax.broadcasted_iota(jnp.int32, (GDN_CHUNK, GDN_CHUNK), 1)
    eye = (ri == ci).astype(F32)

    def chunk_step(c, carry):
        ch = []
        for dr in range(2):
            cidx = c if dr == 0 else nc - 1 - c
            rows = pl.ds(pl.multiple_of(cidx * GDN_CHUNK, GDN_CHUNK), GDN_CHUNK)
            incl = (ri >= ci) if dr == 0 else (ri <= ci)
            strict = (ri > ci) if dr == 0 else (ri < ci)
            gc_blk = gate_s[1 + dr, rows, :]
            gc_t = gc_blk.T
            beta_blk = gate_s[0, rows, :]
            for h in range(GDN_HEADS):
                hs = slice(h * GDN_DK, (h + 1) * GDN_DK)
                lane = dr * GDN_HEADS + h
                gcol = gc_blk[:, lane:lane + 1]
                ch.append(dict(dr=dr, h=h, rows=rows, hs=hs, incl=incl, strict=strict,
                               q=q_s[rows, hs], k=k_s[rows, hs], v=v_s[rows, hs],
                               bcol=beta_blk[:, 8 + lane:9 + lane], gcol=gcol,
                               grow=gc_t[lane:lane + 1, :],
                               gl=gcol[GDN_CHUNK - 1:GDN_CHUNK] if dr == 0 else gcol[0:1]))
        for t in ch:
            t['kb'] = t['k'] * t['bcol']
            t['kq'] = _dot_nt(jnp.concatenate([t['kb'], t['q']], axis=0), t['k'])
        for t in ch:
            decay = jnp.exp(jnp.where(t['incl'], t['gcol'] - t['grow'], NEG_INF))
            t['pw'] = jnp.where(t['strict'], t['kq'][:GDN_CHUNK] * decay, 0.0)
            t['amat'] = jnp.where(t['incl'], t['kq'][GDN_CHUNK:] * decay, 0.0)
            t['tm'] = eye - t['pw']
        for t in ch:
            t['pw'] = _dot3(t['pw'], t['pw'])
        for rnd in range(5):
            for t in ch:
                if rnd < 4:
                    r = _dot3(jnp.concatenate([t['pw'], t['tm']], axis=0), t['pw'])
                    t['pw'] = r[:GDN_CHUNK]
                    t['tm'] = t['tm'] + r[GDN_CHUNK:]
                else:
                    t['tm'] = t['tm'] + _dot3(t['tm'], t['pw'])
        for t in ch:
            rhs = jnp.concatenate([t['v'] * t['bcol'], t['kb'] * jnp.exp(t['gcol'])], axis=1)
            t['uw'] = _dot(t['tm'], rhs)
        for t in ch:
            t['s'] = st_s[t['dr'], t['h']]
            lhs = jnp.concatenate([t['uw'][:, GDN_DV:], t['q'] * jnp.exp(t['gcol'])], axis=0)
            t['ws'] = _dot(lhs, t['s'])
        for t in ch:
            vn = t['uw'][:, :GDN_DV] - t['ws'][:GDN_CHUNK]
            o = t['ws'][GDN_CHUNK:] + _dot(t['amat'], vn)
            st_s[t['dr'], t['h']] = (t['s'] * jnp.exp(t['gl'])
                                     + _dot_tn(t['k'] * jnp.exp(t['gl'] - t['gcol']), vn))
            if t['dr'] == 0:
                of_s[t['rows'], t['hs']] = o
            else:
                ob_s[t['rows'], t['hs']] = o
        return carry

    lax.fori_loop(0, nc, chunk_step, 0)
    sf_ref[0] = st_s[...]

    z = z_ref[0]
    for h in range(GDN_HEADS):
        hs = slice(h * GDN_DV, (h + 1) * GDN_DV)
        o = of_s[:, hs] + ob_s[:, hs]
        o = o * lax.rsqrt(jnp.mean(o * o, axis=-1, keepdims=True) + EPS) * ng_ref[...]
        o_ref[0, :, hs] = o * _silu(z[:, hs])


def _gdn_mixer(qkv, z, ab, p, j, s0):
    bsz, n, _ = qkv.shape
    gp = jnp.zeros((8, 128), F32)
    gp = gp.at[0, :8].set(p['gdn_a_log'][j].reshape(8)).at[1, :8].set(p['gdn_dt_bias'][j].reshape(8))
    cw = jnp.zeros((8, 3 * GDN_WIDTH), F32).at[:3].set(p['gdn_conv_w'][j])
    return pl.pallas_call(
        _gdn_kernel,
        grid=(bsz,),
        in_specs=[pl.BlockSpec((1, n, 3 * GDN_WIDTH), lambda b: (b, 0, 0)),
                  pl.BlockSpec((1, n, GDN_WIDTH), lambda b: (b, 0, 0)),
                  pl.BlockSpec((1, n, 128), lambda b: (b, 0, 0)),
                  pl.BlockSpec((8, 3 * GDN_WIDTH), lambda b: (0, 0)),
                  pl.BlockSpec((8, 128), lambda b: (0, 0)),
                  pl.BlockSpec((1, GDN_DV), lambda b: (0, 0)),
                  pl.BlockSpec((1, 2, GDN_HEADS, GDN_DK, GDN_DV), lambda b: (b, 0, 0, 0, 0))],
        out_specs=[pl.BlockSpec((1, n, GDN_WIDTH), lambda b: (b, 0, 0)),
                   pl.BlockSpec((1, 2, GDN_HEADS, GDN_DK, GDN_DV), lambda b: (b, 0, 0, 0, 0))],
        out_shape=[jax.ShapeDtypeStruct((bsz, n, GDN_WIDTH), F32),
                   jax.ShapeDtypeStruct((bsz, 2, GDN_HEADS, GDN_DK, GDN_DV), F32)],
        scratch_shapes=[pltpu.VMEM((n, GDN_WIDTH), F32), pltpu.VMEM((n, GDN_WIDTH), F32),
                        pltpu.VMEM((n, GDN_WIDTH), F32), pltpu.VMEM((3, n, 128), F32),
                        pltpu.VMEM((n, GDN_WIDTH), F32), pltpu.VMEM((n, GDN_WIDTH), F32),
                        pltpu.VMEM((2, GDN_HEADS, GDN_DK, GDN_DV), F32)],
        compiler_params=_cparams(("parallel",)),
        name="gdn_mixer",
    )(qkv, z, ab, cw, gp, p['gdn_norm_g'][j][None], s0)


def _proj_cd_kernel(x_ref, g_ref, sh_ref, sc_ref, w_ref, gm_ref, gain_ref, cos_ref, sin_ref,
                    qc_ref, kc_ref, vc_ref, qd_ref, kd_ref, vd_ref, *, rope):
    h = _rms_mod(x_ref[0], g_ref[...], sh_ref[0], sc_ref[0])
    y = _dot(h, w_ref[...])
    lane = lax.broadcasted_iota(jnp.int32, (1, 512), 1)
    low = (lane % 32) < 16

    def head_norm(t, gain, scale):
        w = t.shape[1]
        ms = _split_dot(t * t, gm_ref[:w, :w])
        t = t * lax.rsqrt(ms + EPS) * gain
        if rope:
            part = jnp.where(low[:, :w], pltpu.roll(t, w - 16, 1), pltpu.roll(t, 16, 1))
            t = t * cos_ref[:, :w] + part * sin_ref[:, :w]
        return t * scale if scale != 1.0 else t

    qc_ref[0] = head_norm(y[:, 0:512], gain_ref[0:1, :], ATTN_SCALE)
    kc_ref[0] = head_norm(y[:, 512:640], gain_ref[1:2, :128], 1.0)
    vc_ref[0] = y[:, 640:768]
    qd_ref[0] = head_norm(y[:, 768:1280], gain_ref[2:3, :], ATTN_SCALE)
    kd_ref[0] = head_norm(y[:, 1280:1792], gain_ref[3:4, :], 1.0)
    vd_ref[0] = y[:, 1792:2304]


def _rope_tables(n):
    rows = n // GRID_W
    row = jnp.repeat(jnp.arange(rows), GRID_W).astype(F32)
    col = jnp.tile(jnp.arange(GRID_W), rows).astype(F32)
    quarter = HEAD_DIM // 4
    inv = ROPE_THETA ** (-jnp.arange(quarter, dtype=F32) / quarter)
    ang_r = row[:, None] * inv[None, :]
    ang_c = col[:, None] * inv[None, :]
    cos = jnp.concatenate([jnp.cos(ang_r), jnp.cos(ang_r), jnp.cos(ang_c), jnp.cos(ang_c)], axis=-1)
    sin = jnp.concatenate([-jnp.sin(ang_r), jnp.sin(ang_r), -jnp.sin(ang_c), jnp.sin(ang_c)], axis=-1)
    return jnp.tile(cos, (1, 8)), jnp.tile(sin, (1, 8))


def _proj_cd(x, g, shift, scale, w, p, j, rope):
    bsz, n, d = x.shape
    tm = 256
    per_seq = shift.shape[0] > 1
    midx = (lambda b, i: (b, 0, 0)) if per_seq else (lambda b, i: (0, 0, 0))
    lane = jnp.arange(512)
    gmat = ((lane[:, None] // HEAD_DIM) == (lane[None, :] // HEAD_DIM)).astype(F32) / HEAD_DIM
    gains = jnp.zeros((8, 512), F32)
    gains = gains.at[0].set(jnp.tile(p['c_qn'][j], 8)).at[1].set(jnp.tile(p['c_kn'][j], 8))
    gains = gains.at[2].set(jnp.tile(p['d_qn'][j], 8)).at[3].set(jnp.tile(p['d_kn'][j], 8))
    if rope:
        cos, sin = _rope_tables(n)
    else:
        cos, sin = jnp.ones((tm, 512), F32), jnp.zeros((tm, 512), F32)
    tidx = (lambda b, i: (i, 0)) if rope else (lambda b, i: (0, 0))
    blk = lambda w_: pl.BlockSpec((1, tm, w_), lambda b, i: (b, i, 0))
    return pl.pallas_call(
        functools.partial(_proj_cd_kernel, rope=rope),
        grid=(bsz, n // tm),
        in_specs=[blk(d),
                  pl.BlockSpec((1, d), lambda b, i: (0, 0)),
                  pl.BlockSpec((1, 1, d), midx),
                  pl.BlockSpec((1, 1, d), midx),
                  pl.BlockSpec((d, CD_IN), lambda b, i: (0, 0)),
                  pl.BlockSpec((512, 512), lambda b, i: (0, 0)),
                  pl.BlockSpec((8, 512), lambda b, i: (0, 0)),
                  pl.BlockSpec((tm, 512), tidx),
                  pl.BlockSpec((tm, 512), tidx)],
        out_specs=[blk(512), blk(128), blk(128), blk(512), blk(512), blk(512)],
        out_shape=[jax.ShapeDtypeStruct((bsz, n, w_), F32) for w_ in (512, 128, 128, 512, 512, 512)],
        compiler_params=_cparams(("parallel", "parallel")),
        name="proj_cd",
    )(x, g, shift, scale, w, gmat.astype(BF16), gains, cos, sin)


def _softmax_parts(scores, extra=None):
    m = scores[0].max(axis=-1, keepdims=True)
    for s in scores[1:]:
        m = jnp.maximum(m, s.max(axis=-1, keepdims=True))
    if extra is not None:
        m = jnp.maximum(m, extra)
    ps = [jnp.exp(s - m) for s in scores]
    den = ps[0].sum(axis=-1, keepdims=True)
    for pp in ps[1:]:
        den = den + pp.sum(axis=-1, keepdims=True)
    if extra is not None:
        den = den + jnp.exp(extra - m)
    return ps, den


def _attn_kernel(qc_ref, kc_ref, vc_ref, qd_ref, kd_ref, vd_ref, *rest, windowed, n_ctx, lam_scale):
    if n_ctx:
        ck_ref, cv_ref, dk_ref, dv_ref, misc_ref, sub_ref, oc_ref, od_ref = rest
    else:
        misc_ref, sub_ref, oc_ref, od_ref = rest
    tq = qc_ref.shape[1]
    n = kc_ref.shape[1]
    start = pl.program_id(1) * tq
    if windowed:
        span = tq + 2 * WINDOW
        k0 = pl.multiple_of(jnp.clip(start - WINDOW, 0, n - span), 128)
        krows = pl.ds(k0, span)
        qpos = start + lax.broadcasted_iota(jnp.int32, (tq, span), 0)
        kpos = k0 + lax.broadcasted_iota(jnp.int32, (tq, span), 1)
        ok = jnp.abs(qpos - kpos) <= WINDOW
    else:
        krows = pl.ds(0, n)

    qc = qc_ref[0].astype(BF16)
    qd = qd_ref[0].astype(BF16)
    kc = kc_ref[0, krows, :].astype(BF16)
    vc = vc_ref[0, krows, :].astype(BF16)
    kd = kd_ref[0].astype(BF16)
    c_scores = []
    for h in range(C_HEADS):
        gsl = slice((h // C_GROUP) * HEAD_DIM, (h // C_GROUP + 1) * HEAD_DIM)
        q = qc[:, h * HEAD_DIM:(h + 1) * HEAD_DIM]
        sc = [_dot_nt(q, kc[:, gsl])]
        if n_ctx:
            sc.append(_dot_nt(q, ck_ref[0, :, gsl]))
        c_scores.append(sc)
    d_scores = []
    for hc in range(2 * D_HEADS):
        sl = slice(hc * HEAD_DIM, (hc + 1) * HEAD_DIM)
        q = qd[:, sl]
        sc = [_dot_nt(q, kd[:, sl])]
        if n_ctx:
            sc.append(_dot_nt(q, dk_ref[0, :, sl]))
        d_scores.append(sc)

    c_probs = []
    for h in range(C_HEADS):
        sc = c_scores[h]
        if windowed:
            sc[0] = jnp.where(ok, sc[0], NEG_INF)
        c_probs.append(_softmax_parts(sc, misc_ref[0:1, h:h + 1]))
    lam = misc_ref[1:2, 0:1]
    d_att = []
    for h in range(D_HEADS):
        ps0, den0 = _softmax_parts(d_scores[2 * h])
        ps1, den1 = _softmax_parts(d_scores[2 * h + 1])
        d_att.append([p0 / den0 - lam * (p1 / den1) for p0, p1 in zip(ps0, ps1)])

    for h in range(C_HEADS):
        gsl = slice((h // C_GROUP) * HEAD_DIM, (h // C_GROUP + 1) * HEAD_DIM)
        ps, den = c_probs[h]
        o = _dot(ps[0], vc[:, gsl])
        if n_ctx:
            o = o + _dot(ps[1], cv_ref[0, :, gsl])
        oc_ref[0, :, h * HEAD_DIM:(h + 1) * HEAD_DIM] = o / den
    for h in range(D_HEADS):
        vsl = slice(h * D_VDIM, (h + 1) * D_VDIM)
        o = _dot(d_att[h][0], vd_ref[0, :, vsl])
        if n_ctx:
            o = o + _dot(d_att[h][1], dv_ref[0, :, vsl])
        o = o * lax.rsqrt(jnp.mean(o * o, axis=-1, keepdims=True) + EPS) * sub_ref[...] * lam_scale
        od_ref[0, :, h * D_VDIM:(h + 1) * D_VDIM] = o


def _attention(qc, kc, vc, qd, kd, vd, caches, misc, subln, lam_init, tq, windowed):
    bsz, n, _ = qc.shape
    qblk = lambda w_: pl.BlockSpec((1, tq, w_), lambda b, i: (b, i, 0))
    kblk = lambda rows, w_: pl.BlockSpec((1, rows, w_), lambda b, i: (b, 0, 0))
    in_specs = [qblk(512), kblk(n, 128), kblk(n, 128), qblk(512), kblk(n, 512), kblk(n, 512)]
    args = [qc, kc, vc, qd, kd, vd]
    n_ctx = 0
    if caches is not None:
        n_ctx = caches[0].shape[1]
        in_specs += [kblk(n_ctx, 128), kblk(n_ctx, 128), kblk(n_ctx, 512), kblk(n_ctx, 512)]
        args += list(caches)
    in_specs += [pl.BlockSpec((8, 128), lambda b, i: (0, 0)), pl.BlockSpec((1, D_VDIM), lambda b, i: (0, 0))]
    args += [misc, subln]
    return pl.pallas_call(
        functools.partial(_attn_kernel, windowed=windowed, n_ctx=n_ctx, lam_scale=1.0 - lam_init),
        grid=(bsz, n // tq),
        in_specs=in_specs,
        out_specs=[qblk(512), qblk(512)],
        out_shape=[jax.ShapeDtypeStruct((bsz, n, 512), F32), jax.ShapeDtypeStruct((bsz, n, 512), F32)],
        compiler_params=_cparams(("parallel", "parallel")),
        name="attn_win" if windowed else "attn_ctx",
    )(*args)


def _post_kernel(x_ref, xp_ref, xn_ref, ma_ref, map_ref, man_ref, mb_ref, mbp_ref, mbn_ref,
                 g1_ref, sh_ref, sc_ref, g2_ref, ng_ref, wo_ref, wup_ref, cw_ref, wdn_ref, o_ref, act_ref,
                 *, seq_len):
    rows = x_ref.shape[0]
    ext = rows + 2 * POST_HALO
    half = ma_ref.shape[1]
    xe = jnp.concatenate([xp_ref[...], x_ref[...], xn_ref[...]], axis=0)
    mae = jnp.concatenate([map_ref[...], ma_ref[...], man_ref[...]], axis=0)
    mbe = jnp.concatenate([mbp_ref[...], mb_ref[...], mbn_ref[...]], axis=0)
    x1 = xe + g1_ref[0] * (_dot(mae, wo_ref[:half, :]) + _dot(mbe, wo_ref[half:, :]))
    h = _rms_mod(x1, ng_ref[...], sh_ref[0], sc_ref[0]).astype(BF16)
    x1 = x1[POST_HALO:POST_HALO + rows]
    row0 = pl.program_id(0) * rows - POST_HALO
    pos = (row0 + lax.broadcasted_iota(jnp.int32, (ext, 1), 0)) % seq_len
    first = pos == 0
    last = pos == seq_len - 1
    for c in range(FF_CHUNKS):
        cs = slice(c * FF_CHUNK, (c + 1) * FF_CHUNK)
        a = jnp.dot(h, wup_ref[:, cs], preferred_element_type=F32)
        b = jnp.dot(h, wup_ref[:, D_FF + c * FF_CHUNK:D_FF + (c + 1) * FF_CHUNK], preferred_element_type=F32)
        am = jnp.where(first, 0.0, pltpu.roll(a, 1, 0))
        ap = jnp.where(last, 0.0, pltpu.roll(a, ext - 1, 0))
        a = am * cw_ref[0:1, cs] + a * cw_ref[1:2, cs] + ap * cw_ref[2:3, cs] + cw_ref[3:4, cs]
        act_ref[:, cs] = (_silu(a) * b)[POST_HALO:POST_HALO + rows].astype(BF16)
    ffn = jnp.dot(act_ref[...], wdn_ref[...], preferred_element_type=F32)
    o_ref[...] = x1 + g2_ref[0] * ffn


def _post(x, mix_a, mix_b, g1, sh2, sc2, g2, norm_g, w_out, wup, cw, wdn):
    bsz, seq_len, d = x.shape
    half = mix_a.shape[-1]
    rows = POST_ROWS
    total = bsz * seq_len
    nhalo = total // POST_HALO
    per_seq = g1.shape[0] > 1
    midx = (lambda i: ((i * rows) // seq_len, 0, 0)) if per_seq else (lambda i: (0, 0, 0))
    mspec = pl.BlockSpec((1, 1, d), midx)
    pidx = lambda i: (jnp.maximum(i * (rows // POST_HALO) - 1, 0), 0)
    nidx = lambda i: (jnp.minimum((i + 1) * (rows // POST_HALO), nhalo - 1), 0)
    trio = lambda w_: [pl.BlockSpec((rows, w_), lambda i: (i, 0)), pl.BlockSpec((POST_HALO, w_), pidx),
                       pl.BlockSpec((POST_HALO, w_), nidx)]
    whole = lambda shape: pl.BlockSpec(shape, lambda i: (0,) * len(shape), pipeline_mode=pl.Buffered(1))
    x2 = x.reshape(total, d)
    a2 = mix_a.reshape(total, half)
    b2 = mix_b.reshape(total, half)
    out = pl.pallas_call(
        functools.partial(_post_kernel, seq_len=seq_len),
        grid=(total // rows,),
        in_specs=trio(d) + trio(half) + trio(half) + [
            mspec, mspec, mspec, mspec,
            pl.BlockSpec((1, d), lambda i: (0, 0)),
            whole((d, d)), whole((d, 2 * D_FF)), whole((8, D_FF)), whole((D_FF, d))],
        out_specs=pl.BlockSpec((rows, d), lambda i: (i, 0)),
        out_shape=jax.ShapeDtypeStruct((total, d), F32),
        scratch_shapes=[pltpu.VMEM((rows, D_FF), BF16)],
        compiler_params=_cparams(("parallel",)),
        name="post_ffn",
    )(x2, x2, x2, a2, a2, a2, b2, b2, b2, g1, sh2, sc2, g2, norm_g, w_out, wup, cw, wdn)
    return out.reshape(bsz, seq_len, d)


def _ffn_weights(p, l):
    cw = jnp.zeros((8, D_FF), F32).at[:3].set(p['ffn_conv_w'][l]).at[3].set(p['ffn_conv_b'][l])
    return p['ffn_up'][l].astype(BF16), cw, p['ffn_down'][l].astype(BF16)


def _lambda_init(layer):
    return 0.8 - 0.6 * math.exp(-0.3 * layer)


def _trunk(x, mods, p, states, caches):
    bsz, n, d = x.shape
    nseg = n // S5_SEG
    depth = p['w_mod'].shape[0]
    news = {k: [] for k in ('s5r', 's5i', 'gdn', 'ck', 'cv', 'dk', 'dv')}
    for l in range(depth):
        j = l // 2
        sh1, sc1, g1, sh2, sc2, g2 = mods[l]
        ng1 = p['norm1_g'][l][None]
        if l % 2 == 0:
            w_in = p['w_in_ab'][j]
            w_gate = jnp.zeros((d, 128), BF16).at[:, :w_in.shape[1] - AB_MAIN].set(
                w_in[:, AB_MAIN:].astype(BF16))
            u_t, qkv, z, ab = _proj_ab(x, ng1, sh1, sc1, w_in[:, :AB_MAIN].astype(BF16), w_gate)
            if states is None:
                h0r = jnp.zeros((2, bsz, S5_GROUPS * S5_STATE), F32)
                h0i = h0r
                s0 = jnp.zeros((bsz, 2, GDN_HEADS, GDN_DK, GDN_DV), F32)
            else:
                h0r = states[0][:, j].reshape(bsz, 2, -1).transpose(1, 0, 2)
                h0i = states[1][:, j].reshape(bsz, 2, -1).transpose(1, 0, 2)
                s0 = states[2][:, j]
            ya, fr, fi = _s5_mixer(u_t, p, j, h0r, h0i, bsz, nseg)
            yb, sg = _gdn_mixer(qkv, z, ab, p, j, s0)
            mix = (ya, yb)
            w_out = p['w_out_ab'][j]
            news['s5r'].append(fr.transpose(1, 0, 2).reshape(bsz, 2, S5_GROUPS, S5_STATE))
            news['s5i'].append(fi.transpose(1, 0, 2).reshape(bsz, 2, S5_GROUPS, S5_STATE))
            news['gdn'].append(sg)
        else:
            lam_init = _lambda_init(l)
            f = lambda name: p[name][j]
            lam = (jnp.exp(jnp.sum(f('d_lq1') * f('d_lk1'))) - jnp.exp(jnp.sum(f('d_lq2') * f('d_lk2')))
                   + lam_init)
            misc = jnp.zeros((8, 128), F32).at[0, :C_HEADS].set(p['c_sink'][j]).at[1, :].set(lam)
            rope = caches is not None
            qc, kc, vc, qd, kd, vd = _proj_cd(x, ng1, sh1, sc1, p['w_in_cd'][j].astype(BF16), p, j, rope)
            if caches is None:
                mix = _attention(qc, kc, vc, qd, kd, vd, None, misc, p['d_subln'][j][None],
                                 lam_init, n, False)
            else:
                n_ctx = caches[0].shape[2]
                cc = (caches[0][:, j].reshape(bsz, n_ctx, 128), caches[1][:, j].reshape(bsz, n_ctx, 128),
                      caches[2][:, j].reshape(bsz, n_ctx, 512), caches[3][:, j].reshape(bsz, n_ctx, 512))
                mix = _attention(qc, kc, vc, qd, kd, vd, cc, misc, p['d_subln'][j][None],
                                 lam_init, Q_BLOCK, True)
            w_out = p['w_out_cd'][j]
            news['ck'].append(kc.reshape(bsz, n, C_KV_HEADS, HEAD_DIM))
            news['cv'].append(vc.reshape(bsz, n, C_KV_HEADS, HEAD_DIM))
            news['dk'].append(kd.reshape(bsz, n, D_HEADS, 2, HEAD_DIM))
            news['dv'].append(vd.reshape(bsz, n, D_HEADS, D_VDIM))
        wup, cw, wdn = _ffn_weights(p, l)
        x = _post(x, mix[0], mix[1], g1, sh2, sc2, g2, p['norm2_g'][l][None], w_out.astype(BF16),
                  wup, cw, wdn)
    return x, news


def kernel(x_prompt, x_sample, c, state_s5_re, state_s5_im, state_gdn, cache_c_k, cache_c_v, cache_d_k, cache_d_v, c_ctx, w_mod, b_mod, norm1_g, norm2_g, w_in_ab, w_out_ab, s5_lam_re, s5_lam_im, s5_log_dt, s5_b_re, s5_b_im, s5_c_re, s5_c_im, s5_d, s5_w_glu, s5_b_glu, gdn_conv_w, gdn_a_log, gdn_dt_bias, gdn_norm_g, w_in_cd, w_out_cd, c_qn, c_kn, c_sink, d_qn, d_kn, d_lq1, d_lk1, d_lq2, d_lk2, d_subln, ffn_up, ffn_conv_w, ffn_conv_b, ffn_down):
    p = dict(w_mod=w_mod, b_mod=b_mod, norm1_g=norm1_g, norm2_g=norm2_g, w_in_ab=w_in_ab, w_out_ab=w_out_ab,
             s5_lam_re=s5_lam_re, s5_lam_im=s5_lam_im, s5_log_dt=s5_log_dt, s5_b_re=s5_b_re, s5_b_im=s5_b_im,
             s5_c_re=s5_c_re, s5_c_im=s5_c_im, s5_d=s5_d, s5_w_glu=s5_w_glu, s5_b_glu=s5_b_glu,
             gdn_conv_w=gdn_conv_w, gdn_a_log=gdn_a_log, gdn_dt_bias=gdn_dt_bias, gdn_norm_g=gdn_norm_g,
             w_in_cd=w_in_cd, w_out_cd=w_out_cd, c_qn=c_qn, c_kn=c_kn, c_sink=c_sink, d_qn=d_qn, d_kn=d_kn,
             d_lq1=d_lq1, d_lk1=d_lk1, d_lq2=d_lq2, d_lk2=d_lk2, d_subln=d_subln,
             ffn_up=ffn_up, ffn_conv_w=ffn_conv_w, ffn_conv_b=ffn_conv_b, ffn_down=ffn_down)
    depth = w_mod.shape[0]
    n_dec = c.shape[0]
    mod = _modulation(jnp.concatenate([c_ctx[None], c], axis=0), w_mod, b_mod)
    split6 = lambda m: [m[:, None, k * D_MODEL:(k + 1) * D_MODEL] for k in range(6)]
    mods_ctx = [split6(mod[l, 0:1]) for l in range(depth)]
    mods_dec = [split6(mod[l, 1:1 + n_dec]) for l in range(depth)]

    y_prompt, nw = _trunk(x_prompt, mods_ctx, p, None, None)
    y_sample, _ = _trunk(x_sample, mods_dec, p, (state_s5_re, state_s5_im, state_gdn),
                         (cache_c_k, cache_c_v, cache_d_k, cache_d_v))
    st = lambda name: jnp.stack(nw[name], axis=1)
    return (y_prompt, y_sample, st('s5r'), st('s5i'), st('gdn'), st('ck'), st('cv'), st('dk'), st('dv'))
```

```python
import functools
import math

import jax
import jax.numpy as jnp
from jax import lax
from jax.experimental import pallas as pl
from jax.experimental.pallas import tpu as pltpu

F32 = jnp.float32
BF16 = jnp.bfloat16

D_MODEL = 1024
GRID_W = 64
EPS = 1e-6
NEG_INF = -1e30

S5_WIDTH = 512
S5_GROUP = 16
S5_GROUPS = 32
S5_STATE = 64
S5_TILE_GROUPS = 8
S5_TILE_CH = S5_TILE_GROUPS * S5_GROUP
S5_TILE_ST = S5_TILE_GROUPS * S5_STATE
S5_TILES = S5_GROUPS // S5_TILE_GROUPS
S5_SEG = 256
S5_TCHUNK = 128
S5_ROWS = 8

GDN_DK = 128
GDN_DV = 128
GDN_HEADS = 4
GDN_WIDTH = 512
GDN_CHUNK = 64
GDN_GROUP = 4

HEAD_DIM = 64
C_HEADS = 8
C_KV_HEADS = 2
C_GROUP = 4
WINDOW = 128
Q_BLOCK = 128
D_HEADS = 4
D_VDIM = 128
ATTN_SCALE = HEAD_DIM ** -0.5
ROPE_THETA = 10000.0

D_FF = 2816
FF_CHUNK = 256
FF_CHUNKS = D_FF // FF_CHUNK
POST_ROWS = 512
POST_HALO = 8

AB_MAIN = S5_WIDTH + 4 * GDN_WIDTH
AB_PAD = AB_MAIN + 128
CD_IN = 2304

VMEM_LIMIT = 56 * 1024 * 1024


def _cparams(sem):
    return pltpu.CompilerParams(dimension_semantics=sem, vmem_limit_bytes=VMEM_LIMIT)


def _sigmoid(x):
    return 1.0 / (1.0 + jnp.exp(-x))


def _silu(x):
    return x * _sigmoid(x)


def _softplus(x):
    return jnp.maximum(x, 0.0) + jnp.log(1.0 + jnp.exp(-jnp.abs(x)))


def _gelu_tanh(x):
    return 0.5 * x * (1.0 + jnp.tanh(math.sqrt(2.0 / math.pi) * (x + 0.044715 * (x * x * x))))


def _rms_mod(x, g, shift, scale):
    y = x * lax.rsqrt(jnp.mean(x * x, axis=-1, keepdims=True) + EPS)
    return (y * g) * (1.0 + scale) + shift


def _dot(a, b):
    return jnp.dot(a.astype(BF16), b.astype(BF16), preferred_element_type=F32)


def _dot_nt(a, b):
    return lax.dot_general(a.astype(BF16), b.astype(BF16), (((1,), (1,)), ((), ())),
                           preferred_element_type=F32)


def _dot_tn(a, b):
    return lax.dot_general(a.astype(BF16), b.astype(BF16), (((0,), (0,)), ((), ())),
                           preferred_element_type=F32)


def _hi_lo(a):
    hi = a.astype(BF16)
    return hi, (a - hi.astype(F32)).astype(BF16)


def _dot3(a, b):
    a_hi, a_lo = _hi_lo(a)
    b_hi, b_lo = _hi_lo(b)
    mm = functools.partial(jnp.dot, preferred_element_type=F32)
    return mm(a_hi, b_hi) + (mm(a_lo, b_hi) + mm(a_hi, b_lo))


def _split_dot(a, b_bf16):
    hi = a.astype(BF16)
    lo = (a - hi.astype(F32)).astype(BF16)
    return (jnp.dot(hi, b_bf16, preferred_element_type=F32)
            + jnp.dot(lo, b_bf16, preferred_element_type=F32))


def _mod_kernel(ct_ref, w_ref, b_ref, o_ref, *, n_rows):
    c = ct_ref[...]
    s = _silu(c)
    w = w_ref[0]
    o_ref[0] = jnp.zeros(o_ref.shape[1:], F32)
    for m in range(n_rows):
        o_ref[0, m:m + 1, :] = jnp.sum(w * s[:, m:m + 1], axis=0, keepdims=True) + b_ref[0]


def _modulation(cvecs, w_mod, b_mod):
    n, d = cvecs.shape
    depth, _, n_out = w_mod.shape
    tn = 512
    ct = jnp.zeros((d, 8), F32).at[:, :n].set(cvecs.T)
    return pl.pallas_call(
        functools.partial(_mod_kernel, n_rows=n),
        grid=(depth, n_out // tn),
        in_specs=[pl.BlockSpec((d, 8), lambda l, j: (0, 0)),
                  pl.BlockSpec((1, d, tn), lambda l, j: (l, 0, j)),
                  pl.BlockSpec((1, 1, tn), lambda l, j: (l, 0, j))],
        out_specs=pl.BlockSpec((1, 8, tn), lambda l, j: (l, 0, j)),
        out_shape=jax.ShapeDtypeStruct((depth, 8, n_out), F32),
        compiler_params=_cparams(("parallel", "parallel")),
        name="adaln_mod",
    )(ct, w_mod, b_mod.reshape(depth, 1, n_out))


def _proj_ab_kernel(x_ref, g_ref, sh_ref, sc_ref, w_ref, wg_ref, u_ref, qkv_ref, z_ref, ab_ref):
    h = _rms_mod(x_ref[0], g_ref[...], sh_ref[0], sc_ref[0]).astype(BF16)
    y = jnp.dot(h, w_ref[...], preferred_element_type=F32)
    u_ref[...] = y[:, :S5_WIDTH]
    qkv_ref[0] = y[:, S5_WIDTH:S5_WIDTH + 3 * GDN_WIDTH]
    z_ref[0] = y[:, S5_WIDTH + 3 * GDN_WIDTH:AB_MAIN]
    ab_ref[0] = jnp.dot(h, wg_ref[...], preferred_element_type=F32)


def _proj_ab(x, g, shift, scale, w, w_gate):
    bsz, n, d = x.shape
    tm = S5_SEG
    nseg = n // tm
    per_seq = shift.shape[0] > 1
    midx = (lambda b, i: (b, 0, 0)) if per_seq else (lambda b, i: (0, 0, 0))
    return pl.pallas_call(
        _proj_ab_kernel,
        grid=(bsz, nseg),
        in_specs=[pl.BlockSpec((1, tm, d), lambda b, i: (b, i, 0)),
                  pl.BlockSpec((1, d), lambda b, i: (0, 0)),
                  pl.BlockSpec((1, 1, d), midx),
                  pl.BlockSpec((1, 1, d), midx),
                  pl.BlockSpec((d, AB_MAIN), lambda b, i: (0, 0)),
                  pl.BlockSpec((d, 128), lambda b, i: (0, 0))],
        out_specs=[pl.BlockSpec((tm, S5_WIDTH), lambda b, i: (0, b * nseg + i)),
                   pl.BlockSpec((1, tm, 3 * GDN_WIDTH), lambda b, i: (b, i, 0)),
                   pl.BlockSpec((1, tm, GDN_WIDTH), lambda b, i: (b, i, 0)),
                   pl.BlockSpec((1, tm, 128), lambda b, i: (b, i, 0))],
        out_shape=[jax.ShapeDtypeStruct((tm, bsz * nseg * S5_WIDTH), F32),
                   jax.ShapeDtypeStruct((bsz, n, 3 * GDN_WIDTH), F32),
                   jax.ShapeDtypeStruct((bsz, n, GDN_WIDTH), F32),
                   jax.ShapeDtypeStruct((bsz, n, 128), F32)],
        compiler_params=_cparams(("parallel", "parallel")),
        name="proj_ab",
    )(x, g, shift, scale, w, w_gate)


def _s5_kernel(u_ref, bm_ref, cm_ref, a_ref, h0r_ref, h0i_ref, y_ref, fr_ref, fi_ref,
               xs_ref, hr_ref, hi_ref):
    d = pl.program_id(0)
    i = pl.program_id(3)
    nt = pl.num_programs(3)
    tc = u_ref.shape[0]

    @pl.when(i == 0)
    def _():
        hr_ref[...] = h0r_ref[0]
        hi_ref[...] = h0i_ref[0]

    u2 = u_ref[...].reshape(tc * S5_ROWS, S5_TILE_CH)
    xs_ref[...] = _dot(u2, bm_ref[0, 0]).reshape(tc, S5_ROWS, 2 * S5_TILE_ST)
    ar = jnp.broadcast_to(a_ref[0, 0, 0:1, :], (S5_ROWS, S5_TILE_ST))
    ai = jnp.broadcast_to(a_ref[0, 0, 1:2, :], (S5_ROWS, S5_TILE_ST))

    def step(t, carry):
        hr, hi = carry
        tt = jnp.where(d == 0, t, tc - 1 - t)
        x = xs_ref[tt]
        nr = ar * hr - ai * hi + x[:, :S5_TILE_ST]
        ni = ar * hi + ai * hr + x[:, S5_TILE_ST:]
        xs_ref[tt] = jnp.concatenate([nr, ni], axis=-1)
        return nr, ni

    hr, hi = lax.fori_loop(0, tc, step, (hr_ref[...], hi_ref[...]), unroll=4)
    hr_ref[...] = hr
    hi_ref[...] = hi
    hs = xs_ref[...].reshape(tc * S5_ROWS, 2 * S5_TILE_ST)
    y_ref[0] = _dot(hs, cm_ref[0, 0]).reshape(tc, S5_ROWS, S5_TILE_CH)

    @pl.when(i == nt - 1)
    def _():
        fr_ref[0] = hr
        fi_ref[0] = hi


def _s5_scan(u_t, bmat, cmat, amat, h0r, h0i):
    n, rows, _ = u_t.shape
    nt = n // S5_TCHUNK
    tmap = lambda d, r, j, i: (i + d * (nt - 1 - 2 * i), r, j)
    return pl.pallas_call(
        _s5_kernel,
        grid=(2, rows // S5_ROWS, S5_TILES, nt),
        in_specs=[pl.BlockSpec((S5_TCHUNK, S5_ROWS, S5_TILE_CH), tmap),
                  pl.BlockSpec((1, 1, S5_TILE_CH, 2 * S5_TILE_ST), lambda d, r, j, i: (d, j, 0, 0)),
                  pl.BlockSpec((1, 1, 2 * S5_TILE_ST, S5_TILE_CH), lambda d, r, j, i: (d, j, 0, 0)),
                  pl.BlockSpec((1, 1, 8, S5_TILE_ST), lambda d, r, j, i: (d, j, 0, 0)),
                  pl.BlockSpec((1, S5_ROWS, S5_TILE_ST), lambda d, r, j, i: (d, r, j)),
                  pl.BlockSpec((1, S5_ROWS, S5_TILE_ST), lambda d, r, j, i: (d, r, j))],
        out_specs=[pl.BlockSpec((1, S5_TCHUNK, S5_ROWS, S5_TILE_CH),
                                lambda d, r, j, i: (d,) + tmap(d, r, j, i)),
                   pl.BlockSpec((1, S5_ROWS, S5_TILE_ST), lambda d, r, j, i: (d, r, j)),
                   pl.BlockSpec((1, S5_ROWS, S5_TILE_ST), lambda d, r, j, i: (d, r, j))],
        out_shape=[jax.ShapeDtypeStruct((2, n, rows, S5_WIDTH), F32),
                   jax.ShapeDtypeStruct((2, rows, S5_GROUPS * S5_STATE), F32),
                   jax.ShapeDtypeStruct((2, rows, S5_GROUPS * S5_STATE), F32)],
        scratch_shapes=[pltpu.VMEM((S5_TCHUNK, S5_ROWS, 2 * S5_TILE_ST), F32),
                        pltpu.VMEM((S5_ROWS, S5_TILE_ST), F32),
                        pltpu.VMEM((S5_ROWS, S5_TILE_ST), F32)],
        compiler_params=_cparams(("parallel", "parallel", "parallel", "arbitrary")),
        name="s5_scan",
    )(u_t, bmat, cmat, amat, h0r, h0i)


def _s5_params(p, j):
    lam_re, lam_im, log_dt = p['s5_lam_re'][j], p['s5_lam_im'][j], p['s5_log_dt'][j]
    dt = jnp.exp(log_dt)[..., None]
    mag = jnp.exp(lam_re * dt)
    ar, ai = mag * jnp.cos(lam_im * dt), mag * jnp.sin(lam_im * dt)
    den = lam_re * lam_re + lam_im * lam_im
    fr = ((ar - 1.0) * lam_re + ai * lam_im) / den
    fi = (ai * lam_re - (ar - 1.0) * lam_im) / den
    b_re, b_im = p['s5_b_re'][j], p['s5_b_im'][j]
    bbr = fr[..., None] * b_re - fi[..., None] * b_im
    bbi = fr[..., None] * b_im + fi[..., None] * b_re
    eye = jnp.eye(S5_TILE_GROUPS, dtype=F32)

    def in_blocks(t):
        t = t.reshape(2, S5_TILES, S5_TILE_GROUPS, S5_STATE, S5_GROUP)
        t = jnp.einsum('dtgpc,gh->dtgchp', t, eye)
        return t.reshape(2, S5_TILES, S5_TILE_CH, S5_TILE_ST)

    def out_blocks(t):
        t = t.reshape(2, S5_TILES, S5_TILE_GROUPS, S5_GROUP, S5_STATE)
        t = jnp.einsum('dtgcp,gh->dtgphc', t, eye)
        return t.reshape(2, S5_TILES, S5_TILE_ST, S5_TILE_CH)

    bmat = jnp.concatenate([in_blocks(bbr), in_blocks(bbi)], axis=-1).astype(BF16)
    cmat = jnp.concatenate([out_blocks(p['s5_c_re'][j]), -out_blocks(p['s5_c_im'][j])], axis=-2).astype(BF16)
    seg_mag = jnp.exp(lam_re * dt * S5_SEG)
    pr, pi = seg_mag * jnp.cos(lam_im * dt * S5_SEG), seg_mag * jnp.sin(lam_im * dt * S5_SEG)
    flat = lambda t: t.reshape(2, S5_TILES, 1, S5_TILE_ST)
    amat = jnp.concatenate([flat(ar), flat(ai), flat(pr), flat(pi),
                            jnp.zeros((2, S5_TILES, 4, S5_TILE_ST), F32)], axis=2)
    return bmat, cmat, amat


def _s5_glu_kernel(yf_ref, yb_ref, u_ref, d_ref, w_ref, b_ref, o_ref):
    y = yf_ref[0] + yb_ref[0] + d_ref[...] * u_ref[...]
    g = _gelu_tanh(y)
    o_ref[0] = g * _sigmoid(_dot(g, w_ref[...]) + b_ref[...])


def _s5_glu(y_t, u_t, s5_d, w_glu, b_glu, bsz, nseg):
    tm = S5_SEG
    return pl.pallas_call(
        _s5_glu_kernel,
        grid=(bsz, nseg),
        in_specs=[pl.BlockSpec((1, tm, S5_WIDTH), lambda b, i: (0, 0, b * nseg + i)),
                  pl.BlockSpec((1, tm, S5_WIDTH), lambda b, i: (1, 0, b * nseg + i)),
                  pl.BlockSpec((tm, S5_WIDTH), lambda b, i: (0, b * nseg + i)),
                  pl.BlockSpec((1, S5_WIDTH), lambda b, i: (0, 0)),
                  pl.BlockSpec((S5_WIDTH, S5_WIDTH), lambda b, i: (0, 0)),
                  pl.BlockSpec((1, S5_WIDTH), lambda b, i: (0, 0))],
        out_specs=pl.BlockSpec((1, tm, S5_WIDTH), lambda b, i: (b, i, 0)),
        out_shape=jax.ShapeDtypeStruct((bsz, nseg * tm, S5_WIDTH), F32),
        compiler_params=_cparams(("parallel", "parallel")),
        name="s5_glu",
    )(y_t, y_t, u_t, s5_d, w_glu, b_glu)


def _s5_mixer(u_t, p, j, h0r, h0i, bsz, nseg):
    rows = bsz * nseg
    bmat, cmat, amat = _s5_params(p, j)
    u3 = u_t.reshape(S5_SEG, rows, S5_WIDTH)
    if nseg == 1:
        y_t, fr, fi = _s5_scan(u3, bmat, cmat, amat, h0r, h0i)
    else:
        zero = jnp.zeros((2, bsz, nseg, S5_GROUPS * S5_STATE), F32)
        first = jnp.array([0, nseg - 1])
        seed = lambda h0: zero.at[jnp.arange(2), :, first].set(h0).reshape(2, rows, -1)
        _, fr, fi = _s5_scan(u3, bmat, cmat, amat, seed(h0r), seed(h0i))
        fr = fr.reshape(2, bsz, nseg, -1)
        fi = fi.reshape(2, bsz, nseg, -1)
        pr = amat[:, :, 2].reshape(2, 1, -1)
        pi = amat[:, :, 3].reshape(2, 1, -1)

        def chain(dr, order):
            hr, hi = (h0r[dr], h0i[dr])
            outs_r, outs_i = {}, {}
            for n_done, k in enumerate(order):
                outs_r[k], outs_i[k] = hr, hi
                if n_done == 0:
                    hr, hi = fr[dr, :, k], fi[dr, :, k]
                else:
                    hr, hi = (pr[dr] * hr - pi[dr] * hi + fr[dr, :, k],
                              pr[dr] * hi + pi[dr] * hr + fi[dr, :, k])
            st = lambda o: jnp.stack([o[k] for k in range(nseg)], axis=1)
            return st(outs_r), st(outs_i), hr, hi

        sr0, si0, er0, ei0 = chain(0, list(range(nseg)))
        sr1, si1, er1, ei1 = chain(1, list(range(nseg - 1, -1, -1)))
        start_r = jnp.stack([sr0, sr1]).reshape(2, rows, -1)
        start_i = jnp.stack([si0, si1]).reshape(2, rows, -1)
        y_t, _, _ = _s5_scan(u3, bmat, cmat, amat, start_r, start_i)
        fr = jnp.stack([er0, er1])
        fi = jnp.stack([ei0, ei1])
    ya = _s5_glu(y_t.reshape(2, S5_SEG, rows * S5_WIDTH), u_t, p['s5_d'][j][None],
                 p['s5_w_glu'][j].astype(BF16), p['s5_b_glu'][j][None], bsz, nseg)
    return ya, fr, fi


def _gdn_kernel(qkv_ref, z_ref, ab_ref, cw_ref, gp_ref, ng_ref, s0_ref, o_ref, sf_ref,
                q_s, k_s, v_s, gate_s, of_s, ob_s, st_s, uw_s, a_s):
    n = qkv_ref.shape[1]
    nc = n // GDN_CHUNK
    row = lax.broadcasted_iota(jnp.int32, (n, 1), 0)

    for blk in range(3 * GDN_HEADS):
        cols = slice(blk * GDN_DK, (blk + 1) * GDN_DK)
        hs = slice((blk % GDN_HEADS) * GDN_DK, (blk % GDN_HEADS + 1) * GDN_DK)
        x = qkv_ref[0, :, cols]
        xm = jnp.where(row == 0, 0.0, pltpu.roll(x, 1, 0))
        xp = jnp.where(row == n - 1, 0.0, pltpu.roll(x, n - 1, 0))
        y = _silu(xm * cw_ref[0:1, cols] + x * cw_ref[1:2, cols] + xp * cw_ref[2:3, cols])
        if blk < GDN_HEADS:
            q_s[:, hs] = y * lax.rsqrt(jnp.sum(y * y, axis=-1, keepdims=True) + EPS) * (GDN_DK ** -0.5)
        elif blk < 2 * GDN_HEADS:
            k_s[:, hs] = y * lax.rsqrt(jnp.sum(y * y, axis=-1, keepdims=True) + EPS)
        else:
            v_s[:, hs] = y

    ab = ab_ref[0]
    beta = _sigmoid(ab)
    g = -jnp.exp(gp_ref[0:1, :]) * _softplus(ab + gp_ref[1:2, :])
    pos = row % GDN_CHUNK
    pre, suf = g, g
    sft = 1
    while sft < GDN_CHUNK:
        pre = pre + jnp.where(pos >= sft, pltpu.roll(pre, sft, 0), 0.0)
        suf = suf + jnp.where(pos < GDN_CHUNK - sft, pltpu.roll(suf, n - sft, 0), 0.0)
        sft *= 2
    gate_s[0] = beta
    gate_s[1] = pre
    gate_s[2] = suf

    st_s[...] = s0_ref[0]
    cs = GDN_CHUNK
    pk = GDN_HEADS * cs
    ri = lax.broadcasted_iota(jnp.int32, (cs, pk), 0)
    lane_pk = lax.broadcasted_iota(jnp.int32, (cs, pk), 1)
    ci = lane_pk % cs
    eye_pk = (ri == ci).astype(F32)
    head_pk = [(lax.broadcasted_iota(jnp.int32, (1, pk), 1) // cs) == h for h in range(GDN_HEADS)]
    head_w = [(lax.broadcasted_iota(jnp.int32, (1, GDN_WIDTH), 1) // GDN_DK) == h for h in range(GDN_HEADS)]

    def block_diag(p):
        return jnp.concatenate([jnp.where(m, p, jnp.zeros_like(p)) for m in head_pk], axis=0)

    def dot3_bd(a, p):
        a_hi, a_lo = _hi_lo(a)
        p_hi, p_lo = _hi_lo(p)
        b_hi, b_lo = block_diag(p_hi), block_diag(p_lo)
        mm = functools.partial(jnp.dot, preferred_element_type=F32)
        return mm(a_hi, b_hi) + (mm(a_lo, b_hi) + mm(a_hi, b_lo))

    def lanes(cols, width):
        return jnp.concatenate([jnp.broadcast_to(c, (cs, width)) for c in cols], axis=1)

    def phase_a(it, carry):
        st = []
        for cc in range(GDN_GROUP):
            rows = pl.ds(pl.multiple_of((it * GDN_GROUP + cc) * cs, cs), cs)
            q_all, k_all, v_all = q_s[rows, :], k_s[rows, :], v_s[rows, :]
            beta_blk = gate_s[0, rows, :]
            for dr in range(2):
                gc_blk = gate_s[1 + dr, rows, :]
                lane0 = dr * GDN_HEADS
                bcols = [beta_blk[:, 8 + lane0 + h:9 + lane0 + h] for h in range(GDN_HEADS)]
                gcols = [gc_blk[:, lane0 + h:lane0 + h + 1] for h in range(GDN_HEADS)]
                st.append(dict(dr=dr, rows=rows, q=q_all, k=k_all, v=v_all, bcols=bcols, gcols=gcols,
                               incl=(ri >= ci) if dr == 0 else (ri <= ci),
                               strict=(ri > ci) if dr == 0 else (ri < ci)))
        for t in st:
            b_w = lanes(t['bcols'], GDN_DK)
            t['kb'] = t['k'] * b_w
            t['vb'] = t['v'] * b_w
            k_bd = jnp.concatenate([jnp.where(m, t['k'], 0.0) for m in head_w], axis=0)
            t['kq'] = _dot_nt(jnp.concatenate([t['kb'], t['q']], axis=0), k_bd)
        for t in st:
            gcol = lanes(t['gcols'], cs)
            grow = jnp.sum(eye_pk * gcol, axis=0, keepdims=True)
            decay = jnp.exp(jnp.where(t['incl'], gcol - grow, NEG_INF))
            t['pw'] = jnp.where(t['strict'], t['kq'][:cs] * decay, 0.0)
            a_s[t['dr'], t['rows'], :] = jnp.where(t['incl'], t['kq'][cs:] * decay, 0.0)
            t['tm'] = eye_pk - t['pw']
        for t in st:
            t['pw'] = dot3_bd(t['pw'], t['pw'])
        for rnd in range(5):
            for t in st:
                if rnd < 4:
                    r = dot3_bd(jnp.concatenate([t['pw'], t['tm']], axis=0), t['pw'])
                    t['pw'] = r[:cs]
                    t['tm'] = t['tm'] + r[cs:]
                else:
                    t['tm'] = t['tm'] + dot3_bd(t['tm'], t['pw'])
        for t in st:
            kbg = t['kb'] * lanes([jnp.exp(g) for g in t['gcols']], GDN_DK)
            for h in range(GDN_HEADS):
                hs = slice(h * GDN_DK, (h + 1) * GDN_DK)
                rhs = jnp.concatenate([t['vb'][:, hs], kbg[:, hs]], axis=1)
                uw_s[t['dr'], t['rows'], 2 * h * GDN_DK:2 * (h + 1) * GDN_DK] = _dot(
                    t['tm'][:, h * cs:(h + 1) * cs], rhs)
        return carry

    lax.fori_loop(0, nc // GDN_GROUP, phase_a, 0)

    def phase_b(c, carry):
        ch = []
        for dr in range(2):
            cidx = c if dr == 0 else nc - 1 - c
            rows = pl.ds(pl.multiple_of(cidx * cs, cs), cs)
            gc_blk = gate_s[1 + dr, rows, :]
            for h in range(GDN_HEADS):
                hs = slice(h * GDN_DK, (h + 1) * GDN_DK)
                lane = dr * GDN_HEADS + h
                gcol = gc_blk[:, lane:lane + 1]
                ch.append(dict(dr=dr, h=h, rows=rows, hs=hs, gcol=gcol,
                               gl=gcol[cs - 1:cs] if dr == 0 else gcol[0:1],
                               uw=uw_s[dr, rows, 2 * h * GDN_DK:2 * (h + 1) * GDN_DK],
                               amat=a_s[dr, rows, h * cs:(h + 1) * cs]))
        for t in ch:
            t['s'] = st_s[t['dr'], t['h']]
            lhs = jnp.concatenate([t['uw'][:, GDN_DV:], q_s[t['rows'], t['hs']] * jnp.exp(t['gcol'])], axis=0)
            t['ws'] = _dot(lhs, t['s'])
        for t in ch:
            vn = t['uw'][:, :GDN_DV] - t['ws'][:cs]
            o = t['ws'][cs:] + _dot(t['amat'], vn)
            kg = k_s[t['rows'], t['hs']] * jnp.exp(t['gl'] - t['gcol'])
            st_s[t['dr'], t['h']] = t['s'] * jnp.exp(t['gl']) + _dot_tn(kg, vn)
            if t['dr'] == 0:
                of_s[t['rows'], t['hs']] = o
            else:
                ob_s[t['rows'], t['hs']] = o
        return carry

    lax.fori_loop(0, nc, phase_b, 0)
    sf_ref[0] = st_s[...]

    z = z_ref[0]
    for h in range(GDN_HEADS):
        hs = slice(h * GDN_DV, (h + 1) * GDN_DV)
        o = of_s[:, hs] + ob_s[:, hs]
        o = o * lax.rsqrt(jnp.mean(o * o, axis=-1, keepdims=True) + EPS) * ng_ref[...]
        o_ref[0, :, hs] = o * _silu(z[:, hs])


def _gdn_mixer(qkv, z, ab, p, j, s0):
    bsz, n, _ = qkv.shape
    gp = jnp.zeros((8, 128), F32)
    gp = gp.at[0, :8].set(p['gdn_a_log'][j].reshape(8)).at[1, :8].set(p['gdn_dt_bias'][j].reshape(8))
    cw = jnp.zeros((8, 3 * GDN_WIDTH), F32).at[:3].set(p['gdn_conv_w'][j])
    return pl.pallas_call(
        _gdn_kernel,
        grid=(bsz,),
        in_specs=[pl.BlockSpec((1, n, 3 * GDN_WIDTH), lambda b: (b, 0, 0)),
                  pl.BlockSpec((1, n, GDN_WIDTH), lambda b: (b, 0, 0)),
                  pl.BlockSpec((1, n, 128), lambda b: (b, 0, 0)),
                  pl.BlockSpec((8, 3 * GDN_WIDTH), lambda b: (0, 0)),
                  pl.BlockSpec((8, 128), lambda b: (0, 0)),
                  pl.BlockSpec((1, GDN_DV), lambda b: (0, 0)),
                  pl.BlockSpec((1, 2, GDN_HEADS, GDN_DK, GDN_DV), lambda b: (b, 0, 0, 0, 0))],
        out_specs=[pl.BlockSpec((1, n, GDN_WIDTH), lambda b: (b, 0, 0)),
                   pl.BlockSpec((1, 2, GDN_HEADS, GDN_DK, GDN_DV), lambda b: (b, 0, 0, 0, 0))],
        out_shape=[jax.ShapeDtypeStruct((bsz, n, GDN_WIDTH), F32),
                   jax.ShapeDtypeStruct((bsz, 2, GDN_HEADS, GDN_DK, GDN_DV), F32)],
        scratch_shapes=[pltpu.VMEM((n, GDN_WIDTH), F32), pltpu.VMEM((n, GDN_WIDTH), F32),
                        pltpu.VMEM((n, GDN_WIDTH), F32), pltpu.VMEM((3, n, 128), F32),
                        pltpu.VMEM((n, GDN_WIDTH), F32), pltpu.VMEM((n, GDN_WIDTH), F32),
                        pltpu.VMEM((2, GDN_HEADS, GDN_DK, GDN_DV), F32),
                        pltpu.VMEM((2, n, 2 * GDN_WIDTH), F32),
                        pltpu.VMEM((2, n, GDN_HEADS * GDN_CHUNK), F32)],
        compiler_params=_cparams(("parallel",)),
        name="gdn_mixer",
    )(qkv, z, ab, cw, gp, p['gdn_norm_g'][j][None], s0)


def _proj_cd_kernel(x_ref, g_ref, sh_ref, sc_ref, w_ref, gm_ref, gain_ref, cos_ref, sin_ref,
                    qc_ref, kc_ref, vc_ref, qd_ref, kd_ref, vd_ref, *, rope):
    h = _rms_mod(x_ref[0], g_ref[...], sh_ref[0], sc_ref[0])
    y = _dot(h, w_ref[...])
    lane = lax.broadcasted_iota(jnp.int32, (1, 512), 1)
    low = (lane % 32) < 16

    def head_norm(t, gain, scale):
        w = t.shape[1]
        ms = _split_dot(t * t, gm_ref[:w, :w])
        t = t * lax.rsqrt(ms + EPS) * gain
        if rope:
            part = jnp.where(low[:, :w], pltpu.roll(t, w - 16, 1), pltpu.roll(t, 16, 1))
            t = t * cos_ref[:, :w] + part * sin_ref[:, :w]
        return t * scale if scale != 1.0 else t

    qc_ref[0] = head_norm(y[:, 0:512], gain_ref[0:1, :], ATTN_SCALE)
    kc_ref[0] = head_norm(y[:, 512:640], gain_ref[1:2, :128], 1.0)
    vc_ref[0] = y[:, 640:768]
    qd_ref[0] = head_norm(y[:, 768:1280], gain_ref[2:3, :], ATTN_SCALE)
    kd_ref[0] = head_norm(y[:, 1280:1792], gain_ref[3:4, :], 1.0)
    vd_ref[0] = y[:, 1792:2304]


def _rope_tables(n):
    rows = n // GRID_W
    row = jnp.repeat(jnp.arange(rows), GRID_W).astype(F32)
    col = jnp.tile(jnp.arange(GRID_W), rows).astype(F32)
    quarter = HEAD_DIM // 4
    inv = ROPE_THETA ** (-jnp.arange(quarter, dtype=F32) / quarter)
    ang_r = row[:, None] * inv[None, :]
    ang_c = col[:, None] * inv[None, :]
    cos = jnp.concatenate([jnp.cos(ang_r), jnp.cos(ang_r), jnp.cos(ang_c), jnp.cos(ang_c)], axis=-1)
    sin = jnp.concatenate([-jnp.sin(ang_r), jnp.sin(ang_r), -jnp.sin(ang_c), jnp.sin(ang_c)], axis=-1)
    return jnp.tile(cos, (1, 8)), jnp.tile(sin, (1, 8))


def _proj_cd(x, g, shift, scale, w, p, j, rope):
    bsz, n, d = x.shape
    tm = 256
    per_seq = shift.shape[0] > 1
    midx = (lambda b, i: (b, 0, 0)) if per_seq else (lambda b, i: (0, 0, 0))
    lane = jnp.arange(512)
    gmat = ((lane[:, None] // HEAD_DIM) == (lane[None, :] // HEAD_DIM)).astype(F32) / HEAD_DIM
    gains = jnp.zeros((8, 512), F32)
    gains = gains.at[0].set(jnp.tile(p['c_qn'][j], 8)).at[1].set(jnp.tile(p['c_kn'][j], 8))
    gains = gains.at[2].set(jnp.tile(p['d_qn'][j], 8)).at[3].set(jnp.tile(p['d_kn'][j], 8))
    if rope:
        cos, sin = _rope_tables(n)
    else:
        cos, sin = jnp.ones((tm, 512), F32), jnp.zeros((tm, 512), F32)
    tidx = (lambda b, i: (i, 0)) if rope else (lambda b, i: (0, 0))
    blk = lambda w_: pl.BlockSpec((1, tm, w_), lambda b, i: (b, i, 0))
    return pl.pallas_call(
        functools.partial(_proj_cd_kernel, rope=rope),
        grid=(bsz, n // tm),
        in_specs=[blk(d),
                  pl.BlockSpec((1, d), lambda b, i: (0, 0)),
                  pl.BlockSpec((1, 1, d), midx),
                  pl.BlockSpec((1, 1, d), midx),
                  pl.BlockSpec((d, CD_IN), lambda b, i: (0, 0)),
                  pl.BlockSpec((512, 512), lambda b, i: (0, 0)),
                  pl.BlockSpec((8, 512), lambda b, i: (0, 0)),
                  pl.BlockSpec((tm, 512), tidx),
                  pl.BlockSpec((tm, 512), tidx)],
        out_specs=[blk(512), blk(128), blk(128), blk(512), blk(512), blk(512)],
        out_shape=[jax.ShapeDtypeStruct((bsz, n, w_), F32) for w_ in (512, 128, 128, 512, 512, 512)],
        compiler_params=_cparams(("parallel", "parallel")),
        name="proj_cd",
    )(x, g, shift, scale, w, gmat.astype(BF16), gains, cos, sin)


def _probs(scores, extra=None):
    m = scores[0].max(axis=-1, keepdims=True)
    for s in scores[1:]:
        m = jnp.maximum(m, s.max(axis=-1, keepdims=True))
    if extra is not None:
        m = jnp.maximum(m, extra)
    return [jnp.exp(s - m).astype(BF16) for s in scores], m


def _attn_kernel(qc_ref, kc_ref, vc_ref, qd_ref, kd_ref, vd_ref, *rest, windowed, n_ctx, lam_scale):
    if n_ctx:
        ck_ref, cv_ref, dk_ref, dv_ref, misc_ref, sub_ref, oc_ref, od_ref = rest
    else:
        misc_ref, sub_ref, oc_ref, od_ref = rest
    tq = qc_ref.shape[1]
    n = kc_ref.shape[1]
    start = pl.program_id(1) * tq
    if windowed:
        span = tq + 2 * WINDOW
        k0 = pl.multiple_of(jnp.clip(start - WINDOW, 0, n - span), 128)
        krows = pl.ds(k0, span)
        qpos = start + lax.broadcasted_iota(jnp.int32, (tq, span), 0)
        kpos = k0 + lax.broadcasted_iota(jnp.int32, (tq, span), 1)
        ok = jnp.abs(qpos - kpos) <= WINDOW
    else:
        krows = pl.ds(0, n)
    low = lax.broadcasted_iota(jnp.int32, (1, 2 * HEAD_DIM), 1) < HEAD_DIM

    def key_ops(k):
        kr = pltpu.roll(k, HEAD_DIM, 1)
        z = jnp.zeros_like(k)
        return {(0, 0): jnp.where(low, k, z).astype(BF16), (0, 1): jnp.where(low, z, kr).astype(BF16),
                (1, 0): jnp.where(low, kr, z).astype(BF16), (1, 1): jnp.where(low, z, k).astype(BF16)}

    def with_ones(v):
        return jnp.concatenate([v, jnp.ones_like(v)], axis=1)

    def val_ops(v):
        return {key: with_ones(op) for key, op in key_ops(v).items()}

    def half_ops(k):
        z = jnp.zeros_like(k)
        return [jnp.where(low, k, z).astype(BF16), jnp.where(low, z, k).astype(BF16)]

    qc = qc_ref[0].astype(BF16)
    qd = qd_ref[0].astype(BF16)
    kc_ops = key_ops(kc_ref[0, krows, :])
    ck_ops = key_ops(ck_ref[0]) if n_ctx else None
    c_scores = []
    for h in range(C_HEADS):
        key = (h // C_GROUP, h % 2)
        q = qc[:, (h // 2) * 128:(h // 2 + 1) * 128]
        sc = [_dot_nt(q, kc_ops[key])]
        if n_ctx:
            sc.append(_dot_nt(q, ck_ops[key]))
        c_scores.append(sc)
    d_scores = []
    for h in range(D_HEADS):
        q = qd[:, h * 128:(h + 1) * 128]
        kd_ops = half_ops(kd_ref[0, :, h * 128:(h + 1) * 128])
        dk_ops = half_ops(dk_ref[0, :, h * 128:(h + 1) * 128]) if n_ctx else None
        for c in range(2):
            sc = [_dot_nt(q, kd_ops[c])]
            if n_ctx:
                sc.append(_dot_nt(q, dk_ops[c]))
            d_scores.append(sc)

    c_probs = []
    for h in range(C_HEADS):
        sc = c_scores[h]
        if windowed:
            sc[0] = jnp.where(ok, sc[0], NEG_INF)
        c_probs.append(_probs(sc, misc_ref[0:1, h:h + 1]))
    d_probs = [_probs(sc)[0] for sc in d_scores]

    vc_ops = val_ops(vc_ref[0, krows, :])
    cv_ops = val_ops(cv_ref[0]) if n_ctx else None
    for j in range(C_HEADS // 2):
        pair = None
        for h in (2 * j, 2 * j + 1):
            key = (h // C_GROUP, h % 2)
            ps, m = c_probs[h]
            r = jnp.dot(ps[0], vc_ops[key], preferred_element_type=F32)
            if n_ctx:
                r = r + jnp.dot(ps[1], cv_ops[key], preferred_element_type=F32)
            o = r[:, :128] / (r[:, 128:] + jnp.exp(misc_ref[0:1, h:h + 1] - m))
            pair = o if pair is None else pair + o
        oc_ref[0, :, j * 128:(j + 1) * 128] = pair
    lam = misc_ref[1:2, 0:1]
    for h in range(D_HEADS):
        vsl = slice(h * D_VDIM, (h + 1) * D_VDIM)
        v_op = with_ones(vd_ref[0, :, vsl]).astype(BF16)
        dv_op = with_ones(dv_ref[0, :, vsl]).astype(BF16) if n_ctx else None
        parts = []
        for c in range(2):
            ps = d_probs[2 * h + c]
            r = jnp.dot(ps[0], v_op, preferred_element_type=F32)
            if n_ctx:
                r = r + jnp.dot(ps[1], dv_op, preferred_element_type=F32)
            parts.append(r[:, :D_VDIM] / r[:, D_VDIM:])
        o = parts[0] - lam * parts[1]
        o = o * lax.rsqrt(jnp.mean(o * o, axis=-1, keepdims=True) + EPS) * sub_ref[...] * lam_scale
        od_ref[0, :, vsl] = o


def _attention(qc, kc, vc, qd, kd, vd, caches, misc, subln, lam_init, tq, windowed):
    bsz, n, _ = qc.shape
    qblk = lambda w_: pl.BlockSpec((1, tq, w_), lambda b, i: (b, i, 0))
    kblk = lambda rows, w_: pl.BlockSpec((1, rows, w_), lambda b, i: (b, 0, 0))
    in_specs = [qblk(512), kblk(n, 128), kblk(n, 128), qblk(512), kblk(n, 512), kblk(n, 512)]
    args = [qc, kc, vc, qd, kd, vd]
    n_ctx = 0
    if caches is not None:
        n_ctx = caches[0].shape[1]
        in_specs += [kblk(n_ctx, 128), kblk(n_ctx, 128), kblk(n_ctx, 512), kblk(n_ctx, 512)]
        args += list(caches)
    in_specs += [pl.BlockSpec((8, 128), lambda b, i: (0, 0)), pl.BlockSpec((1, D_VDIM), lambda b, i: (0, 0))]
    args += [misc, subln]
    return pl.pallas_call(
        functools.partial(_attn_kernel, windowed=windowed, n_ctx=n_ctx, lam_scale=1.0 - lam_init),
        grid=(bsz, n // tq),
        in_specs=in_specs,
        out_specs=[qblk(512), qblk(512)],
        out_shape=[jax.ShapeDtypeStruct((bsz, n, 512), F32), jax.ShapeDtypeStruct((bsz, n, 512), F32)],
        compiler_params=_cparams(("parallel", "parallel")),
        name="attn_win" if windowed else "attn_ctx",
    )(*args)


def _post_kernel(x_ref, xp_ref, xn_ref, ma_ref, map_ref, man_ref, mb_ref, mbp_ref, mbn_ref,
                 g1_ref, sh_ref, sc_ref, g2_ref, ng_ref, wo_ref, wup_ref, cw_ref, wdn_ref, o_ref, act_ref,
                 *, seq_len):
    rows = x_ref.shape[0]
    ext = rows + 2 * POST_HALO
    half = ma_ref.shape[1]
    xe = jnp.concatenate([xp_ref[...], x_ref[...], xn_ref[...]], axis=0)
    mae = jnp.concatenate([map_ref[...], ma_ref[...], man_ref[...]], axis=0)
    mbe = jnp.concatenate([mbp_ref[...], mb_ref[...], mbn_ref[...]], axis=0)
    x1 = xe + g1_ref[0] * (_dot(mae, wo_ref[:half, :]) + _dot(mbe, wo_ref[half:, :]))
    h = _rms_mod(x1, ng_ref[...], sh_ref[0], sc_ref[0]).astype(BF16)
    x1 = x1[POST_HALO:POST_HALO + rows]
    row0 = pl.program_id(0) * rows - POST_HALO
    pos = (row0 + lax.broadcasted_iota(jnp.int32, (ext, 1), 0)) % seq_len
    first = pos == 0
    last = pos == seq_len - 1
    for c in range(FF_CHUNKS):
        cs = slice(c * FF_CHUNK, (c + 1) * FF_CHUNK)
        a = jnp.dot(h, wup_ref[:, cs], preferred_element_type=F32)
        b = jnp.dot(h, wup_ref[:, D_FF + c * FF_CHUNK:D_FF + (c + 1) * FF_CHUNK], preferred_element_type=F32)
        am = jnp.where(first, 0.0, pltpu.roll(a, 1, 0))
        ap = jnp.where(last, 0.0, pltpu.roll(a, ext - 1, 0))
        a = am * cw_ref[0:1, cs] + a * cw_ref[1:2, cs] + ap * cw_ref[2:3, cs] + cw_ref[3:4, cs]
        act_ref[:, cs] = (_silu(a) * b)[POST_HALO:POST_HALO + rows].astype(BF16)
    ffn = jnp.dot(act_ref[...], wdn_ref[...], preferred_element_type=F32)
    o_ref[...] = x1 + g2_ref[0] * ffn


def _post(x, mix_a, mix_b, g1, sh2, sc2, g2, norm_g, w_out, wup, cw, wdn):
    bsz, seq_len, d = x.shape
    half = mix_a.shape[-1]
    rows = POST_ROWS
    total = bsz * seq_len
    nhalo = total // POST_HALO
    per_seq = g1.shape[0] > 1
    midx = (lambda i: ((i * rows) // seq_len, 0, 0)) if per_seq else (lambda i: (0, 0, 0))
    mspec = pl.BlockSpec((1, 1, d), midx)
    pidx = lambda i: (jnp.maximum(i * (rows // POST_HALO) - 1, 0), 0)
    nidx = lambda i: (jnp.minimum((i + 1) * (rows // POST_HALO), nhalo - 1), 0)
    trio = lambda w_: [pl.BlockSpec((rows, w_), lambda i: (i, 0)), pl.BlockSpec((POST_HALO, w_), pidx),
                       pl.BlockSpec((POST_HALO, w_), nidx)]
    whole = lambda shape: pl.BlockSpec(shape, lambda i: (0,) * len(shape), pipeline_mode=pl.Buffered(1))
    x2 = x.reshape(total, d)
    a2 = mix_a.reshape(total, half)
    b2 = mix_b.reshape(total, half)
    out = pl.pallas_call(
        functools.partial(_post_kernel, seq_len=seq_len),
        grid=(total // rows,),
        in_specs=trio(d) + trio(half) + trio(half) + [
            mspec, mspec, mspec, mspec,
            pl.BlockSpec((1, d), lambda i: (0, 0)),
            whole((d, d)), whole((d, 2 * D_FF)), whole((8, D_FF)), whole((D_FF, d))],
        out_specs=pl.BlockSpec((rows, d), lambda i: (i, 0)),
        out_shape=jax.ShapeDtypeStruct((total, d), F32),
        scratch_shapes=[pltpu.VMEM((rows, D_FF), BF16)],
        compiler_params=_cparams(("parallel",)),
        name="post_ffn",
    )(x2, x2, x2, a2, a2, a2, b2, b2, b2, g1, sh2, sc2, g2, norm_g, w_out, wup, cw, wdn)
    return out.reshape(bsz, seq_len, d)


def _ffn_weights(p, l):
    cw = jnp.zeros((8, D_FF), F32).at[:3].set(p['ffn_conv_w'][l]).at[3].set(p['ffn_conv_b'][l])
    return p['ffn_up'][l].astype(BF16), cw, p['ffn_down'][l].astype(BF16)


def _lambda_init(layer):
    return 0.8 - 0.6 * math.exp(-0.3 * layer)


def _trunk(x, mods, p, states, caches):
    bsz, n, d = x.shape
    nseg = n // S5_SEG
    depth = p['w_mod'].shape[0]
    news = {k: [] for k in ('s5r', 's5i', 'gdn', 'ck', 'cv', 'dk', 'dv')}
    for l in range(depth):
        j = l // 2
        sh1, sc1, g1, sh2, sc2, g2 = mods[l]
        ng1 = p['norm1_g'][l][None]
        if l % 2 == 0:
            w_in = p['w_in_ab'][j]
            w_gate = jnp.zeros((d, 128), BF16).at[:, :w_in.shape[1] - AB_MAIN].set(
                w_in[:, AB_MAIN:].astype(BF16))
            u_t, qkv, z, ab = _proj_ab(x, ng1, sh1, sc1, w_in[:, :AB_MAIN].astype(BF16), w_gate)
            if states is None:
                h0r = jnp.zeros((2, bsz, S5_GROUPS * S5_STATE), F32)
                h0i = h0r
                s0 = jnp.zeros((bsz, 2, GDN_HEADS, GDN_DK, GDN_DV), F32)
            else:
                h0r = states[0][:, j].reshape(bsz, 2, -1).transpose(1, 0, 2)
                h0i = states[1][:, j].reshape(bsz, 2, -1).transpose(1, 0, 2)
                s0 = states[2][:, j]
            ya, fr, fi = _s5_mixer(u_t, p, j, h0r, h0i, bsz, nseg)
            yb, sg = _gdn_mixer(qkv, z, ab, p, j, s0)
            mix = (ya, yb)
            w_out = p['w_out_ab'][j]
            news['s5r'].append(fr.transpose(1, 0, 2).reshape(bsz, 2, S5_GROUPS, S5_STATE))
            news['s5i'].append(fi.transpose(1, 0, 2).reshape(bsz, 2, S5_GROUPS, S5_STATE))
            news['gdn'].append(sg)
        else:
            lam_init = _lambda_init(l)
            f = lambda name: p[name][j]
            lam = (jnp.exp(jnp.sum(f('d_lq1') * f('d_lk1'))) - jnp.exp(jnp.sum(f('d_lq2') * f('d_lk2')))
                   + lam_init)
            misc = jnp.zeros((8, 128), F32).at[0, :C_HEADS].set(p['c_sink'][j]).at[1, :].set(lam)
            rope = caches is not None
            qc, kc, vc, qd, kd, vd = _proj_cd(x, ng1, sh1, sc1, p['w_in_cd'][j].astype(BF16), p, j, rope)
            if caches is None:
                mix = _attention(qc, kc, vc, qd, kd, vd, None, misc, p['d_subln'][j][None],
                                 lam_init, n, False)
            else:
                n_ctx = caches[0].shape[2]
                cc = (caches[0][:, j].reshape(bsz, n_ctx, 128), caches[1][:, j].reshape(bsz, n_ctx, 128),
                      caches[2][:, j].reshape(bsz, n_ctx, 512), caches[3][:, j].reshape(bsz, n_ctx, 512))
                mix = _attention(qc, kc, vc, qd, kd, vd, cc, misc, p['d_subln'][j][None],
                                 lam_init, Q_BLOCK, True)
            w_out = p['w_out_cd'][j]
            news['ck'].append(kc.reshape(bsz, n, C_KV_HEADS, HEAD_DIM))
            news['cv'].append(vc.reshape(bsz, n, C_KV_HEADS, HEAD_DIM))
            news['dk'].append(kd.reshape(bsz, n, D_HEADS, 2, HEAD_DIM))
            news['dv'].append(vd.reshape(bsz, n, D_HEADS, D_VDIM))
        wup, cw, wdn = _ffn_weights(p, l)
        x = _post(x, mix[0], mix[1], g1, sh2, sc2, g2, p['norm2_g'][l][None], w_out.astype(BF16),
                  wup, cw, wdn)
    return x, news


def kernel(x_prompt, x_sample, c, state_s5_re, state_s5_im, state_gdn, cache_c_k, cache_c_v, cache_d_k, cache_d_v, c_ctx, w_mod, b_mod, norm1_g, norm2_g, w_in_ab, w_out_ab, s5_lam_re, s5_lam_im, s5_log_dt, s5_b_re, s5_b_im, s5_c_re, s5_c_im, s5_d, s5_w_glu, s5_b_glu, gdn_conv_w, gdn_a_log, gdn_dt_bias, gdn_norm_g, w_in_cd, w_out_cd, c_qn, c_kn, c_sink, d_qn, d_kn, d_lq1, d_lk1, d_lq2, d_lk2, d_subln, ffn_up, ffn_conv_w, ffn_conv_b, ffn_down):
    p = dict(w_mod=w_mod, b_mod=b_mod, norm1_g=norm1_g, norm2_g=norm2_g, w_in_ab=w_in_ab, w_out_ab=w_out_ab,
             s5_lam_re=s5_lam_re, s5_lam_im=s5_lam_im, s5_log_dt=s5_log_dt, s5_b_re=s5_b_re, s5_b_im=s5_b_im,
             s5_c_re=s5_c_re, s5_c_im=s5_c_im, s5_d=s5_d, s5_w_glu=s5_w_glu, s5_b_glu=s5_b_glu,
             gdn_conv_w=gdn_conv_w, gdn_a_log=gdn_a_log, gdn_dt_bias=gdn_dt_bias, gdn_norm_g=gdn_norm_g,
             w_in_cd=w_in_cd, w_out_cd=w_out_cd, c_qn=c_qn, c_kn=c_kn, c_sink=c_sink, d_qn=d_qn, d_kn=d_kn,
             d_lq1=d_lq1, d_lk1=d_lk1, d_lq2=d_lq2, d_lk2=d_lk2, d_subln=d_subln,
             ffn_up=ffn_up, ffn_conv_w=ffn_conv_w, ffn_conv_b=ffn_conv_b, ffn_down=ffn_down)
    depth = w_mod.shape[0]
    n_dec = c.shape[0]
    mod = _modulation(jnp.concatenate([c_ctx[None], c], axis=0), w_mod, b_mod)
    split6 = lambda m: [m[:, None, k * D_MODEL:(k + 1) * D_MODEL] for k in range(6)]
    mods_ctx = [split6(mod[l, 0:1]) for l in range(depth)]
    mods_dec = [split6(mod[l, 1:1 + n_dec]) for l in range(depth)]

    y_prompt, nw = _trunk(x_prompt, mods_ctx, p, None, None)
    y_sample, _ = _trunk(x_sample, mods_dec, p, (state_s5_re, state_s5_im, state_gdn),
                         (cache_c_k, cache_c_v, cache_d_k, cache_d_v))
    st = lambda name: jnp.stack(nw[name], axis=1)
    return (y_prompt, y_sample, st('s5r'), st('s5i'), st('gdn'), st('ck'), st('cv'), st('dk'), st('dv'))
```

```python
import functools
import math

import jax
import jax.numpy as jnp
from jax import lax
from jax.experimental import pallas as pl
from jax.experimental.pallas import tpu as pltpu

F32 = jnp.float32
BF16 = jnp.bfloat16

D_MODEL = 1024
GRID_W = 64
EPS = 1e-6
NEG_INF = -1e30

S5_WIDTH = 512
S5_GROUP = 16
S5_GROUPS = 32
S5_STATE = 64
S5_TILE_GROUPS = 8
S5_TILE_CH = S5_TILE_GROUPS * S5_GROUP
S5_TILE_ST = S5_TILE_GROUPS * S5_STATE
S5_TILES = S5_GROUPS // S5_TILE_GROUPS
S5_SEG = 256
S5_SUB = 32
PROJ_AB_TOKENS = 64
S5_ROWS = 8

GDN_DK = 128
GDN_DV = 128
GDN_HEADS = 4
GDN_WIDTH = 512
GDN_CHUNK = 64
GDN_GROUP = 4

HEAD_DIM = 64
C_HEADS = 8
C_KV_HEADS = 2
C_GROUP = 4
WINDOW = 128
Q_BLOCK = 128
D_HEADS = 4
D_VDIM = 128
ATTN_SCALE = HEAD_DIM ** -0.5
ROPE_THETA = 10000.0

D_FF = 2816
FF_CHUNK = 256
FF_CHUNKS = D_FF // FF_CHUNK
POST_ROWS = 512
POST_HALO = 8

AB_MAIN = S5_WIDTH + 4 * GDN_WIDTH
AB_PAD = AB_MAIN + 128
CD_IN = 2304

VMEM_LIMIT = 56 * 1024 * 1024


def _cparams(sem):
    return pltpu.CompilerParams(dimension_semantics=sem, vmem_limit_bytes=VMEM_LIMIT)


def _sigmoid(x):
    return 1.0 / (1.0 + jnp.exp(-x))


def _silu(x):
    return x * _sigmoid(x)


def _softplus(x):
    return jnp.maximum(x, 0.0) + jnp.log(1.0 + jnp.exp(-jnp.abs(x)))


def _gelu_tanh(x):
    return 0.5 * x * (1.0 + jnp.tanh(math.sqrt(2.0 / math.pi) * (x + 0.044715 * (x * x * x))))


def _rms_mod(x, g, shift, scale):
    y = x * lax.rsqrt(jnp.mean(x * x, axis=-1, keepdims=True) + EPS)
    return (y * g) * (1.0 + scale) + shift


def _dot(a, b):
    return jnp.dot(a.astype(BF16), b.astype(BF16), preferred_element_type=F32)


def _dot_nt(a, b):
    return lax.dot_general(a.astype(BF16), b.astype(BF16), (((1,), (1,)), ((), ())),
                           preferred_element_type=F32)


def _dot_tn(a, b):
    return lax.dot_general(a.astype(BF16), b.astype(BF16), (((0,), (0,)), ((), ())),
                           preferred_element_type=F32)


def _hi_lo(a):
    hi = a.astype(BF16)
    return hi, (a - hi.astype(F32)).astype(BF16)


def _dot3(a, b):
    a_hi, a_lo = _hi_lo(a)
    b_hi, b_lo = _hi_lo(b)
    mm = functools.partial(jnp.dot, preferred_element_type=F32)
    return mm(a_hi, b_hi) + (mm(a_lo, b_hi) + mm(a_hi, b_lo))


def _split_dot(a, b_bf16):
    hi = a.astype(BF16)
    lo = (a - hi.astype(F32)).astype(BF16)
    return (jnp.dot(hi, b_bf16, preferred_element_type=F32)
            + jnp.dot(lo, b_bf16, preferred_element_type=F32))


def _mod_kernel(ct_ref, w_ref, b_ref, o_ref, *, n_rows):
    c = ct_ref[...]
    s = _silu(c)
    w = w_ref[0]
    o_ref[0] = jnp.zeros(o_ref.shape[1:], F32)
    for m in range(n_rows):
        o_ref[0, m:m + 1, :] = jnp.sum(w * s[:, m:m + 1], axis=0, keepdims=True) + b_ref[0]


def _modulation(cvecs, w_mod, b_mod):
    n, d = cvecs.shape
    depth, _, n_out = w_mod.shape
    tn = 512
    ct = jnp.zeros((d, 8), F32).at[:, :n].set(cvecs.T)
    return pl.pallas_call(
        functools.partial(_mod_kernel, n_rows=n),
        grid=(depth, n_out // tn),
        in_specs=[pl.BlockSpec((d, 8), lambda l, j: (0, 0)),
                  pl.BlockSpec((1, d, tn), lambda l, j: (l, 0, j)),
                  pl.BlockSpec((1, 1, tn), lambda l, j: (l, 0, j))],
        out_specs=pl.BlockSpec((1, 8, tn), lambda l, j: (l, 0, j)),
        out_shape=jax.ShapeDtypeStruct((depth, 8, n_out), F32),
        compiler_params=_cparams(("parallel", "parallel")),
        name="adaln_mod",
    )(ct, w_mod, b_mod.reshape(depth, 1, n_out))


def _proj_ab_kernel(x_ref, g_ref, sh_ref, sc_ref, w_ref, wg_ref, u_ref, qkv_ref, z_ref, ab_ref):
    ns, tm, d = x_ref.shape
    h = _rms_mod(x_ref[...], g_ref[...], sh_ref[...], sc_ref[...])
    h = h.reshape(ns * tm, d).astype(BF16)
    y = jnp.dot(h, w_ref[...], preferred_element_type=F32)
    for s in range(ns):
        u_ref[:, s, :] = y[s * tm:(s + 1) * tm, :S5_WIDTH]
    qkv_ref[...] = y[:, S5_WIDTH:S5_WIDTH + 3 * GDN_WIDTH].reshape(ns, tm, 3 * GDN_WIDTH)
    z_ref[...] = y[:, S5_WIDTH + 3 * GDN_WIDTH:AB_MAIN].reshape(ns, tm, GDN_WIDTH)
    ab_ref[...] = jnp.dot(h, wg_ref[...], preferred_element_type=F32).reshape(ns, tm, 128)


def _proj_ab(x, g, shift, scale, w, w_gate):
    rows, n, d = x.shape
    tm = PROJ_AB_TOKENS
    per_row = shift.shape[0] > 1
    mspec = pl.BlockSpec((S5_ROWS, 1, d), lambda r, i: (r, 0, 0)) if per_row else \
        pl.BlockSpec((1, 1, d), lambda r, i: (0, 0, 0))
    blk = lambda w_: pl.BlockSpec((S5_ROWS, tm, w_), lambda r, i: (r, i, 0))
    return pl.pallas_call(
        _proj_ab_kernel,
        grid=(rows // S5_ROWS, n // tm),
        in_specs=[blk(d),
                  pl.BlockSpec((1, d), lambda r, i: (0, 0)),
                  mspec, mspec,
                  pl.BlockSpec((d, AB_MAIN), lambda r, i: (0, 0)),
                  pl.BlockSpec((d, 128), lambda r, i: (0, 0))],
        out_specs=[pl.BlockSpec((tm, S5_ROWS, S5_WIDTH), lambda r, i: (i, r, 0)),
                   blk(3 * GDN_WIDTH), blk(GDN_WIDTH), blk(128)],
        out_shape=[jax.ShapeDtypeStruct((n, rows, S5_WIDTH), F32),
                   jax.ShapeDtypeStruct((rows, n, 3 * GDN_WIDTH), F32),
                   jax.ShapeDtypeStruct((rows, n, GDN_WIDTH), F32),
                   jax.ShapeDtypeStruct((rows, n, 128), F32)],
        compiler_params=_cparams(("parallel", "parallel")),
        name="proj_ab",
    )(x, g, shift, scale, w, w_gate)


def _s5_kernel(u_ref, bm_ref, cm_ref, a_ref, h0r_ref, h0i_ref, *rest, want_y):
    if want_y:
        y_ref, fr_ref, fi_ref, xs_ref = rest
    else:
        fr_ref, fi_ref, xs_ref = rest
    n = u_ref.shape[0]
    sub = S5_SUB
    nsub = n // sub
    st = S5_TILE_ST

    def x_proj(d, k):
        u2 = u_ref[k * sub:(k + 1) * sub].reshape(sub * S5_ROWS, S5_TILE_CH)
        xs_ref[d, k * sub:(k + 1) * sub] = _dot(u2, bm_ref[d, 0]).reshape(sub, S5_ROWS, 2 * st)

    ar = [jnp.broadcast_to(a_ref[d, 0, 0:1, :], (S5_ROWS, st)) for d in range(2)]
    ai = [jnp.broadcast_to(a_ref[d, 0, 1:2, :], (S5_ROWS, st)) for d in range(2)]
    hr = [h0r_ref[0], h0r_ref[1]]
    hi = [h0i_ref[0], h0i_ref[1]]
    x_proj(0, 0)
    x_proj(1, nsub - 1)
    written = set()
    for k in range(nsub):
        ks = (k, nsub - 1 - k)
        if k + 1 < nsub:
            x_proj(0, ks[0] + 1)
            x_proj(1, ks[1] - 1)
        for t in range(sub):
            for d in range(2):
                tt = ks[d] * sub + (t if d == 0 else sub - 1 - t)
                x = xs_ref[d, tt]
                nr = ar[d] * hr[d] - ai[d] * hi[d] + x[:, :st]
                ni = ar[d] * hi[d] + ai[d] * hr[d] + x[:, st:]
                xs_ref[d, tt] = jnp.concatenate([nr, ni], axis=-1)
                hr[d], hi[d] = nr, ni
        if want_y:
            for d in range(2):
                rows = slice(ks[d] * sub, (ks[d] + 1) * sub)
                hs = xs_ref[d, rows].reshape(sub * S5_ROWS, 2 * st)
                yv = _dot(hs, cm_ref[d, 0]).reshape(sub, S5_ROWS, S5_TILE_CH)
                if ks[d] in written:
                    y_ref[rows] += yv
                else:
                    y_ref[rows] = yv
                    written.add(ks[d])
    for d in range(2):
        fr_ref[d] = hr[d]
        fi_ref[d] = hi[d]


def _s5_scan(u_t, bmat, cmat, amat, h0r, h0i, want_y=True):
    n, rows, _ = u_t.shape
    state = lambda: pl.BlockSpec((2, S5_ROWS, S5_TILE_ST), lambda r, j: (0, r, j))
    st_shape = jax.ShapeDtypeStruct((2, rows, S5_GROUPS * S5_STATE), F32)
    out_specs = [state(), state()]
    out_shape = [st_shape, st_shape]
    if want_y:
        out_specs.insert(0, pl.BlockSpec((n, S5_ROWS, S5_TILE_CH), lambda r, j: (0, r, j)))
        out_shape.insert(0, jax.ShapeDtypeStruct((n, rows, S5_WIDTH), F32))
    return pl.pallas_call(
        functools.partial(_s5_kernel, want_y=want_y),
        grid=(rows // S5_ROWS, S5_TILES),
        in_specs=[pl.BlockSpec((n, S5_ROWS, S5_TILE_CH), lambda r, j: (0, r, j)),
                  pl.BlockSpec((2, 1, S5_TILE_CH, 2 * S5_TILE_ST), lambda r, j: (0, j, 0, 0)),
                  pl.BlockSpec((2, 1, 2 * S5_TILE_ST, S5_TILE_CH), lambda r, j: (0, j, 0, 0)),
                  pl.BlockSpec((2, 1, 8, S5_TILE_ST), lambda r, j: (0, j, 0, 0)),
                  state(), state()],
        out_specs=out_specs,
        out_shape=out_shape,
        scratch_shapes=[pltpu.VMEM((2, n, S5_ROWS, 2 * S5_TILE_ST), F32)],
        compiler_params=_cparams(("parallel", "parallel")),
        name="s5_scan" if want_y else "s5_states",
    )(u_t, bmat, cmat, amat, h0r, h0i)


def _s5_params(p, j):
    lam_re, lam_im, log_dt = p['s5_lam_re'][j], p['s5_lam_im'][j], p['s5_log_dt'][j]
    dt = jnp.exp(log_dt)[..., None]
    mag = jnp.exp(lam_re * dt)
    ar, ai = mag * jnp.cos(lam_im * dt), mag * jnp.sin(lam_im * dt)
    den = lam_re * lam_re + lam_im * lam_im
    fr = ((ar - 1.0) * lam_re + ai * lam_im) / den
    fi = (ai * lam_re - (ar - 1.0) * lam_im) / den
    b_re, b_im = p['s5_b_re'][j], p['s5_b_im'][j]
    bbr = fr[..., None] * b_re - fi[..., None] * b_im
    bbi = fr[..., None] * b_im + fi[..., None] * b_re
    eye = jnp.eye(S5_TILE_GROUPS, dtype=F32)

    def in_blocks(t):
        t = t.reshape(2, S5_TILES, S5_TILE_GROUPS, S5_STATE, S5_GROUP)
        t = jnp.einsum('dtgpc,gh->dtgchp', t, eye)
        return t.reshape(2, S5_TILES, S5_TILE_CH, S5_TILE_ST)

    def out_blocks(t):
        t = t.reshape(2, S5_TILES, S5_TILE_GROUPS, S5_GROUP, S5_STATE)
        t = jnp.einsum('dtgcp,gh->dtgphc', t, eye)
        return t.reshape(2, S5_TILES, S5_TILE_ST, S5_TILE_CH)

    bmat = jnp.concatenate([in_blocks(bbr), in_blocks(bbi)], axis=-1).astype(BF16)
    cmat = jnp.concatenate([out_blocks(p['s5_c_re'][j]), -out_blocks(p['s5_c_im'][j])], axis=-2).astype(BF16)
    seg_mag = jnp.exp(lam_re * dt * S5_SEG)
    pr, pi = seg_mag * jnp.cos(lam_im * dt * S5_SEG), seg_mag * jnp.sin(lam_im * dt * S5_SEG)
    flat = lambda t: t.reshape(2, S5_TILES, 1, S5_TILE_ST)
    amat = jnp.concatenate([flat(ar), flat(ai), flat(pr), flat(pi),
                            jnp.zeros((2, S5_TILES, 4, S5_TILE_ST), F32)], axis=2)
    return bmat, cmat, amat


def _s5_glu_kernel(y_ref, u_ref, d_ref, w_ref, b_ref, o_ref):
    tm, ns, _ = y_ref.shape
    y = jnp.concatenate([y_ref[:, s, :] + d_ref[...] * u_ref[:, s, :] for s in range(ns)], axis=0)
    g = _gelu_tanh(y)
    out = g * _sigmoid(_dot(g, w_ref[...]) + b_ref[...])
    o_ref[...] = out.reshape(ns, tm, S5_WIDTH)


def _s5_glu(y_t, u_t, s5_d, w_glu, b_glu):
    n, rows, _ = y_t.shape
    tm = PROJ_AB_TOKENS
    tblk = pl.BlockSpec((tm, S5_ROWS, S5_WIDTH), lambda r, i: (i, r, 0))
    return pl.pallas_call(
        _s5_glu_kernel,
        grid=(rows // S5_ROWS, n // tm),
        in_specs=[tblk, tblk,
                  pl.BlockSpec((1, S5_WIDTH), lambda r, i: (0, 0)),
                  pl.BlockSpec((S5_WIDTH, S5_WIDTH), lambda r, i: (0, 0)),
                  pl.BlockSpec((1, S5_WIDTH), lambda r, i: (0, 0))],
        out_specs=pl.BlockSpec((S5_ROWS, tm, S5_WIDTH), lambda r, i: (r, i, 0)),
        out_shape=jax.ShapeDtypeStruct((rows, n, S5_WIDTH), F32),
        compiler_params=_cparams(("parallel", "parallel")),
        name="s5_glu",
    )(y_t, u_t, s5_d, w_glu, b_glu)


def _s5_mixer(u3, p, j, h0r, h0i, bsz, nseg):
    rows = bsz * nseg
    bmat, cmat, amat = _s5_params(p, j)
    if nseg == 1:
        y_t, fr, fi = _s5_scan(u3, bmat, cmat, amat, h0r, h0i)
    else:
        zero = jnp.zeros((2, bsz, nseg, S5_GROUPS * S5_STATE), F32)
        first = jnp.array([0, nseg - 1])
        seed = lambda h0: zero.at[jnp.arange(2), :, first].set(h0).reshape(2, rows, -1)
        fr, fi = _s5_scan(u3, bmat, cmat, amat, seed(h0r), seed(h0i), want_y=False)
        fr = fr.reshape(2, bsz, nseg, -1)
        fi = fi.reshape(2, bsz, nseg, -1)
        pr = amat[:, :, 2].reshape(2, 1, -1)
        pi = amat[:, :, 3].reshape(2, 1, -1)

        def chain(dr, order):
            hr, hi = (h0r[dr], h0i[dr])
            outs_r, outs_i = {}, {}
            for n_done, k in enumerate(order):
                outs_r[k], outs_i[k] = hr, hi
                if n_done == 0:
                    hr, hi = fr[dr, :, k], fi[dr, :, k]
                else:
                    hr, hi = (pr[dr] * hr - pi[dr] * hi + fr[dr, :, k],
                              pr[dr] * hi + pi[dr] * hr + fi[dr, :, k])
            st = lambda o: jnp.stack([o[k] for k in range(nseg)], axis=1)
            return st(outs_r), st(outs_i), hr, hi

        sr0, si0, er0, ei0 = chain(0, list(range(nseg)))
        sr1, si1, er1, ei1 = chain(1, list(range(nseg - 1, -1, -1)))
        start_r = jnp.stack([sr0, sr1]).reshape(2, rows, -1)
        start_i = jnp.stack([si0, si1]).reshape(2, rows, -1)
        y_t, _, _ = _s5_scan(u3, bmat, cmat, amat, start_r, start_i)
        fr = jnp.stack([er0, er1])
        fi = jnp.stack([ei0, ei1])
    ya = _s5_glu(y_t, u3, p['s5_d'][j][None], p['s5_w_glu'][j].astype(BF16), p['s5_b_glu'][j][None])
    return ya, fr, fi


def _gdn_kernel(qkv_ref, z_ref, ab_ref, cw_ref, gp_ref, ng_ref, s0_ref, o_ref, sf_ref,
                q_s, k_s, v_s, gate_s, of_s, ob_s, st_s, uw_s, a_s):
    n = qkv_ref.shape[1]
    nc = n // GDN_CHUNK
    row = lax.broadcasted_iota(jnp.int32, (n, 1), 0)

    for blk in range(3 * GDN_HEADS):
        cols = slice(blk * GDN_DK, (blk + 1) * GDN_DK)
        hs = slice((blk % GDN_HEADS) * GDN_DK, (blk % GDN_HEADS + 1) * GDN_DK)
        x = qkv_ref[0, :, cols]
        xm = jnp.where(row == 0, 0.0, pltpu.roll(x, 1, 0))
        xp = jnp.where(row == n - 1, 0.0, pltpu.roll(x, n - 1, 0))
        y = _silu(xm * cw_ref[0:1, cols] + x * cw_ref[1:2, cols] + xp * cw_ref[2:3, cols])
        if blk < GDN_HEADS:
            q_s[:, hs] = y * lax.rsqrt(jnp.sum(y * y, axis=-1, keepdims=True) + EPS) * (GDN_DK ** -0.5)
        elif blk < 2 * GDN_HEADS:
            k_s[:, hs] = y * lax.rsqrt(jnp.sum(y * y, axis=-1, keepdims=True) + EPS)
        else:
            v_s[:, hs] = y

    ab = ab_ref[0]
    beta = _sigmoid(ab)
    g = -jnp.exp(gp_ref[0:1, :]) * _softplus(ab + gp_ref[1:2, :])
    pos = row % GDN_CHUNK
    pre, suf = g, g
    sft = 1
    while sft < GDN_CHUNK:
        pre = pre + jnp.where(pos >= sft, pltpu.roll(pre, sft, 0), 0.0)
        suf = suf + jnp.where(pos < GDN_CHUNK - sft, pltpu.roll(suf, n - sft, 0), 0.0)
        sft *= 2
    gate_s[0] = beta
    gate_s[1] = pre
    gate_s[2] = suf

    st_s[...] = s0_ref[0]
    cs = GDN_CHUNK
    pk = GDN_HEADS * cs
    ri = lax.broadcasted_iota(jnp.int32, (cs, pk), 0)
    lane_pk = lax.broadcasted_iota(jnp.int32, (cs, pk), 1)
    ci = lane_pk % cs
    eye_pk = (ri == ci).astype(F32)
    head_pk = [(lax.broadcasted_iota(jnp.int32, (1, pk), 1) // cs) == h for h in range(GDN_HEADS)]
    head_w = [(lax.broadcasted_iota(jnp.int32, (1, GDN_WIDTH), 1) // GDN_DK) == h for h in range(GDN_HEADS)]

    def block_diag(p):
        return jnp.concatenate([jnp.where(m, p, jnp.zeros_like(p)) for m in head_pk], axis=0)

    def dot3_bd(a, p):
        a_hi, a_lo = _hi_lo(a)
        p_hi, p_lo = _hi_lo(p)
        b_hi, b_lo = block_diag(p_hi), block_diag(p_lo)
        mm = functools.partial(jnp.dot, preferred_element_type=F32)
        return mm(a_hi, b_hi) + (mm(a_lo, b_hi) + mm(a_hi, b_lo))

    def lanes(cols, width):
        return jnp.concatenate([jnp.broadcast_to(c, (cs, width)) for c in cols], axis=1)

    def phase_a(it, carry):
        st = []
        for cc in range(GDN_GROUP):
            rows = pl.ds(pl.multiple_of((it * GDN_GROUP + cc) * cs, cs), cs)
            q_all, k_all, v_all = q_s[rows, :], k_s[rows, :], v_s[rows, :]
            beta_blk = gate_s[0, rows, :]
            for dr in range(2):
                gc_blk = gate_s[1 + dr, rows, :]
                lane0 = dr * GDN_HEADS
                bcols = [beta_blk[:, 8 + lane0 + h:9 + lane0 + h] for h in range(GDN_HEADS)]
                gcols = [gc_blk[:, lane0 + h:lane0 + h + 1] for h in range(GDN_HEADS)]
                st.append(dict(dr=dr, rows=rows, q=q_all, k=k_all, v=v_all, bcols=bcols, gcols=gcols,
                               incl=(ri >= ci) if dr == 0 else (ri <= ci),
                               strict=(ri > ci) if dr == 0 else (ri < ci)))
        for t in st:
            b_w = lanes(t['bcols'], GDN_DK)
            t['kb'] = t['k'] * b_w
            t['vb'] = t['v'] * b_w
            k_bd = jnp.concatenate([jnp.where(m, t['k'], 0.0) for m in head_w], axis=0)
            t['kq'] = _dot_nt(jnp.concatenate([t['kb'], t['q']], axis=0), k_bd)
        for t in st:
            gcol = lanes(t['gcols'], cs)
            grow = jnp.sum(eye_pk * gcol, axis=0, keepdims=True)
            decay = jnp.exp(jnp.where(t['incl'], gcol - grow, NEG_INF))
            t['pw'] = jnp.where(t['strict'], t['kq'][:cs] * decay, 0.0)
            a_s[t['dr'], t['rows'], :] = jnp.where(t['incl'], t['kq'][cs:] * decay, 0.0)
            t['tm'] = eye_pk - t['pw']
        for t in st:
            t['pw'] = dot3_bd(t['pw'], t['pw'])
        for rnd in range(5):
            for t in st:
                if rnd < 4:
                    r = dot3_bd(jnp.concatenate([t['pw'], t['tm']], axis=0), t['pw'])
                    t['pw'] = r[:cs]
                    t['tm'] = t['tm'] + r[cs:]
                else:
                    t['tm'] = t['tm'] + dot3_bd(t['tm'], t['pw'])
        for t in st:
            kbg = t['kb'] * lanes([jnp.exp(g) for g in t['gcols']], GDN_DK)
            for h in range(GDN_HEADS):
                hs = slice(h * GDN_DK, (h + 1) * GDN_DK)
                rhs = jnp.concatenate([t['vb'][:, hs], kbg[:, hs]], axis=1)
                uw_s[t['dr'], t['rows'], 2 * h * GDN_DK:2 * (h + 1) * GDN_DK] = _dot(
                    t['tm'][:, h * cs:(h + 1) * cs], rhs)
        return carry

    lax.fori_loop(0, nc // GDN_GROUP, phase_a, 0)

    def phase_b(c, carry):
        ch = []
        for dr in range(2):
            cidx = c if dr == 0 else nc - 1 - c
            rows = pl.ds(pl.multiple_of(cidx * cs, cs), cs)
            gc_blk = gate_s[1 + dr, rows, :]
            for h in range(GDN_HEADS):
                hs = slice(h * GDN_DK, (h + 1) * GDN_DK)
                lane = dr * GDN_HEADS + h
                gcol = gc_blk[:, lane:lane + 1]
                ch.append(dict(dr=dr, h=h, rows=rows, hs=hs, gcol=gcol,
                               gl=gcol[cs - 1:cs] if dr == 0 else gcol[0:1],
                               uw=uw_s[dr, rows, 2 * h * GDN_DK:2 * (h + 1) * GDN_DK],
                               amat=a_s[dr, rows, h * cs:(h + 1) * cs]))
        for t in ch:
            t['s'] = st_s[t['dr'], t['h']]
            lhs = jnp.concatenate([t['uw'][:, GDN_DV:], q_s[t['rows'], t['hs']] * jnp.exp(t['gcol'])], axis=0)
            t['ws'] = _dot(lhs, t['s'])
        for t in ch:
            vn = t['uw'][:, :GDN_DV] - t['ws'][:cs]
            o = t['ws'][cs:] + _dot(t['amat'], vn)
            kg = k_s[t['rows'], t['hs']] * jnp.exp(t['gl'] - t['gcol'])
            st_s[t['dr'], t['h']] = t['s'] * jnp.exp(t['gl']) + _dot_tn(kg, vn)
            if t['dr'] == 0:
                of_s[t['rows'], t['hs']] = o
            else:
                ob_s[t['rows'], t['hs']] = o
        return carry

    lax.fori_loop(0, nc, phase_b, 0)
    sf_ref[0] = st_s[...]

    z = z_ref[0]
    for h in range(GDN_HEADS):
        hs = slice(h * GDN_DV, (h + 1) * GDN_DV)
        o = of_s[:, hs] + ob_s[:, hs]
        o = o * lax.rsqrt(jnp.mean(o * o, axis=-1, keepdims=True) + EPS) * ng_ref[...]
        o_ref[0, :, hs] = o * _silu(z[:, hs])


def _gdn_mixer(qkv, z, ab, p, j, s0):
    bsz, n, _ = qkv.shape
    gp = jnp.zeros((8, 128), F32)
    gp = gp.at[0, :8].set(p['gdn_a_log'][j].reshape(8)).at[1, :8].set(p['gdn_dt_bias'][j].reshape(8))
    cw = jnp.zeros((8, 3 * GDN_WIDTH), F32).at[:3].set(p['gdn_conv_w'][j])
    return pl.pallas_call(
        _gdn_kernel,
        grid=(bsz,),
        in_specs=[pl.BlockSpec((1, n, 3 * GDN_WIDTH), lambda b: (b, 0, 0)),
                  pl.BlockSpec((1, n, GDN_WIDTH), lambda b: (b, 0, 0)),
                  pl.BlockSpec((1, n, 128), lambda b: (b, 0, 0)),
                  pl.BlockSpec((8, 3 * GDN_WIDTH), lambda b: (0, 0)),
                  pl.BlockSpec((8, 128), lambda b: (0, 0)),
                  pl.BlockSpec((1, GDN_DV), lambda b: (0, 0)),
                  pl.BlockSpec((1, 2, GDN_HEADS, GDN_DK, GDN_DV), lambda b: (b, 0, 0, 0, 0))],
        out_specs=[pl.BlockSpec((1, n, GDN_WIDTH), lambda b: (b, 0, 0)),
                   pl.BlockSpec((1, 2, GDN_HEADS, GDN_DK, GDN_DV), lambda b: (b, 0, 0, 0, 0))],
        out_shape=[jax.ShapeDtypeStruct((bsz, n, GDN_WIDTH), F32),
                   jax.ShapeDtypeStruct((bsz, 2, GDN_HEADS, GDN_DK, GDN_DV), F32)],
        scratch_shapes=[pltpu.VMEM((n, GDN_WIDTH), F32), pltpu.VMEM((n, GDN_WIDTH), F32),
                        pltpu.VMEM((n, GDN_WIDTH), F32), pltpu.VMEM((3, n, 128), F32),
                        pltpu.VMEM((n, GDN_WIDTH), F32), pltpu.VMEM((n, GDN_WIDTH), F32),
                        pltpu.VMEM((2, GDN_HEADS, GDN_DK, GDN_DV), F32),
                        pltpu.VMEM((2, n, 2 * GDN_WIDTH), F32),
                        pltpu.VMEM((2, n, GDN_HEADS * GDN_CHUNK), F32)],
        compiler_params=_cparams(("parallel",)),
        name="gdn_mixer",
    )(qkv, z, ab, cw, gp, p['gdn_norm_g'][j][None], s0)


def _proj_cd_kernel(x_ref, g_ref, sh_ref, sc_ref, w_ref, gm_ref, gain_ref, cos_ref, sin_ref,
                    qc_ref, kc_ref, vc_ref, qd_ref, kd_ref, vd_ref, *, rope):
    h = _rms_mod(x_ref[0], g_ref[...], sh_ref[0], sc_ref[0])
    y = _dot(h, w_ref[...])
    lane = lax.broadcasted_iota(jnp.int32, (1, 512), 1)
    low = (lane % 32) < 16

    def head_norm(t, gain, scale):
        w = t.shape[1]
        ms = _split_dot(t * t, gm_ref[:w, :w])
        t = t * lax.rsqrt(ms + EPS) * gain
        if rope:
            part = jnp.where(low[:, :w], pltpu.roll(t, w - 16, 1), pltpu.roll(t, 16, 1))
            t = t * cos_ref[:, :w] + part * sin_ref[:, :w]
        return t * scale if scale != 1.0 else t

    qc_ref[0] = head_norm(y[:, 0:512], gain_ref[0:1, :], ATTN_SCALE)
    kc_ref[0] = head_norm(y[:, 512:640], gain_ref[1:2, :128], 1.0)
    vc_ref[0] = y[:, 640:768]
    qd_ref[0] = head_norm(y[:, 768:1280], gain_ref[2:3, :], ATTN_SCALE)
    kd_ref[0] = head_norm(y[:, 1280:1792], gain_ref[3:4, :], 1.0)
    vd_ref[0] = y[:, 1792:2304]


def _rope_tables(n):
    rows = n // GRID_W
    row = jnp.repeat(jnp.arange(rows), GRID_W).astype(F32)
    col = jnp.tile(jnp.arange(GRID_W), rows).astype(F32)
    quarter = HEAD_DIM // 4
    inv = ROPE_THETA ** (-jnp.arange(quarter, dtype=F32) / quarter)
    ang_r = row[:, None] * inv[None, :]
    ang_c = col[:, None] * inv[None, :]
    cos = jnp.concatenate([jnp.cos(ang_r), jnp.cos(ang_r), jnp.cos(ang_c), jnp.cos(ang_c)], axis=-1)
    sin = jnp.concatenate([-jnp.sin(ang_r), jnp.sin(ang_r), -jnp.sin(ang_c), jnp.sin(ang_c)], axis=-1)
    return jnp.tile(cos, (1, 8)), jnp.tile(sin, (1, 8))


def _proj_cd(x, g, shift, scale, w, p, j, rope):
    bsz, n, d = x.shape
    tm = 256
    per_seq = shift.shape[0] > 1
    midx = (lambda b, i: (b, 0, 0)) if per_seq else (lambda b, i: (0, 0, 0))
    lane = jnp.arange(512)
    gmat = ((lane[:, None] // HEAD_DIM) == (lane[None, :] // HEAD_DIM)).astype(F32) / HEAD_DIM
    gains = jnp.zeros((8, 512), F32)
    gains = gains.at[0].set(jnp.tile(p['c_qn'][j], 8)).at[1].set(jnp.tile(p['c_kn'][j], 8))
    gains = gains.at[2].set(jnp.tile(p['d_qn'][j], 8)).at[3].set(jnp.tile(p['d_kn'][j], 8))
    if rope:
        cos, sin = _rope_tables(n)
    else:
        cos, sin = jnp.ones((tm, 512), F32), jnp.zeros((tm, 512), F32)
    tidx = (lambda b, i: (i, 0)) if rope else (lambda b, i: (0, 0))
    blk = lambda w_: pl.BlockSpec((1, tm, w_), lambda b, i: (b, i, 0))
    return pl.pallas_call(
        functools.partial(_proj_cd_kernel, rope=rope),
        grid=(bsz, n // tm),
        in_specs=[blk(d),
                  pl.BlockSpec((1, d), lambda b, i: (0, 0)),
                  pl.BlockSpec((1, 1, d), midx),
                  pl.BlockSpec((1, 1, d), midx),
                  pl.BlockSpec((d, CD_IN), lambda b, i: (0, 0)),
                  pl.BlockSpec((512, 512), lambda b, i: (0, 0)),
                  pl.BlockSpec((8, 512), lambda b, i: (0, 0)),
                  pl.BlockSpec((tm, 512), tidx),
                  pl.BlockSpec((tm, 512), tidx)],
        out_specs=[blk(512), blk(128), blk(128), blk(512), blk(512), blk(512)],
        out_shape=[jax.ShapeDtypeStruct((bsz, n, w_), F32) for w_ in (512, 128, 128, 512, 512, 512)],
        compiler_params=_cparams(("parallel", "parallel")),
        name="proj_cd",
    )(x, g, shift, scale, w, gmat.astype(BF16), gains, cos, sin)


def _probs(scores, extra=None):
    m = scores[0].max(axis=-1, keepdims=True)
    for s in scores[1:]:
        m = jnp.maximum(m, s.max(axis=-1, keepdims=True))
    if extra is not None:
        m = jnp.maximum(m, extra)
    return [jnp.exp(s - m).astype(BF16) for s in scores], m


def _attn_kernel(qc_ref, kc_ref, vc_ref, qd_ref, kd_ref, vd_ref, *rest, windowed, n_ctx, lam_scale):
    if n_ctx:
        ck_ref, cv_ref, dk_ref, dv_ref, misc_ref, sub_ref, oc_ref, od_ref = rest
    else:
        misc_ref, sub_ref, oc_ref, od_ref = rest
    tq = qc_ref.shape[1]
    n = kc_ref.shape[1]
    start = pl.program_id(1) * tq
    if windowed:
        span = tq + 2 * WINDOW
        k0 = pl.multiple_of(jnp.clip(start - WINDOW, 0, n - span), 128)
        krows = pl.ds(k0, span)
        qpos = start + lax.broadcasted_iota(jnp.int32, (tq, span), 0)
        kpos = k0 + lax.broadcasted_iota(jnp.int32, (tq, span), 1)
        ok = jnp.abs(qpos - kpos) <= WINDOW
    else:
        krows = pl.ds(0, n)
    low = lax.broadcasted_iota(jnp.int32, (1, 2 * HEAD_DIM), 1) < HEAD_DIM

    def key_ops(k):
        kr = pltpu.roll(k, HEAD_DIM, 1)
        z = jnp.zeros_like(k)
        return {(0, 0): jnp.where(low, k, z).astype(BF16), (0, 1): jnp.where(low, z, kr).astype(BF16),
                (1, 0): jnp.where(low, kr, z).astype(BF16), (1, 1): jnp.where(low, z, k).astype(BF16)}

    def with_ones(v):
        return jnp.concatenate([v, jnp.ones_like(v)], axis=1)

    def val_ops(v):
        return {key: with_ones(op) for key, op in key_ops(v).items()}

    def half_ops(k):
        z = jnp.zeros_like(k)
        return [jnp.where(low, k, z).astype(BF16), jnp.where(low, z, k).astype(BF16)]

    qc = qc_ref[0].astype(BF16)
    qd = qd_ref[0].astype(BF16)
    kc_ops = key_ops(kc_ref[0, krows, :])
    ck_ops = key_ops(ck_ref[0]) if n_ctx else None
    c_scores = []
    for h in range(C_HEADS):
        key = (h // C_GROUP, h % 2)
        q = qc[:, (h // 2) * 128:(h // 2 + 1) * 128]
        sc = [_dot_nt(q, kc_ops[key])]
        if n_ctx:
            sc.append(_dot_nt(q, ck_ops[key]))
        c_scores.append(sc)
    d_scores = []
    for h in range(D_HEADS):
        q = qd[:, h * 128:(h + 1) * 128]
        kd_ops = half_ops(kd_ref[0, :, h * 128:(h + 1) * 128])
        dk_ops = half_ops(dk_ref[0, :, h * 128:(h + 1) * 128]) if n_ctx else None
        for c in range(2):
            sc = [_dot_nt(q, kd_ops[c])]
            if n_ctx:
                sc.append(_dot_nt(q, dk_ops[c]))
            d_scores.append(sc)

    c_probs = []
    for h in range(C_HEADS):
        sc = c_scores[h]
        if windowed:
            sc[0] = jnp.where(ok, sc[0], NEG_INF)
        c_probs.append(_probs(sc, misc_ref[0:1, h:h + 1]))
    d_probs = [_probs(sc)[0] for sc in d_scores]

    vc_ops = val_ops(vc_ref[0, krows, :])
    cv_ops = val_ops(cv_ref[0]) if n_ctx else None
    for j in range(C_HEADS // 2):
        pair = None
        for h in (2 * j, 2 * j + 1):
            key = (h // C_GROUP, h % 2)
            ps, m = c_probs[h]
            r = jnp.dot(ps[0], vc_ops[key], preferred_element_type=F32)
            if n_ctx:
                r = r + jnp.dot(ps[1], cv_ops[key], preferred_element_type=F32)
            o = r[:, :128] / (r[:, 128:] + jnp.exp(misc_ref[0:1, h:h + 1] - m))
            pair = o if pair is None else pair + o
        oc_ref[0, :, j * 128:(j + 1) * 128] = pair
    lam = misc_ref[1:2, 0:1]
    for h in range(D_HEADS):
        vsl = slice(h * D_VDIM, (h + 1) * D_VDIM)
        v_op = with_ones(vd_ref[0, :, vsl]).astype(BF16)
        dv_op = with_ones(dv_ref[0, :, vsl]).astype(BF16) if n_ctx else None
        parts = []
        for c in range(2):
            ps = d_probs[2 * h + c]
            r = jnp.dot(ps[0], v_op, preferred_element_type=F32)
            if n_ctx:
                r = r + jnp.dot(ps[1], dv_op, preferred_element_type=F32)
            parts.append(r[:, :D_VDIM] / r[:, D_VDIM:])
        o = parts[0] - lam * parts[1]
        o = o * lax.rsqrt(jnp.mean(o * o, axis=-1, keepdims=True) + EPS) * sub_ref[...] * lam_scale
        od_ref[0, :, vsl] = o


def _attention(qc, kc, vc, qd, kd, vd, caches, misc, subln, lam_init, tq, windowed):
    bsz, n, _ = qc.shape
    qblk = lambda w_: pl.BlockSpec((1, tq, w_), lambda b, i: (b, i, 0))
    kblk = lambda rows, w_: pl.BlockSpec((1, rows, w_), lambda b, i: (b, 0, 0))
    in_specs = [qblk(512), kblk(n, 128), kblk(n, 128), qblk(512), kblk(n, 512), kblk(n, 512)]
    args = [qc, kc, vc, qd, kd, vd]
    n_ctx = 0
    if caches is not None:
        n_ctx = caches[0].shape[1]
        in_specs += [kblk(n_ctx, 128), kblk(n_ctx, 128), kblk(n_ctx, 512), kblk(n_ctx, 512)]
        args += list(caches)
    in_specs += [pl.BlockSpec((8, 128), lambda b, i: (0, 0)), pl.BlockSpec((1, D_VDIM), lambda b, i: (0, 0))]
    args += [misc, subln]
    return pl.pallas_call(
        functools.partial(_attn_kernel, windowed=windowed, n_ctx=n_ctx, lam_scale=1.0 - lam_init),
        grid=(bsz, n // tq),
        in_specs=in_specs,
        out_specs=[qblk(512), qblk(512)],
        out_shape=[jax.ShapeDtypeStruct((bsz, n, 512), F32), jax.ShapeDtypeStruct((bsz, n, 512), F32)],
        compiler_params=_cparams(("parallel", "parallel")),
        name="attn_win" if windowed else "attn_ctx",
    )(*args)


def _post_kernel(x_ref, xp_ref, xn_ref, ma_ref, map_ref, man_ref, mb_ref, mbp_ref, mbn_ref,
                 g1_ref, sh_ref, sc_ref, g2_ref, ng_ref, wo_ref, wup_ref, cw_ref, wdn_ref, o_ref, act_ref,
                 *, seq_len):
    rows = x_ref.shape[0]
    ext = rows + 2 * POST_HALO
    half = ma_ref.shape[1]
    xe = jnp.concatenate([xp_ref[...], x_ref[...], xn_ref[...]], axis=0)
    mae = jnp.concatenate([map_ref[...], ma_ref[...], man_ref[...]], axis=0)
    mbe = jnp.concatenate([mbp_ref[...], mb_ref[...], mbn_ref[...]], axis=0)
    x1 = xe + g1_ref[0] * (_dot(mae, wo_ref[:half, :]) + _dot(mbe, wo_ref[half:, :]))
    h = _rms_mod(x1, ng_ref[...], sh_ref[0], sc_ref[0]).astype(BF16)
    x1 = x1[POST_HALO:POST_HALO + rows]
    row0 = pl.program_id(0) * rows - POST_HALO
    pos = (row0 + lax.broadcasted_iota(jnp.int32, (ext, 1), 0)) % seq_len
    first = pos == 0
    last = pos == seq_len - 1
    for c in range(FF_CHUNKS):
        cs = slice(c * FF_CHUNK, (c + 1) * FF_CHUNK)
        a = jnp.dot(h, wup_ref[:, cs], preferred_element_type=F32)
        b = jnp.dot(h, wup_ref[:, D_FF + c * FF_CHUNK:D_FF + (c + 1) * FF_CHUNK], preferred_element_type=F32)
        am = jnp.where(first, 0.0, pltpu.roll(a, 1, 0))
        ap = jnp.where(last, 0.0, pltpu.roll(a, ext - 1, 0))
        a = am * cw_ref[0:1, cs] + a * cw_ref[1:2, cs] + ap * cw_ref[2:3, cs] + cw_ref[3:4, cs]
        act_ref[:, cs] = (_silu(a) * b)[POST_HALO:POST_HALO + rows].astype(BF16)
    ffn = jnp.dot(act_ref[...], wdn_ref[...], preferred_element_type=F32)
    o_ref[...] = x1 + g2_ref[0] * ffn


def _post(x, mix_a, mix_b, g1, sh2, sc2, g2, norm_g, w_out, wup, cw, wdn):
    bsz, seq_len, d = x.shape
    half = mix_a.shape[-1]
    rows = POST_ROWS
    total = bsz * seq_len
    nhalo = total // POST_HALO
    per_seq = g1.shape[0] > 1
    midx = (lambda i: ((i * rows) // seq_len, 0, 0)) if per_seq else (lambda i: (0, 0, 0))
    mspec = pl.BlockSpec((1, 1, d), midx)
    pidx = lambda i: (jnp.maximum(i * (rows // POST_HALO) - 1, 0), 0)
    nidx = lambda i: (jnp.minimum((i + 1) * (rows // POST_HALO), nhalo - 1), 0)
    trio = lambda w_: [pl.BlockSpec((rows, w_), lambda i: (i, 0)), pl.BlockSpec((POST_HALO, w_), pidx),
                       pl.BlockSpec((POST_HALO, w_), nidx)]
    whole = lambda shape: pl.BlockSpec(shape, lambda i: (0,) * len(shape), pipeline_mode=pl.Buffered(1))
    x2 = x.reshape(total, d)
    a2 = mix_a.reshape(total, half)
    b2 = mix_b.reshape(total, half)
    out = pl.pallas_call(
        functools.partial(_post_kernel, seq_len=seq_len),
        grid=(total // rows,),
        in_specs=trio(d) + trio(half) + trio(half) + [
            mspec, mspec, mspec, mspec,
            pl.BlockSpec((1, d), lambda i: (0, 0)),
            whole((d, d)), whole((d, 2 * D_FF)), whole((8, D_FF)), whole((D_FF, d))],
        out_specs=pl.BlockSpec((rows, d), lambda i: (i, 0)),
        out_shape=jax.ShapeDtypeStruct((total, d), F32),
        scratch_shapes=[pltpu.VMEM((rows, D_FF), BF16)],
        compiler_params=_cparams(("parallel",)),
        name="post_ffn",
    )(x2, x2, x2, a2, a2, a2, b2, b2, b2, g1, sh2, sc2, g2, norm_g, w_out, wup, cw, wdn)
    return out.reshape(bsz, seq_len, d)


def _ffn_weights(p, l):
    cw = jnp.zeros((8, D_FF), F32).at[:3].set(p['ffn_conv_w'][l]).at[3].set(p['ffn_conv_b'][l])
    return p['ffn_up'][l].astype(BF16), cw, p['ffn_down'][l].astype(BF16)


def _lambda_init(layer):
    return 0.8 - 0.6 * math.exp(-0.3 * layer)


def _trunk(x, mods, p, states, caches):
    bsz, n, d = x.shape
    nseg = n // S5_SEG
    depth = p['w_mod'].shape[0]
    news = {k: [] for k in ('s5r', 's5i', 'gdn', 'ck', 'cv', 'dk', 'dv')}
    for l in range(depth):
        j = l // 2
        sh1, sc1, g1, sh2, sc2, g2 = mods[l]
        ng1 = p['norm1_g'][l][None]
        if l % 2 == 0:
            w_in = p['w_in_ab'][j]
            w_gate = jnp.zeros((d, 128), BF16).at[:, :w_in.shape[1] - AB_MAIN].set(
                w_in[:, AB_MAIN:].astype(BF16))
            rows = bsz * nseg
            per_row = lambda m: jnp.repeat(m, nseg, axis=0) if m.shape[0] > 1 else m
            u_t, qkv, z, ab = _proj_ab(x.reshape(rows, S5_SEG, d), ng1, per_row(sh1), per_row(sc1),
                                       w_in[:, :AB_MAIN].astype(BF16), w_gate)
            qkv, z, ab = (t.reshape(bsz, n, t.shape[-1]) for t in (qkv, z, ab))
            if states is None:
                h0r = jnp.zeros((2, bsz, S5_GROUPS * S5_STATE), F32)
                h0i = h0r
                s0 = jnp.zeros((bsz, 2, GDN_HEADS, GDN_DK, GDN_DV), F32)
            else:
                h0r = states[0][:, j].reshape(bsz, 2, -1).transpose(1, 0, 2)
                h0i = states[1][:, j].reshape(bsz, 2, -1).transpose(1, 0, 2)
                s0 = states[2][:, j]
            ya, fr, fi = _s5_mixer(u_t, p, j, h0r, h0i, bsz, nseg)
            yb, sg = _gdn_mixer(qkv, z, ab, p, j, s0)
            mix = (ya.reshape(bsz, n, S5_WIDTH), yb)
            w_out = p['w_out_ab'][j]
            news['s5r'].append(fr.transpose(1, 0, 2).reshape(bsz, 2, S5_GROUPS, S5_STATE))
            news['s5i'].append(fi.transpose(1, 0, 2).reshape(bsz, 2, S5_GROUPS, S5_STATE))
            news['gdn'].append(sg)
        else:
            lam_init = _lambda_init(l)
            f = lambda name: p[name][j]
            lam = (jnp.exp(jnp.sum(f('d_lq1') * f('d_lk1'))) - jnp.exp(jnp.sum(f('d_lq2') * f('d_lk2')))
                   + lam_init)
            misc = jnp.zeros((8, 128), F32).at[0, :C_HEADS].set(p['c_sink'][j]).at[1, :].set(lam)
            rope = caches is not None
            qc, kc, vc, qd, kd, vd = _proj_cd(x, ng1, sh1, sc1, p['w_in_cd'][j].astype(BF16), p, j, rope)
            if caches is None:
                mix = _attention(qc, kc, vc, qd, kd, vd, None, misc, p['d_subln'][j][None],
                                 lam_init, n, False)
            else:
                n_ctx = caches[0].shape[2]
                cc = (caches[0][:, j].reshape(bsz, n_ctx, 128), caches[1][:, j].reshape(bsz, n_ctx, 128),
                      caches[2][:, j].reshape(bsz, n_ctx, 512), caches[3][:, j].reshape(bsz, n_ctx, 512))
                mix = _attention(qc, kc, vc, qd, kd, vd, cc, misc, p['d_subln'][j][None],
                                 lam_init, Q_BLOCK, True)
            w_out = p['w_out_cd'][j]
            news['ck'].append(kc.reshape(bsz, n, C_KV_HEADS, HEAD_DIM))
            news['cv'].append(vc.reshape(bsz, n, C_KV_HEADS, HEAD_DIM))
            news['dk'].append(kd.reshape(bsz, n, D_HEADS, 2, HEAD_DIM))
            news['dv'].append(vd.reshape(bsz, n, D_HEADS, D_VDIM))
        wup, cw, wdn = _ffn_weights(p, l)
        x = _post(x, mix[0], mix[1], g1, sh2, sc2, g2, p['norm2_g'][l][None], w_out.astype(BF16),
                  wup, cw, wdn)
    return x, news


def kernel(x_prompt, x_sample, c, state_s5_re, state_s5_im, state_gdn, cache_c_k, cache_c_v, cache_d_k, cache_d_v, c_ctx, w_mod, b_mod, norm1_g, norm2_g, w_in_ab, w_out_ab, s5_lam_re, s5_lam_im, s5_log_dt, s5_b_re, s5_b_im, s5_c_re, s5_c_im, s5_d, s5_w_glu, s5_b_glu, gdn_conv_w, gdn_a_log, gdn_dt_bias, gdn_norm_g, w_in_cd, w_out_cd, c_qn, c_kn, c_sink, d_qn, d_kn, d_lq1, d_lk1, d_lq2, d_lk2, d_subln, ffn_up, ffn_conv_w, ffn_conv_b, ffn_down):
    p = dict(w_mod=w_mod, b_mod=b_mod, norm1_g=norm1_g, norm2_g=norm2_g, w_in_ab=w_in_ab, w_out_ab=w_out_ab,
             s5_lam_re=s5_lam_re, s5_lam_im=s5_lam_im, s5_log_dt=s5_log_dt, s5_b_re=s5_b_re, s5_b_im=s5_b_im,
             s5_c_re=s5_c_re, s5_c_im=s5_c_im, s5_d=s5_d, s5_w_glu=s5_w_glu, s5_b_glu=s5_b_glu,
             gdn_conv_w=gdn_conv_w, gdn_a_log=gdn_a_log, gdn_dt_bias=gdn_dt_bias, gdn_norm_g=gdn_norm_g,
             w_in_cd=w_in_cd, w_out_cd=w_out_cd, c_qn=c_qn, c_kn=c_kn, c_sink=c_sink, d_qn=d_qn, d_kn=d_kn,
             d_lq1=d_lq1, d_lk1=d_lk1, d_lq2=d_lq2, d_lk2=d_lk2, d_subln=d_subln,
             ffn_up=ffn_up, ffn_conv_w=ffn_conv_w, ffn_conv_b=ffn_conv_b, ffn_down=ffn_down)
    depth = w_mod.shape[0]
    n_dec = c.shape[0]
    mod = _modulation(jnp.concatenate([c_ctx[None], c], axis=0), w_mod, b_mod)
    split6 = lambda m: [m[:, None, k * D_MODEL:(k + 1) * D_MODEL] for k in range(6)]
    mods_ctx = [split6(mod[l, 0:1]) for l in range(depth)]
    mods_dec = [split6(mod[l, 1:1 + n_dec]) for l in range(depth)]

    y_prompt, nw = _trunk(x_prompt, mods_ctx, p, None, None)
    y_sample, _ = _trunk(x_sample, mods_dec, p, (state_s5_re, state_s5_im, state_gdn),
                         (cache_c_k, cache_c_v, cache_d_k, cache_d_v))
    st = lambda name: jnp.stack(nw[name], axis=1)
    return (y_prompt, y_sample, st('s5r'), st('s5i'), st('gdn'), st('ck'), st('cv'), st('dk'), st('dv'))
```

```python
import functools
import math

import jax
import jax.numpy as jnp
from jax import lax
from jax.experimental import pallas as pl
from jax.experimental.pallas import tpu as pltpu

F32 = jnp.float32
BF16 = jnp.bfloat16

D_MODEL = 1024
GRID_W = 64
EPS = 1e-6
NEG_INF = -1e30

S5_WIDTH = 512
S5_GROUP = 16
S5_GROUPS = 32
S5_STATE = 64
S5_TILE_GROUPS = 8
S5_TILE_CH = S5_TILE_GROUPS * S5_GROUP
S5_TILE_ST = S5_TILE_GROUPS * S5_STATE
S5_TILES = S5_GROUPS // S5_TILE_GROUPS
S5_SEG = 256
S5_SUB = 32
PROJ_AB_TOKENS = 64
S5_ROWS = 8

GDN_DK = 128
GDN_DV = 128
GDN_HEADS = 4
GDN_WIDTH = 512
GDN_CHUNK = 64
GDN_GROUP = 4

HEAD_DIM = 64
C_HEADS = 8
C_KV_HEADS = 2
C_GROUP = 4
WINDOW = 128
Q_BLOCK = 128
D_HEADS = 4
D_VDIM = 128
ATTN_SCALE = HEAD_DIM ** -0.5
ROPE_THETA = 10000.0

D_FF = 2816
FF_CHUNK = 256
FF_CHUNKS = D_FF // FF_CHUNK
POST_ROWS = 512
POST_HALO = 8

AB_MAIN = S5_WIDTH + 4 * GDN_WIDTH
AB_PAD = AB_MAIN + 128
CD_IN = 2304

MOD_TK = 128
VMEM_LIMIT = 56 * 1024 * 1024


def _cparams(sem):
    return pltpu.CompilerParams(dimension_semantics=sem, vmem_limit_bytes=VMEM_LIMIT)


def _sigmoid(x):
    return 1.0 / (1.0 + jnp.exp(-x))


def _silu(x):
    return x * _sigmoid(x)


def _softplus(x):
    return jnp.maximum(x, 0.0) + jnp.log(1.0 + jnp.exp(-jnp.abs(x)))


def _gelu_tanh(x):
    return 0.5 * x * (1.0 + jnp.tanh(math.sqrt(2.0 / math.pi) * (x + 0.044715 * (x * x * x))))


def _rms_mod(x, g, shift, scale):
    y = x * lax.rsqrt(jnp.mean(x * x, axis=-1, keepdims=True) + EPS)
    return (y * g) * (1.0 + scale) + shift


def _dot(a, b):
    return jnp.dot(a.astype(BF16), b.astype(BF16), preferred_element_type=F32)


def _dot_nt(a, b):
    return lax.dot_general(a.astype(BF16), b.astype(BF16), (((1,), (1,)), ((), ())),
                           preferred_element_type=F32)


def _dot_tn(a, b):
    return lax.dot_general(a.astype(BF16), b.astype(BF16), (((0,), (0,)), ((), ())),
                           preferred_element_type=F32)


def _hi_lo(a):
    hi = a.astype(BF16)
    return hi, (a - hi.astype(F32)).astype(BF16)


def _mod_kernel(ct_ref, w_ref, b_ref, o_ref, *, n_rows):
    @pl.when(pl.program_id(1) == 0)
    def _():
        o_ref[0] = jnp.zeros(o_ref.shape[1:], F32)
        for m in range(n_rows):
            o_ref[0, m:m + 1, :] = b_ref[0]

    s = _silu(ct_ref[...])
    w = w_ref[0]
    for m in range(n_rows):
        o_ref[0, m:m + 1, :] += jnp.sum(w * s[:, m:m + 1], axis=0, keepdims=True)


def _modulation(cvecs, w_mod, b_mod):
    n, d = cvecs.shape
    depth, _, n_out = w_mod.shape
    ct = jnp.zeros((d, 8), F32).at[:, :n].set(cvecs.T)
    return pl.pallas_call(
        functools.partial(_mod_kernel, n_rows=n),
        grid=(depth, d // MOD_TK),
        in_specs=[pl.BlockSpec((MOD_TK, 8), lambda l, k: (k, 0)),
                  pl.BlockSpec((1, MOD_TK, n_out), lambda l, k: (l, k, 0)),
                  pl.BlockSpec((1, 1, n_out), lambda l, k: (l, 0, 0))],
        out_specs=pl.BlockSpec((1, 8, n_out), lambda l, k: (l, 0, 0)),
        out_shape=jax.ShapeDtypeStruct((depth, 8, n_out), F32),
        compiler_params=_cparams(("parallel", "arbitrary")),
        name="adaln_mod",
    )(ct, w_mod, b_mod.reshape(depth, 1, n_out))


def _proj_ab_kernel(x_ref, g_ref, sh_ref, sc_ref, w_ref, wg_ref, u_ref, qkv_ref, z_ref, ab_ref):
    ns, tm, d = x_ref.shape
    h = _rms_mod(x_ref[...], g_ref[...], sh_ref[...], sc_ref[...])
    h = h.reshape(ns * tm, d).astype(BF16)
    y = jnp.dot(h, w_ref[...], preferred_element_type=F32)
    for s in range(ns):
        u_ref[:, s, :] = y[s * tm:(s + 1) * tm, :S5_WIDTH]
    qkv_ref[...] = y[:, S5_WIDTH:S5_WIDTH + 3 * GDN_WIDTH].reshape(ns, tm, 3 * GDN_WIDTH)
    z_ref[...] = y[:, S5_WIDTH + 3 * GDN_WIDTH:AB_MAIN].reshape(ns, tm, GDN_WIDTH)
    ab_ref[...] = jnp.dot(h, wg_ref[...], preferred_element_type=F32).reshape(ns, tm, 128)


def _proj_ab(x, g, shift, scale, w, w_gate):
    rows, n, d = x.shape
    tm = PROJ_AB_TOKENS
    per_row = shift.shape[0] > 1
    mspec = pl.BlockSpec((S5_ROWS, 1, d), lambda r, i: (r, 0, 0)) if per_row else \
        pl.BlockSpec((1, 1, d), lambda r, i: (0, 0, 0))
    blk = lambda w_: pl.BlockSpec((S5_ROWS, tm, w_), lambda r, i: (r, i, 0))
    return pl.pallas_call(
        _proj_ab_kernel,
        grid=(rows // S5_ROWS, n // tm),
        in_specs=[blk(d),
                  pl.BlockSpec((1, d), lambda r, i: (0, 0)),
                  mspec, mspec,
                  pl.BlockSpec((d, AB_MAIN), lambda r, i: (0, 0)),
                  pl.BlockSpec((d, 128), lambda r, i: (0, 0))],
        out_specs=[pl.BlockSpec((tm, S5_ROWS, S5_WIDTH), lambda r, i: (i, r, 0)),
                   blk(3 * GDN_WIDTH), blk(GDN_WIDTH), blk(128)],
        out_shape=[jax.ShapeDtypeStruct((n, rows, S5_WIDTH), F32),
                   jax.ShapeDtypeStruct((rows, n, 3 * GDN_WIDTH), F32),
                   jax.ShapeDtypeStruct((rows, n, GDN_WIDTH), F32),
                   jax.ShapeDtypeStruct((rows, n, 128), F32)],
        compiler_params=_cparams(("parallel", "parallel")),
        name="proj_ab",
    )(x, g, shift, scale, w, w_gate)


def _s5_kernel(u_ref, bm_ref, cm_ref, a_ref, h0r_ref, h0i_ref, *rest, want_y):
    if want_y:
        y_ref, fr_ref, fi_ref, xs_ref = rest
    else:
        fr_ref, fi_ref, xs_ref = rest
    n = u_ref.shape[0]
    sub = S5_SUB
    nsub = n // sub
    st = S5_TILE_ST

    def x_proj(d, k):
        u2 = u_ref[k * sub:(k + 1) * sub].reshape(sub * S5_ROWS, S5_TILE_CH)
        xs_ref[d, k * sub:(k + 1) * sub] = _dot(u2, bm_ref[d, 0]).reshape(sub, S5_ROWS, 2 * st)

    ar = [jnp.broadcast_to(a_ref[d, 0, 0:1, :], (S5_ROWS, st)) for d in range(2)]
    ai = [jnp.broadcast_to(a_ref[d, 0, 1:2, :], (S5_ROWS, st)) for d in range(2)]
    hr = [h0r_ref[0], h0r_ref[1]]
    hi = [h0i_ref[0], h0i_ref[1]]
    x_proj(0, 0)
    x_proj(1, nsub - 1)
    written = set()
    for k in range(nsub):
        ks = (k, nsub - 1 - k)
        if k + 1 < nsub:
            x_proj(0, ks[0] + 1)
            x_proj(1, ks[1] - 1)
        for t in range(sub):
            for d in range(2):
                tt = ks[d] * sub + (t if d == 0 else sub - 1 - t)
                x = xs_ref[d, tt]
                nr = ar[d] * hr[d] - ai[d] * hi[d] + x[:, :st]
                ni = ar[d] * hi[d] + ai[d] * hr[d] + x[:, st:]
                xs_ref[d, tt] = jnp.concatenate([nr, ni], axis=-1)
                hr[d], hi[d] = nr, ni
        if want_y:
            for d in range(2):
                rows = slice(ks[d] * sub, (ks[d] + 1) * sub)
                hs = xs_ref[d, rows].reshape(sub * S5_ROWS, 2 * st)
                yv = _dot(hs, cm_ref[d, 0]).reshape(sub, S5_ROWS, S5_TILE_CH)
                if ks[d] in written:
                    y_ref[rows] += yv
                else:
                    y_ref[rows] = yv
                    written.add(ks[d])
    for d in range(2):
        fr_ref[d] = hr[d]
        fi_ref[d] = hi[d]


def _s5_scan(u_t, bmat, cmat, amat, h0r, h0i, want_y=True):
    n, rows, _ = u_t.shape
    state = lambda: pl.BlockSpec((2, S5_ROWS, S5_TILE_ST), lambda r, j: (0, r, j))
    st_shape = jax.ShapeDtypeStruct((2, rows, S5_GROUPS * S5_STATE), F32)
    out_specs = [state(), state()]
    out_shape = [st_shape, st_shape]
    if want_y:
        out_specs.insert(0, pl.BlockSpec((n, S5_ROWS, S5_TILE_CH), lambda r, j: (0, r, j)))
        out_shape.insert(0, jax.ShapeDtypeStruct((n, rows, S5_WIDTH), F32))
    return pl.pallas_call(
        functools.partial(_s5_kernel, want_y=want_y),
        grid=(rows // S5_ROWS, S5_TILES),
        in_specs=[pl.BlockSpec((n, S5_ROWS, S5_TILE_CH), lambda r, j: (0, r, j)),
                  pl.BlockSpec((2, 1, S5_TILE_CH, 2 * S5_TILE_ST), lambda r, j: (0, j, 0, 0)),
                  pl.BlockSpec((2, 1, 2 * S5_TILE_ST, S5_TILE_CH), lambda r, j: (0, j, 0, 0)),
                  pl.BlockSpec((2, 1, 8, S5_TILE_ST), lambda r, j: (0, j, 0, 0)),
                  state(), state()],
        out_specs=out_specs,
        out_shape=out_shape,
        scratch_shapes=[pltpu.VMEM((2, n, S5_ROWS, 2 * S5_TILE_ST), F32)],
        compiler_params=_cparams(("parallel", "parallel")),
        name="s5_scan" if want_y else "s5_states",
    )(u_t, bmat, cmat, amat, h0r, h0i)


def _s5_params(p, j):
    lam_re, lam_im, log_dt = p['s5_lam_re'][j], p['s5_lam_im'][j], p['s5_log_dt'][j]
    dt = jnp.exp(log_dt)[..., None]
    mag = jnp.exp(lam_re * dt)
    ar, ai = mag * jnp.cos(lam_im * dt), mag * jnp.sin(lam_im * dt)
    den = lam_re * lam_re + lam_im * lam_im
    fr = ((ar - 1.0) * lam_re + ai * lam_im) / den
    fi = (ai * lam_re - (ar - 1.0) * lam_im) / den
    b_re, b_im = p['s5_b_re'][j], p['s5_b_im'][j]
    bbr = fr[..., None] * b_re - fi[..., None] * b_im
    bbi = fr[..., None] * b_im + fi[..., None] * b_re
    eye = jnp.eye(S5_TILE_GROUPS, dtype=F32)

    def in_blocks(t):
        t = t.reshape(2, S5_TILES, S5_TILE_GROUPS, S5_STATE, S5_GROUP)
        t = jnp.einsum('dtgpc,gh->dtgchp', t, eye)
        return t.reshape(2, S5_TILES, S5_TILE_CH, S5_TILE_ST)

    def out_blocks(t):
        t = t.reshape(2, S5_TILES, S5_TILE_GROUPS, S5_GROUP, S5_STATE)
        t = jnp.einsum('dtgcp,gh->dtgphc', t, eye)
        return t.reshape(2, S5_TILES, S5_TILE_ST, S5_TILE_CH)

    bmat = jnp.concatenate([in_blocks(bbr), in_blocks(bbi)], axis=-1).astype(BF16)
    cmat = jnp.concatenate([out_blocks(p['s5_c_re'][j]), -out_blocks(p['s5_c_im'][j])], axis=-2).astype(BF16)
    seg_mag = jnp.exp(lam_re * dt * S5_SEG)
    pr, pi = seg_mag * jnp.cos(lam_im * dt * S5_SEG), seg_mag * jnp.sin(lam_im * dt * S5_SEG)
    flat = lambda t: t.reshape(2, S5_TILES, 1, S5_TILE_ST)
    amat = jnp.concatenate([flat(ar), flat(ai), flat(pr), flat(pi),
                            jnp.zeros((2, S5_TILES, 4, S5_TILE_ST), F32)], axis=2)
    return bmat, cmat, amat


def _s5_glu_kernel(y_ref, u_ref, d_ref, w_ref, b_ref, o_ref):
    tm, ns, _ = y_ref.shape
    y = jnp.concatenate([y_ref[:, s, :] + d_ref[...] * u_ref[:, s, :] for s in range(ns)], axis=0)
    g = _gelu_tanh(y)
    out = g * _sigmoid(_dot(g, w_ref[...]) + b_ref[...])
    o_ref[...] = out.reshape(ns, tm, S5_WIDTH)


def _s5_glu(y_t, u_t, s5_d, w_glu, b_glu):
    n, rows, _ = y_t.shape
    tm = PROJ_AB_TOKENS
    tblk = pl.BlockSpec((tm, S5_ROWS, S5_WIDTH), lambda r, i: (i, r, 0))
    return pl.pallas_call(
        _s5_glu_kernel,
        grid=(rows // S5_ROWS, n // tm),
        in_specs=[tblk, tblk,
                  pl.BlockSpec((1, S5_WIDTH), lambda r, i: (0, 0)),
                  pl.BlockSpec((S5_WIDTH, S5_WIDTH), lambda r, i: (0, 0)),
                  pl.BlockSpec((1, S5_WIDTH), lambda r, i: (0, 0))],
        out_specs=pl.BlockSpec((S5_ROWS, tm, S5_WIDTH), lambda r, i: (r, i, 0)),
        out_shape=jax.ShapeDtypeStruct((rows, n, S5_WIDTH), F32),
        compiler_params=_cparams(("parallel", "parallel")),
        name="s5_glu",
    )(y_t, u_t, s5_d, w_glu, b_glu)


def _s5_mixer(u3, p, j, h0r, h0i, bsz, nseg):
    rows = bsz * nseg
    bmat, cmat, amat = _s5_params(p, j)
    if nseg == 1:
        y_t, fr, fi = _s5_scan(u3, bmat, cmat, amat, h0r, h0i)
    else:
        zero = jnp.zeros((2, bsz, nseg, S5_GROUPS * S5_STATE), F32)
        first = jnp.array([0, nseg - 1])
        seed = lambda h0: zero.at[jnp.arange(2), :, first].set(h0).reshape(2, rows, -1)
        fr, fi = _s5_scan(u3, bmat, cmat, amat, seed(h0r), seed(h0i), want_y=False)
        fr = fr.reshape(2, bsz, nseg, -1)
        fi = fi.reshape(2, bsz, nseg, -1)
        pr = amat[:, :, 2].reshape(2, 1, -1)
        pi = amat[:, :, 3].reshape(2, 1, -1)

        def chain(dr, order):
            hr, hi = (h0r[dr], h0i[dr])
            outs_r, outs_i = {}, {}
            for n_done, k in enumerate(order):
                outs_r[k], outs_i[k] = hr, hi
                if n_done == 0:
                    hr, hi = fr[dr, :, k], fi[dr, :, k]
                else:
                    hr, hi = (pr[dr] * hr - pi[dr] * hi + fr[dr, :, k],
                              pr[dr] * hi + pi[dr] * hr + fi[dr, :, k])
            st = lambda o: jnp.stack([o[k] for k in range(nseg)], axis=1)
            return st(outs_r), st(outs_i), hr, hi

        sr0, si0, er0, ei0 = chain(0, list(range(nseg)))
        sr1, si1, er1, ei1 = chain(1, list(range(nseg - 1, -1, -1)))
        start_r = jnp.stack([sr0, sr1]).reshape(2, rows, -1)
        start_i = jnp.stack([si0, si1]).reshape(2, rows, -1)
        y_t, _, _ = _s5_scan(u3, bmat, cmat, amat, start_r, start_i)
        fr = jnp.stack([er0, er1])
        fi = jnp.stack([ei0, ei1])
    ya = _s5_glu(y_t, u3, p['s5_d'][j][None], p['s5_w_glu'][j].astype(BF16), p['s5_b_glu'][j][None])
    return ya, fr, fi


def _gdn_kernel(qkv_ref, z_ref, ab_ref, cw_ref, gp_ref, ng_ref, *rest, zero_init):
    if zero_init:
        s0_ref = None
        o_ref, sf_ref, q_s, k_s, v_s, gate_s, of_s, ob_s, st_s, uw_s, a_s = rest
    else:
        s0_ref, o_ref, sf_ref, q_s, k_s, v_s, gate_s, of_s, ob_s, st_s, uw_s, a_s = rest
    n = qkv_ref.shape[1]
    nc = n // GDN_CHUNK
    row = lax.broadcasted_iota(jnp.int32, (n, 1), 0)

    for blk in range(3 * GDN_HEADS):
        cols = slice(blk * GDN_DK, (blk + 1) * GDN_DK)
        hs = slice((blk % GDN_HEADS) * GDN_DK, (blk % GDN_HEADS + 1) * GDN_DK)
        x = qkv_ref[0, :, cols]
        xm = jnp.where(row == 0, 0.0, pltpu.roll(x, 1, 0))
        xp = jnp.where(row == n - 1, 0.0, pltpu.roll(x, n - 1, 0))
        y = _silu(xm * cw_ref[0:1, cols] + x * cw_ref[1:2, cols] + xp * cw_ref[2:3, cols])
        if blk < GDN_HEADS:
            q_s[:, hs] = y * lax.rsqrt(jnp.sum(y * y, axis=-1, keepdims=True) + EPS) * (GDN_DK ** -0.5)
        elif blk < 2 * GDN_HEADS:
            k_s[:, hs] = y * lax.rsqrt(jnp.sum(y * y, axis=-1, keepdims=True) + EPS)
        else:
            v_s[:, hs] = y

    ab = ab_ref[0]
    beta = _sigmoid(ab)
    g = -jnp.exp(gp_ref[0:1, :]) * _softplus(ab + gp_ref[1:2, :])
    pos = row % GDN_CHUNK
    pre, suf = g, g
    sft = 1
    while sft < GDN_CHUNK:
        pre = pre + jnp.where(pos >= sft, pltpu.roll(pre, sft, 0), 0.0)
        suf = suf + jnp.where(pos < GDN_CHUNK - sft, pltpu.roll(suf, n - sft, 0), 0.0)
        sft *= 2
    gate_s[0] = beta
    gate_s[1] = pre
    gate_s[2] = suf

    st_s[...] = jnp.zeros_like(st_s) if zero_init else s0_ref[0]
    cs = GDN_CHUNK
    pk = GDN_HEADS * cs
    ri = lax.broadcasted_iota(jnp.int32, (cs, pk), 0)
    lane_pk = lax.broadcasted_iota(jnp.int32, (cs, pk), 1)
    ci = lane_pk % cs
    eye_pk = (ri == ci).astype(F32)
    head_pk = [(lax.broadcasted_iota(jnp.int32, (1, pk), 1) // cs) == h for h in range(GDN_HEADS)]
    head_w = [(lax.broadcasted_iota(jnp.int32, (1, GDN_WIDTH), 1) // GDN_DK) == h for h in range(GDN_HEADS)]

    def block_diag(p):
        return jnp.concatenate([jnp.where(m, p, jnp.zeros_like(p)) for m in head_pk], axis=0)

    def dot3_bd(a, p):
        a_hi, a_lo = _hi_lo(a)
        p_hi, p_lo = _hi_lo(p)
        b_hi, b_lo = block_diag(p_hi), block_diag(p_lo)
        mm = functools.partial(jnp.dot, preferred_element_type=F32)
        return mm(a_hi, b_hi) + (mm(a_lo, b_hi) + mm(a_hi, b_lo))

    def lanes(cols, width):
        return jnp.concatenate([jnp.broadcast_to(c, (cs, width)) for c in cols], axis=1)

    def phase_a(it, carry):
        st = []
        for cc in range(GDN_GROUP):
            rows = pl.ds(pl.multiple_of((it * GDN_GROUP + cc) * cs, cs), cs)
            q_all, k_all, v_all = q_s[rows, :], k_s[rows, :], v_s[rows, :]
            beta_blk = gate_s[0, rows, :]
            for dr in range(2):
                gc_blk = gate_s[1 + dr, rows, :]
                lane0 = dr * GDN_HEADS
                bcols = [beta_blk[:, 8 + lane0 + h:9 + lane0 + h] for h in range(GDN_HEADS)]
                gcols = [gc_blk[:, lane0 + h:lane0 + h + 1] for h in range(GDN_HEADS)]
                st.append(dict(dr=dr, rows=rows, q=q_all, k=k_all, v=v_all, bcols=bcols, gcols=gcols,
                               incl=(ri >= ci) if dr == 0 else (ri <= ci),
                               strict=(ri > ci) if dr == 0 else (ri < ci)))
        for t in st:
            b_w = lanes(t['bcols'], GDN_DK)
            t['kb'] = t['k'] * b_w
            t['vb'] = t['v'] * b_w
            k_bd = jnp.concatenate([jnp.where(m, t['k'], 0.0) for m in head_w], axis=0)
            t['kq'] = _dot_nt(jnp.concatenate([t['kb'], t['q']], axis=0), k_bd)
        for t in st:
            gcol = lanes(t['gcols'], cs)
            grow = jnp.sum(eye_pk * gcol, axis=0, keepdims=True)
            decay = jnp.exp(jnp.where(t['incl'], gcol - grow, NEG_INF))
            t['pw'] = jnp.where(t['strict'], t['kq'][:cs] * decay, 0.0)
            a_s[t['dr'], t['rows'], :] = jnp.where(t['incl'], t['kq'][cs:] * decay, 0.0)
            t['tm'] = eye_pk - t['pw']
        for t in st:
            t['pw'] = dot3_bd(t['pw'], t['pw'])
        for rnd in range(5):
            for t in st:
                if rnd < 4:
                    r = dot3_bd(jnp.concatenate([t['pw'], t['tm']], axis=0), t['pw'])
                    t['pw'] = r[:cs]
                    t['tm'] = t['tm'] + r[cs:]
                else:
                    t['tm'] = t['tm'] + dot3_bd(t['tm'], t['pw'])
        for t in st:
            kbg = t['kb'] * lanes([jnp.exp(g) for g in t['gcols']], GDN_DK)
            for h in range(GDN_HEADS):
                hs = slice(h * GDN_DK, (h + 1) * GDN_DK)
                rhs = jnp.concatenate([t['vb'][:, hs], kbg[:, hs]], axis=1)
                uw_s[t['dr'], t['rows'], 2 * h * GDN_DK:2 * (h + 1) * GDN_DK] = _dot(
                    t['tm'][:, h * cs:(h + 1) * cs], rhs)
        return carry

    lax.fori_loop(0, nc // GDN_GROUP, phase_a, 0)

    def phase_b(c, carry):
        ch = []
        for dr in range(2):
            cidx = c if dr == 0 else nc - 1 - c
            rows = pl.ds(pl.multiple_of(cidx * cs, cs), cs)
            gc_blk = gate_s[1 + dr, rows, :]
            for h in range(GDN_HEADS):
                hs = slice(h * GDN_DK, (h + 1) * GDN_DK)
                lane = dr * GDN_HEADS + h
                gcol = gc_blk[:, lane:lane + 1]
                ch.append(dict(dr=dr, h=h, rows=rows, hs=hs, gcol=gcol,
                               gl=gcol[cs - 1:cs] if dr == 0 else gcol[0:1],
                               uw=uw_s[dr, rows, 2 * h * GDN_DK:2 * (h + 1) * GDN_DK],
                               amat=a_s[dr, rows, h * cs:(h + 1) * cs]))
        for t in ch:
            t['s'] = st_s[t['dr'], t['h']]
            lhs = jnp.concatenate([t['uw'][:, GDN_DV:], q_s[t['rows'], t['hs']] * jnp.exp(t['gcol'])], axis=0)
            t['ws'] = _dot(lhs, t['s'])
        for t in ch:
            vn = t['uw'][:, :GDN_DV] - t['ws'][:cs]
            o = t['ws'][cs:] + _dot(t['amat'], vn)
            kg = k_s[t['rows'], t['hs']] * jnp.exp(t['gl'] - t['gcol'])
            st_s[t['dr'], t['h']] = t['s'] * jnp.exp(t['gl']) + _dot_tn(kg, vn)
            if t['dr'] == 0:
                of_s[t['rows'], t['hs']] = o
            else:
                ob_s[t['rows'], t['hs']] = o
        return carry

    lax.fori_loop(0, nc, phase_b, 0)
    sf_ref[0] = st_s[...]

    z = z_ref[0]
    for h in range(GDN_HEADS):
        hs = slice(h * GDN_DV, (h + 1) * GDN_DV)
        o = of_s[:, hs] + ob_s[:, hs]
        o = o * lax.rsqrt(jnp.mean(o * o, axis=-1, keepdims=True) + EPS) * ng_ref[...]
        o_ref[0, :, hs] = o * _silu(z[:, hs])


def _gdn_mixer(qkv, z, ab, p, j, s0):
    bsz, n, _ = qkv.shape
    gp = jnp.zeros((8, 128), F32)
    gp = gp.at[0, :8].set(p['gdn_a_log'][j].reshape(8)).at[1, :8].set(p['gdn_dt_bias'][j].reshape(8))
    cw = jnp.zeros((8, 3 * GDN_WIDTH), F32).at[:3].set(p['gdn_conv_w'][j])
    return pl.pallas_call(
        functools.partial(_gdn_kernel, zero_init=s0 is None),
        grid=(bsz,),
        in_specs=[pl.BlockSpec((1, n, 3 * GDN_WIDTH), lambda b: (b, 0, 0)),
                  pl.BlockSpec((1, n, GDN_WIDTH), lambda b: (b, 0, 0)),
                  pl.BlockSpec((1, n, 128), lambda b: (b, 0, 0)),
                  pl.BlockSpec((8, 3 * GDN_WIDTH), lambda b: (0, 0)),
                  pl.BlockSpec((8, 128), lambda b: (0, 0)),
                  pl.BlockSpec((1, GDN_DV), lambda b: (0, 0))] + ([] if s0 is None else [
                      pl.BlockSpec((1, 2, GDN_HEADS, GDN_DK, GDN_DV), lambda b: (b, 0, 0, 0, 0))]),
        out_specs=[pl.BlockSpec((1, n, GDN_WIDTH), lambda b: (b, 0, 0)),
                   pl.BlockSpec((1, 2, GDN_HEADS, GDN_DK, GDN_DV), lambda b: (b, 0, 0, 0, 0))],
        out_shape=[jax.ShapeDtypeStruct((bsz, n, GDN_WIDTH), F32),
                   jax.ShapeDtypeStruct((bsz, 2, GDN_HEADS, GDN_DK, GDN_DV), F32)],
        scratch_shapes=[pltpu.VMEM((n, GDN_WIDTH), F32), pltpu.VMEM((n, GDN_WIDTH), F32),
                        pltpu.VMEM((n, GDN_WIDTH), F32), pltpu.VMEM((3, n, 128), F32),
                        pltpu.VMEM((n, GDN_WIDTH), F32), pltpu.VMEM((n, GDN_WIDTH), F32),
                        pltpu.VMEM((2, GDN_HEADS, GDN_DK, GDN_DV), F32),
                        pltpu.VMEM((2, n, 2 * GDN_WIDTH), F32),
                        pltpu.VMEM((2, n, GDN_HEADS * GDN_CHUNK), F32)],
        compiler_params=_cparams(("parallel",)),
        name="gdn_mixer",
    )(qkv, z, ab, cw, gp, p['gdn_norm_g'][j][None], *(() if s0 is None else (s0,)))


def _proj_cd_kernel(x_ref, g_ref, sh_ref, sc_ref, w_ref, gm_ref, gain_ref, cos_ref, sin_ref,
                    qc_ref, kc_ref, vc_ref, qd_ref, kd_ref, vd_ref, *, rope):
    h = _rms_mod(x_ref[0], g_ref[...], sh_ref[0], sc_ref[0])
    y = _dot(h, w_ref[...])
    lane = lax.broadcasted_iota(jnp.int32, (1, 512), 1)
    low = (lane % 32) < 16

    def head_norm(t, gain, scale):
        w = t.shape[1]
        ms = _dot(t * t, gm_ref[:w, :w])
        t = t * lax.rsqrt(ms + EPS) * gain
        if rope:
            part = jnp.where(low[:, :w], pltpu.roll(t, w - 16, 1), pltpu.roll(t, 16, 1))
            t = t * cos_ref[:, :w] + part * sin_ref[:, :w]
        return t * scale if scale != 1.0 else t

    qc_ref[0] = head_norm(y[:, 0:512], gain_ref[0:1, :], ATTN_SCALE)
    kc_ref[0] = head_norm(y[:, 512:640], gain_ref[1:2, :128], 1.0)
    vc_ref[0] = y[:, 640:768]
    qd_ref[0] = head_norm(y[:, 768:1280], gain_ref[2:3, :], ATTN_SCALE)
    kd_ref[0] = head_norm(y[:, 1280:1792], gain_ref[3:4, :], 1.0)
    vd_ref[0] = y[:, 1792:2304]


def _rope_tables(n):
    rows = n // GRID_W
    row = jnp.repeat(jnp.arange(rows), GRID_W).astype(F32)
    col = jnp.tile(jnp.arange(GRID_W), rows).astype(F32)
    quarter = HEAD_DIM // 4
    inv = ROPE_THETA ** (-jnp.arange(quarter, dtype=F32) / quarter)
    ang_r = row[:, None] * inv[None, :]
    ang_c = col[:, None] * inv[None, :]
    cos = jnp.concatenate([jnp.cos(ang_r), jnp.cos(ang_r), jnp.cos(ang_c), jnp.cos(ang_c)], axis=-1)
    sin = jnp.concatenate([-jnp.sin(ang_r), jnp.sin(ang_r), -jnp.sin(ang_c), jnp.sin(ang_c)], axis=-1)
    return jnp.tile(cos, (1, 8)), jnp.tile(sin, (1, 8))


def _proj_cd(x, g, shift, scale, w, p, j, rope):
    bsz, n, d = x.shape
    tm = 256
    per_seq = shift.shape[0] > 1
    midx = (lambda b, i: (b, 0, 0)) if per_seq else (lambda b, i: (0, 0, 0))
    lane = jnp.arange(512)
    gmat = ((lane[:, None] // HEAD_DIM) == (lane[None, :] // HEAD_DIM)).astype(F32) / HEAD_DIM
    gains = jnp.zeros((8, 512), F32)
    gains = gains.at[0].set(jnp.tile(p['c_qn'][j], 8)).at[1].set(jnp.tile(p['c_kn'][j], 8))
    gains = gains.at[2].set(jnp.tile(p['d_qn'][j], 8)).at[3].set(jnp.tile(p['d_kn'][j], 8))
    if rope:
        cos, sin = _rope_tables(n)
    else:
        cos, sin = jnp.ones((tm, 512), F32), jnp.zeros((tm, 512), F32)
    tidx = (lambda b, i: (i, 0)) if rope else (lambda b, i: (0, 0))
    blk = lambda w_: pl.BlockSpec((1, tm, w_), lambda b, i: (b, i, 0))
    return pl.pallas_call(
        functools.partial(_proj_cd_kernel, rope=rope),
        grid=(bsz, n // tm),
        in_specs=[blk(d),
                  pl.BlockSpec((1, d), lambda b, i: (0, 0)),
                  pl.BlockSpec((1, 1, d), midx),
                  pl.BlockSpec((1, 1, d), midx),
                  pl.BlockSpec((d, CD_IN), lambda b, i: (0, 0)),
                  pl.BlockSpec((512, 512), lambda b, i: (0, 0)),
                  pl.BlockSpec((8, 512), lambda b, i: (0, 0)),
                  pl.BlockSpec((tm, 512), tidx),
                  pl.BlockSpec((tm, 512), tidx)],
        out_specs=[blk(512), blk(128), blk(128), blk(512), blk(512), blk(512)],
        out_shape=[jax.ShapeDtypeStruct((bsz, n, w_), F32) for w_ in (512, 128, 128, 512, 512, 512)],
        compiler_params=_cparams(("parallel", "parallel")),
        name="proj_cd",
    )(x, g, shift, scale, w, gmat.astype(BF16), gains, cos, sin)


def _probs(scores, extra=None):
    m = scores[0].max(axis=-1, keepdims=True)
    for s in scores[1:]:
        m = jnp.maximum(m, s.max(axis=-1, keepdims=True))
    if extra is not None:
        m = jnp.maximum(m, extra)
    return [jnp.exp(s - m).astype(BF16) for s in scores], m


def _attn_kernel(qc_ref, kc_ref, vc_ref, qd_ref, kd_ref, vd_ref, *rest, windowed, n_ctx, lam_scale):
    if n_ctx:
        ck_ref, cv_ref, dk_ref, dv_ref, misc_ref, sub_ref, oc_ref, od_ref = rest
    else:
        misc_ref, sub_ref, oc_ref, od_ref = rest
    tq = qc_ref.shape[1]
    n = kc_ref.shape[1]
    start = pl.program_id(1) * tq
    if windowed:
        span = tq + 2 * WINDOW
        k0 = pl.multiple_of(jnp.clip(start - WINDOW, 0, n - span), 128)
        krows = pl.ds(k0, span)
        qpos = start + lax.broadcasted_iota(jnp.int32, (tq, span), 0)
        kpos = k0 + lax.broadcasted_iota(jnp.int32, (tq, span), 1)
        ok = jnp.abs(qpos - kpos) <= WINDOW
    else:
        krows = pl.ds(0, n)
    low = lax.broadcasted_iota(jnp.int32, (1, 2 * HEAD_DIM), 1) < HEAD_DIM

    def key_ops(k):
        kr = pltpu.roll(k, HEAD_DIM, 1)
        z = jnp.zeros_like(k)
        return {(0, 0): jnp.where(low, k, z).astype(BF16), (0, 1): jnp.where(low, z, kr).astype(BF16),
                (1, 0): jnp.where(low, kr, z).astype(BF16), (1, 1): jnp.where(low, z, k).astype(BF16)}

    def with_ones(v):
        return jnp.concatenate([v, jnp.ones_like(v)], axis=1)

    def val_ops(v):
        return {key: with_ones(op) for key, op in key_ops(v).items()}

    def half_ops(k):
        z = jnp.zeros_like(k)
        return [jnp.where(low, k, z).astype(BF16), jnp.where(low, z, k).astype(BF16)]

    qc = qc_ref[0].astype(BF16)
    qd = qd_ref[0].astype(BF16)
    kc_ops = key_ops(kc_ref[0, krows, :])
    ck_ops = key_ops(ck_ref[0]) if n_ctx else None
    c_scores = []
    for h in range(C_HEADS):
        key = (h // C_GROUP, h % 2)
        q = qc[:, (h // 2) * 128:(h // 2 + 1) * 128]
        sc = [_dot_nt(q, kc_ops[key])]
        if n_ctx:
            sc.append(_dot_nt(q, ck_ops[key]))
        c_scores.append(sc)
    d_scores = []
    for h in range(D_HEADS):
        q = qd[:, h * 128:(h + 1) * 128]
        kd_ops = half_ops(kd_ref[0, :, h * 128:(h + 1) * 128])
        dk_ops = half_ops(dk_ref[0, :, h * 128:(h + 1) * 128]) if n_ctx else None
        for c in range(2):
            sc = [_dot_nt(q, kd_ops[c])]
            if n_ctx:
                sc.append(_dot_nt(q, dk_ops[c]))
            d_scores.append(sc)

    c_probs = []
    for h in range(C_HEADS):
        sc = c_scores[h]
        if windowed:
            sc[0] = jnp.where(ok, sc[0], NEG_INF)
        c_probs.append(_probs(sc, misc_ref[0:1, h:h + 1]))
    d_probs = [_probs(sc)[0] for sc in d_scores]

    vc_ops = val_ops(vc_ref[0, krows, :])
    cv_ops = val_ops(cv_ref[0]) if n_ctx else None
    for j in range(C_HEADS // 2):
        pair = None
        for h in (2 * j, 2 * j + 1):
            key = (h // C_GROUP, h % 2)
            ps, m = c_probs[h]
            r = jnp.dot(ps[0], vc_ops[key], preferred_element_type=F32)
            if n_ctx:
                r = r + jnp.dot(ps[1], cv_ops[key], preferred_element_type=F32)
            o = r[:, :128] / (r[:, 128:] + jnp.exp(misc_ref[0:1, h:h + 1] - m))
            pair = o if pair is None else pair + o
        oc_ref[0, :, j * 128:(j + 1) * 128] = pair
    lam = misc_ref[1:2, 0:1]
    for h in range(D_HEADS):
        vsl = slice(h * D_VDIM, (h + 1) * D_VDIM)
        v_op = with_ones(vd_ref[0, :, vsl]).astype(BF16)
        dv_op = with_ones(dv_ref[0, :, vsl]).astype(BF16) if n_ctx else None
        parts = []
        for c in range(2):
            ps = d_probs[2 * h + c]
            r = jnp.dot(ps[0], v_op, preferred_element_type=F32)
            if n_ctx:
                r = r + jnp.dot(ps[1], dv_op, preferred_element_type=F32)
            parts.append(r[:, :D_VDIM] / r[:, D_VDIM:])
        o = parts[0] - lam * parts[1]
        o = o * lax.rsqrt(jnp.mean(o * o, axis=-1, keepdims=True) + EPS) * sub_ref[...] * lam_scale
        od_ref[0, :, vsl] = o


def _attention(qc, kc, vc, qd, kd, vd, caches, misc, subln, lam_init, tq, windowed):
    bsz, n, _ = qc.shape
    qblk = lambda w_: pl.BlockSpec((1, tq, w_), lambda b, i: (b, i, 0))
    kblk = lambda rows, w_: pl.BlockSpec((1, rows, w_), lambda b, i: (b, 0, 0))
    in_specs = [qblk(512), kblk(n, 128), kblk(n, 128), qblk(512), kblk(n, 512), kblk(n, 512)]
    args = [qc, kc, vc, qd, kd, vd]
    n_ctx = 0
    if caches is not None:
        n_ctx = caches[0].shape[1]
        in_specs += [kblk(n_ctx, 128), kblk(n_ctx, 128), kblk(n_ctx, 512), kblk(n_ctx, 512)]
        args += list(caches)
    in_specs += [pl.BlockSpec((8, 128), lambda b, i: (0, 0)), pl.BlockSpec((1, D_VDIM), lambda b, i: (0, 0))]
    args += [misc, subln]
    return pl.pallas_call(
        functools.partial(_attn_kernel, windowed=windowed, n_ctx=n_ctx, lam_scale=1.0 - lam_init),
        grid=(bsz, n // tq),
        in_specs=in_specs,
        out_specs=[qblk(512), qblk(512)],
        out_shape=[jax.ShapeDtypeStruct((bsz, n, 512), F32), jax.ShapeDtypeStruct((bsz, n, 512), F32)],
        compiler_params=_cparams(("parallel", "parallel")),
        name="attn_win" if windowed else "attn_ctx",
    )(*args)


def _post_kernel(x_ref, xp_ref, xn_ref, ma_ref, map_ref, man_ref, mb_ref, mbp_ref, mbn_ref,
                 g1_ref, sh_ref, sc_ref, g2_ref, ng_ref, wo_ref, wup_ref, cw_ref, wdn_ref, o_ref, act_ref,
                 *, seq_len):
    rows = x_ref.shape[0]
    ext = rows + 2 * POST_HALO
    half = ma_ref.shape[1]
    xe = jnp.concatenate([xp_ref[...], x_ref[...], xn_ref[...]], axis=0)
    mae = jnp.concatenate([map_ref[...], ma_ref[...], man_ref[...]], axis=0)
    mbe = jnp.concatenate([mbp_ref[...], mb_ref[...], mbn_ref[...]], axis=0)
    x1 = xe + g1_ref[0] * (_dot(mae, wo_ref[:half, :]) + _dot(mbe, wo_ref[half:, :]))
    h = _rms_mod(x1, ng_ref[...], sh_ref[0], sc_ref[0]).astype(BF16)
    x1 = x1[POST_HALO:POST_HALO + rows]
    row0 = pl.program_id(0) * rows - POST_HALO
    pos = (row0 + lax.broadcasted_iota(jnp.int32, (ext, 1), 0)) % seq_len
    first = pos == 0
    last = pos == seq_len - 1
    for c in range(FF_CHUNKS):
        cs = slice(c * FF_CHUNK, (c + 1) * FF_CHUNK)
        a = jnp.dot(h, wup_ref[:, cs], preferred_element_type=F32)
        b = jnp.dot(h, wup_ref[:, D_FF + c * FF_CHUNK:D_FF + (c + 1) * FF_CHUNK], preferred_element_type=F32)
        am = jnp.where(first, 0.0, pltpu.roll(a, 1, 0))
        ap = jnp.where(last, 0.0, pltpu.roll(a, ext - 1, 0))
        a = am * cw_ref[0:1, cs] + a * cw_ref[1:2, cs] + ap * cw_ref[2:3, cs] + cw_ref[3:4, cs]
        act_ref[:, cs] = (_silu(a) * b)[POST_HALO:POST_HALO + rows].astype(BF16)
    ffn = jnp.dot(act_ref[...], wdn_ref[...], preferred_element_type=F32)
    o_ref[...] = x1 + g2_ref[0] * ffn


def _post(x, mix_a, mix_b, g1, sh2, sc2, g2, norm_g, w_out, wup, cw, wdn):
    bsz, seq_len, d = x.shape
    half = mix_a.shape[-1]
    rows = POST_ROWS
    total = bsz * seq_len
    nhalo = total // POST_HALO
    per_seq = g1.shape[0] > 1
    midx = (lambda i: ((i * rows) // seq_len, 0, 0)) if per_seq else (lambda i: (0, 0, 0))
    mspec = pl.BlockSpec((1, 1, d), midx)
    pidx = lambda i: (jnp.maximum(i * (rows // POST_HALO) - 1, 0), 0)
    nidx = lambda i: (jnp.minimum((i + 1) * (rows // POST_HALO), nhalo - 1), 0)
    trio = lambda w_: [pl.BlockSpec((rows, w_), lambda i: (i, 0)), pl.BlockSpec((POST_HALO, w_), pidx),
                       pl.BlockSpec((POST_HALO, w_), nidx)]
    whole = lambda shape: pl.BlockSpec(shape, lambda i: (0,) * len(shape), pipeline_mode=pl.Buffered(1))
    x2 = x.reshape(total, d)
    a2 = mix_a.reshape(total, half)
    b2 = mix_b.reshape(total, half)
    out = pl.pallas_call(
        functools.partial(_post_kernel, seq_len=seq_len),
        grid=(total // rows,),
        in_specs=trio(d) + trio(half) + trio(half) + [
            mspec, mspec, mspec, mspec,
            pl.BlockSpec((1, d), lambda i: (0, 0)),
            whole((d, d)), whole((d, 2 * D_FF)), whole((8, D_FF)), whole((D_FF, d))],
        out_specs=pl.BlockSpec((rows, d), lambda i: (i, 0)),
        out_shape=jax.ShapeDtypeStruct((total, d), F32),
        scratch_shapes=[pltpu.VMEM((rows, D_FF), BF16)],
        compiler_params=_cparams(("parallel",)),
        name="post_ffn",
    )(x2, x2, x2, a2, a2, a2, b2, b2, b2, g1, sh2, sc2, g2, norm_g, w_out, wup, cw, wdn)
    return out.reshape(bsz, seq_len, d)


def _ffn_weights(p, l):
    cw = jnp.zeros((8, D_FF), F32).at[:3].set(p['ffn_conv_w'][l]).at[3].set(p['ffn_conv_b'][l])
    return p['ffn_up'][l].astype(BF16), cw, p['ffn_down'][l].astype(BF16)


def _lambda_init(layer):
    return 0.8 - 0.6 * math.exp(-0.3 * layer)


def _trunk(x, mods, p, states, caches):
    bsz, n, d = x.shape
    nseg = n // S5_SEG
    depth = p['w_mod'].shape[0]
    news = {k: [] for k in ('s5r', 's5i', 'gdn', 'ck', 'cv', 'dk', 'dv')}
    for l in range(depth):
        j = l // 2
        sh1, sc1, g1, sh2, sc2, g2 = mods[l]
        ng1 = p['norm1_g'][l][None]
        if l % 2 == 0:
            w_in = p['w_in_ab'][j]
            w_gate = jnp.zeros((d, 128), BF16).at[:, :w_in.shape[1] - AB_MAIN].set(
                w_in[:, AB_MAIN:].astype(BF16))
            rows = bsz * nseg
            per_row = lambda m: jnp.repeat(m, nseg, axis=0) if m.shape[0] > 1 else m
            u_t, qkv, z, ab = _proj_ab(x.reshape(rows, S5_SEG, d), ng1, per_row(sh1), per_row(sc1),
                                       w_in[:, :AB_MAIN].astype(BF16), w_gate)
            qkv, z, ab = (t.reshape(bsz, n, t.shape[-1]) for t in (qkv, z, ab))
            if states is None:
                h0r = jnp.zeros((2, bsz, S5_GROUPS * S5_STATE), F32)
                h0i = h0r
                s0 = None
            else:
                h0r = states[0][:, j].reshape(bsz, 2, -1).transpose(1, 0, 2)
                h0i = states[1][:, j].reshape(bsz, 2, -1).transpose(1, 0, 2)
                s0 = states[2][:, j]
            ya, fr, fi = _s5_mixer(u_t, p, j, h0r, h0i, bsz, nseg)
            yb, sg = _gdn_mixer(qkv, z, ab, p, j, s0)
            mix = (ya.reshape(bsz, n, S5_WIDTH), yb)
            w_out = p['w_out_ab'][j]
            news['s5r'].append(fr.transpose(1, 0, 2).reshape(bsz, 2, S5_GROUPS, S5_STATE))
            news['s5i'].append(fi.transpose(1, 0, 2).reshape(bsz, 2, S5_GROUPS, S5_STATE))
            news['gdn'].append(sg)
        else:
            lam_init = _lambda_init(l)
            f = lambda name: p[name][j]
            lam = (jnp.exp(jnp.sum(f('d_lq1') * f('d_lk1'))) - jnp.exp(jnp.sum(f('d_lq2') * f('d_lk2')))
                   + lam_init)
            misc = jnp.zeros((8, 128), F32).at[0, :C_HEADS].set(p['c_sink'][j]).at[1, :].set(lam)
            rope = caches is not None
            qc, kc, vc, qd, kd, vd = _proj_cd(x, ng1, sh1, sc1, p['w_in_cd'][j].astype(BF16), p, j, rope)
            if caches is None:
                mix = _attention(qc, kc, vc, qd, kd, vd, None, misc, p['d_subln'][j][None],
                                 lam_init, n, False)
            else:
                n_ctx = caches[0].shape[2]
                cc = (caches[0][:, j].reshape(bsz, n_ctx, 128), caches[1][:, j].reshape(bsz, n_ctx, 128),
                      caches[2][:, j].reshape(bsz, n_ctx, 512), caches[3][:, j].reshape(bsz, n_ctx, 512))
                mix = _attention(qc, kc, vc, qd, kd, vd, cc, misc, p['d_subln'][j][None],
                                 lam_init, Q_BLOCK, True)
            w_out = p['w_out_cd'][j]
            news['ck'].append(kc.reshape(bsz, n, C_KV_HEADS, HEAD_DIM))
            news['cv'].append(vc.reshape(bsz, n, C_KV_HEADS, HEAD_DIM))
            news['dk'].append(kd.reshape(bsz, n, D_HEADS, 2, HEAD_DIM))
            news['dv'].append(vd.reshape(bsz, n, D_HEADS, D_VDIM))
        wup, cw, wdn = _ffn_weights(p, l)
        x = _post(x, mix[0], mix[1], g1, sh2, sc2, g2, p['norm2_g'][l][None], w_out.astype(BF16),
                  wup, cw, wdn)
    return x, news


def kernel(x_prompt, x_sample, c, state_s5_re, state_s5_im, state_gdn, cache_c_k, cache_c_v, cache_d_k, cache_d_v, c_ctx, w_mod, b_mod, norm1_g, norm2_g, w_in_ab, w_out_ab, s5_lam_re, s5_lam_im, s5_log_dt, s5_b_re, s5_b_im, s5_c_re, s5_c_im, s5_d, s5_w_glu, s5_b_glu, gdn_conv_w, gdn_a_log, gdn_dt_bias, gdn_norm_g, w_in_cd, w_out_cd, c_qn, c_kn, c_sink, d_qn, d_kn, d_lq1, d_lk1, d_lq2, d_lk2, d_subln, ffn_up, ffn_conv_w, ffn_conv_b, ffn_down):
    p = dict(w_mod=w_mod, b_mod=b_mod, norm1_g=norm1_g, norm2_g=norm2_g, w_in_ab=w_in_ab, w_out_ab=w_out_ab,
             s5_lam_re=s5_lam_re, s5_lam_im=s5_lam_im, s5_log_dt=s5_log_dt, s5_b_re=s5_b_re, s5_b_im=s5_b_im,
             s5_c_re=s5_c_re, s5_c_im=s5_c_im, s5_d=s5_d, s5_w_glu=s5_w_glu, s5_b_glu=s5_b_glu,
             gdn_conv_w=gdn_conv_w, gdn_a_log=gdn_a_log, gdn_dt_bias=gdn_dt_bias, gdn_norm_g=gdn_norm_g,
             w_in_cd=w_in_cd, w_out_cd=w_out_cd, c_qn=c_qn, c_kn=c_kn, c_sink=c_sink, d_qn=d_qn, d_kn=d_kn,
             d_lq1=d_lq1, d_lk1=d_lk1, d_lq2=d_lq2, d_lk2=d_lk2, d_subln=d_subln,
             ffn_up=ffn_up, ffn_conv_w=ffn_conv_w, ffn_conv_b=ffn_conv_b, ffn_down=ffn_down)
    depth = w_mod.shape[0]
    n_dec = c.shape[0]
    mod = _modulation(jnp.concatenate([c_ctx[None], c], axis=0), w_mod, b_mod)
    split6 = lambda m: [m[:, None, k * D_MODEL:(k + 1) * D_MODEL] for k in range(6)]
    mods_ctx = [split6(mod[l, 0:1]) for l in range(depth)]
    mods_dec = [split6(mod[l, 1:1 + n_dec]) for l in range(depth)]

    y_prompt, nw = _trunk(x_prompt, mods_ctx, p, None, None)
    y_sample, _ = _trunk(x_sample, mods_dec, p, (state_s5_re, state_s5_im, state_gdn),
                         (cache_c_k, cache_c_v, cache_d_k, cache_d_v))
    st = lambda name: jnp.stack(nw[name], axis=1)
    return (y_prompt, y_sample, st('s5r'), st('s5i'), st('gdn'), st('ck'), st('cv'), st('dk'), st('dv'))
```

```python
import functools
import math

import jax
import jax.numpy as jnp
from jax import lax
from jax.experimental import pallas as pl
from jax.experimental.pallas import tpu as pltpu

F32 = jnp.float32
BF16 = jnp.bfloat16

D_MODEL = 1024
GRID_W = 64
EPS = 1e-6
NEG_INF = -1e30

S5_WIDTH = 512
S5_GROUP = 16
S5_GROUPS = 32
S5_STATE = 64
S5_TILE_GROUPS = 8
S5_TILE_CH = S5_TILE_GROUPS * S5_GROUP
S5_TILE_ST = S5_TILE_GROUPS * S5_STATE
S5_TILES = S5_GROUPS // S5_TILE_GROUPS
S5_SEG = 256
S5_SUB = 32
PROJ_AB_TOKENS = 64
S5_ROWS = 8

GDN_DK = 128
GDN_DV = 128
GDN_HEADS = 4
GDN_WIDTH = 512
GDN_CHUNK = 64
GDN_GROUP = 4
GDN_SHORT_LEN = 256
GDN_SEQS_SHORT = 2

HEAD_DIM = 64
C_HEADS = 8
C_KV_HEADS = 2
C_GROUP = 4
WINDOW = 128
Q_BLOCK = 128
D_HEADS = 4
D_VDIM = 128
ATTN_SCALE = HEAD_DIM ** -0.5
ATTN_SEQS_SHORT = 2
ROPE_THETA = 10000.0

D_FF = 2816
FF_CHUNK = 256
FF_CHUNKS = D_FF // FF_CHUNK
POST_ROWS = 512
POST_HALO = 8

AB_MAIN = S5_WIDTH + 4 * GDN_WIDTH
AB_PAD = AB_MAIN + 128
CD_IN = 2304

MOD_TK = 128
MOD_LANES = 512
VMEM_LIMIT = 56 * 1024 * 1024


def _cparams(sem):
    return pltpu.CompilerParams(dimension_semantics=sem, vmem_limit_bytes=VMEM_LIMIT)


def _sigmoid(x):
    return 1.0 / (1.0 + jnp.exp(-x))


def _silu(x):
    return x * _sigmoid(x)


def _softplus(x):
    return jnp.maximum(x, 0.0) + jnp.log(1.0 + jnp.exp(-jnp.abs(x)))


def _gelu_tanh(x):
    return 0.5 * x * (1.0 + jnp.tanh(math.sqrt(2.0 / math.pi) * (x + 0.044715 * (x * x * x))))


def _rms_mod(x, g, shift, scale):
    y = x * lax.rsqrt(jnp.mean(x * x, axis=-1, keepdims=True) + EPS)
    return (y * g) * (1.0 + scale) + shift


def _dot(a, b):
    return jnp.dot(a.astype(BF16), b.astype(BF16), preferred_element_type=F32)


def _dot_nt(a, b):
    return lax.dot_general(a.astype(BF16), b.astype(BF16), (((1,), (1,)), ((), ())),
                           preferred_element_type=F32)


def _dot_tn(a, b):
    return lax.dot_general(a.astype(BF16), b.astype(BF16), (((0,), (0,)), ((), ())),
                           preferred_element_type=F32)


def _hi_lo(a):
    hi = a.astype(BF16)
    return hi, (a - hi.astype(F32)).astype(BF16)


def _mod_kernel(ct_ref, w_ref, b_ref, o_ref, acc_ref, *, n_rows):
    k = pl.program_id(1)

    @pl.when(k == 0)
    def _():
        acc_ref[...] = jnp.zeros_like(acc_ref)

    tk, n_out = w_ref.shape[1:]
    s = _silu(ct_ref[...])
    sb = [[jnp.broadcast_to(s[g * 8:(g + 1) * 8, m:m + 1], (8, MOD_LANES)) for g in range(tk // 8)]
          for m in range(n_rows)]
    for c in range(n_out // MOD_LANES):
        cols = slice(c * MOD_LANES, (c + 1) * MOD_LANES)
        accs = [acc_ref[m, :, cols] for m in range(n_rows)]
        for g in range(tk // 8):
            wg = w_ref[0, g * 8:(g + 1) * 8, cols]
            accs = [a + wg * sb[m][g] for m, a in enumerate(accs)]
        for m in range(n_rows):
            acc_ref[m, :, cols] = accs[m]

    @pl.when(k == pl.num_programs(1) - 1)
    def _():
        o_ref[0] = jnp.zeros(o_ref.shape[1:], F32)
        for m in range(n_rows):
            o_ref[0, m:m + 1, :] = jnp.sum(acc_ref[m], axis=0, keepdims=True) + b_ref[0]


def _modulation(cvecs, w_mod, b_mod):
    n, d = cvecs.shape
    depth, _, n_out = w_mod.shape
    ct = jnp.zeros((d, 8), F32).at[:, :n].set(cvecs.T)
    return pl.pallas_call(
        functools.partial(_mod_kernel, n_rows=n),
        grid=(depth, d // MOD_TK),
        in_specs=[pl.BlockSpec((MOD_TK, 8), lambda l, k: (k, 0)),
                  pl.BlockSpec((1, MOD_TK, n_out), lambda l, k: (l, k, 0)),
                  pl.BlockSpec((1, 1, n_out), lambda l, k: (l, 0, 0))],
        out_specs=pl.BlockSpec((1, 8, n_out), lambda l, k: (l, 0, 0)),
        out_shape=jax.ShapeDtypeStruct((depth, 8, n_out), F32),
        scratch_shapes=[pltpu.VMEM((n, 8, n_out), F32)],
        compiler_params=_cparams(("parallel", "arbitrary")),
        name="adaln_mod",
    )(ct, w_mod, b_mod.reshape(depth, 1, n_out))


def _proj_ab_kernel(x_ref, g_ref, sh_ref, sc_ref, w_ref, wg_ref, u_ref, qkv_ref, z_ref, ab_ref):
    ns, tm, d = x_ref.shape
    h = _rms_mod(x_ref[...], g_ref[...], sh_ref[...], sc_ref[...])
    h = h.reshape(ns * tm, d).astype(BF16)
    y = jnp.dot(h, w_ref[...], preferred_element_type=F32)
    for s in range(ns):
        u_ref[:, s, :] = y[s * tm:(s + 1) * tm, :S5_WIDTH]
    qkv_ref[...] = y[:, S5_WIDTH:S5_WIDTH + 3 * GDN_WIDTH].reshape(ns, tm, 3 * GDN_WIDTH)
    z_ref[...] = y[:, S5_WIDTH + 3 * GDN_WIDTH:AB_MAIN].reshape(ns, tm, GDN_WIDTH)
    ab_ref[...] = jnp.dot(h, wg_ref[...], preferred_element_type=F32).reshape(ns, tm, 128)


def _proj_ab(x, g, shift, scale, w, w_gate):
    rows, n, d = x.shape
    tm = PROJ_AB_TOKENS
    per_row = shift.shape[0] > 1
    mspec = pl.BlockSpec((S5_ROWS, 1, d), lambda r, i: (r, 0, 0)) if per_row else \
        pl.BlockSpec((1, 1, d), lambda r, i: (0, 0, 0))
    blk = lambda w_: pl.BlockSpec((S5_ROWS, tm, w_), lambda r, i: (r, i, 0))
    return pl.pallas_call(
        _proj_ab_kernel,
        grid=(rows // S5_ROWS, n // tm),
        in_specs=[blk(d),
                  pl.BlockSpec((1, d), lambda r, i: (0, 0)),
                  mspec, mspec,
                  pl.BlockSpec((d, AB_MAIN), lambda r, i: (0, 0)),
                  pl.BlockSpec((d, 128), lambda r, i: (0, 0))],
        out_specs=[pl.BlockSpec((tm, S5_ROWS, S5_WIDTH), lambda r, i: (i, r, 0)),
                   blk(3 * GDN_WIDTH), blk(GDN_WIDTH), blk(128)],
        out_shape=[jax.ShapeDtypeStruct((n, rows, S5_WIDTH), F32),
                   jax.ShapeDtypeStruct((rows, n, 3 * GDN_WIDTH), F32),
                   jax.ShapeDtypeStruct((rows, n, GDN_WIDTH), F32),
                   jax.ShapeDtypeStruct((rows, n, 128), F32)],
        compiler_params=_cparams(("parallel", "parallel")),
        name="proj_ab",
    )(x, g, shift, scale, w, w_gate)


def _s5_kernel(u_ref, bm_ref, cm_ref, a_ref, h0r_ref, h0i_ref, *rest, want_y):
    if want_y:
        y_ref, fr_ref, fi_ref, xs_ref = rest
    else:
        fr_ref, fi_ref, xs_ref = rest
    n = u_ref.shape[0]
    sub = S5_SUB
    nsub = n // sub
    st = S5_TILE_ST

    def x_proj(d, k):
        u2 = u_ref[k * sub:(k + 1) * sub].reshape(sub * S5_ROWS, S5_TILE_CH)
        xs_ref[d, k * sub:(k + 1) * sub] = _dot(u2, bm_ref[d, 0]).reshape(sub, S5_ROWS, 2 * st)

    ar = [jnp.broadcast_to(a_ref[d, 0, 0:1, :], (S5_ROWS, st)) for d in range(2)]
    ai = [jnp.broadcast_to(a_ref[d, 0, 1:2, :], (S5_ROWS, st)) for d in range(2)]
    hr = [h0r_ref[0], h0r_ref[1]]
    hi = [h0i_ref[0], h0i_ref[1]]
    x_proj(0, 0)
    x_proj(1, nsub - 1)
    written = set()
    for k in range(nsub):
        ks = (k, nsub - 1 - k)
        if k + 1 < nsub:
            x_proj(0, ks[0] + 1)
            x_proj(1, ks[1] - 1)
        for t in range(sub):
            for d in range(2):
                tt = ks[d] * sub + (t if d == 0 else sub - 1 - t)
                x = xs_ref[d, tt]
                nr = ar[d] * hr[d] - ai[d] * hi[d] + x[:, :st]
                ni = ar[d] * hi[d] + ai[d] * hr[d] + x[:, st:]
                xs_ref[d, tt] = jnp.concatenate([nr, ni], axis=-1)
                hr[d], hi[d] = nr, ni
        if want_y:
            for d in range(2):
                rows = slice(ks[d] * sub, (ks[d] + 1) * sub)
                hs = xs_ref[d, rows].reshape(sub * S5_ROWS, 2 * st)
                yv = _dot(hs, cm_ref[d, 0]).reshape(sub, S5_ROWS, S5_TILE_CH)
                if ks[d] in written:
                    y_ref[rows] += yv
                else:
                    y_ref[rows] = yv
                    written.add(ks[d])
    for d in range(2):
        fr_ref[d] = hr[d]
        fi_ref[d] = hi[d]


def _s5_scan(u_t, bmat, cmat, amat, h0r, h0i, want_y=True):
    n, rows, _ = u_t.shape
    state = lambda: pl.BlockSpec((2, S5_ROWS, S5_TILE_ST), lambda r, j: (0, r, j))
    st_shape = jax.ShapeDtypeStruct((2, rows, S5_GROUPS * S5_STATE), F32)
    out_specs = [state(), state()]
    out_shape = [st_shape, st_shape]
    if want_y:
        out_specs.insert(0, pl.BlockSpec((n, S5_ROWS, S5_TILE_CH), lambda r, j: (0, r, j)))
        out_shape.insert(0, jax.ShapeDtypeStruct((n, rows, S5_WIDTH), F32))
    return pl.pallas_call(
        functools.partial(_s5_kernel, want_y=want_y),
        grid=(rows // S5_ROWS, S5_TILES),
        in_specs=[pl.BlockSpec((n, S5_ROWS, S5_TILE_CH), lambda r, j: (0, r, j)),
                  pl.BlockSpec((2, 1, S5_TILE_CH, 2 * S5_TILE_ST), lambda r, j: (0, j, 0, 0)),
                  pl.BlockSpec((2, 1, 2 * S5_TILE_ST, S5_TILE_CH), lambda r, j: (0, j, 0, 0)),
                  pl.BlockSpec((2, 1, 8, S5_TILE_ST), lambda r, j: (0, j, 0, 0)),
                  state(), state()],
        out_specs=out_specs,
        out_shape=out_shape,
        scratch_shapes=[pltpu.VMEM((2, n, S5_ROWS, 2 * S5_TILE_ST), F32)],
        compiler_params=_cparams(("parallel", "parallel")),
        name="s5_scan" if want_y else "s5_states",
    )(u_t, bmat, cmat, amat, h0r, h0i)


def _s5_params(p, j):
    lam_re, lam_im, log_dt = p['s5_lam_re'][j], p['s5_lam_im'][j], p['s5_log_dt'][j]
    dt = jnp.exp(log_dt)[..., None]
    mag = jnp.exp(lam_re * dt)
    ar, ai = mag * jnp.cos(lam_im * dt), mag * jnp.sin(lam_im * dt)
    den = lam_re * lam_re + lam_im * lam_im
    fr = ((ar - 1.0) * lam_re + ai * lam_im) / den
    fi = (ai * lam_re - (ar - 1.0) * lam_im) / den
    b_re, b_im = p['s5_b_re'][j], p['s5_b_im'][j]
    bbr = fr[..., None] * b_re - fi[..., None] * b_im
    bbi = fr[..., None] * b_im + fi[..., None] * b_re
    eye = jnp.eye(S5_TILE_GROUPS, dtype=F32)

    def in_blocks(t):
        t = t.reshape(2, S5_TILES, S5_TILE_GROUPS, S5_STATE, S5_GROUP)
        t = jnp.einsum('dtgpc,gh->dtgchp', t, eye)
        return t.reshape(2, S5_TILES, S5_TILE_CH, S5_TILE_ST)

    def out_blocks(t):
        t = t.reshape(2, S5_TILES, S5_TILE_GROUPS, S5_GROUP, S5_STATE)
        t = jnp.einsum('dtgcp,gh->dtgphc', t, eye)
        return t.reshape(2, S5_TILES, S5_TILE_ST, S5_TILE_CH)

    bmat = jnp.concatenate([in_blocks(bbr), in_blocks(bbi)], axis=-1).astype(BF16)
    cmat = jnp.concatenate([out_blocks(p['s5_c_re'][j]), -out_blocks(p['s5_c_im'][j])], axis=-2).astype(BF16)
    seg_mag = jnp.exp(lam_re * dt * S5_SEG)
    pr, pi = seg_mag * jnp.cos(lam_im * dt * S5_SEG), seg_mag * jnp.sin(lam_im * dt * S5_SEG)
    flat = lambda t: t.reshape(2, S5_TILES, 1, S5_TILE_ST)
    amat = jnp.concatenate([flat(ar), flat(ai), flat(pr), flat(pi),
                            jnp.zeros((2, S5_TILES, 4, S5_TILE_ST), F32)], axis=2)
    return bmat, cmat, amat


def _s5_glu_kernel(y_ref, u_ref, d_ref, w_ref, b_ref, o_ref):
    tm, ns, _ = y_ref.shape
    y = jnp.concatenate([y_ref[:, s, :] + d_ref[...] * u_ref[:, s, :] for s in range(ns)], axis=0)
    g = _gelu_tanh(y)
    out = g * _sigmoid(_dot(g, w_ref[...]) + b_ref[...])
    o_ref[...] = out.reshape(ns, tm, S5_WIDTH)


def _s5_glu(y_t, u_t, s5_d, w_glu, b_glu):
    n, rows, _ = y_t.shape
    tm = PROJ_AB_TOKENS
    tblk = pl.BlockSpec((tm, S5_ROWS, S5_WIDTH), lambda r, i: (i, r, 0))
    return pl.pallas_call(
        _s5_glu_kernel,
        grid=(rows // S5_ROWS, n // tm),
        in_specs=[tblk, tblk,
                  pl.BlockSpec((1, S5_WIDTH), lambda r, i: (0, 0)),
                  pl.BlockSpec((S5_WIDTH, S5_WIDTH), lambda r, i: (0, 0)),
                  pl.BlockSpec((1, S5_WIDTH), lambda r, i: (0, 0))],
        out_specs=pl.BlockSpec((S5_ROWS, tm, S5_WIDTH), lambda r, i: (r, i, 0)),
        out_shape=jax.ShapeDtypeStruct((rows, n, S5_WIDTH), F32),
        compiler_params=_cparams(("parallel", "parallel")),
        name="s5_glu",
    )(y_t, u_t, s5_d, w_glu, b_glu)


def _s5_mixer(u3, p, j, h0r, h0i, bsz, nseg):
    rows = bsz * nseg
    bmat, cmat, amat = _s5_params(p, j)
    if nseg == 1:
        y_t, fr, fi = _s5_scan(u3, bmat, cmat, amat, h0r, h0i)
    else:
        zero = jnp.zeros((2, bsz, nseg, S5_GROUPS * S5_STATE), F32)
        first = jnp.array([0, nseg - 1])
        seed = lambda h0: zero.at[jnp.arange(2), :, first].set(h0).reshape(2, rows, -1)
        fr, fi = _s5_scan(u3, bmat, cmat, amat, seed(h0r), seed(h0i), want_y=False)
        fr = fr.reshape(2, bsz, nseg, -1)
        fi = fi.reshape(2, bsz, nseg, -1)
        pr = amat[:, :, 2].reshape(2, 1, -1)
        pi = amat[:, :, 3].reshape(2, 1, -1)

        def chain(dr, order):
            hr, hi = (h0r[dr], h0i[dr])
            outs_r, outs_i = {}, {}
            for n_done, k in enumerate(order):
                outs_r[k], outs_i[k] = hr, hi
                if n_done == 0:
                    hr, hi = fr[dr, :, k], fi[dr, :, k]
                else:
                    hr, hi = (pr[dr] * hr - pi[dr] * hi + fr[dr, :, k],
                              pr[dr] * hi + pi[dr] * hr + fi[dr, :, k])
            st = lambda o: jnp.stack([o[k] for k in range(nseg)], axis=1)
            return st(outs_r), st(outs_i), hr, hi

        sr0, si0, er0, ei0 = chain(0, list(range(nseg)))
        sr1, si1, er1, ei1 = chain(1, list(range(nseg - 1, -1, -1)))
        start_r = jnp.stack([sr0, sr1]).reshape(2, rows, -1)
        start_i = jnp.stack([si0, si1]).reshape(2, rows, -1)
        y_t, _, _ = _s5_scan(u3, bmat, cmat, amat, start_r, start_i)
        fr = jnp.stack([er0, er1])
        fi = jnp.stack([ei0, ei1])
    ya = _s5_glu(y_t, u3, p['s5_d'][j][None], p['s5_w_glu'][j].astype(BF16), p['s5_b_glu'][j][None])
    return ya, fr, fi


def _gdn_kernel(qkv_ref, z_ref, ab_ref, cw_ref, gp_ref, ng_ref, *rest, zero_init):
    if zero_init:
        s0_ref = None
        o_ref, sf_ref, q_s, k_s, v_s, gate_s, of_s, ob_s, st_s, uw_s, a_s = rest
    else:
        s0_ref, o_ref, sf_ref, q_s, k_s, v_s, gate_s, of_s, ob_s, st_s, uw_s, a_s = rest
    ns, n = qkv_ref.shape[:2]
    nc = n // GDN_CHUNK
    row = lax.broadcasted_iota(jnp.int32, (n, 1), 0)

    for sq in range(ns):
        for blk in range(3 * GDN_HEADS):
            cols = slice(blk * GDN_DK, (blk + 1) * GDN_DK)
            hs = slice((blk % GDN_HEADS) * GDN_DK, (blk % GDN_HEADS + 1) * GDN_DK)
            x = qkv_ref[sq, :, cols]
            xm = jnp.where(row == 0, 0.0, pltpu.roll(x, 1, 0))
            xp = jnp.where(row == n - 1, 0.0, pltpu.roll(x, n - 1, 0))
            y = _silu(xm * cw_ref[0:1, cols] + x * cw_ref[1:2, cols] + xp * cw_ref[2:3, cols])
            if blk < GDN_HEADS:
                q_s[sq, :, hs] = y * lax.rsqrt(jnp.sum(y * y, axis=-1, keepdims=True) + EPS) * (GDN_DK ** -0.5)
            elif blk < 2 * GDN_HEADS:
                k_s[sq, :, hs] = y * lax.rsqrt(jnp.sum(y * y, axis=-1, keepdims=True) + EPS)
            else:
                v_s[sq, :, hs] = y

        ab = ab_ref[sq]
        beta = _sigmoid(ab)
        g = -jnp.exp(gp_ref[0:1, :]) * _softplus(ab + gp_ref[1:2, :])
        pos = row % GDN_CHUNK
        pre, suf = g, g
        sft = 1
        while sft < GDN_CHUNK:
            pre = pre + jnp.where(pos >= sft, pltpu.roll(pre, sft, 0), 0.0)
            suf = suf + jnp.where(pos < GDN_CHUNK - sft, pltpu.roll(suf, n - sft, 0), 0.0)
            sft *= 2
        gate_s[sq, 0] = beta
        gate_s[sq, 1] = pre
        gate_s[sq, 2] = suf

    st_s[...] = jnp.zeros_like(st_s) if zero_init else s0_ref[...]
    cs = GDN_CHUNK
    pk = GDN_HEADS * cs
    ri = lax.broadcasted_iota(jnp.int32, (cs, pk), 0)
    lane_pk = lax.broadcasted_iota(jnp.int32, (cs, pk), 1)
    ci = lane_pk % cs
    eye_pk = (ri == ci).astype(F32)
    head_pk = [(lax.broadcasted_iota(jnp.int32, (1, pk), 1) // cs) == h for h in range(GDN_HEADS)]
    head_w = [(lax.broadcasted_iota(jnp.int32, (1, GDN_WIDTH), 1) // GDN_DK) == h for h in range(GDN_HEADS)]

    def block_diag(p):
        return jnp.concatenate([jnp.where(m, p, jnp.zeros_like(p)) for m in head_pk], axis=0)

    def dot3_bd(a, p):
        a_hi, a_lo = _hi_lo(a)
        p_hi, p_lo = _hi_lo(p)
        b_hi, b_lo = block_diag(p_hi), block_diag(p_lo)
        mm = functools.partial(jnp.dot, preferred_element_type=F32)
        return mm(a_hi, b_hi) + (mm(a_lo, b_hi) + mm(a_hi, b_lo))

    def lanes(cols, width):
        return jnp.concatenate([jnp.broadcast_to(c, (cs, width)) for c in cols], axis=1)

    group = GDN_GROUP // ns

    def phase_a(it, carry):
        st = []
        for sq, cc in [(sq, cc) for sq in range(ns) for cc in range(group)]:
            rows = pl.ds(pl.multiple_of((it * group + cc) * cs, cs), cs)
            q_all, k_all, v_all = q_s[sq, rows, :], k_s[sq, rows, :], v_s[sq, rows, :]
            beta_blk = gate_s[sq, 0, rows, :]
            for dr in range(2):
                gc_blk = gate_s[sq, 1 + dr, rows, :]
                lane0 = dr * GDN_HEADS
                bcols = [beta_blk[:, 8 + lane0 + h:9 + lane0 + h] for h in range(GDN_HEADS)]
                gcols = [gc_blk[:, lane0 + h:lane0 + h + 1] for h in range(GDN_HEADS)]
                st.append(dict(sq=sq, dr=dr, rows=rows, q=q_all, k=k_all, v=v_all, bcols=bcols, gcols=gcols,
                               incl=(ri >= ci) if dr == 0 else (ri <= ci),
                               strict=(ri > ci) if dr == 0 else (ri < ci)))
        for t in st:
            b_w = lanes(t['bcols'], GDN_DK)
            t['kb'] = t['k'] * b_w
            t['vb'] = t['v'] * b_w
            k_bd = jnp.concatenate([jnp.where(m, t['k'], 0.0) for m in head_w], axis=0)
            t['kq'] = _dot_nt(jnp.concatenate([t['kb'], t['q']], axis=0), k_bd)
        for t in st:
            gcol = lanes(t['gcols'], cs)
            grow = jnp.sum(eye_pk * gcol, axis=0, keepdims=True)
            decay = jnp.exp(jnp.where(t['incl'], gcol - grow, NEG_INF))
            t['pw'] = jnp.where(t['strict'], t['kq'][:cs] * decay, 0.0)
            a_s[t['sq'], t['dr'], t['rows'], :] =jnp.where(t['incl'], t['kq'][cs:] * decay, 0.0)
            t['tm'] = eye_pk - t['pw']
        for t in st:
            t['pw'] = dot3_bd(t['pw'], t['pw'])
        for rnd in range(5):
            for t in st:
                if rnd < 4:
                    r = dot3_bd(jnp.concatenate([t['pw'], t['tm']], axis=0), t['pw'])
                    t['pw'] = r[:cs]
                    t['tm'] = t['tm'] + r[cs:]
                else:
                    t['tm'] = t['tm'] + dot3_bd(t['tm'], t['pw'])
        for t in st:
            kbg = t['kb'] * lanes([jnp.exp(g) for g in t['gcols']], GDN_DK)
            for h in range(GDN_HEADS):
                hs = slice(h * GDN_DK, (h + 1) * GDN_DK)
                rhs = jnp.concatenate([t['vb'][:, hs], kbg[:, hs]], axis=1)
                uw_s[t['sq'], t['dr'], t['rows'], 2 * h * GDN_DK:2 * (h + 1) * GDN_DK] = _dot(
                    t['tm'][:, h * cs:(h + 1) * cs], rhs)
        return carry

    lax.fori_loop(0, nc // group, phase_a, 0)

    def phase_b(c, carry):
        ch = []
        for sq, dr in [(sq, dr) for sq in range(ns) for dr in range(2)]:
            cidx = c if dr == 0 else nc - 1 - c
            rows = pl.ds(pl.multiple_of(cidx * cs, cs), cs)
            gc_blk = gate_s[sq, 1 + dr, rows, :]
            for h in range(GDN_HEADS):
                hs = slice(h * GDN_DK, (h + 1) * GDN_DK)
                lane = dr * GDN_HEADS + h
                gcol = gc_blk[:, lane:lane + 1]
                ch.append(dict(sq=sq, dr=dr, h=h, rows=rows, hs=hs, gcol=gcol,
                               gl=gcol[cs - 1:cs] if dr == 0 else gcol[0:1],
                               uw=uw_s[sq, dr, rows, 2 * h * GDN_DK:2 * (h + 1) * GDN_DK],
                               amat=a_s[sq, dr, rows, h * cs:(h + 1) * cs]))
        for t in ch:
            t['s'] = st_s[t['sq'], t['dr'], t['h']]
            qg = q_s[t['sq'], t['rows'], t['hs']] * jnp.exp(t['gcol'])
            t['ws'] = _dot(jnp.concatenate([t['uw'][:, GDN_DV:], qg], axis=0), t['s'])
        for t in ch:
            vn = t['uw'][:, :GDN_DV] - t['ws'][:cs]
            o = t['ws'][cs:] + _dot(t['amat'], vn)
            kg = k_s[t['sq'], t['rows'], t['hs']] * jnp.exp(t['gl'] - t['gcol'])
            st_s[t['sq'], t['dr'], t['h']] = t['s'] * jnp.exp(t['gl']) + _dot_tn(kg, vn)
            if t['dr'] == 0:
                of_s[t['sq'], t['rows'], t['hs']] = o
            else:
                ob_s[t['sq'], t['rows'], t['hs']] = o
        return carry

    lax.fori_loop(0, nc, phase_b, 0)
    sf_ref[...] = st_s[...]

    for sq in range(ns):
        z = z_ref[sq]
        for h in range(GDN_HEADS):
            hs = slice(h * GDN_DV, (h + 1) * GDN_DV)
            o = of_s[sq, :, hs] + ob_s[sq, :, hs]
            o = o * lax.rsqrt(jnp.mean(o * o, axis=-1, keepdims=True) + EPS) * ng_ref[...]
            o_ref[sq, :, hs] = o * _silu(z[:, hs])


def _gdn_mixer(qkv, z, ab, p, j, s0):
    bsz, n, _ = qkv.shape
    gp = jnp.zeros((8, 128), F32)
    gp = gp.at[0, :8].set(p['gdn_a_log'][j].reshape(8)).at[1, :8].set(p['gdn_dt_bias'][j].reshape(8))
    cw = jnp.zeros((8, 3 * GDN_WIDTH), F32).at[:3].set(p['gdn_conv_w'][j])
    ns = GDN_SEQS_SHORT if (n <= GDN_SHORT_LEN and bsz % GDN_SEQS_SHORT == 0) else 1
    blk = lambda w_: pl.BlockSpec((ns, n, w_), lambda b: (b, 0, 0))
    sblk = pl.BlockSpec((ns, 2, GDN_HEADS, GDN_DK, GDN_DV), lambda b: (b, 0, 0, 0, 0))
    tok = lambda w_: pltpu.VMEM((ns, n, w_), F32)
    return pl.pallas_call(
        functools.partial(_gdn_kernel, zero_init=s0 is None),
        grid=(bsz // ns,),
        in_specs=[blk(3 * GDN_WIDTH), blk(GDN_WIDTH), blk(128),
                  pl.BlockSpec((8, 3 * GDN_WIDTH), lambda b: (0, 0)),
                  pl.BlockSpec((8, 128), lambda b: (0, 0)),
                  pl.BlockSpec((1, GDN_DV), lambda b: (0, 0))] + ([] if s0 is None else [sblk]),
        out_specs=[blk(GDN_WIDTH), sblk],
        out_shape=[jax.ShapeDtypeStruct((bsz, n, GDN_WIDTH), F32),
                   jax.ShapeDtypeStruct((bsz, 2, GDN_HEADS, GDN_DK, GDN_DV), F32)],
        scratch_shapes=[tok(GDN_WIDTH), tok(GDN_WIDTH), tok(GDN_WIDTH),
                        pltpu.VMEM((ns, 3, n, 128), F32),
                        tok(GDN_WIDTH), tok(GDN_WIDTH),
                        pltpu.VMEM((ns, 2, GDN_HEADS, GDN_DK, GDN_DV), F32),
                        pltpu.VMEM((ns, 2, n, 2 * GDN_WIDTH), F32),
                        pltpu.VMEM((ns, 2, n, GDN_HEADS * GDN_CHUNK), F32)],
        compiler_params=_cparams(("parallel",)),
        name="gdn_mixer",
    )(qkv, z, ab, cw, gp, p['gdn_norm_g'][j][None], *(() if s0 is None else (s0,)))


def _proj_cd_kernel(x_ref, g_ref, sh_ref, sc_ref, w_ref, gm_ref, gain_ref, cos_ref, sin_ref,
                    qc_ref, kc_ref, vc_ref, qd_ref, kd_ref, vd_ref, *, rope):
    h = _rms_mod(x_ref[0], g_ref[...], sh_ref[0], sc_ref[0])
    y = _dot(h, w_ref[...])
    lane = lax.broadcasted_iota(jnp.int32, (1, 512), 1)
    low = (lane % 32) < 16

    def head_norm(t, gain, scale):
        w = t.shape[1]
        ms = _dot(t * t, gm_ref[:w, :w])
        t = t * lax.rsqrt(ms + EPS) * gain
        if rope:
            part = jnp.where(low[:, :w], pltpu.roll(t, w - 16, 1), pltpu.roll(t, 16, 1))
            t = t * cos_ref[:, :w] + part * sin_ref[:, :w]
        return t * scale if scale != 1.0 else t

    qc_ref[0] = head_norm(y[:, 0:512], gain_ref[0:1, :], ATTN_SCALE)
    kc_ref[0] = head_norm(y[:, 512:640], gain_ref[1:2, :128], 1.0)
    vc_ref[0] = y[:, 640:768]
    qd_ref[0] = head_norm(y[:, 768:1280], gain_ref[2:3, :], ATTN_SCALE)
    kd_ref[0] = head_norm(y[:, 1280:1792], gain_ref[3:4, :], 1.0)
    vd_ref[0] = y[:, 1792:2304]


def _rope_tables(n):
    rows = n // GRID_W
    row = jnp.repeat(jnp.arange(rows), GRID_W).astype(F32)
    col = jnp.tile(jnp.arange(GRID_W), rows).astype(F32)
    quarter = HEAD_DIM // 4
    inv = ROPE_THETA ** (-jnp.arange(quarter, dtype=F32) / quarter)
    ang_r = row[:, None] * inv[None, :]
    ang_c = col[:, None] * inv[None, :]
    cos = jnp.concatenate([jnp.cos(ang_r), jnp.cos(ang_r), jnp.cos(ang_c), jnp.cos(ang_c)], axis=-1)
    sin = jnp.concatenate([-jnp.sin(ang_r), jnp.sin(ang_r), -jnp.sin(ang_c), jnp.sin(ang_c)], axis=-1)
    return jnp.tile(cos, (1, 8)), jnp.tile(sin, (1, 8))


def _proj_cd(x, g, shift, scale, w, p, j, rope):
    bsz, n, d = x.shape
    tm = 256
    per_seq = shift.shape[0] > 1
    midx = (lambda b, i: (b, 0, 0)) if per_seq else (lambda b, i: (0, 0, 0))
    lane = jnp.arange(512)
    gmat = ((lane[:, None] // HEAD_DIM) == (lane[None, :] // HEAD_DIM)).astype(F32) / HEAD_DIM
    gains = jnp.zeros((8, 512), F32)
    gains = gains.at[0].set(jnp.tile(p['c_qn'][j], 8)).at[1].set(jnp.tile(p['c_kn'][j], 8))
    gains = gains.at[2].set(jnp.tile(p['d_qn'][j], 8)).at[3].set(jnp.tile(p['d_kn'][j], 8))
    if rope:
        cos, sin = _rope_tables(n)
    else:
        cos, sin = jnp.ones((tm, 512), F32), jnp.zeros((tm, 512), F32)
    tidx = (lambda b, i: (i, 0)) if rope else (lambda b, i: (0, 0))
    blk = lambda w_: pl.BlockSpec((1, tm, w_), lambda b, i: (b, i, 0))
    return pl.pallas_call(
        functools.partial(_proj_cd_kernel, rope=rope),
        grid=(bsz, n // tm),
        in_specs=[blk(d),
                  pl.BlockSpec((1, d), lambda b, i: (0, 0)),
                  pl.BlockSpec((1, 1, d), midx),
                  pl.BlockSpec((1, 1, d), midx),
                  pl.BlockSpec((d, CD_IN), lambda b, i: (0, 0)),
                  pl.BlockSpec((512, 512), lambda b, i: (0, 0)),
                  pl.BlockSpec((8, 512), lambda b, i: (0, 0)),
                  pl.BlockSpec((tm, 512), tidx),
                  pl.BlockSpec((tm, 512), tidx)],
        out_specs=[blk(512), blk(128), blk(128), blk(512), blk(512), blk(512)],
        out_shape=[jax.ShapeDtypeStruct((bsz, n, w_), F32) for w_ in (512, 128, 128, 512, 512, 512)],
        compiler_params=_cparams(("parallel", "parallel")),
        name="proj_cd",
    )(x, g, shift, scale, w, gmat.astype(BF16), gains, cos, sin)


def _probs(scores, extra=None):
    m = scores[0].max(axis=-1, keepdims=True)
    for s in scores[1:]:
        m = jnp.maximum(m, s.max(axis=-1, keepdims=True))
    if extra is not None:
        m = jnp.maximum(m, extra)
    return [jnp.exp(s - m).astype(BF16) for s in scores], m


def _attn_kernel(qc_ref, kc_ref, vc_ref, qd_ref, kd_ref, vd_ref, *rest, windowed, n_ctx, lam_scale):
    if n_ctx:
        ck_ref, cv_ref, dk_ref, dv_ref, misc_ref, sub_ref, oc_ref, od_ref = rest
    else:
        misc_ref, sub_ref, oc_ref, od_ref = rest
    tq = qc_ref.shape[1]
    n = kc_ref.shape[1]
    start = pl.program_id(1) * tq
    if windowed:
        span = tq + 2 * WINDOW
        k0 = pl.multiple_of(jnp.clip(start - WINDOW, 0, n - span), 128)
        krows = pl.ds(k0, span)
        qpos = start + lax.broadcasted_iota(jnp.int32, (tq, span), 0)
        kpos = k0 + lax.broadcasted_iota(jnp.int32, (tq, span), 1)
        ok = jnp.abs(qpos - kpos) <= WINDOW
    else:
        krows = pl.ds(0, n)
    low = lax.broadcasted_iota(jnp.int32, (1, 2 * HEAD_DIM), 1) < HEAD_DIM

    def key_ops(k):
        kr = pltpu.roll(k, HEAD_DIM, 1)
        z = jnp.zeros_like(k)
        return {(0, 0): jnp.where(low, k, z).astype(BF16), (0, 1): jnp.where(low, z, kr).astype(BF16),
                (1, 0): jnp.where(low, kr, z).astype(BF16), (1, 1): jnp.where(low, z, k).astype(BF16)}

    def with_ones(v):
        return jnp.concatenate([v, jnp.ones_like(v)], axis=1)

    def val_ops(v):
        return {key: with_ones(op) for key, op in key_ops(v).items()}

    def half_ops(k):
        z = jnp.zeros_like(k)
        return [jnp.where(low, k, z).astype(BF16), jnp.where(low, z, k).astype(BF16)]

    seqs = range(qc_ref.shape[0])

    c_scores, d_scores = [[] for _ in seqs], [[] for _ in seqs]
    for sq in seqs:
        qc = qc_ref[sq].astype(BF16)
        qd = qd_ref[sq].astype(BF16)
        kc_ops = key_ops(kc_ref[sq, krows, :])
        ck_ops = key_ops(ck_ref[sq]) if n_ctx else None
        for h in range(C_HEADS):
            key = (h // C_GROUP, h % 2)
            q = qc[:, (h // 2) * 128:(h // 2 + 1) * 128]
            sc = [_dot_nt(q, kc_ops[key])]
            if n_ctx:
                sc.append(_dot_nt(q, ck_ops[key]))
            c_scores[sq].append(sc)
        for h in range(D_HEADS):
            q = qd[:, h * 128:(h + 1) * 128]
            kd_ops = half_ops(kd_ref[sq, :, h * 128:(h + 1) * 128])
            dk_ops = half_ops(dk_ref[sq, :, h * 128:(h + 1) * 128]) if n_ctx else None
            for c in range(2):
                sc = [_dot_nt(q, kd_ops[c])]
                if n_ctx:
                    sc.append(_dot_nt(q, dk_ops[c]))
                d_scores[sq].append(sc)

    c_probs, d_probs = [[] for _ in seqs], [[] for _ in seqs]
    for sq in seqs:
        for h in range(C_HEADS):
            sc = c_scores[sq][h]
            if windowed:
                sc[0] = jnp.where(ok, sc[0], NEG_INF)
            c_probs[sq].append(_probs(sc, misc_ref[0:1, h:h + 1]))
        d_probs[sq] = [_probs(sc)[0] for sc in d_scores[sq]]

    lam = misc_ref[1:2, 0:1]
    for sq in seqs:
        vc_ops = val_ops(vc_ref[sq, krows, :])
        cv_ops = val_ops(cv_ref[sq]) if n_ctx else None
        for j in range(C_HEADS // 2):
            pair = None
            for h in (2 * j, 2 * j + 1):
                key = (h // C_GROUP, h % 2)
                ps, m = c_probs[sq][h]
                r = jnp.dot(ps[0], vc_ops[key], preferred_element_type=F32)
                if n_ctx:
                    r = r + jnp.dot(ps[1], cv_ops[key], preferred_element_type=F32)
                o = r[:, :128] / (r[:, 128:] + jnp.exp(misc_ref[0:1, h:h + 1] - m))
                pair = o if pair is None else pair + o
            oc_ref[sq, :, j * 128:(j + 1) * 128] = pair
        for h in range(D_HEADS):
            vsl = slice(h * D_VDIM, (h + 1) * D_VDIM)
            v_op = with_ones(vd_ref[sq, :, vsl]).astype(BF16)
            dv_op = with_ones(dv_ref[sq, :, vsl]).astype(BF16) if n_ctx else None
            parts = []
            for c in range(2):
                ps = d_probs[sq][2 * h + c]
                r = jnp.dot(ps[0], v_op, preferred_element_type=F32)
                if n_ctx:
                    r = r + jnp.dot(ps[1], dv_op, preferred_element_type=F32)
                parts.append(r[:, :D_VDIM] / r[:, D_VDIM:])
            o = parts[0] - lam * parts[1]
            o = o * lax.rsqrt(jnp.mean(o * o, axis=-1, keepdims=True) + EPS) * sub_ref[...] * lam_scale
            od_ref[sq, :, vsl] = o


def _attention(qc, kc, vc, qd, kd, vd, caches, misc, subln, lam_init, tq, windowed):
    bsz, n, _ = qc.shape
    ns = ATTN_SEQS_SHORT if (not windowed and tq == n and bsz % ATTN_SEQS_SHORT == 0) else 1
    qblk = lambda w_: pl.BlockSpec((ns, tq, w_), lambda b, i: (b, i, 0))
    kblk = lambda rows, w_: pl.BlockSpec((ns, rows, w_), lambda b, i: (b, 0, 0))
    in_specs = [qblk(512), kblk(n, 128), kblk(n, 128), qblk(512), kblk(n, 512), kblk(n, 512)]
    args = [qc, kc, vc, qd, kd, vd]
    n_ctx = 0
    if caches is not None:
        n_ctx = caches[0].shape[1]
        in_specs += [kblk(n_ctx, 128), kblk(n_ctx, 128), kblk(n_ctx, 512), kblk(n_ctx, 512)]
        args += list(caches)
    in_specs += [pl.BlockSpec((8, 128), lambda b, i: (0, 0)), pl.BlockSpec((1, D_VDIM), lambda b, i: (0, 0))]
    args += [misc, subln]
    return pl.pallas_call(
        functools.partial(_attn_kernel, windowed=windowed, n_ctx=n_ctx, lam_scale=1.0 - lam_init),
        grid=(bsz // ns, n // tq),
        in_specs=in_specs,
        out_specs=[qblk(512), qblk(512)],
        out_shape=[jax.ShapeDtypeStruct((bsz, n, 512), F32), jax.ShapeDtypeStruct((bsz, n, 512), F32)],
        compiler_params=_cparams(("parallel", "parallel")),
        name="attn_win" if windowed else "attn_ctx",
    )(*args)


def _post_kernel(x_ref, xp_ref, xn_ref, ma_ref, map_ref, man_ref, mb_ref, mbp_ref, mbn_ref,
                 g1_ref, sh_ref, sc_ref, g2_ref, ng_ref, wo_ref, wup_ref, cw_ref, wdn_ref, o_ref, act_ref,
                 *, seq_len):
    rows = x_ref.shape[0]
    ext = rows + 2 * POST_HALO
    half = ma_ref.shape[1]
    xe = jnp.concatenate([xp_ref[...], x_ref[...], xn_ref[...]], axis=0)
    mae = jnp.concatenate([map_ref[...], ma_ref[...], man_ref[...]], axis=0)
    mbe = jnp.concatenate([mbp_ref[...], mb_ref[...], mbn_ref[...]], axis=0)
    x1 = xe + g1_ref[0] * (_dot(mae, wo_ref[:half, :]) + _dot(mbe, wo_ref[half:, :]))
    h = _rms_mod(x1, ng_ref[...], sh_ref[0], sc_ref[0]).astype(BF16)
    x1 = x1[POST_HALO:POST_HALO + rows]
    row0 = pl.program_id(0) * rows - POST_HALO
    pos = (row0 + lax.broadcasted_iota(jnp.int32, (ext, 1), 0)) % seq_len
    first = pos == 0
    last = pos == seq_len - 1
    for c in range(FF_CHUNKS):
        cs = slice(c * FF_CHUNK, (c + 1) * FF_CHUNK)
        a = jnp.dot(h, wup_ref[:, cs], preferred_element_type=F32)
        b = jnp.dot(h, wup_ref[:, D_FF + c * FF_CHUNK:D_FF + (c + 1) * FF_CHUNK], preferred_element_type=F32)
        am = jnp.where(first, 0.0, pltpu.roll(a, 1, 0))
        ap = jnp.where(last, 0.0, pltpu.roll(a, ext - 1, 0))
        a = am * cw_ref[0:1, cs] + a * cw_ref[1:2, cs] + ap * cw_ref[2:3, cs] + cw_ref[3:4, cs]
        act_ref[:, cs] = (_silu(a) * b)[POST_HALO:POST_HALO + rows].astype(BF16)
    ffn = jnp.dot(act_ref[...], wdn_ref[...], preferred_element_type=F32)
    o_ref[...] = x1 + g2_ref[0] * ffn


def _post(x, mix_a, mix_b, g1, sh2, sc2, g2, norm_g, w_out, wup, cw, wdn):
    bsz, seq_len, d = x.shape
    half = mix_a.shape[-1]
    rows = POST_ROWS
    total = bsz * seq_len
    nhalo = total // POST_HALO
    per_seq = g1.shape[0] > 1
    midx = (lambda i: ((i * rows) // seq_len, 0, 0)) if per_seq else (lambda i: (0, 0, 0))
    mspec = pl.BlockSpec((1, 1, d), midx)
    pidx = lambda i: (jnp.maximum(i * (rows // POST_HALO) - 1, 0), 0)
    nidx = lambda i: (jnp.minimum((i + 1) * (rows // POST_HALO), nhalo - 1), 0)
    trio = lambda w_: [pl.BlockSpec((rows, w_), lambda i: (i, 0)), pl.BlockSpec((POST_HALO, w_), pidx),
                       pl.BlockSpec((POST_HALO, w_), nidx)]
    whole = lambda shape: pl.BlockSpec(shape, lambda i: (0,) * len(shape), pipeline_mode=pl.Buffered(1))
    x2 = x.reshape(total, d)
    a2 = mix_a.reshape(total, half)
    b2 = mix_b.reshape(total, half)
    out = pl.pallas_call(
        functools.partial(_post_kernel, seq_len=seq_len),
        grid=(total // rows,),
        in_specs=trio(d) + trio(half) + trio(half) + [
            mspec, mspec, mspec, mspec,
            pl.BlockSpec((1, d), lambda i: (0, 0)),
            whole((d, d)), whole((d, 2 * D_FF)), whole((8, D_FF)), whole((D_FF, d))],
        out_specs=pl.BlockSpec((rows, d), lambda i: (i, 0)),
        out_shape=jax.ShapeDtypeStruct((total, d), F32),
        scratch_shapes=[pltpu.VMEM((rows, D_FF), BF16)],
        compiler_params=_cparams(("parallel",)),
        name="post_ffn",
    )(x2, x2, x2, a2, a2, a2, b2, b2, b2, g1, sh2, sc2, g2, norm_g, w_out, wup, cw, wdn)
    return out.reshape(bsz, seq_len, d)


def _ffn_weights(p, l):
    cw = jnp.zeros((8, D_FF), F32).at[:3].set(p['ffn_conv_w'][l]).at[3].set(p['ffn_conv_b'][l])
    return p['ffn_up'][l].astype(BF16), cw, p['ffn_down'][l].astype(BF16)


def _lambda_init(layer):
    return 0.8 - 0.6 * math.exp(-0.3 * layer)


def _trunk(x, mods, p, states, caches):
    bsz, n, d = x.shape
    nseg = n // S5_SEG
    depth = p['w_mod'].shape[0]
    news = {k: [] for k in ('s5r', 's5i', 'gdn', 'ck', 'cv', 'dk', 'dv')}
    for l in range(depth):
        j = l // 2
        sh1, sc1, g1, sh2, sc2, g2 = mods[l]
        ng1 = p['norm1_g'][l][None]
        if l % 2 == 0:
            w_in = p['w_in_ab'][j]
            w_gate = jnp.zeros((d, 128), BF16).at[:, :w_in.shape[1] - AB_MAIN].set(
                w_in[:, AB_MAIN:].astype(BF16))
            rows = bsz * nseg
            per_row = lambda m: jnp.repeat(m, nseg, axis=0) if m.shape[0] > 1 else m
            u_t, qkv, z, ab = _proj_ab(x.reshape(rows, S5_SEG, d), ng1, per_row(sh1), per_row(sc1),
                                       w_in[:, :AB_MAIN].astype(BF16), w_gate)
            qkv, z, ab = (t.reshape(bsz, n, t.shape[-1]) for t in (qkv, z, ab))
            if states is None:
                h0r = jnp.zeros((2, bsz, S5_GROUPS * S5_STATE), F32)
                h0i = h0r
                s0 = None
            else:
                h0r = states[0][:, j].reshape(bsz, 2, -1).transpose(1, 0, 2)
                h0i = states[1][:, j].reshape(bsz, 2, -1).transpose(1, 0, 2)
                s0 = states[2][:, j]
            ya, fr, fi = _s5_mixer(u_t, p, j, h0r, h0i, bsz, nseg)
            yb, sg = _gdn_mixer(qkv, z, ab, p, j, s0)
            mix = (ya.reshape(bsz, n, S5_WIDTH), yb)
            w_out = p['w_out_ab'][j]
            news['s5r'].append(fr.transpose(1, 0, 2).reshape(bsz, 2, S5_GROUPS, S5_STATE))
            news['s5i'].append(fi.transpose(1, 0, 2).reshape(bsz, 2, S5_GROUPS, S5_STATE))
            news['gdn'].append(sg)
        else:
            lam_init = _lambda_init(l)
            f = lambda name: p[name][j]
            lam = (jnp.exp(jnp.sum(f('d_lq1') * f('d_lk1'))) - jnp.exp(jnp.sum(f('d_lq2') * f('d_lk2')))
                   + lam_init)
            misc = jnp.zeros((8, 128), F32).at[0, :C_HEADS].set(p['c_sink'][j]).at[1, :].set(lam)
            rope = caches is not None
            qc, kc, vc, qd, kd, vd = _proj_cd(x, ng1, sh1, sc1, p['w_in_cd'][j].astype(BF16), p, j, rope)
            if caches is None:
                mix = _attention(qc, kc, vc, qd, kd, vd, None, misc, p['d_subln'][j][None],
                                 lam_init, n, False)
            else:
                n_ctx = caches[0].shape[2]
                cc = (caches[0][:, j].reshape(bsz, n_ctx, 128), caches[1][:, j].reshape(bsz, n_ctx, 128),
                      caches[2][:, j].reshape(bsz, n_ctx, 512), caches[3][:, j].reshape(bsz, n_ctx, 512))
                mix = _attention(qc, kc, vc, qd, kd, vd, cc, misc, p['d_subln'][j][None],
                                 lam_init, Q_BLOCK, True)
            w_out = p['w_out_cd'][j]
            news['ck'].append(kc.reshape(bsz, n, C_KV_HEADS, HEAD_DIM))
            news['cv'].append(vc.reshape(bsz, n, C_KV_HEADS, HEAD_DIM))
            news['dk'].append(kd.reshape(bsz, n, D_HEADS, 2, HEAD_DIM))
            news['dv'].append(vd.reshape(bsz, n, D_HEADS, D_VDIM))
        wup, cw, wdn = _ffn_weights(p, l)
        x = _post(x, mix[0], mix[1], g1, sh2, sc2, g2, p['norm2_g'][l][None], w_out.astype(BF16),
                  wup, cw, wdn)
    return x, news


def kernel(x_prompt, x_sample, c, state_s5_re, state_s5_im, state_gdn, cache_c_k, cache_c_v, cache_d_k, cache_d_v, c_ctx, w_mod, b_mod, norm1_g, norm2_g, w_in_ab, w_out_ab, s5_lam_re, s5_lam_im, s5_log_dt, s5_b_re, s5_b_im, s5_c_re, s5_c_im, s5_d, s5_w_glu, s5_b_glu, gdn_conv_w, gdn_a_log, gdn_dt_bias, gdn_norm_g, w_in_cd, w_out_cd, c_qn, c_kn, c_sink, d_qn, d_kn, d_lq1, d_lk1, d_lq2, d_lk2, d_subln, ffn_up, ffn_conv_w, ffn_conv_b, ffn_down):
    p = dict(w_mod=w_mod, b_mod=b_mod, norm1_g=norm1_g, norm2_g=norm2_g, w_in_ab=w_in_ab, w_out_ab=w_out_ab,
             s5_lam_re=s5_lam_re, s5_lam_im=s5_lam_im, s5_log_dt=s5_log_dt, s5_b_re=s5_b_re, s5_b_im=s5_b_im,
             s5_c_re=s5_c_re, s5_c_im=s5_c_im, s5_d=s5_d, s5_w_glu=s5_w_glu, s5_b_glu=s5_b_glu,
             gdn_conv_w=gdn_conv_w, gdn_a_log=gdn_a_log, gdn_dt_bias=gdn_dt_bias, gdn_norm_g=gdn_norm_g,
             w_in_cd=w_in_cd, w_out_cd=w_out_cd, c_qn=c_qn, c_kn=c_kn, c_sink=c_sink, d_qn=d_qn, d_kn=d_kn,
             d_lq1=d_lq1, d_lk1=d_lk1, d_lq2=d_lq2, d_lk2=d_lk2, d_subln=d_subln,
             ffn_up=ffn_up, ffn_conv_w=ffn_conv_w, ffn_conv_b=ffn_conv_b, ffn_down=ffn_down)
    depth = w_mod.shape[0]
    n_dec = c.shape[0]
    mod = _modulation(jnp.concatenate([c_ctx[None], c], axis=0), w_mod, b_mod)
    split6 = lambda m: [m[:, None, k * D_MODEL:(k + 1) * D_MODEL] for k in range(6)]
    mods_ctx = [split6(mod[l, 0:1]) for l in range(depth)]
    mods_dec = [split6(mod[l, 1:1 + n_dec]) for l in range(depth)]

    y_prompt, nw = _trunk(x_prompt, mods_ctx, p, None, None)
    y_sample, _ = _trunk(x_sample, mods_dec, p, (state_s5_re, state_s5_im, state_gdn),
                         (cache_c_k, cache_c_v, cache_d_k, cache_d_v))
    st = lambda name: jnp.stack(nw[name], axis=1)
    return (y_prompt, y_sample, st('s5r'), st('s5i'), st('gdn'), st('ck'), st('cv'), st('dk'), st('dv'))
```

```python
import functools
import math

import jax
import jax.numpy as jnp
from jax import lax
from jax.experimental import pallas as pl
from jax.experimental.pallas import tpu as pltpu

F32 = jnp.float32
BF16 = jnp.bfloat16

D_MODEL = 1024
GRID_W = 64
EPS = 1e-6
NEG_INF = -1e30

S5_WIDTH = 512
S5_GROUP = 16
S5_GROUPS = 32
S5_STATE = 64
S5_TILE_GROUPS = 8
S5_TILE_CH = S5_TILE_GROUPS * S5_GROUP
S5_TILE_ST = S5_TILE_GROUPS * S5_STATE
S5_TILES = S5_GROUPS // S5_TILE_GROUPS
S5_SEG = 256
S5_SUB = 32
PROJ_AB_TOKENS = 64
S5_ROWS = 8

GDN_DK = 128
GDN_DV = 128
GDN_HEADS = 4
GDN_WIDTH = 512
GDN_CHUNK = 64
GDN_GROUP = 4
GDN_SHORT_LEN = 256
GDN_SEQS_SHORT = 2

HEAD_DIM = 64
C_HEADS = 8
C_KV_HEADS = 2
C_GROUP = 4
WINDOW = 128
Q_BLOCK = 128
D_HEADS = 4
D_VDIM = 128
ATTN_SCALE = HEAD_DIM ** -0.5
ATTN_SEQS_SHORT = 2
ROPE_THETA = 10000.0

D_FF = 2816
FF_CHUNK = 256
FF_CHUNKS = D_FF // FF_CHUNK
POST_ROWS = 512
POST_HALO = 8

AB_MAIN = S5_WIDTH + 4 * GDN_WIDTH
AB_PAD = AB_MAIN + 128
CD_IN = 2304

MOD_TK = 128
MOD_LANES = 512
VMEM_LIMIT = 56 * 1024 * 1024


def _cparams(sem):
    return pltpu.CompilerParams(dimension_semantics=sem, vmem_limit_bytes=VMEM_LIMIT)


def _sigmoid(x):
    return 1.0 / (1.0 + jnp.exp(-x))


def _silu(x):
    return x * _sigmoid(x)


def _softplus(x):
    return jnp.maximum(x, 0.0) + jnp.log(1.0 + jnp.exp(-jnp.abs(x)))


def _gelu_tanh(x):
    return 0.5 * x * (1.0 + jnp.tanh(math.sqrt(2.0 / math.pi) * (x + 0.044715 * (x * x * x))))


def _rms_mod(x, g, shift, scale):
    y = x * lax.rsqrt(jnp.mean(x * x, axis=-1, keepdims=True) + EPS)
    return (y * g) * (1.0 + scale) + shift


def _dot(a, b):
    return jnp.dot(a.astype(BF16), b.astype(BF16), preferred_element_type=F32)


def _dot_nt(a, b):
    return lax.dot_general(a.astype(BF16), b.astype(BF16), (((1,), (1,)), ((), ())),
                           preferred_element_type=F32)


def _dot_tn(a, b):
    return lax.dot_general(a.astype(BF16), b.astype(BF16), (((0,), (0,)), ((), ())),
                           preferred_element_type=F32)


def _hi_lo(a):
    hi = a.astype(BF16)
    return hi, (a - hi.astype(F32)).astype(BF16)


def _mod_kernel(ct_ref, w_ref, b_ref, o_ref, acc_ref, *, n_rows):
    k = pl.program_id(1)

    @pl.when(k == 0)
    def _():
        acc_ref[...] = jnp.zeros_like(acc_ref)

    tk, n_out = w_ref.shape[1:]
    s = _silu(ct_ref[...])
    sb = [[jnp.broadcast_to(s[g * 8:(g + 1) * 8, m:m + 1], (8, MOD_LANES)) for g in range(tk // 8)]
          for m in range(n_rows)]
    for c in range(n_out // MOD_LANES):
        cols = slice(c * MOD_LANES, (c + 1) * MOD_LANES)
        accs = [acc_ref[m, :, cols] for m in range(n_rows)]
        for g in range(tk // 8):
            wg = w_ref[0, g * 8:(g + 1) * 8, cols]
            accs = [a + wg * sb[m][g] for m, a in enumerate(accs)]
        for m in range(n_rows):
            acc_ref[m, :, cols] = accs[m]

    @pl.when(k == pl.num_programs(1) - 1)
    def _():
        o_ref[0] = jnp.zeros(o_ref.shape[1:], F32)
        for m in range(n_rows):
            o_ref[0, m:m + 1, :] = jnp.sum(acc_ref[m], axis=0, keepdims=True) + b_ref[0]


def _modulation(cvecs, w_mod, b_mod):
    n, d = cvecs.shape
    depth, _, n_out = w_mod.shape
    ct = jnp.zeros((d, 8), F32).at[:, :n].set(cvecs.T)
    return pl.pallas_call(
        functools.partial(_mod_kernel, n_rows=n),
        grid=(depth, d // MOD_TK),
        in_specs=[pl.BlockSpec((MOD_TK, 8), lambda l, k: (k, 0)),
                  pl.BlockSpec((1, MOD_TK, n_out), lambda l, k: (l, k, 0)),
                  pl.BlockSpec((1, 1, n_out), lambda l, k: (l, 0, 0))],
        out_specs=pl.BlockSpec((1, 8, n_out), lambda l, k: (l, 0, 0)),
        out_shape=jax.ShapeDtypeStruct((depth, 8, n_out), F32),
        scratch_shapes=[pltpu.VMEM((n, 8, n_out), F32)],
        compiler_params=_cparams(("parallel", "arbitrary")),
        name="adaln_mod",
    )(ct, w_mod, b_mod.reshape(depth, 1, n_out))


def _proj_ab_kernel(x_ref, g_ref, sh_ref, sc_ref, w_ref, wg_ref, u_ref, qkv_ref, z_ref, ab_ref):
    ns, tm, d = x_ref.shape
    h = _rms_mod(x_ref[...], g_ref[...], sh_ref[...], sc_ref[...])
    h = h.reshape(ns * tm, d).astype(BF16)
    y = jnp.dot(h, w_ref[...], preferred_element_type=F32)
    for s in range(ns):
        u_ref[:, s, :] = y[s * tm:(s + 1) * tm, :S5_WIDTH]
    qkv_ref[...] = y[:, S5_WIDTH:S5_WIDTH + 3 * GDN_WIDTH].reshape(ns, tm, 3 * GDN_WIDTH)
    z_ref[...] = y[:, S5_WIDTH + 3 * GDN_WIDTH:AB_MAIN].reshape(ns, tm, GDN_WIDTH)
    ab_ref[...] = jnp.dot(h, wg_ref[...], preferred_element_type=F32).reshape(ns, tm, 128)


def _proj_ab(x, g, shift, scale, w, w_gate):
    rows, n, d = x.shape
    tm = PROJ_AB_TOKENS
    per_row = shift.shape[0] > 1
    mspec = pl.BlockSpec((S5_ROWS, 1, d), lambda r, i: (r, 0, 0)) if per_row else \
        pl.BlockSpec((1, 1, d), lambda r, i: (0, 0, 0))
    blk = lambda w_: pl.BlockSpec((S5_ROWS, tm, w_), lambda r, i: (r, i, 0))
    return pl.pallas_call(
        _proj_ab_kernel,
        grid=(rows // S5_ROWS, n // tm),
        in_specs=[blk(d),
                  pl.BlockSpec((1, d), lambda r, i: (0, 0)),
                  mspec, mspec,
                  pl.BlockSpec((d, AB_MAIN), lambda r, i: (0, 0)),
                  pl.BlockSpec((d, 128), lambda r, i: (0, 0))],
        out_specs=[pl.BlockSpec((tm, S5_ROWS, S5_WIDTH), lambda r, i: (i, r, 0)),
                   blk(3 * GDN_WIDTH), blk(GDN_WIDTH), blk(128)],
        out_shape=[jax.ShapeDtypeStruct((n, rows, S5_WIDTH), F32),
                   jax.ShapeDtypeStruct((rows, n, 3 * GDN_WIDTH), F32),
                   jax.ShapeDtypeStruct((rows, n, GDN_WIDTH), F32),
                   jax.ShapeDtypeStruct((rows, n, 128), F32)],
        compiler_params=_cparams(("parallel", "parallel")),
        name="proj_ab",
    )(x, g, shift, scale, w, w_gate)


def _s5_kernel(u_ref, bm_ref, cm_ref, a_ref, h0r_ref, h0i_ref, *rest, want_y):
    if want_y:
        y_ref, fr_ref, fi_ref, xs_ref = rest
    else:
        fr_ref, fi_ref, xs_ref = rest
    n = u_ref.shape[0]
    sub = S5_SUB
    nsub = n // sub
    st = S5_TILE_ST

    def x_proj(d, k):
        u2 = u_ref[k * sub:(k + 1) * sub].reshape(sub * S5_ROWS, S5_TILE_CH)
        xs_ref[d, k * sub:(k + 1) * sub] = _dot(u2, bm_ref[d, 0]).reshape(sub, S5_ROWS, 2 * st)

    ar = [jnp.broadcast_to(a_ref[d, 0, 0:1, :], (S5_ROWS, st)) for d in range(2)]
    ai = [jnp.broadcast_to(a_ref[d, 0, 1:2, :], (S5_ROWS, st)) for d in range(2)]
    hr = [h0r_ref[0], h0r_ref[1]]
    hi = [h0i_ref[0], h0i_ref[1]]
    x_proj(0, 0)
    x_proj(1, nsub - 1)
    written = set()
    for k in range(nsub):
        ks = (k, nsub - 1 - k)
        if k + 1 < nsub:
            x_proj(0, ks[0] + 1)
            x_proj(1, ks[1] - 1)
        for t in range(sub):
            for d in range(2):
                tt = ks[d] * sub + (t if d == 0 else sub - 1 - t)
                x = xs_ref[d, tt]
                nr = ar[d] * hr[d] - ai[d] * hi[d] + x[:, :st]
                ni = ar[d] * hi[d] + ai[d] * hr[d] + x[:, st:]
                xs_ref[d, tt] = jnp.concatenate([nr, ni], axis=-1)
                hr[d], hi[d] = nr, ni
        if want_y:
            for d in range(2):
                rows = slice(ks[d] * sub, (ks[d] + 1) * sub)
                hs = xs_ref[d, rows].reshape(sub * S5_ROWS, 2 * st)
                yv = _dot(hs, cm_ref[d, 0]).reshape(sub, S5_ROWS, S5_TILE_CH)
                if ks[d] in written:
                    y_ref[rows] += yv
                else:
                    y_ref[rows] = yv
                    written.add(ks[d])
    for d in range(2):
        fr_ref[d] = hr[d]
        fi_ref[d] = hi[d]


def _s5_scan(u_t, bmat, cmat, amat, h0r, h0i, want_y=True):
    n, rows, _ = u_t.shape
    state = lambda: pl.BlockSpec((2, S5_ROWS, S5_TILE_ST), lambda r, j: (0, r, j))
    st_shape = jax.ShapeDtypeStruct((2, rows, S5_GROUPS * S5_STATE), F32)
    out_specs = [state(), state()]
    out_shape = [st_shape, st_shape]
    if want_y:
        out_specs.insert(0, pl.BlockSpec((n, S5_ROWS, S5_TILE_CH), lambda r, j: (0, r, j)))
        out_shape.insert(0, jax.ShapeDtypeStruct((n, rows, S5_WIDTH), F32))
    return pl.pallas_call(
        functools.partial(_s5_kernel, want_y=want_y),
        grid=(rows // S5_ROWS, S5_TILES),
        in_specs=[pl.BlockSpec((n, S5_ROWS, S5_TILE_CH), lambda r, j: (0, r, j)),
                  pl.BlockSpec((2, 1, S5_TILE_CH, 2 * S5_TILE_ST), lambda r, j: (0, j, 0, 0)),
                  pl.BlockSpec((2, 1, 2 * S5_TILE_ST, S5_TILE_CH), lambda r, j: (0, j, 0, 0)),
                  pl.BlockSpec((2, 1, 8, S5_TILE_ST), lambda r, j: (0, j, 0, 0)),
                  state(), state()],
        out_specs=out_specs,
        out_shape=out_shape,
        scratch_shapes=[pltpu.VMEM((2, n, S5_ROWS, 2 * S5_TILE_ST), F32)],
        compiler_params=_cparams(("parallel", "parallel")),
        name="s5_scan" if want_y else "s5_states",
    )(u_t, bmat, cmat, amat, h0r, h0i)


def _s5_params(p, j):
    lam_re, lam_im, log_dt = p['s5_lam_re'][j], p['s5_lam_im'][j], p['s5_log_dt'][j]
    dt = jnp.exp(log_dt)[..., None]
    mag = jnp.exp(lam_re * dt)
    ar, ai = mag * jnp.cos(lam_im * dt), mag * jnp.sin(lam_im * dt)
    den = lam_re * lam_re + lam_im * lam_im
    fr = ((ar - 1.0) * lam_re + ai * lam_im) / den
    fi = (ai * lam_re - (ar - 1.0) * lam_im) / den
    b_re, b_im = p['s5_b_re'][j], p['s5_b_im'][j]
    bbr = fr[..., None] * b_re - fi[..., None] * b_im
    bbi = fr[..., None] * b_im + fi[..., None] * b_re
    eye = jnp.eye(S5_TILE_GROUPS, dtype=F32)

    def in_blocks(t):
        t = t.reshape(2, S5_TILES, S5_TILE_GROUPS, S5_STATE, S5_GROUP)
        t = jnp.einsum('dtgpc,gh->dtgchp', t, eye)
        return t.reshape(2, S5_TILES, S5_TILE_CH, S5_TILE_ST)

    def out_blocks(t):
        t = t.reshape(2, S5_TILES, S5_TILE_GROUPS, S5_GROUP, S5_STATE)
        t = jnp.einsum('dtgcp,gh->dtgphc', t, eye)
        return t.reshape(2, S5_TILES, S5_TILE_ST, S5_TILE_CH)

    bmat = jnp.concatenate([in_blocks(bbr), in_blocks(bbi)], axis=-1).astype(BF16)
    cmat = jnp.concatenate([out_blocks(p['s5_c_re'][j]), -out_blocks(p['s5_c_im'][j])], axis=-2).astype(BF16)
    seg_mag = jnp.exp(lam_re * dt * S5_SEG)
    pr, pi = seg_mag * jnp.cos(lam_im * dt * S5_SEG), seg_mag * jnp.sin(lam_im * dt * S5_SEG)
    flat = lambda t: t.reshape(2, S5_TILES, 1, S5_TILE_ST)
    amat = jnp.concatenate([flat(ar), flat(ai), flat(pr), flat(pi),
                            jnp.zeros((2, S5_TILES, 4, S5_TILE_ST), F32)], axis=2)
    return bmat, cmat, amat


def _s5_glu_kernel(y_ref, u_ref, d_ref, w_ref, b_ref, o_ref):
    tm, ns, _ = y_ref.shape
    y = jnp.concatenate([y_ref[:, s, :] + d_ref[...] * u_ref[:, s, :] for s in range(ns)], axis=0)
    g = _gelu_tanh(y)
    out = g * _sigmoid(_dot(g, w_ref[...]) + b_ref[...])
    o_ref[...] = out.reshape(ns, tm, S5_WIDTH)


def _s5_glu(y_t, u_t, s5_d, w_glu, b_glu):
    n, rows, _ = y_t.shape
    tm = PROJ_AB_TOKENS
    tblk = pl.BlockSpec((tm, S5_ROWS, S5_WIDTH), lambda r, i: (i, r, 0))
    return pl.pallas_call(
        _s5_glu_kernel,
        grid=(rows // S5_ROWS, n // tm),
        in_specs=[tblk, tblk,
                  pl.BlockSpec((1, S5_WIDTH), lambda r, i: (0, 0)),
                  pl.BlockSpec((S5_WIDTH, S5_WIDTH), lambda r, i: (0, 0)),
                  pl.BlockSpec((1, S5_WIDTH), lambda r, i: (0, 0))],
        out_specs=pl.BlockSpec((S5_ROWS, tm, S5_WIDTH), lambda r, i: (r, i, 0)),
        out_shape=jax.ShapeDtypeStruct((rows, n, S5_WIDTH), F32),
        compiler_params=_cparams(("parallel", "parallel")),
        name="s5_glu",
    )(y_t, u_t, s5_d, w_glu, b_glu)


def _s5_mixer(u3, p, j, h0r, h0i, bsz, nseg):
    rows = bsz * nseg
    bmat, cmat, amat = _s5_params(p, j)
    if nseg == 1:
        y_t, fr, fi = _s5_scan(u3, bmat, cmat, amat, h0r, h0i)
    else:
        zero = jnp.zeros((2, bsz, nseg, S5_GROUPS * S5_STATE), F32)
        first = jnp.array([0, nseg - 1])
        seed = lambda h0: zero.at[jnp.arange(2), :, first].set(h0).reshape(2, rows, -1)
        fr, fi = _s5_scan(u3, bmat, cmat, amat, seed(h0r), seed(h0i), want_y=False)
        fr = fr.reshape(2, bsz, nseg, -1)
        fi = fi.reshape(2, bsz, nseg, -1)
        pr = amat[:, :, 2].reshape(2, 1, -1)
        pi = amat[:, :, 3].reshape(2, 1, -1)

        def chain(dr, order):
            hr, hi = (h0r[dr], h0i[dr])
            outs_r, outs_i = {}, {}
            for n_done, k in enumerate(order):
                outs_r[k], outs_i[k] = hr, hi
                if n_done == 0:
                    hr, hi = fr[dr, :, k], fi[dr, :, k]
                else:
                    hr, hi = (pr[dr] * hr - pi[dr] * hi + fr[dr, :, k],
                              pr[dr] * hi + pi[dr] * hr + fi[dr, :, k])
            st = lambda o: jnp.stack([o[k] for k in range(nseg)], axis=1)
            return st(outs_r), st(outs_i), hr, hi

        sr0, si0, er0, ei0 = chain(0, list(range(nseg)))
        sr1, si1, er1, ei1 = chain(1, list(range(nseg - 1, -1, -1)))
        start_r = jnp.stack([sr0, sr1]).reshape(2, rows, -1)
        start_i = jnp.stack([si0, si1]).reshape(2, rows, -1)
        y_t, _, _ = _s5_scan(u3, bmat, cmat, amat, start_r, start_i)
        fr = jnp.stack([er0, er1])
        fi = jnp.stack([ei0, ei1])
    ya = _s5_glu(y_t, u3, p['s5_d'][j][None], p['s5_w_glu'][j].astype(BF16), p['s5_b_glu'][j][None])
    return ya, fr, fi


def _gdn_kernel(qkv_ref, z_ref, ab_ref, cw_ref, gp_ref, ng_ref, *rest, zero_init):
    if zero_init:
        s0_ref = None
        o_ref, sf_ref, q_s, k_s, v_s, gate_s, of_s, ob_s, st_s, uw_s, a_s = rest
    else:
        s0_ref, o_ref, sf_ref, q_s, k_s, v_s, gate_s, of_s, ob_s, st_s, uw_s, a_s = rest
    ns, n = qkv_ref.shape[:2]
    nc = n // GDN_CHUNK
    row = lax.broadcasted_iota(jnp.int32, (n, 1), 0)

    for sq in range(ns):
        for blk in range(3 * GDN_HEADS):
            cols = slice(blk * GDN_DK, (blk + 1) * GDN_DK)
            hs = slice((blk % GDN_HEADS) * GDN_DK, (blk % GDN_HEADS + 1) * GDN_DK)
            x = qkv_ref[sq, :, cols]
            xm = jnp.where(row == 0, 0.0, pltpu.roll(x, 1, 0))
            xp = jnp.where(row == n - 1, 0.0, pltpu.roll(x, n - 1, 0))
            y = _silu(xm * cw_ref[0:1, cols] + x * cw_ref[1:2, cols] + xp * cw_ref[2:3, cols])
            if blk < GDN_HEADS:
                q_s[sq, :, hs] = y * lax.rsqrt(jnp.sum(y * y, axis=-1, keepdims=True) + EPS) * (GDN_DK ** -0.5)
            elif blk < 2 * GDN_HEADS:
                k_s[sq, :, hs] = y * lax.rsqrt(jnp.sum(y * y, axis=-1, keepdims=True) + EPS)
            else:
                v_s[sq, :, hs] = y

        ab = ab_ref[sq]
        beta = _sigmoid(ab)
        g = -jnp.exp(gp_ref[0:1, :]) * _softplus(ab + gp_ref[1:2, :])
        pos = row % GDN_CHUNK
        pre, suf = g, g
        sft = 1
        while sft < GDN_CHUNK:
            pre = pre + jnp.where(pos >= sft, pltpu.roll(pre, sft, 0), 0.0)
            suf = suf + jnp.where(pos < GDN_CHUNK - sft, pltpu.roll(suf, n - sft, 0), 0.0)
            sft *= 2
        gate_s[sq, 0] = beta
        gate_s[sq, 1] = pre
        gate_s[sq, 2] = suf

    st_s[...] = jnp.zeros_like(st_s) if zero_init else s0_ref[...]
    cs = GDN_CHUNK
    pk = GDN_HEADS * cs
    ri = lax.broadcasted_iota(jnp.int32, (cs, pk), 0)
    lane_pk = lax.broadcasted_iota(jnp.int32, (cs, pk), 1)
    ci = lane_pk % cs
    eye_pk = (ri == ci).astype(F32)
    head_pk = [(lax.broadcasted_iota(jnp.int32, (1, pk), 1) // cs) == h for h in range(GDN_HEADS)]
    head_w = [(lax.broadcasted_iota(jnp.int32, (1, GDN_WIDTH), 1) // GDN_DK) == h for h in range(GDN_HEADS)]

    def block_diag(p):
        return jnp.concatenate([jnp.where(m, p, jnp.zeros_like(p)) for m in head_pk], axis=0)

    def dot3_bd(a, p):
        a_hi, a_lo = _hi_lo(a)
        p_hi, p_lo = _hi_lo(p)
        b_hi, b_lo = block_diag(p_hi), block_diag(p_lo)
        mm = functools.partial(jnp.dot, preferred_element_type=F32)
        return mm(a_hi, b_hi) + (mm(a_lo, b_hi) + mm(a_hi, b_lo))

    def lanes(cols, width):
        return jnp.concatenate([jnp.broadcast_to(c, (cs, width)) for c in cols], axis=1)

    group = GDN_GROUP // ns

    def phase_a(it, carry):
        st = []
        for sq, cc in [(sq, cc) for sq in range(ns) for cc in range(group)]:
            rows = pl.ds(pl.multiple_of((it * group + cc) * cs, cs), cs)
            q_all, k_all, v_all = q_s[sq, rows, :], k_s[sq, rows, :], v_s[sq, rows, :]
            beta_blk = gate_s[sq, 0, rows, :]
            for dr in range(2):
                gc_blk = gate_s[sq, 1 + dr, rows, :]
                lane0 = dr * GDN_HEADS
                bcols = [beta_blk[:, 8 + lane0 + h:9 + lane0 + h] for h in range(GDN_HEADS)]
                gcols = [gc_blk[:, lane0 + h:lane0 + h + 1] for h in range(GDN_HEADS)]
                st.append(dict(sq=sq, dr=dr, rows=rows, q=q_all, k=k_all, v=v_all, bcols=bcols, gcols=gcols,
                               incl=(ri >= ci) if dr == 0 else (ri <= ci),
                               strict=(ri > ci) if dr == 0 else (ri < ci)))
        for t in st:
            b_w = lanes(t['bcols'], GDN_DK)
            t['kb'] = t['k'] * b_w
            t['vb'] = t['v'] * b_w
            k_bd = jnp.concatenate([jnp.where(m, t['k'], 0.0) for m in head_w], axis=0)
            t['kq'] = _dot_nt(jnp.concatenate([t['kb'], t['q']], axis=0), k_bd)
        for t in st:
            gcol = lanes(t['gcols'], cs)
            grow = jnp.sum(eye_pk * gcol, axis=0, keepdims=True)
            decay = jnp.exp(jnp.where(t['incl'], gcol - grow, NEG_INF))
            t['pw'] = jnp.where(t['strict'], t['kq'][:cs] * decay, 0.0)
            a_s[t['sq'], t['dr'], t['rows'], :] =jnp.where(t['incl'], t['kq'][cs:] * decay, 0.0)
            t['tm'] = eye_pk - t['pw']
        for t in st:
            t['pw'] = dot3_bd(t['pw'], t['pw'])
        for rnd in range(5):
            for t in st:
                if rnd < 4:
                    r = dot3_bd(jnp.concatenate([t['pw'], t['tm']], axis=0), t['pw'])
                    t['pw'] = r[:cs]
                    t['tm'] = t['tm'] + r[cs:]
                else:
                    t['tm'] = t['tm'] + dot3_bd(t['tm'], t['pw'])
        for t in st:
            kbg = t['kb'] * lanes([jnp.exp(g) for g in t['gcols']], GDN_DK)
            for h in range(GDN_HEADS):
                hs = slice(h * GDN_DK, (h + 1) * GDN_DK)
                rhs = jnp.concatenate([t['vb'][:, hs], kbg[:, hs]], axis=1)
                uw_s[t['sq'], t['dr'], t['rows'], 2 * h * GDN_DK:2 * (h + 1) * GDN_DK] = _dot(
                    t['tm'][:, h * cs:(h + 1) * cs], rhs)
        return carry

    lax.fori_loop(0, nc // group, phase_a, 0)

    def phase_b(c, carry):
        ch = []
        for sq, dr in [(sq, dr) for sq in range(ns) for dr in range(2)]:
            cidx = c if dr == 0 else nc - 1 - c
            rows = pl.ds(pl.multiple_of(cidx * cs, cs), cs)
            gc_blk = gate_s[sq, 1 + dr, rows, :]
            for h in range(GDN_HEADS):
                hs = slice(h * GDN_DK, (h + 1) * GDN_DK)
                lane = dr * GDN_HEADS + h
                gcol = gc_blk[:, lane:lane + 1]
                ch.append(dict(sq=sq, dr=dr, h=h, rows=rows, hs=hs, gcol=gcol,
                               gl=gcol[cs - 1:cs] if dr == 0 else gcol[0:1],
                               uw=uw_s[sq, dr, rows, 2 * h * GDN_DK:2 * (h + 1) * GDN_DK],
                               amat=a_s[sq, dr, rows, h * cs:(h + 1) * cs]))
        for t in ch:
            t['s'] = st_s[t['sq'], t['dr'], t['h']]
            qg = q_s[t['sq'], t['rows'], t['hs']] * jnp.exp(t['gcol'])
            t['ws'] = _dot(jnp.concatenate([t['uw'][:, GDN_DV:], qg], axis=0), t['s'])
        for t in ch:
            vn = t['uw'][:, :GDN_DV] - t['ws'][:cs]
            o = t['ws'][cs:] + _dot(t['amat'], vn)
            kg = k_s[t['sq'], t['rows'], t['hs']] * jnp.exp(t['gl'] - t['gcol'])
            st_s[t['sq'], t['dr'], t['h']] = t['s'] * jnp.exp(t['gl']) + _dot_tn(kg, vn)
            if t['dr'] == 0:
                of_s[t['sq'], t['rows'], t['hs']] = o
            else:
                ob_s[t['sq'], t['rows'], t['hs']] = o
        return carry

    lax.fori_loop(0, nc, phase_b, 0)
    sf_ref[...] = st_s[...]

    for sq in range(ns):
        z = z_ref[sq]
        for h in range(GDN_HEADS):
            hs = slice(h * GDN_DV, (h + 1) * GDN_DV)
            o = of_s[sq, :, hs] + ob_s[sq, :, hs]
            o = o * lax.rsqrt(jnp.mean(o * o, axis=-1, keepdims=True) + EPS) * ng_ref[...]
            o_ref[sq, :, hs] = o * _silu(z[:, hs])


def _gdn_mixer(qkv, z, ab, p, j, s0):
    bsz, n, _ = qkv.shape
    gp = jnp.zeros((8, 128), F32)
    gp = gp.at[0, :8].set(p['gdn_a_log'][j].reshape(8)).at[1, :8].set(p['gdn_dt_bias'][j].reshape(8))
    cw = jnp.zeros((8, 3 * GDN_WIDTH), F32).at[:3].set(p['gdn_conv_w'][j])
    ns = GDN_SEQS_SHORT if (n <= GDN_SHORT_LEN and bsz % GDN_SEQS_SHORT == 0) else 1
    blk = lambda w_: pl.BlockSpec((ns, n, w_), lambda b: (b, 0, 0))
    sblk = pl.BlockSpec((ns, 2, GDN_HEADS, GDN_DK, GDN_DV), lambda b: (b, 0, 0, 0, 0))
    tok = lambda w_: pltpu.VMEM((ns, n, w_), F32)
    return pl.pallas_call(
        functools.partial(_gdn_kernel, zero_init=s0 is None),
        grid=(bsz // ns,),
        in_specs=[blk(3 * GDN_WIDTH), blk(GDN_WIDTH), blk(128),
                  pl.BlockSpec((8, 3 * GDN_WIDTH), lambda b: (0, 0)),
                  pl.BlockSpec((8, 128), lambda b: (0, 0)),
                  pl.BlockSpec((1, GDN_DV), lambda b: (0, 0))] + ([] if s0 is None else [sblk]),
        out_specs=[blk(GDN_WIDTH), sblk],
        out_shape=[jax.ShapeDtypeStruct((bsz, n, GDN_WIDTH), F32),
                   jax.ShapeDtypeStruct((bsz, 2, GDN_HEADS, GDN_DK, GDN_DV), F32)],
        scratch_shapes=[tok(GDN_WIDTH), tok(GDN_WIDTH), tok(GDN_WIDTH),
                        pltpu.VMEM((ns, 3, n, 128), F32),
                        tok(GDN_WIDTH), tok(GDN_WIDTH),
                        pltpu.VMEM((ns, 2, GDN_HEADS, GDN_DK, GDN_DV), F32),
                        pltpu.VMEM((ns, 2, n, 2 * GDN_WIDTH), F32),
                        pltpu.VMEM((ns, 2, n, GDN_HEADS * GDN_CHUNK), F32)],
        compiler_params=_cparams(("parallel",)),
        name="gdn_mixer",
    )(qkv, z, ab, cw, gp, p['gdn_norm_g'][j][None], *(() if s0 is None else (s0,)))


def _proj_cd_kernel(x_ref, g_ref, sh_ref, sc_ref, w_ref, gm_ref, gain_ref, cos_ref, sin_ref,
                    qc_ref, kc_ref, vc_ref, qd_ref, kd_ref, vd_ref, *, rope):
    h = _rms_mod(x_ref[0], g_ref[...], sh_ref[0], sc_ref[0])
    y = _dot(h, w_ref[...])
    lane = lax.broadcasted_iota(jnp.int32, (1, 512), 1)
    low = (lane % 32) < 16

    def head_norm(t, gain, scale):
        w = t.shape[1]
        ms = _dot(t * t, gm_ref[:w, :w])
        t = t * lax.rsqrt(ms + EPS) * gain
        if rope:
            part = jnp.where(low[:, :w], pltpu.roll(t, w - 16, 1), pltpu.roll(t, 16, 1))
            t = t * cos_ref[:, :w] + part * sin_ref[:, :w]
        return t * scale if scale != 1.0 else t

    qc_ref[0] = head_norm(y[:, 0:512], gain_ref[0:1, :], ATTN_SCALE)
    kc_ref[0] = head_norm(y[:, 512:640], gain_ref[1:2, :128], 1.0)
    vc_ref[0] = y[:, 640:768]
    qd_ref[0] = head_norm(y[:, 768:1280], gain_ref[2:3, :], ATTN_SCALE)
    kd_ref[0] = head_norm(y[:, 1280:1792], gain_ref[3:4, :], 1.0)
    vd_ref[0] = y[:, 1792:2304]


def _rope_tables(n):
    rows = n // GRID_W
    row = jnp.repeat(jnp.arange(rows), GRID_W).astype(F32)
    col = jnp.tile(jnp.arange(GRID_W), rows).astype(F32)
    quarter = HEAD_DIM // 4
    inv = ROPE_THETA ** (-jnp.arange(quarter, dtype=F32) / quarter)
    ang_r = row[:, None] * inv[None, :]
    ang_c = col[:, None] * inv[None, :]
    cos = jnp.concatenate([jnp.cos(ang_r), jnp.cos(ang_r), jnp.cos(ang_c), jnp.cos(ang_c)], axis=-1)
    sin = jnp.concatenate([-jnp.sin(ang_r), jnp.sin(ang_r), -jnp.sin(ang_c), jnp.sin(ang_c)], axis=-1)
    return jnp.tile(cos, (1, 8)), jnp.tile(sin, (1, 8))


def _proj_cd(x, g, shift, scale, w, p, j, rope):
    bsz, n, d = x.shape
    tm = 256
    per_seq = shift.shape[0] > 1
    midx = (lambda b, i: (b, 0, 0)) if per_seq else (lambda b, i: (0, 0, 0))
    lane = jnp.arange(512)
    gmat = ((lane[:, None] // HEAD_DIM) == (lane[None, :] // HEAD_DIM)).astype(F32) / HEAD_DIM
    gains = jnp.zeros((8, 512), F32)
    gains = gains.at[0].set(jnp.tile(p['c_qn'][j], 8)).at[1].set(jnp.tile(p['c_kn'][j], 8))
    gains = gains.at[2].set(jnp.tile(p['d_qn'][j], 8)).at[3].set(jnp.tile(p['d_kn'][j], 8))
    if rope:
        cos, sin = _rope_tables(n)
    else:
        cos, sin = jnp.ones((tm, 512), F32), jnp.zeros((tm, 512), F32)
    tidx = (lambda b, i: (i, 0)) if rope else (lambda b, i: (0, 0))
    blk = lambda w_: pl.BlockSpec((1, tm, w_), lambda b, i: (b, i, 0))
    return pl.pallas_call(
        functools.partial(_proj_cd_kernel, rope=rope),
        grid=(bsz, n // tm),
        in_specs=[blk(d),
                  pl.BlockSpec((1, d), lambda b, i: (0, 0)),
                  pl.BlockSpec((1, 1, d), midx),
                  pl.BlockSpec((1, 1, d), midx),
                  pl.BlockSpec((d, CD_IN), lambda b, i: (0, 0)),
                  pl.BlockSpec((512, 512), lambda b, i: (0, 0)),
                  pl.BlockSpec((8, 512), lambda b, i: (0, 0)),
                  pl.BlockSpec((tm, 512), tidx),
                  pl.BlockSpec((tm, 512), tidx)],
        out_specs=[blk(512), blk(128), blk(128), blk(512), blk(512), blk(512)],
        out_shape=[jax.ShapeDtypeStruct((bsz, n, w_), F32) for w_ in (512, 128, 128, 512, 512, 512)],
        compiler_params=_cparams(("parallel", "parallel")),
        name="proj_cd",
    )(x, g, shift, scale, w, gmat.astype(BF16), gains, cos, sin)


def _probs(scores, extra=None):
    m = scores[0].max(axis=-1, keepdims=True)
    for s in scores[1:]:
        m = jnp.maximum(m, s.max(axis=-1, keepdims=True))
    if extra is not None:
        m = jnp.maximum(m, extra)
    return [jnp.exp(s - m).astype(BF16) for s in scores], m


def _attn_kernel(qc_ref, kc_ref, vc_ref, qd_ref, kd_ref, vd_ref, *rest, windowed, n_ctx, lam_scale):
    if n_ctx:
        ck_ref, cv_ref, dk_ref, dv_ref, misc_ref, sub_ref, oc_ref, od_ref = rest
    else:
        misc_ref, sub_ref, oc_ref, od_ref = rest
    tq = qc_ref.shape[1]
    n = kc_ref.shape[1]
    start = pl.program_id(1) * tq
    if windowed:
        span = tq + 2 * WINDOW
        k0 = pl.multiple_of(jnp.clip(start - WINDOW, 0, n - span), 128)
        krows = pl.ds(k0, span)
        qpos = start + lax.broadcasted_iota(jnp.int32, (tq, span), 0)
        kpos = k0 + lax.broadcasted_iota(jnp.int32, (tq, span), 1)
        ok = jnp.abs(qpos - kpos) <= WINDOW
    else:
        krows = pl.ds(0, n)
    low = lax.broadcasted_iota(jnp.int32, (1, 2 * HEAD_DIM), 1) < HEAD_DIM

    def key_ops(k):
        kr = pltpu.roll(k, HEAD_DIM, 1)
        z = jnp.zeros_like(k)
        return {(0, 0): jnp.where(low, k, z).astype(BF16), (0, 1): jnp.where(low, z, kr).astype(BF16),
                (1, 0): jnp.where(low, kr, z).astype(BF16), (1, 1): jnp.where(low, z, k).astype(BF16)}

    def with_ones(v):
        return jnp.concatenate([v, jnp.ones_like(v)], axis=1)

    def val_ops(v):
        return {key: with_ones(op) for key, op in key_ops(v).items()}

    def half_ops(k):
        z = jnp.zeros_like(k)
        return [jnp.where(low, k, z).astype(BF16), jnp.where(low, z, k).astype(BF16)]

    seqs = range(qc_ref.shape[0])

    c_scores, d_scores = [[] for _ in seqs], [[] for _ in seqs]
    for sq in seqs:
        qc = qc_ref[sq].astype(BF16)
        qd = qd_ref[sq].astype(BF16)
        kc_ops = key_ops(kc_ref[sq, krows, :])
        ck_ops = key_ops(ck_ref[sq]) if n_ctx else None
        for h in range(C_HEADS):
            key = (h // C_GROUP, h % 2)
            q = qc[:, (h // 2) * 128:(h // 2 + 1) * 128]
            sc = [_dot_nt(q, kc_ops[key])]
            if n_ctx:
                sc.append(_dot_nt(q, ck_ops[key]))
            c_scores[sq].append(sc)
        for h in range(D_HEADS):
            q = qd[:, h * 128:(h + 1) * 128]
            kd_ops = half_ops(kd_ref[sq, :, h * 128:(h + 1) * 128])
            dk_ops = half_ops(dk_ref[sq, :, h * 128:(h + 1) * 128]) if n_ctx else None
            for c in range(2):
                sc = [_dot_nt(q, kd_ops[c])]
                if n_ctx:
                    sc.append(_dot_nt(q, dk_ops[c]))
                d_scores[sq].append(sc)

    c_probs, d_probs = [[] for _ in seqs], [[] for _ in seqs]
    for sq in seqs:
        for h in range(C_HEADS):
            sc = c_scores[sq][h]
            if windowed:
                sc[0] = jnp.where(ok, sc[0], NEG_INF)
            c_probs[sq].append(_probs(sc, misc_ref[0:1, h:h + 1]))
        d_probs[sq] = [_probs(sc)[0] for sc in d_scores[sq]]

    lam = misc_ref[1:2, 0:1]
    for sq in seqs:
        vc_ops = val_ops(vc_ref[sq, krows, :])
        cv_ops = val_ops(cv_ref[sq]) if n_ctx else None
        for j in range(C_HEADS // 2):
            pair = None
            for h in (2 * j, 2 * j + 1):
                key = (h // C_GROUP, h % 2)
                ps, m = c_probs[sq][h]
                r = jnp.dot(ps[0], vc_ops[key], preferred_element_type=F32)
                if n_ctx:
                    r = r + jnp.dot(ps[1], cv_ops[key], preferred_element_type=F32)
                o = r[:, :128] / (r[:, 128:] + jnp.exp(misc_ref[0:1, h:h + 1] - m))
                pair = o if pair is None else pair + o
            oc_ref[sq, :, j * 128:(j + 1) * 128] = pair
        for h in range(D_HEADS):
            vsl = slice(h * D_VDIM, (h + 1) * D_VDIM)
            v_op = with_ones(vd_ref[sq, :, vsl]).astype(BF16)
            dv_op = with_ones(dv_ref[sq, :, vsl]).astype(BF16) if n_ctx else None
            parts = []
            for c in range(2):
                ps = d_probs[sq][2 * h + c]
                r = jnp.dot(ps[0], v_op, preferred_element_type=F32)
                if n_ctx:
                    r = r + jnp.dot(ps[1], dv_op, preferred_element_type=F32)
                parts.append(r[:, :D_VDIM] / r[:, D_VDIM:])
            o = parts[0] - lam * parts[1]
            o = o * lax.rsqrt(jnp.mean(o * o, axis=-1, keepdims=True) + EPS) * sub_ref[...] * lam_scale
            od_ref[sq, :, vsl] = o


def _attention(qc, kc, vc, qd, kd, vd, caches, misc, subln, lam_init, tq, windowed):
    bsz, n, _ = qc.shape
    ns = ATTN_SEQS_SHORT if (not windowed and tq == n and bsz % ATTN_SEQS_SHORT == 0) else 1
    qblk = lambda w_: pl.BlockSpec((ns, tq, w_), lambda b, i: (b, i, 0))
    kblk = lambda rows, w_: pl.BlockSpec((ns, rows, w_), lambda b, i: (b, 0, 0))
    in_specs = [qblk(512), kblk(n, 128), kblk(n, 128), qblk(512), kblk(n, 512), kblk(n, 512)]
    args = [qc, kc, vc, qd, kd, vd]
    n_ctx = 0
    if caches is not None:
        n_ctx = caches[0].shape[1]
        in_specs += [kblk(n_ctx, 128), kblk(n_ctx, 128), kblk(n_ctx, 512), kblk(n_ctx, 512)]
        args += list(caches)
    in_specs += [pl.BlockSpec((8, 128), lambda b, i: (0, 0)), pl.BlockSpec((1, D_VDIM), lambda b, i: (0, 0))]
    args += [misc, subln]
    return pl.pallas_call(
        functools.partial(_attn_kernel, windowed=windowed, n_ctx=n_ctx, lam_scale=1.0 - lam_init),
        grid=(bsz // ns, n // tq),
        in_specs=in_specs,
        out_specs=[qblk(512), qblk(512)],
        out_shape=[jax.ShapeDtypeStruct((bsz, n, 512), F32), jax.ShapeDtypeStruct((bsz, n, 512), F32)],
        compiler_params=_cparams(("parallel", "parallel")),
        name="attn_win" if windowed else "attn_ctx",
    )(*args)


def _post_kernel(x_ref, xp_ref, xn_ref, ma_ref, map_ref, man_ref, mb_ref, mbp_ref, mbn_ref,
                 g1_ref, sh_ref, sc_ref, g2_ref, ng_ref, wo_ref, wup_ref, cw_ref, wdn_ref, o_ref, act_ref,
                 *, seq_len):
    rows = x_ref.shape[0]
    ext = rows + 2 * POST_HALO
    half = ma_ref.shape[1]
    xe = jnp.concatenate([xp_ref[...], x_ref[...], xn_ref[...]], axis=0)
    mae = jnp.concatenate([map_ref[...], ma_ref[...], man_ref[...]], axis=0)
    mbe = jnp.concatenate([mbp_ref[...], mb_ref[...], mbn_ref[...]], axis=0)
    x1 = xe + g1_ref[0] * (_dot(mae, wo_ref[:half, :]) + _dot(mbe, wo_ref[half:, :]))
    h = _rms_mod(x1, ng_ref[...], sh_ref[0], sc_ref[0]).astype(BF16)
    x1 = x1[POST_HALO:POST_HALO + rows]
    row0 = pl.program_id(0) * rows - POST_HALO
    pos = (row0 + lax.broadcasted_iota(jnp.int32, (ext, 1), 0)) % seq_len
    first = pos == 0
    last = pos == seq_len - 1
    for c in range(FF_CHUNKS):
        cs = slice(c * FF_CHUNK, (c + 1) * FF_CHUNK)
        a = jnp.dot(h, wup_ref[:, cs], preferred_element_type=F32)
        b = jnp.dot(h, wup_ref[:, D_FF + c * FF_CHUNK:D_FF + (c + 1) * FF_CHUNK], preferred_element_type=F32)
        am = jnp.where(first, 0.0, pltpu.roll(a, 1, 0))
        ap = jnp.where(last, 0.0, pltpu.roll(a, ext - 1, 0))
        a = am * cw_ref[0:1, cs] + a * cw_ref[1:2, cs] + ap * cw_ref[2:3, cs] + cw_ref[3:4, cs]
        act_ref[:, cs] = (_silu(a) * b)[POST_HALO:POST_HALO + rows].astype(BF16)
    ffn = jnp.dot(act_ref[...], wdn_ref[...], preferred_element_type=F32)
    o_ref[...] = x1 + g2_ref[0] * ffn


def _post(x, mix_a, mix_b, g1, sh2, sc2, g2, norm_g, w_out, wup, cw, wdn, layer):
    bsz, seq_len, d = x.shape
    half = mix_a.shape[-1]
    rows = POST_ROWS
    total = bsz * seq_len
    nhalo = total // POST_HALO
    per_seq = g1.shape[0] > 1
    midx = (lambda i: ((i * rows) // seq_len, 0, 0)) if per_seq else (lambda i: (0, 0, 0))
    mspec = pl.BlockSpec((1, 1, d), midx)
    pidx = lambda i: (jnp.maximum(i * (rows // POST_HALO) - 1, 0), 0)
    nidx = lambda i: (jnp.minimum((i + 1) * (rows // POST_HALO), nhalo - 1), 0)
    trio = lambda w_: [pl.BlockSpec((rows, w_), lambda i: (i, 0)), pl.BlockSpec((POST_HALO, w_), pidx),
                       pl.BlockSpec((POST_HALO, w_), nidx)]
    whole = lambda shape: pl.BlockSpec(shape, lambda i: (0,) * len(shape), pipeline_mode=pl.Buffered(1))
    x2 = x.reshape(total, d)
    a2 = mix_a.reshape(total, half)
    b2 = mix_b.reshape(total, half)
    out = pl.pallas_call(
        functools.partial(_post_kernel, seq_len=seq_len),
        grid=(total // rows,),
        in_specs=trio(d) + trio(half) + trio(half) + [
            mspec, mspec, mspec, mspec,
            pl.BlockSpec((1, d), lambda i: (0, 0)),
            whole((d, d)),
            pl.BlockSpec((None, d, 2 * D_FF), lambda i: (layer, 0, 0), pipeline_mode=pl.Buffered(1)),
            whole((8, D_FF)),
            pl.BlockSpec((None, D_FF, d), lambda i: (layer, 0, 0), pipeline_mode=pl.Buffered(1))],
        out_specs=pl.BlockSpec((rows, d), lambda i: (i, 0)),
        out_shape=jax.ShapeDtypeStruct((total, d), F32),
        scratch_shapes=[pltpu.VMEM((rows, D_FF), BF16)],
        compiler_params=_cparams(("parallel",)),
        name="post_ffn",
    )(x2, x2, x2, a2, a2, a2, b2, b2, b2, g1, sh2, sc2, g2, norm_g, w_out, wup, cw, wdn)
    return out.reshape(bsz, seq_len, d)


def _ffn_weights(p, l):
    cw = jnp.zeros((8, D_FF), F32).at[:3].set(p['ffn_conv_w'][l]).at[3].set(p['ffn_conv_b'][l])
    return p['ffn_up'].astype(BF16), cw, p['ffn_down'].astype(BF16)


def _lambda_init(layer):
    return 0.8 - 0.6 * math.exp(-0.3 * layer)


def _trunk(x, mods, p, states, caches):
    bsz, n, d = x.shape
    nseg = n // S5_SEG
    depth = p['w_mod'].shape[0]
    news = {k: [] for k in ('s5r', 's5i', 'gdn', 'ck', 'cv', 'dk', 'dv')}
    for l in range(depth):
        j = l // 2
        sh1, sc1, g1, sh2, sc2, g2 = mods[l]
        ng1 = p['norm1_g'][l][None]
        if l % 2 == 0:
            w_in = p['w_in_ab'][j]
            w_gate = jnp.zeros((d, 128), BF16).at[:, :w_in.shape[1] - AB_MAIN].set(
                w_in[:, AB_MAIN:].astype(BF16))
            rows = bsz * nseg
            per_row = lambda m: jnp.repeat(m, nseg, axis=0) if m.shape[0] > 1 else m
            u_t, qkv, z, ab = _proj_ab(x.reshape(rows, S5_SEG, d), ng1, per_row(sh1), per_row(sc1),
                                       w_in[:, :AB_MAIN].astype(BF16), w_gate)
            qkv, z, ab = (t.reshape(bsz, n, t.shape[-1]) for t in (qkv, z, ab))
            if states is None:
                h0r = jnp.zeros((2, bsz, S5_GROUPS * S5_STATE), F32)
                h0i = h0r
                s0 = None
            else:
                h0r = states[0][:, j].reshape(bsz, 2, -1).transpose(1, 0, 2)
                h0i = states[1][:, j].reshape(bsz, 2, -1).transpose(1, 0, 2)
                s0 = states[2][:, j]
            ya, fr, fi = _s5_mixer(u_t, p, j, h0r, h0i, bsz, nseg)
            yb, sg = _gdn_mixer(qkv, z, ab, p, j, s0)
            mix = (ya.reshape(bsz, n, S5_WIDTH), yb)
            w_out = p['w_out_ab'][j]
            news['s5r'].append(fr.transpose(1, 0, 2).reshape(bsz, 2, S5_GROUPS, S5_STATE))
            news['s5i'].append(fi.transpose(1, 0, 2).reshape(bsz, 2, S5_GROUPS, S5_STATE))
            news['gdn'].append(sg)
        else:
            lam_init = _lambda_init(l)
            f = lambda name: p[name][j]
            lam = (jnp.exp(jnp.sum(f('d_lq1') * f('d_lk1'))) - jnp.exp(jnp.sum(f('d_lq2') * f('d_lk2')))
                   + lam_init)
            misc = jnp.zeros((8, 128), F32).at[0, :C_HEADS].set(p['c_sink'][j]).at[1, :].set(lam)
            rope = caches is not None
            qc, kc, vc, qd, kd, vd = _proj_cd(x, ng1, sh1, sc1, p['w_in_cd'][j].astype(BF16), p, j, rope)
            if caches is None:
                mix = _attention(qc, kc, vc, qd, kd, vd, None, misc, p['d_subln'][j][None],
                                 lam_init, n, False)
            else:
                n_ctx = caches[0].shape[2]
                cc = (caches[0][:, j].reshape(bsz, n_ctx, 128), caches[1][:, j].reshape(bsz, n_ctx, 128),
                      caches[2][:, j].reshape(bsz, n_ctx, 512), caches[3][:, j].reshape(bsz, n_ctx, 512))
                mix = _attention(qc, kc, vc, qd, kd, vd, cc, misc, p['d_subln'][j][None],
                                 lam_init, Q_BLOCK, True)
            w_out = p['w_out_cd'][j]
            news['ck'].append(kc.reshape(bsz, n, C_KV_HEADS, HEAD_DIM))
            news['cv'].append(vc.reshape(bsz, n, C_KV_HEADS, HEAD_DIM))
            news['dk'].append(kd.reshape(bsz, n, D_HEADS, 2, HEAD_DIM))
            news['dv'].append(vd.reshape(bsz, n, D_HEADS, D_VDIM))
        wup, cw, wdn = _ffn_weights(p, l)
        x = _post(x, mix[0], mix[1], g1, sh2, sc2, g2, p['norm2_g'][l][None], w_out.astype(BF16),
                  wup, cw, wdn, l)
    return x, news


def kernel(x_prompt, x_sample, c, state_s5_re, state_s5_im, state_gdn, cache_c_k, cache_c_v, cache_d_k, cache_d_v, c_ctx, w_mod, b_mod, norm1_g, norm2_g, w_in_ab, w_out_ab, s5_lam_re, s5_lam_im, s5_log_dt, s5_b_re, s5_b_im, s5_c_re, s5_c_im, s5_d, s5_w_glu, s5_b_glu, gdn_conv_w, gdn_a_log, gdn_dt_bias, gdn_norm_g, w_in_cd, w_out_cd, c_qn, c_kn, c_sink, d_qn, d_kn, d_lq1, d_lk1, d_lq2, d_lk2, d_subln, ffn_up, ffn_conv_w, ffn_conv_b, ffn_down):
    p = dict(w_mod=w_mod, b_mod=b_mod, norm1_g=norm1_g, norm2_g=norm2_g, w_in_ab=w_in_ab, w_out_ab=w_out_ab,
             s5_lam_re=s5_lam_re, s5_lam_im=s5_lam_im, s5_log_dt=s5_log_dt, s5_b_re=s5_b_re, s5_b_im=s5_b_im,
             s5_c_re=s5_c_re, s5_c_im=s5_c_im, s5_d=s5_d, s5_w_glu=s5_w_glu, s5_b_glu=s5_b_glu,
             gdn_conv_w=gdn_conv_w, gdn_a_log=gdn_a_log, gdn_dt_bias=gdn_dt_bias, gdn_norm_g=gdn_norm_g,
             w_in_cd=w_in_cd, w_out_cd=w_out_cd, c_qn=c_qn, c_kn=c_kn, c_sink=c_sink, d_qn=d_qn, d_kn=d_kn,
             d_lq1=d_lq1, d_lk1=d_lk1, d_lq2=d_lq2, d_lk2=d_lk2, d_subln=d_subln,
             ffn_up=ffn_up, ffn_conv_w=ffn_conv_w, ffn_conv_b=ffn_conv_b, ffn_down=ffn_down)
    depth = w_mod.shape[0]
    n_dec = c.shape[0]
    mod = _modulation(jnp.concatenate([c_ctx[None], c], axis=0), w_mod, b_mod)
    split6 = lambda m: [m[:, None, k * D_MODEL:(k + 1) * D_MODEL] for k in range(6)]
    mods_ctx = [split6(mod[l, 0:1]) for l in range(depth)]
    mods_dec = [split6(mod[l, 1:1 + n_dec]) for l in range(depth)]

    y_prompt, nw = _trunk(x_prompt, mods_ctx, p, None, None)
    y_sample, _ = _trunk(x_sample, mods_dec, p, (state_s5_re, state_s5_im, state_gdn),
                         (cache_c_k, cache_c_v, cache_d_k, cache_d_v))
    st = lambda name: jnp.stack(nw[name], axis=1)
    return (y_prompt, y_sample, st('s5r'), st('s5i'), st('gdn'), st('ck'), st('cv'), st('dk'), st('dv'))
```

```python
import functools
import math

import jax
import jax.numpy as jnp
from jax import lax
from jax.experimental import pallas as pl
from jax.experimental.pallas import tpu as pltpu

F32 = jnp.float32
BF16 = jnp.bfloat16

D_MODEL = 1024
GRID_W = 64
EPS = 1e-6
NEG_INF = -1e30

S5_WIDTH = 512
S5_GROUP = 16
S5_GROUPS = 32
S5_STATE = 64
S5_TILE_GROUPS = 8
S5_TILE_CH = S5_TILE_GROUPS * S5_GROUP
S5_TILE_ST = S5_TILE_GROUPS * S5_STATE
S5_TILES = S5_GROUPS // S5_TILE_GROUPS
S5_SEG = 256
S5_SUB = 32
PROJ_AB_TOKENS = 64
S5_ROWS = 8

GDN_DK = 128
GDN_DV = 128
GDN_HEADS = 4
GDN_WIDTH = 512
GDN_CHUNK = 64
GDN_GROUP = 4
GDN_SHORT_LEN = 256
GDN_SEQS_SHORT = 2

HEAD_DIM = 64
C_HEADS = 8
C_KV_HEADS = 2
C_GROUP = 4
WINDOW = 128
Q_BLOCK = 128
D_HEADS = 4
D_VDIM = 128
ATTN_SCALE = HEAD_DIM ** -0.5
ATTN_SEQS_SHORT = 2
ROPE_THETA = 10000.0

D_FF = 2816
FF_CHUNK = 256
FF_CHUNKS = D_FF // FF_CHUNK
POST_ROWS = 512
POST_HALO = 8

AB_MAIN = S5_WIDTH + 4 * GDN_WIDTH
AB_PAD = AB_MAIN + 128
CD_IN = 2304

MOD_TK = 128
MOD_LANES = 512
VMEM_LIMIT = 56 * 1024 * 1024


def _cparams(sem):
    return pltpu.CompilerParams(dimension_semantics=sem, vmem_limit_bytes=VMEM_LIMIT)


def _sigmoid(x):
    return 1.0 / (1.0 + jnp.exp(-x))


def _silu(x):
    return x * _sigmoid(x)


def _softplus(x):
    return jnp.maximum(x, 0.0) + jnp.log(1.0 + jnp.exp(-jnp.abs(x)))


def _gelu_tanh(x):
    return 0.5 * x * (1.0 + jnp.tanh(math.sqrt(2.0 / math.pi) * (x + 0.044715 * (x * x * x))))


def _rms_mod(x, g, shift, scale):
    y = x * lax.rsqrt(jnp.mean(x * x, axis=-1, keepdims=True) + EPS)
    return (y * g) * (1.0 + scale) + shift


def _dot(a, b):
    return jnp.dot(a.astype(BF16), b.astype(BF16), preferred_element_type=F32)


def _dot_nt(a, b):
    return lax.dot_general(a.astype(BF16), b.astype(BF16), (((1,), (1,)), ((), ())),
                           preferred_element_type=F32)


def _dot_tn(a, b):
    return lax.dot_general(a.astype(BF16), b.astype(BF16), (((0,), (0,)), ((), ())),
                           preferred_element_type=F32)


def _hi_lo(a):
    hi = a.astype(BF16)
    return hi, (a - hi.astype(F32)).astype(BF16)


def _mod_kernel(ct_ref, w_ref, b_ref, o_ref, acc_ref, *, n_rows):
    k = pl.program_id(1)

    @pl.when(k == 0)
    def _():
        acc_ref[...] = jnp.zeros_like(acc_ref)

    tk, n_out = w_ref.shape[1:]
    s = _silu(ct_ref[...])
    sb = [[jnp.broadcast_to(s[g * 8:(g + 1) * 8, m:m + 1], (8, MOD_LANES)) for g in range(tk // 8)]
          for m in range(n_rows)]
    for c in range(n_out // MOD_LANES):
        cols = slice(c * MOD_LANES, (c + 1) * MOD_LANES)
        accs = [acc_ref[m, :, cols] for m in range(n_rows)]
        for g in range(tk // 8):
            wg = w_ref[0, g * 8:(g + 1) * 8, cols]
            accs = [a + wg * sb[m][g] for m, a in enumerate(accs)]
        for m in range(n_rows):
            acc_ref[m, :, cols] = accs[m]

    @pl.when(k == pl.num_programs(1) - 1)
    def _():
        o_ref[0] = jnp.zeros(o_ref.shape[1:], F32)
        for m in range(n_rows):
            o_ref[0, m:m + 1, :] = jnp.sum(acc_ref[m], axis=0, keepdims=True) + b_ref[0]


def _modulation(cvecs, w_mod, b_mod):
    n, d = cvecs.shape
    depth, _, n_out = w_mod.shape
    ct = jnp.zeros((d, 8), F32).at[:, :n].set(cvecs.T)
    return pl.pallas_call(
        functools.partial(_mod_kernel, n_rows=n),
        grid=(depth, d // MOD_TK),
        in_specs=[pl.BlockSpec((MOD_TK, 8), lambda l, k: (k, 0)),
                  pl.BlockSpec((1, MOD_TK, n_out), lambda l, k: (l, k, 0)),
                  pl.BlockSpec((1, 1, n_out), lambda l, k: (l, 0, 0))],
        out_specs=pl.BlockSpec((1, 8, n_out), lambda l, k: (l, 0, 0)),
        out_shape=jax.ShapeDtypeStruct((depth, 8, n_out), F32),
        scratch_shapes=[pltpu.VMEM((n, 8, n_out), F32)],
        compiler_params=_cparams(("parallel", "arbitrary")),
        name="adaln_mod",
    )(ct, w_mod, b_mod.reshape(depth, 1, n_out))


def _proj_ab_kernel(x_ref, g_ref, sh_ref, sc_ref, w_ref, wg_ref, u_ref, qkv_ref, z_ref, ab_ref):
    ns, tm, d = x_ref.shape
    h = _rms_mod(x_ref[...], g_ref[...], sh_ref[...], sc_ref[...])
    h = h.reshape(ns * tm, d).astype(BF16)
    y = jnp.dot(h, w_ref[...], preferred_element_type=F32)
    for s in range(ns):
        u_ref[:, s, :] = y[s * tm:(s + 1) * tm, :S5_WIDTH]
    qkv_ref[...] = y[:, S5_WIDTH:S5_WIDTH + 3 * GDN_WIDTH].reshape(ns, tm, 3 * GDN_WIDTH)
    z_ref[...] = y[:, S5_WIDTH + 3 * GDN_WIDTH:AB_MAIN].reshape(ns, tm, GDN_WIDTH)
    ab_ref[...] = jnp.dot(h, wg_ref[...], preferred_element_type=F32).reshape(ns, tm, 128)


def _proj_ab(x, g, shift, scale, w, w_gate):
    rows, n, d = x.shape
    tm = PROJ_AB_TOKENS
    per_row = shift.shape[0] > 1
    mspec = pl.BlockSpec((S5_ROWS, 1, d), lambda r, i: (r, 0, 0)) if per_row else \
        pl.BlockSpec((1, 1, d), lambda r, i: (0, 0, 0))
    blk = lambda w_: pl.BlockSpec((S5_ROWS, tm, w_), lambda r, i: (r, i, 0))
    return pl.pallas_call(
        _proj_ab_kernel,
        grid=(rows // S5_ROWS, n // tm),
        in_specs=[blk(d),
                  pl.BlockSpec((1, d), lambda r, i: (0, 0)),
                  mspec, mspec,
                  pl.BlockSpec((d, AB_MAIN), lambda r, i: (0, 0)),
                  pl.BlockSpec((d, 128), lambda r, i: (0, 0))],
        out_specs=[pl.BlockSpec((tm, S5_ROWS, S5_WIDTH), lambda r, i: (i, r, 0)),
                   blk(3 * GDN_WIDTH), blk(GDN_WIDTH), blk(128)],
        out_shape=[jax.ShapeDtypeStruct((n, rows, S5_WIDTH), F32),
                   jax.ShapeDtypeStruct((rows, n, 3 * GDN_WIDTH), F32),
                   jax.ShapeDtypeStruct((rows, n, GDN_WIDTH), F32),
                   jax.ShapeDtypeStruct((rows, n, 128), F32)],
        compiler_params=_cparams(("parallel", "parallel")),
        name="proj_ab",
    )(x, g, shift, scale, w, w_gate)


def _s5_kernel(u_ref, bm_ref, cm_ref, a_ref, h0r_ref, h0i_ref, *rest, want_y):
    if want_y:
        y_ref, fr_ref, fi_ref, xs_ref = rest
    else:
        fr_ref, fi_ref, xs_ref = rest
    n = u_ref.shape[0]
    sub = S5_SUB
    nsub = n // sub
    st = S5_TILE_ST

    def x_proj(d, k):
        u2 = u_ref[k * sub:(k + 1) * sub].reshape(sub * S5_ROWS, S5_TILE_CH)
        xs_ref[d, k * sub:(k + 1) * sub] = _dot(u2, bm_ref[d, 0]).reshape(sub, S5_ROWS, 2 * st)

    ar = [jnp.broadcast_to(a_ref[d, 0, 0:1, :], (S5_ROWS, st)) for d in range(2)]
    ai = [jnp.broadcast_to(a_ref[d, 0, 1:2, :], (S5_ROWS, st)) for d in range(2)]
    hr = [h0r_ref[0], h0r_ref[1]]
    hi = [h0i_ref[0], h0i_ref[1]]
    x_proj(0, 0)
    x_proj(1, nsub - 1)
    written = set()
    for k in range(nsub):
        ks = (k, nsub - 1 - k)
        if k + 1 < nsub:
            x_proj(0, ks[0] + 1)
            x_proj(1, ks[1] - 1)
        for t in range(sub):
            for d in range(2):
                tt = ks[d] * sub + (t if d == 0 else sub - 1 - t)
                x = xs_ref[d, tt]
                nr = ar[d] * hr[d] - ai[d] * hi[d] + x[:, :st]
                ni = ar[d] * hi[d] + ai[d] * hr[d] + x[:, st:]
                xs_ref[d, tt] = jnp.concatenate([nr, ni], axis=-1)
                hr[d], hi[d] = nr, ni
        if want_y:
            for d in range(2):
                rows = slice(ks[d] * sub, (ks[d] + 1) * sub)
                hs = xs_ref[d, rows].reshape(sub * S5_ROWS, 2 * st)
                yv = _dot(hs, cm_ref[d, 0]).reshape(sub, S5_ROWS, S5_TILE_CH)
                if ks[d] in written:
                    y_ref[rows] += yv
                else:
                    y_ref[rows] = yv
                    written.add(ks[d])
    for d in range(2):
        fr_ref[d] = hr[d]
        fi_ref[d] = hi[d]


def _s5_scan(u_t, bmat, cmat, amat, h0r, h0i, want_y=True):
    n, rows, _ = u_t.shape
    state = lambda: pl.BlockSpec((2, S5_ROWS, S5_TILE_ST), lambda r, j: (0, r, j))
    st_shape = jax.ShapeDtypeStruct((2, rows, S5_GROUPS * S5_STATE), F32)
    out_specs = [state(), state()]
    out_shape = [st_shape, st_shape]
    if want_y:
        out_specs.insert(0, pl.BlockSpec((n, S5_ROWS, S5_TILE_CH), lambda r, j: (0, r, j)))
        out_shape.insert(0, jax.ShapeDtypeStruct((n, rows, S5_WIDTH), F32))
    return pl.pallas_call(
        functools.partial(_s5_kernel, want_y=want_y),
        grid=(rows // S5_ROWS, S5_TILES),
        in_specs=[pl.BlockSpec((n, S5_ROWS, S5_TILE_CH), lambda r, j: (0, r, j)),
                  pl.BlockSpec((2, 1, S5_TILE_CH, 2 * S5_TILE_ST), lambda r, j: (0, j, 0, 0)),
                  pl.BlockSpec((2, 1, 2 * S5_TILE_ST, S5_TILE_CH), lambda r, j: (0, j, 0, 0)),
                  pl.BlockSpec((2, 1, 8, S5_TILE_ST), lambda r, j: (0, j, 0, 0)),
                  state(), state()],
        out_specs=out_specs,
        out_shape=out_shape,
        scratch_shapes=[pltpu.VMEM((2, n, S5_ROWS, 2 * S5_TILE_ST), F32)],
        compiler_params=_cparams(("parallel", "parallel")),
        name="s5_scan" if want_y else "s5_states",
    )(u_t, bmat, cmat, amat, h0r, h0i)


def _s5_params(p, j):
    lam_re, lam_im, log_dt = p['s5_lam_re'][j], p['s5_lam_im'][j], p['s5_log_dt'][j]
    dt = jnp.exp(log_dt)[..., None]
    mag = jnp.exp(lam_re * dt)
    ar, ai = mag * jnp.cos(lam_im * dt), mag * jnp.sin(lam_im * dt)
    den = lam_re * lam_re + lam_im * lam_im
    fr = ((ar - 1.0) * lam_re + ai * lam_im) / den
    fi = (ai * lam_re - (ar - 1.0) * lam_im) / den
    b_re, b_im = p['s5_b_re'][j], p['s5_b_im'][j]
    bbr = fr[..., None] * b_re - fi[..., None] * b_im
    bbi = fr[..., None] * b_im + fi[..., None] * b_re
    eye = jnp.eye(S5_TILE_GROUPS, dtype=F32)

    def in_blocks(t):
        t = t.reshape(2, S5_TILES, S5_TILE_GROUPS, S5_STATE, S5_GROUP)
        t = jnp.einsum('dtgpc,gh->dtgchp', t, eye)
        return t.reshape(2, S5_TILES, S5_TILE_CH, S5_TILE_ST)

    def out_blocks(t):
        t = t.reshape(2, S5_TILES, S5_TILE_GROUPS, S5_GROUP, S5_STATE)
        t = jnp.einsum('dtgcp,gh->dtgphc', t, eye)
        return t.reshape(2, S5_TILES, S5_TILE_ST, S5_TILE_CH)

    bmat = jnp.concatenate([in_blocks(bbr), in_blocks(bbi)], axis=-1).astype(BF16)
    cmat = jnp.concatenate([out_blocks(p['s5_c_re'][j]), -out_blocks(p['s5_c_im'][j])], axis=-2).astype(BF16)
    seg_mag = jnp.exp(lam_re * dt * S5_SEG)
    pr, pi = seg_mag * jnp.cos(lam_im * dt * S5_SEG), seg_mag * jnp.sin(lam_im * dt * S5_SEG)
    flat = lambda t: t.reshape(2, S5_TILES, 1, S5_TILE_ST)
    amat = jnp.concatenate([flat(ar), flat(ai), flat(pr), flat(pi),
                            jnp.zeros((2, S5_TILES, 4, S5_TILE_ST), F32)], axis=2)
    return bmat, cmat, amat


def _s5_glu_kernel(y_ref, u_ref, d_ref, w_ref, b_ref, o_ref):
    tm, ns, _ = y_ref.shape
    y = jnp.concatenate([y_ref[:, s, :] + d_ref[...] * u_ref[:, s, :] for s in range(ns)], axis=0)
    g = _gelu_tanh(y)
    out = g * _sigmoid(_dot(g, w_ref[...]) + b_ref[...])
    o_ref[...] = out.reshape(ns, tm, S5_WIDTH)


def _s5_glu(y_t, u_t, s5_d, w_glu, b_glu):
    n, rows, _ = y_t.shape
    tm = PROJ_AB_TOKENS
    tblk = pl.BlockSpec((tm, S5_ROWS, S5_WIDTH), lambda r, i: (i, r, 0))
    return pl.pallas_call(
        _s5_glu_kernel,
        grid=(rows // S5_ROWS, n // tm),
        in_specs=[tblk, tblk,
                  pl.BlockSpec((1, S5_WIDTH), lambda r, i: (0, 0)),
                  pl.BlockSpec((S5_WIDTH, S5_WIDTH), lambda r, i: (0, 0)),
                  pl.BlockSpec((1, S5_WIDTH), lambda r, i: (0, 0))],
        out_specs=pl.BlockSpec((S5_ROWS, tm, S5_WIDTH), lambda r, i: (r, i, 0)),
        out_shape=jax.ShapeDtypeStruct((rows, n, S5_WIDTH), F32),
        compiler_params=_cparams(("parallel", "parallel")),
        name="s5_glu",
    )(y_t, u_t, s5_d, w_glu, b_glu)


def _s5_mixer(u3, p, j, h0r, h0i, bsz, nseg):
    rows = bsz * nseg
    bmat, cmat, amat = _s5_params(p, j)
    if nseg == 1:
        y_t, fr, fi = _s5_scan(u3, bmat, cmat, amat, h0r, h0i)
    else:
        zero = jnp.zeros((2, bsz, nseg, S5_GROUPS * S5_STATE), F32)
        first = jnp.array([0, nseg - 1])
        seed = lambda h0: zero.at[jnp.arange(2), :, first].set(h0).reshape(2, rows, -1)
        fr, fi = _s5_scan(u3, bmat, cmat, amat, seed(h0r), seed(h0i), want_y=False)
        fr = fr.reshape(2, bsz, nseg, -1)
        fi = fi.reshape(2, bsz, nseg, -1)
        pr = amat[:, :, 2].reshape(2, 1, -1)
        pi = amat[:, :, 3].reshape(2, 1, -1)

        def chain(dr, order):
            hr, hi = (h0r[dr], h0i[dr])
            outs_r, outs_i = {}, {}
            for n_done, k in enumerate(order):
                outs_r[k], outs_i[k] = hr, hi
                if n_done == 0:
                    hr, hi = fr[dr, :, k], fi[dr, :, k]
                else:
                    hr, hi = (pr[dr] * hr - pi[dr] * hi + fr[dr, :, k],
                              pr[dr] * hi + pi[dr] * hr + fi[dr, :, k])
            st = lambda o: jnp.stack([o[k] for k in range(nseg)], axis=1)
            return st(outs_r), st(outs_i), hr, hi

        sr0, si0, er0, ei0 = chain(0, list(range(nseg)))
        sr1, si1, er1, ei1 = chain(1, list(range(nseg - 1, -1, -1)))
        start_r = jnp.stack([sr0, sr1]).reshape(2, rows, -1)
        start_i = jnp.stack([si0, si1]).reshape(2, rows, -1)
        y_t, _, _ = _s5_scan(u3, bmat, cmat, amat, start_r, start_i)
        fr = jnp.stack([er0, er1])
        fi = jnp.stack([ei0, ei1])
    ya = _s5_glu(y_t, u3, p['s5_d'][j][None], p['s5_w_glu'][j].astype(BF16), p['s5_b_glu'][j][None])
    return ya, fr, fi


def _gdn_kernel(qkv_ref, z_ref, ab_ref, cw_ref, gp_ref, ng_ref, *rest, zero_init):
    if zero_init:
        s0_ref = None
        o_ref, sf_ref, q_s, k_s, v_s, gate_s, of_s, ob_s, st_s, uw_s, a_s = rest
    else:
        s0_ref, o_ref, sf_ref, q_s, k_s, v_s, gate_s, of_s, ob_s, st_s, uw_s, a_s = rest
    ns, n = qkv_ref.shape[:2]
    nc = n // GDN_CHUNK
    row = lax.broadcasted_iota(jnp.int32, (n, 1), 0)

    for sq in range(ns):
        for blk in range(3 * GDN_HEADS):
            cols = slice(blk * GDN_DK, (blk + 1) * GDN_DK)
            hs = slice((blk % GDN_HEADS) * GDN_DK, (blk % GDN_HEADS + 1) * GDN_DK)
            x = qkv_ref[sq, :, cols]
            xm = jnp.where(row == 0, 0.0, pltpu.roll(x, 1, 0))
            xp = jnp.where(row == n - 1, 0.0, pltpu.roll(x, n - 1, 0))
            y = _silu(xm * cw_ref[0:1, cols] + x * cw_ref[1:2, cols] + xp * cw_ref[2:3, cols])
            if blk < GDN_HEADS:
                q_s[sq, :, hs] = y * lax.rsqrt(jnp.sum(y * y, axis=-1, keepdims=True) + EPS) * (GDN_DK ** -0.5)
            elif blk < 2 * GDN_HEADS:
                k_s[sq, :, hs] = y * lax.rsqrt(jnp.sum(y * y, axis=-1, keepdims=True) + EPS)
            else:
                v_s[sq, :, hs] = y

        ab = ab_ref[sq]
        beta = _sigmoid(ab)
        g = -jnp.exp(gp_ref[0:1, :]) * _softplus(ab + gp_ref[1:2, :])
        pos = row % GDN_CHUNK
        pre, suf = g, g
        sft = 1
        while sft < GDN_CHUNK:
            pre = pre + jnp.where(pos >= sft, pltpu.roll(pre, sft, 0), 0.0)
            suf = suf + jnp.where(pos < GDN_CHUNK - sft, pltpu.roll(suf, n - sft, 0), 0.0)
            sft *= 2
        gate_s[sq, 0] = beta
        gate_s[sq, 1] = pre
        gate_s[sq, 2] = suf

    st_s[...] = jnp.zeros_like(st_s) if zero_init else s0_ref[...]
    cs = GDN_CHUNK
    pk = GDN_HEADS * cs
    ri = lax.broadcasted_iota(jnp.int32, (cs, pk), 0)
    lane_pk = lax.broadcasted_iota(jnp.int32, (cs, pk), 1)
    ci = lane_pk % cs
    eye_pk = (ri == ci).astype(F32)
    head_pk = [(lax.broadcasted_iota(jnp.int32, (1, pk), 1) // cs) == h for h in range(GDN_HEADS)]
    head_w = [(lax.broadcasted_iota(jnp.int32, (1, GDN_WIDTH), 1) // GDN_DK) == h for h in range(GDN_HEADS)]

    def block_diag(p):
        return jnp.concatenate([jnp.where(m, p, jnp.zeros_like(p)) for m in head_pk], axis=0)

    def dot3_bd(a, p):
        a_hi, a_lo = _hi_lo(a)
        p_hi, p_lo = _hi_lo(p)
        b_hi, b_lo = block_diag(p_hi), block_diag(p_lo)
        mm = functools.partial(jnp.dot, preferred_element_type=F32)
        return mm(a_hi, b_hi) + (mm(a_lo, b_hi) + mm(a_hi, b_lo))

    def lanes(cols, width):
        return jnp.concatenate([jnp.broadcast_to(c, (cs, width)) for c in cols], axis=1)

    def phase_a(sq, it):
        st = []
        for cc in range(GDN_GROUP):
            rows = pl.ds(pl.multiple_of((it * GDN_GROUP + cc) * cs, cs), cs)
            q_all, k_all, v_all = q_s[sq, rows, :], k_s[sq, rows, :], v_s[sq, rows, :]
            beta_blk = gate_s[sq, 0, rows, :]
            for dr in range(2):
                gc_blk = gate_s[sq, 1 + dr, rows, :]
                lane0 = dr * GDN_HEADS
                bcols = [beta_blk[:, 8 + lane0 + h:9 + lane0 + h] for h in range(GDN_HEADS)]
                gcols = [gc_blk[:, lane0 + h:lane0 + h + 1] for h in range(GDN_HEADS)]
                st.append(dict(sq=sq, dr=dr, rows=rows, q=q_all, k=k_all, v=v_all, bcols=bcols, gcols=gcols,
                               incl=(ri >= ci) if dr == 0 else (ri <= ci),
                               strict=(ri > ci) if dr == 0 else (ri < ci)))
        for t in st:
            b_w = lanes(t['bcols'], GDN_DK)
            t['kb'] = t['k'] * b_w
            t['vb'] = t['v'] * b_w
            k_bd = jnp.concatenate([jnp.where(m, t['k'], 0.0) for m in head_w], axis=0)
            t['kq'] = _dot_nt(jnp.concatenate([t['kb'], t['q']], axis=0), k_bd)
        for t in st:
            gcol = lanes(t['gcols'], cs)
            grow = jnp.sum(eye_pk * gcol, axis=0, keepdims=True)
            decay = jnp.exp(jnp.where(t['incl'], gcol - grow, NEG_INF))
            t['pw'] = jnp.where(t['strict'], t['kq'][:cs] * decay, 0.0)
            a_s[t['sq'], t['dr'], t['rows'], :] =jnp.where(t['incl'], t['kq'][cs:] * decay, 0.0)
            t['tm'] = eye_pk - t['pw']
        for t in st:
            t['pw'] = dot3_bd(t['pw'], t['pw'])
        for rnd in range(5):
            for t in st:
                if rnd < 4:
                    r = dot3_bd(jnp.concatenate([t['pw'], t['tm']], axis=0), t['pw'])
                    t['pw'] = r[:cs]
                    t['tm'] = t['tm'] + r[cs:]
                else:
                    t['tm'] = t['tm'] + dot3_bd(t['tm'], t['pw'])
        for t in st:
            kbg = t['kb'] * lanes([jnp.exp(g) for g in t['gcols']], GDN_DK)
            for h in range(GDN_HEADS):
                hs = slice(h * GDN_DK, (h + 1) * GDN_DK)
                rhs = jnp.concatenate([t['vb'][:, hs], kbg[:, hs]], axis=1)
                uw_s[t['sq'], t['dr'], t['rows'], 2 * h * GDN_DK:2 * (h + 1) * GDN_DK] = _dot(
                    t['tm'][:, h * cs:(h + 1) * cs], rhs)

    for sq in range(ns):
        if nc // GDN_GROUP == 1:
            phase_a(sq, 0)
        else:
            lax.fori_loop(0, nc // GDN_GROUP, lambda it, c, sq=sq: (phase_a(sq, it), c)[1], 0)

    def phase_b(c, carry):
        ch = []
        for sq, dr in [(sq, dr) for sq in range(ns) for dr in range(2)]:
            cidx = c if dr == 0 else nc - 1 - c
            rows = pl.ds(pl.multiple_of(cidx * cs, cs), cs)
            gc_blk = gate_s[sq, 1 + dr, rows, :]
            for h in range(GDN_HEADS):
                hs = slice(h * GDN_DK, (h + 1) * GDN_DK)
                lane = dr * GDN_HEADS + h
                gcol = gc_blk[:, lane:lane + 1]
                ch.append(dict(sq=sq, dr=dr, h=h, rows=rows, hs=hs, gcol=gcol,
                               gl=gcol[cs - 1:cs] if dr == 0 else gcol[0:1],
                               uw=uw_s[sq, dr, rows, 2 * h * GDN_DK:2 * (h + 1) * GDN_DK],
                               amat=a_s[sq, dr, rows, h * cs:(h + 1) * cs]))
        for t in ch:
            t['s'] = st_s[t['sq'], t['dr'], t['h']]
            qg = q_s[t['sq'], t['rows'], t['hs']] * jnp.exp(t['gcol'])
            t['ws'] = _dot(jnp.concatenate([t['uw'][:, GDN_DV:], qg], axis=0), t['s'])
        for t in ch:
            vn = t['uw'][:, :GDN_DV] - t['ws'][:cs]
            o = t['ws'][cs:] + _dot(t['amat'], vn)
            kg = k_s[t['sq'], t['rows'], t['hs']] * jnp.exp(t['gl'] - t['gcol'])
            st_s[t['sq'], t['dr'], t['h']] = t['s'] * jnp.exp(t['gl']) + _dot_tn(kg, vn)
            if t['dr'] == 0:
                of_s[t['sq'], t['rows'], t['hs']] = o
            else:
                ob_s[t['sq'], t['rows'], t['hs']] = o
        return carry

    lax.fori_loop(0, nc, phase_b, 0)
    sf_ref[...] = st_s[...]

    for sq in range(ns):
        z = z_ref[sq]
        for h in range(GDN_HEADS):
            hs = slice(h * GDN_DV, (h + 1) * GDN_DV)
            o = of_s[sq, :, hs] + ob_s[sq, :, hs]
            o = o * lax.rsqrt(jnp.mean(o * o, axis=-1, keepdims=True) + EPS) * ng_ref[...]
            o_ref[sq, :, hs] = o * _silu(z[:, hs])


def _gdn_mixer(qkv, z, ab, p, j, s0):
    bsz, n, _ = qkv.shape
    gp = jnp.zeros((8, 128), F32)
    gp = gp.at[0, :8].set(p['gdn_a_log'][j].reshape(8)).at[1, :8].set(p['gdn_dt_bias'][j].reshape(8))
    cw = jnp.zeros((8, 3 * GDN_WIDTH), F32).at[:3].set(p['gdn_conv_w'][j])
    ns = GDN_SEQS_SHORT if (n <= GDN_SHORT_LEN and bsz % GDN_SEQS_SHORT == 0) else 1
    blk = lambda w_: pl.BlockSpec((ns, n, w_), lambda b: (b, 0, 0))
    sblk = pl.BlockSpec((ns, 2, GDN_HEADS, GDN_DK, GDN_DV), lambda b: (b, 0, 0, 0, 0))
    tok = lambda w_: pltpu.VMEM((ns, n, w_), F32)
    return pl.pallas_call(
        functools.partial(_gdn_kernel, zero_init=s0 is None),
        grid=(bsz // ns,),
        in_specs=[blk(3 * GDN_WIDTH), blk(GDN_WIDTH), blk(128),
                  pl.BlockSpec((8, 3 * GDN_WIDTH), lambda b: (0, 0)),
                  pl.BlockSpec((8, 128), lambda b: (0, 0)),
                  pl.BlockSpec((1, GDN_DV), lambda b: (0, 0))] + ([] if s0 is None else [sblk]),
        out_specs=[blk(GDN_WIDTH), sblk],
        out_shape=[jax.ShapeDtypeStruct((bsz, n, GDN_WIDTH), F32),
                   jax.ShapeDtypeStruct((bsz, 2, GDN_HEADS, GDN_DK, GDN_DV), F32)],
        scratch_shapes=[tok(GDN_WIDTH), tok(GDN_WIDTH), tok(GDN_WIDTH),
                        pltpu.VMEM((ns, 3, n, 128), F32),
                        tok(GDN_WIDTH), tok(GDN_WIDTH),
                        pltpu.VMEM((ns, 2, GDN_HEADS, GDN_DK, GDN_DV), F32),
                        pltpu.VMEM((ns, 2, n, 2 * GDN_WIDTH), F32),
                        pltpu.VMEM((ns, 2, n, GDN_HEADS * GDN_CHUNK), F32)],
        compiler_params=_cparams(("parallel",)),
        name="gdn_mixer",
    )(qkv, z, ab, cw, gp, p['gdn_norm_g'][j][None], *(() if s0 is None else (s0,)))


def _proj_cd_kernel(x_ref, g_ref, sh_ref, sc_ref, w_ref, gm_ref, gain_ref, cos_ref, sin_ref,
                    qc_ref, kc_ref, vc_ref, qd_ref, kd_ref, vd_ref, *, rope):
    h = _rms_mod(x_ref[0], g_ref[...], sh_ref[0], sc_ref[0])
    y = _dot(h, w_ref[...])
    lane = lax.broadcasted_iota(jnp.int32, (1, 512), 1)
    low = (lane % 32) < 16

    def head_norm(t, gain, scale):
        w = t.shape[1]
        ms = _dot(t * t, gm_ref[:w, :w])
        t = t * lax.rsqrt(ms + EPS) * gain
        if rope:
            part = jnp.where(low[:, :w], pltpu.roll(t, w - 16, 1), pltpu.roll(t, 16, 1))
            t = t * cos_ref[:, :w] + part * sin_ref[:, :w]
        return t * scale if scale != 1.0 else t

    qc_ref[0] = head_norm(y[:, 0:512], gain_ref[0:1, :], ATTN_SCALE)
    kc_ref[0] = head_norm(y[:, 512:640], gain_ref[1:2, :128], 1.0)
    vc_ref[0] = y[:, 640:768]
    qd_ref[0] = head_norm(y[:, 768:1280], gain_ref[2:3, :], ATTN_SCALE)
    kd_ref[0] = head_norm(y[:, 1280:1792], gain_ref[3:4, :], 1.0)
    vd_ref[0] = y[:, 1792:2304]


def _rope_tables(n):
    rows = n // GRID_W
    row = jnp.repeat(jnp.arange(rows), GRID_W).astype(F32)
    col = jnp.tile(jnp.arange(GRID_W), rows).astype(F32)
    quarter = HEAD_DIM // 4
    inv = ROPE_THETA ** (-jnp.arange(quarter, dtype=F32) / quarter)
    ang_r = row[:, None] * inv[None, :]
    ang_c = col[:, None] * inv[None, :]
    cos = jnp.concatenate([jnp.cos(ang_r), jnp.cos(ang_r), jnp.cos(ang_c), jnp.cos(ang_c)], axis=-1)
    sin = jnp.concatenate([-jnp.sin(ang_r), jnp.sin(ang_r), -jnp.sin(ang_c), jnp.sin(ang_c)], axis=-1)
    return jnp.tile(cos, (1, 8)), jnp.tile(sin, (1, 8))


def _proj_cd(x, g, shift, scale, w, p, j, rope):
    bsz, n, d = x.shape
    tm = 256
    per_seq = shift.shape[0] > 1
    midx = (lambda b, i: (b, 0, 0)) if per_seq else (lambda b, i: (0, 0, 0))
    lane = jnp.arange(512)
    gmat = ((lane[:, None] // HEAD_DIM) == (lane[None, :] // HEAD_DIM)).astype(F32) / HEAD_DIM
    gains = jnp.zeros((8, 512), F32)
    gains = gains.at[0].set(jnp.tile(p['c_qn'][j], 8)).at[1].set(jnp.tile(p['c_kn'][j], 8))
    gains = gains.at[2].set(jnp.tile(p['d_qn'][j], 8)).at[3].set(jnp.tile(p['d_kn'][j], 8))
    if rope:
        cos, sin = _rope_tables(n)
    else:
        cos, sin = jnp.ones((tm, 512), F32), jnp.zeros((tm, 512), F32)
    tidx = (lambda b, i: (i, 0)) if rope else (lambda b, i: (0, 0))
    blk = lambda w_: pl.BlockSpec((1, tm, w_), lambda b, i: (b, i, 0))
    return pl.pallas_call(
        functools.partial(_proj_cd_kernel, rope=rope),
        grid=(bsz, n // tm),
        in_specs=[blk(d),
                  pl.BlockSpec((1, d), lambda b, i: (0, 0)),
                  pl.BlockSpec((1, 1, d), midx),
                  pl.BlockSpec((1, 1, d), midx),
                  pl.BlockSpec((d, CD_IN), lambda b, i: (0, 0)),
                  pl.BlockSpec((512, 512), lambda b, i: (0, 0)),
                  pl.BlockSpec((8, 512), lambda b, i: (0, 0)),
                  pl.BlockSpec((tm, 512), tidx),
                  pl.BlockSpec((tm, 512), tidx)],
        out_specs=[blk(512), blk(128), blk(128), blk(512), blk(512), blk(512)],
        out_shape=[jax.ShapeDtypeStruct((bsz, n, w_), F32) for w_ in (512, 128, 128, 512, 512, 512)],
        compiler_params=_cparams(("parallel", "parallel")),
        name="proj_cd",
    )(x, g, shift, scale, w, gmat.astype(BF16), gains, cos, sin)


def _probs(scores, extra=None):
    m = scores[0].max(axis=-1, keepdims=True)
    for s in scores[1:]:
        m = jnp.maximum(m, s.max(axis=-1, keepdims=True))
    if extra is not None:
        m = jnp.maximum(m, extra)
    return [jnp.exp(s - m).astype(BF16) for s in scores], m


def _attn_kernel(qc_ref, kc_ref, vc_ref, qd_ref, kd_ref, vd_ref, *rest, windowed, n_ctx, lam_scale):
    if n_ctx:
        ck_ref, cv_ref, dk_ref, dv_ref, misc_ref, sub_ref, oc_ref, od_ref = rest
    else:
        misc_ref, sub_ref, oc_ref, od_ref = rest
    tq = qc_ref.shape[1]
    n = kc_ref.shape[1]
    start = pl.program_id(1) * tq
    if windowed:
        span = tq + 2 * WINDOW
        k0 = pl.multiple_of(jnp.clip(start - WINDOW, 0, n - span), 128)
        krows = pl.ds(k0, span)
        qpos = start + lax.broadcasted_iota(jnp.int32, (tq, span), 0)
        kpos = k0 + lax.broadcasted_iota(jnp.int32, (tq, span), 1)
        ok = jnp.abs(qpos - kpos) <= WINDOW
    else:
        krows = pl.ds(0, n)
    low = lax.broadcasted_iota(jnp.int32, (1, 2 * HEAD_DIM), 1) < HEAD_DIM

    def key_ops(k):
        kr = pltpu.roll(k, HEAD_DIM, 1)
        z = jnp.zeros_like(k)
        return {(0, 0): jnp.where(low, k, z).astype(BF16), (0, 1): jnp.where(low, z, kr).astype(BF16),
                (1, 0): jnp.where(low, kr, z).astype(BF16), (1, 1): jnp.where(low, z, k).astype(BF16)}

    def with_ones(v):
        return jnp.concatenate([v, jnp.ones_like(v)], axis=1)

    def val_ops(v):
        return {key: with_ones(op) for key, op in key_ops(v).items()}

    def half_ops(k):
        z = jnp.zeros_like(k)
        return [jnp.where(low, k, z).astype(BF16), jnp.where(low, z, k).astype(BF16)]

    seqs = range(qc_ref.shape[0])

    c_scores, d_scores = [[] for _ in seqs], [[] for _ in seqs]
    for sq in seqs:
        qc = qc_ref[sq].astype(BF16)
        qd = qd_ref[sq].astype(BF16)
        kc_ops = key_ops(kc_ref[sq, krows, :])
        ck_ops = key_ops(ck_ref[sq]) if n_ctx else None
        for h in range(C_HEADS):
            key = (h // C_GROUP, h % 2)
            q = qc[:, (h // 2) * 128:(h // 2 + 1) * 128]
            sc = [_dot_nt(q, kc_ops[key])]
            if n_ctx:
                sc.append(_dot_nt(q, ck_ops[key]))
            c_scores[sq].append(sc)
        for h in range(D_HEADS):
            q = qd[:, h * 128:(h + 1) * 128]
            kd_ops = half_ops(kd_ref[sq, :, h * 128:(h + 1) * 128])
            dk_ops = half_ops(dk_ref[sq, :, h * 128:(h + 1) * 128]) if n_ctx else None
            for c in range(2):
                sc = [_dot_nt(q, kd_ops[c])]
                if n_ctx:
                    sc.append(_dot_nt(q, dk_ops[c]))
                d_scores[sq].append(sc)

    c_probs, d_probs = [[] for _ in seqs], [[] for _ in seqs]
    for sq in seqs:
        for h in range(C_HEADS):
            sc = c_scores[sq][h]
            if windowed:
                sc[0] = jnp.where(ok, sc[0], NEG_INF)
            c_probs[sq].append(_probs(sc, misc_ref[0:1, h:h + 1]))
        d_probs[sq] = [_probs(sc)[0] for sc in d_scores[sq]]

    lam = misc_ref[1:2, 0:1]
    for sq in seqs:
        vc_ops = val_ops(vc_ref[sq, krows, :])
        cv_ops = val_ops(cv_ref[sq]) if n_ctx else None
        for j in range(C_HEADS // 2):
            pair = None
            for h in (2 * j, 2 * j + 1):
                key = (h // C_GROUP, h % 2)
                ps, m = c_probs[sq][h]
                r = jnp.dot(ps[0], vc_ops[key], preferred_element_type=F32)
                if n_ctx:
                    r = r + jnp.dot(ps[1], cv_ops[key], preferred_element_type=F32)
                o = r[:, :128] / (r[:, 128:] + jnp.exp(misc_ref[0:1, h:h + 1] - m))
                pair = o if pair is None else pair + o
            oc_ref[sq, :, j * 128:(j + 1) * 128] = pair
        for h in range(D_HEADS):
            vsl = slice(h * D_VDIM, (h + 1) * D_VDIM)
            v_op = with_ones(vd_ref[sq, :, vsl]).astype(BF16)
            dv_op = with_ones(dv_ref[sq, :, vsl]).astype(BF16) if n_ctx else None
            parts = []
            for c in range(2):
                ps = d_probs[sq][2 * h + c]
                r = jnp.dot(ps[0], v_op, preferred_element_type=F32)
                if n_ctx:
                    r = r + jnp.dot(ps[1], dv_op, preferred_element_type=F32)
                parts.append(r[:, :D_VDIM] / r[:, D_VDIM:])
            o = parts[0] - lam * parts[1]
            o = o * lax.rsqrt(jnp.mean(o * o, axis=-1, keepdims=True) + EPS) * sub_ref[...] * lam_scale
            od_ref[sq, :, vsl] = o


def _attention(qc, kc, vc, qd, kd, vd, caches, misc, subln, lam_init, tq, windowed):
    bsz, n, _ = qc.shape
    ns = ATTN_SEQS_SHORT if (not windowed and tq == n and bsz % ATTN_SEQS_SHORT == 0) else 1
    qblk = lambda w_: pl.BlockSpec((ns, tq, w_), lambda b, i: (b, i, 0))
    kblk = lambda rows, w_: pl.BlockSpec((ns, rows, w_), lambda b, i: (b, 0, 0))
    in_specs = [qblk(512), kblk(n, 128), kblk(n, 128), qblk(512), kblk(n, 512), kblk(n, 512)]
    args = [qc, kc, vc, qd, kd, vd]
    n_ctx = 0
    if caches is not None:
        n_ctx = caches[0].shape[1]
        in_specs += [kblk(n_ctx, 128), kblk(n_ctx, 128), kblk(n_ctx, 512), kblk(n_ctx, 512)]
        args += list(caches)
    in_specs += [pl.BlockSpec((8, 128), lambda b, i: (0, 0)), pl.BlockSpec((1, D_VDIM), lambda b, i: (0, 0))]
    args += [misc, subln]
    return pl.pallas_call(
        functools.partial(_attn_kernel, windowed=windowed, n_ctx=n_ctx, lam_scale=1.0 - lam_init),
        grid=(bsz // ns, n // tq),
        in_specs=in_specs,
        out_specs=[qblk(512), qblk(512)],
        out_shape=[jax.ShapeDtypeStruct((bsz, n, 512), F32), jax.ShapeDtypeStruct((bsz, n, 512), F32)],
        compiler_params=_cparams(("parallel", "parallel")),
        name="attn_win" if windowed else "attn_ctx",
    )(*args)


def _post_kernel(x_ref, xp_ref, xn_ref, ma_ref, map_ref, man_ref, mb_ref, mbp_ref, mbn_ref,
                 g1_ref, sh_ref, sc_ref, g2_ref, ng_ref, wo_ref, wup_ref, cw_ref, wdn_ref, o_ref, act_ref,
                 *, seq_len):
    rows = x_ref.shape[0]
    ext = rows + 2 * POST_HALO
    half = ma_ref.shape[1]
    xe = jnp.concatenate([xp_ref[...], x_ref[...], xn_ref[...]], axis=0)
    mae = jnp.concatenate([map_ref[...], ma_ref[...], man_ref[...]], axis=0)
    mbe = jnp.concatenate([mbp_ref[...], mb_ref[...], mbn_ref[...]], axis=0)
    x1 = xe + g1_ref[0] * (_dot(mae, wo_ref[:half, :]) + _dot(mbe, wo_ref[half:, :]))
    h = _rms_mod(x1, ng_ref[...], sh_ref[0], sc_ref[0]).astype(BF16)
    x1 = x1[POST_HALO:POST_HALO + rows]
    row0 = pl.program_id(0) * rows - POST_HALO
    pos = (row0 + lax.broadcasted_iota(jnp.int32, (ext, 1), 0)) % seq_len
    first = pos == 0
    last = pos == seq_len - 1
    for c in range(FF_CHUNKS):
        cs = slice(c * FF_CHUNK, (c + 1) * FF_CHUNK)
        a = jnp.dot(h, wup_ref[:, cs], preferred_element_type=F32)
        b = jnp.dot(h, wup_ref[:, D_FF + c * FF_CHUNK:D_FF + (c + 1) * FF_CHUNK], preferred_element_type=F32)
        am = jnp.where(first, 0.0, pltpu.roll(a, 1, 0))
        ap = jnp.where(last, 0.0, pltpu.roll(a, ext - 1, 0))
        a = am * cw_ref[0:1, cs] + a * cw_ref[1:2, cs] + ap * cw_ref[2:3, cs] + cw_ref[3:4, cs]
        act_ref[:, cs] = (_silu(a) * b)[POST_HALO:POST_HALO + rows].astype(BF16)
    ffn = jnp.dot(act_ref[...], wdn_ref[...], preferred_element_type=F32)
    o_ref[...] = x1 + g2_ref[0] * ffn


def _post(x, mix_a, mix_b, g1, sh2, sc2, g2, norm_g, w_out, wup, cw, wdn, layer):
    bsz, seq_len, d = x.shape
    half = mix_a.shape[-1]
    rows = POST_ROWS
    total = bsz * seq_len
    nhalo = total // POST_HALO
    per_seq = g1.shape[0] > 1
    midx = (lambda i: ((i * rows) // seq_len, 0, 0)) if per_seq else (lambda i: (0, 0, 0))
    mspec = pl.BlockSpec((1, 1, d), midx)
    pidx = lambda i: (jnp.maximum(i * (rows // POST_HALO) - 1, 0), 0)
    nidx = lambda i: (jnp.minimum((i + 1) * (rows // POST_HALO), nhalo - 1), 0)
    trio = lambda w_: [pl.BlockSpec((rows, w_), lambda i: (i, 0)), pl.BlockSpec((POST_HALO, w_), pidx),
                       pl.BlockSpec((POST_HALO, w_), nidx)]
    whole = lambda shape: pl.BlockSpec(shape, lambda i: (0,) * len(shape), pipeline_mode=pl.Buffered(1))
    x2 = x.reshape(total, d)
    a2 = mix_a.reshape(total, half)
    b2 = mix_b.reshape(total, half)
    out = pl.pallas_call(
        functools.partial(_post_kernel, seq_len=seq_len),
        grid=(total // rows,),
        in_specs=trio(d) + trio(half) + trio(half) + [
            mspec, mspec, mspec, mspec,
            pl.BlockSpec((1, d), lambda i: (0, 0)),
            whole((d, d)),
            pl.BlockSpec((None, d, 2 * D_FF), lambda i: (layer, 0, 0), pipeline_mode=pl.Buffered(1)),
            whole((8, D_FF)),
            pl.BlockSpec((None, D_FF, d), lambda i: (layer, 0, 0), pipeline_mode=pl.Buffered(1))],
        out_specs=pl.BlockSpec((rows, d), lambda i: (i, 0)),
        out_shape=jax.ShapeDtypeStruct((total, d), F32),
        scratch_shapes=[pltpu.VMEM((rows, D_FF), BF16)],
        compiler_params=_cparams(("parallel",)),
        name="post_ffn",
    )(x2, x2, x2, a2, a2, a2, b2, b2, b2, g1, sh2, sc2, g2, norm_g, w_out, wup, cw, wdn)
    return out.reshape(bsz, seq_len, d)


def _ffn_weights(p, l):
    cw = jnp.zeros((8, D_FF), F32).at[:3].set(p['ffn_conv_w'][l]).at[3].set(p['ffn_conv_b'][l])
    return p['ffn_up'].astype(BF16), cw, p['ffn_down'].astype(BF16)


def _lambda_init(layer):
    return 0.8 - 0.6 * math.exp(-0.3 * layer)


def _trunk(x, mods, p, states, caches):
    bsz, n, d = x.shape
    nseg = n // S5_SEG
    depth = p['w_mod'].shape[0]
    news = {k: [] for k in ('s5r', 's5i', 'gdn', 'ck', 'cv', 'dk', 'dv')}
    for l in range(depth):
        j = l // 2
        sh1, sc1, g1, sh2, sc2, g2 = mods[l]
        ng1 = p['norm1_g'][l][None]
        if l % 2 == 0:
            w_in = p['w_in_ab'][j]
            w_gate = jnp.zeros((d, 128), BF16).at[:, :w_in.shape[1] - AB_MAIN].set(
                w_in[:, AB_MAIN:].astype(BF16))
            rows = bsz * nseg
            per_row = lambda m: jnp.repeat(m, nseg, axis=0) if m.shape[0] > 1 else m
            u_t, qkv, z, ab = _proj_ab(x.reshape(rows, S5_SEG, d), ng1, per_row(sh1), per_row(sc1),
                                       w_in[:, :AB_MAIN].astype(BF16), w_gate)
            qkv, z, ab = (t.reshape(bsz, n, t.shape[-1]) for t in (qkv, z, ab))
            if states is None:
                h0r = jnp.zeros((2, bsz, S5_GROUPS * S5_STATE), F32)
                h0i = h0r
                s0 = None
            else:
                h0r = states[0][:, j].reshape(bsz, 2, -1).transpose(1, 0, 2)
                h0i = states[1][:, j].reshape(bsz, 2, -1).transpose(1, 0, 2)
                s0 = states[2][:, j]
            ya, fr, fi = _s5_mixer(u_t, p, j, h0r, h0i, bsz, nseg)
            yb, sg = _gdn_mixer(qkv, z, ab, p, j, s0)
            mix = (ya.reshape(bsz, n, S5_WIDTH), yb)
            w_out = p['w_out_ab'][j]
            news['s5r'].append(fr.transpose(1, 0, 2).reshape(bsz, 2, S5_GROUPS, S5_STATE))
            news['s5i'].append(fi.transpose(1, 0, 2).reshape(bsz, 2, S5_GROUPS, S5_STATE))
            news['gdn'].append(sg)
        else:
            lam_init = _lambda_init(l)
            f = lambda name: p[name][j]
            lam = (jnp.exp(jnp.sum(f('d_lq1') * f('d_lk1'))) - jnp.exp(jnp.sum(f('d_lq2') * f('d_lk2')))
                   + lam_init)
            misc = jnp.zeros((8, 128), F32).at[0, :C_HEADS].set(p['c_sink'][j]).at[1, :].set(lam)
            rope = caches is not None
            qc, kc, vc, qd, kd, vd = _proj_cd(x, ng1, sh1, sc1, p['w_in_cd'][j].astype(BF16), p, j, rope)
            if caches is None:
                mix = _attention(qc, kc, vc, qd, kd, vd, None, misc, p['d_subln'][j][None],
                                 lam_init, n, False)
            else:
                n_ctx = caches[0].shape[2]
                cc = (caches[0][:, j].reshape(bsz, n_ctx, 128), caches[1][:, j].reshape(bsz, n_ctx, 128),
                      caches[2][:, j].reshape(bsz, n_ctx, 512), caches[3][:, j].reshape(bsz, n_ctx, 512))
                mix = _attention(qc, kc, vc, qd, kd, vd, cc, misc, p['d_subln'][j][None],
                                 lam_init, Q_BLOCK, True)
            w_out = p['w_out_cd'][j]
            news['ck'].append(kc.reshape(bsz, n, C_KV_HEADS, HEAD_DIM))
            news['cv'].append(vc.reshape(bsz, n, C_KV_HEADS, HEAD_DIM))
            news['dk'].append(kd.reshape(bsz, n, D_HEADS, 2, HEAD_DIM))
            news['dv'].append(vd.reshape(bsz, n, D_HEADS, D_VDIM))
        wup, cw, wdn = _ffn_weights(p, l)
        x = _post(x, mix[0], mix[1], g1, sh2, sc2, g2, p['norm2_g'][l][None], w_out.astype(BF16),
                  wup, cw, wdn, l)
    return x, news


def kernel(x_prompt, x_sample, c, state_s5_re, state_s5_im, state_gdn, cache_c_k, cache_c_v, cache_d_k, cache_d_v, c_ctx, w_mod, b_mod, norm1_g, norm2_g, w_in_ab, w_out_ab, s5_lam_re, s5_lam_im, s5_log_dt, s5_b_re, s5_b_im, s5_c_re, s5_c_im, s5_d, s5_w_glu, s5_b_glu, gdn_conv_w, gdn_a_log, gdn_dt_bias, gdn_norm_g, w_in_cd, w_out_cd, c_qn, c_kn, c_sink, d_qn, d_kn, d_lq1, d_lk1, d_lq2, d_lk2, d_subln, ffn_up, ffn_conv_w, ffn_conv_b, ffn_down):
    p = dict(w_mod=w_mod, b_mod=b_mod, norm1_g=norm1_g, norm2_g=norm2_g, w_in_ab=w_in_ab, w_out_ab=w_out_ab,
             s5_lam_re=s5_lam_re, s5_lam_im=s5_lam_im, s5_log_dt=s5_log_dt, s5_b_re=s5_b_re, s5_b_im=s5_b_im,
             s5_c_re=s5_c_re, s5_c_im=s5_c_im, s5_d=s5_d, s5_w_glu=s5_w_glu, s5_b_glu=s5_b_glu,
             gdn_conv_w=gdn_conv_w, gdn_a_log=gdn_a_log, gdn_dt_bias=gdn_dt_bias, gdn_norm_g=gdn_norm_g,
             w_in_cd=w_in_cd, w_out_cd=w_out_cd, c_qn=c_qn, c_kn=c_kn, c_sink=c_sink, d_qn=d_qn, d_kn=d_kn,
             d_lq1=d_lq1, d_lk1=d_lk1, d_lq2=d_lq2, d_lk2=d_lk2, d_subln=d_subln,
             ffn_up=ffn_up, ffn_conv_w=ffn_conv_w, ffn_conv_b=ffn_conv_b, ffn_down=ffn_down)
    depth = w_mod.shape[0]
    n_dec = c.shape[0]
    mod = _modulation(jnp.concatenate([c_ctx[None], c], axis=0), w_mod, b_mod)
    split6 = lambda m: [m[:, None, k * D_MODEL:(k + 1) * D_MODEL] for k in range(6)]
    mods_ctx = [split6(mod[l, 0:1]) for l in range(depth)]
    mods_dec = [split6(mod[l, 1:1 + n_dec]) for l in range(depth)]

    y_prompt, nw = _trunk(x_prompt, mods_ctx, p, None, None)
    y_sample, _ = _trunk(x_sample, mods_dec, p, (state_s5_re, state_s5_im, state_gdn),
                         (cache_c_k, cache_c_v, cache_d_k, cache_d_v))
    st = lambda name: jnp.stack(nw[name], axis=1)
    return (y_prompt, y_sample, st('s5r'), st('s5i'), st('gdn'), st('ck'), st('cv'), st('dk'), st('dv'))
```

```python
import functools
import math

import jax
import jax.numpy as jnp
from jax import lax
from jax.experimental import pallas as pl
from jax.experimental.pallas import tpu as pltpu

F32 = jnp.float32
BF16 = jnp.bfloat16

D_MODEL = 1024
GRID_W = 64
EPS = 1e-6
NEG_INF = -1e30

S5_WIDTH = 512
S5_GROUP = 16
S5_GROUPS = 32
S5_STATE = 64
S5_TILE_GROUPS = 8
S5_TILE_CH = S5_TILE_GROUPS * S5_GROUP
S5_TILE_ST = S5_TILE_GROUPS * S5_STATE
S5_TILES = S5_GROUPS // S5_TILE_GROUPS
S5_SEG = 256
S5_SUB = 32
PROJ_AB_TOKENS = 64
S5_ROWS = 8

GDN_DK = 128
GDN_DV = 128
GDN_HEADS = 4
GDN_WIDTH = 512
GDN_CHUNK = 64
GDN_GROUP = 4
GDN_SHORT_LEN = 256
GDN_SEQS_SHORT = 2

HEAD_DIM = 64
C_HEADS = 8
C_KV_HEADS = 2
C_GROUP = 4
WINDOW = 128
Q_BLOCK = 128
D_HEADS = 4
D_VDIM = 128
ATTN_SCALE = HEAD_DIM ** -0.5
ATTN_SEQS_SHORT = 2
ROPE_THETA = 10000.0

D_FF = 2816
FF_CHUNK = 256
FF_CHUNKS = D_FF // FF_CHUNK
POST_ROWS = 512
POST_HALO = 8

AB_MAIN = S5_WIDTH + 4 * GDN_WIDTH
AB_PAD = AB_MAIN + 128
CD_IN = 2304

MOD_TK = 128
MOD_LANES = 512
VMEM_LIMIT = 56 * 1024 * 1024


def _cparams(sem):
    return pltpu.CompilerParams(dimension_semantics=sem, vmem_limit_bytes=VMEM_LIMIT)


def _sigmoid(x):
    return 1.0 / (1.0 + jnp.exp(-x))


def _silu(x):
    return x * _sigmoid(x)


def _softplus(x):
    return jnp.maximum(x, 0.0) + jnp.log(1.0 + jnp.exp(-jnp.abs(x)))


def _gelu_tanh(x):
    return 0.5 * x * (1.0 + jnp.tanh(math.sqrt(2.0 / math.pi) * (x + 0.044715 * (x * x * x))))


def _rms_mod(x, g, shift, scale):
    y = x * lax.rsqrt(jnp.mean(x * x, axis=-1, keepdims=True) + EPS)
    return (y * g) * (1.0 + scale) + shift


def _dot(a, b):
    return jnp.dot(a.astype(BF16), b.astype(BF16), preferred_element_type=F32)


def _dot_nt(a, b):
    return lax.dot_general(a.astype(BF16), b.astype(BF16), (((1,), (1,)), ((), ())),
                           preferred_element_type=F32)


def _hi_lo(a):
    hi = a.astype(BF16)
    return hi, (a - hi.astype(F32)).astype(BF16)


def _mod_kernel(ct_ref, w_ref, b_ref, o_ref, acc_ref, *, n_rows):
    k = pl.program_id(1)

    @pl.when(k == 0)
    def _():
        acc_ref[...] = jnp.zeros_like(acc_ref)

    tk, n_out = w_ref.shape[1:]
    s = _silu(ct_ref[...])
    sb = [[jnp.broadcast_to(s[g * 8:(g + 1) * 8, m:m + 1], (8, MOD_LANES)) for g in range(tk // 8)]
          for m in range(n_rows)]
    for c in range(n_out // MOD_LANES):
        cols = slice(c * MOD_LANES, (c + 1) * MOD_LANES)
        accs = [acc_ref[m, :, cols] for m in range(n_rows)]
        for g in range(tk // 8):
            wg = w_ref[0, g * 8:(g + 1) * 8, cols]
            accs = [a + wg * sb[m][g] for m, a in enumerate(accs)]
        for m in range(n_rows):
            acc_ref[m, :, cols] = accs[m]

    @pl.when(k == pl.num_programs(1) - 1)
    def _():
        o_ref[0] = jnp.zeros(o_ref.shape[1:], F32)
        for m in range(n_rows):
            o_ref[0, m:m + 1, :] = jnp.sum(acc_ref[m], axis=0, keepdims=True) + b_ref[0]


def _modulation(cvecs, w_mod, b_mod):
    n, d = cvecs.shape
    depth, _, n_out = w_mod.shape
    ct = jnp.zeros((d, 8), F32).at[:, :n].set(cvecs.T)
    return pl.pallas_call(
        functools.partial(_mod_kernel, n_rows=n),
        grid=(depth, d // MOD_TK),
        in_specs=[pl.BlockSpec((MOD_TK, 8), lambda l, k: (k, 0)),
                  pl.BlockSpec((1, MOD_TK, n_out), lambda l, k: (l, k, 0)),
                  pl.BlockSpec((1, 1, n_out), lambda l, k: (l, 0, 0))],
        out_specs=pl.BlockSpec((1, 8, n_out), lambda l, k: (l, 0, 0)),
        out_shape=jax.ShapeDtypeStruct((depth, 8, n_out), F32),
        scratch_shapes=[pltpu.VMEM((n, 8, n_out), F32)],
        compiler_params=_cparams(("parallel", "arbitrary")),
        name="adaln_mod",
    )(ct, w_mod, b_mod.reshape(depth, 1, n_out))


def _proj_ab_kernel(x_ref, g_ref, sh_ref, sc_ref, w_ref, wg_ref, u_ref, qkv_ref, z_ref, ab_ref):
    ns, tm, d = x_ref.shape
    h = _rms_mod(x_ref[...], g_ref[...], sh_ref[...], sc_ref[...])
    h = h.reshape(ns * tm, d).astype(BF16)
    y = jnp.dot(h, w_ref[...], preferred_element_type=F32)
    for s in range(ns):
        u_ref[:, s, :] = y[s * tm:(s + 1) * tm, :S5_WIDTH]
    qkv_ref[...] = y[:, S5_WIDTH:S5_WIDTH + 3 * GDN_WIDTH].reshape(ns, tm, 3 * GDN_WIDTH)
    z_ref[...] = y[:, S5_WIDTH + 3 * GDN_WIDTH:AB_MAIN].reshape(ns, tm, GDN_WIDTH)
    ab_ref[...] = jnp.dot(h, wg_ref[...], preferred_element_type=F32).reshape(ns, tm, 128)


def _proj_ab(x, g, shift, scale, w, w_gate):
    rows, n, d = x.shape
    tm = PROJ_AB_TOKENS
    per_row = shift.shape[0] > 1
    mspec = pl.BlockSpec((S5_ROWS, 1, d), lambda r, i: (r, 0, 0)) if per_row else \
        pl.BlockSpec((1, 1, d), lambda r, i: (0, 0, 0))
    blk = lambda w_: pl.BlockSpec((S5_ROWS, tm, w_), lambda r, i: (r, i, 0))
    return pl.pallas_call(
        _proj_ab_kernel,
        grid=(rows // S5_ROWS, n // tm),
        in_specs=[blk(d),
                  pl.BlockSpec((1, d), lambda r, i: (0, 0)),
                  mspec, mspec,
                  pl.BlockSpec((d, AB_MAIN), lambda r, i: (0, 0)),
                  pl.BlockSpec((d, 128), lambda r, i: (0, 0))],
        out_specs=[pl.BlockSpec((tm, S5_ROWS, S5_WIDTH), lambda r, i: (i, r, 0)),
                   blk(3 * GDN_WIDTH), blk(GDN_WIDTH), blk(128)],
        out_shape=[jax.ShapeDtypeStruct((n, rows, S5_WIDTH), F32),
                   jax.ShapeDtypeStruct((rows, n, 3 * GDN_WIDTH), F32),
                   jax.ShapeDtypeStruct((rows, n, GDN_WIDTH), F32),
                   jax.ShapeDtypeStruct((rows, n, 128), F32)],
        compiler_params=_cparams(("parallel", "parallel")),
        name="proj_ab",
    )(x, g, shift, scale, w, w_gate)


def _s5_kernel(u_ref, bm_ref, cm_ref, a_ref, h0r_ref, h0i_ref, *rest, want_y):
    if want_y:
        y_ref, fr_ref, fi_ref, xs_ref = rest
    else:
        fr_ref, fi_ref, xs_ref = rest
    n = u_ref.shape[0]
    sub = S5_SUB
    nsub = n // sub
    st = S5_TILE_ST

    def x_proj(d, k):
        u2 = u_ref[k * sub:(k + 1) * sub].reshape(sub * S5_ROWS, S5_TILE_CH)
        xs_ref[d, k * sub:(k + 1) * sub] = _dot(u2, bm_ref[d, 0]).reshape(sub, S5_ROWS, 2 * st)

    ar = [jnp.broadcast_to(a_ref[d, 0, 0:1, :], (S5_ROWS, st)) for d in range(2)]
    ai = [jnp.broadcast_to(a_ref[d, 0, 1:2, :], (S5_ROWS, st)) for d in range(2)]
    hr = [h0r_ref[0], h0r_ref[1]]
    hi = [h0i_ref[0], h0i_ref[1]]
    x_proj(0, 0)
    x_proj(1, nsub - 1)
    written = set()
    for k in range(nsub):
        ks = (k, nsub - 1 - k)
        if k + 1 < nsub:
            x_proj(0, ks[0] + 1)
            x_proj(1, ks[1] - 1)
        for t in range(sub):
            for d in range(2):
                tt = ks[d] * sub + (t if d == 0 else sub - 1 - t)
                x = xs_ref[d, tt]
                nr = ar[d] * hr[d] - ai[d] * hi[d] + x[:, :st]
                ni = ar[d] * hi[d] + ai[d] * hr[d] + x[:, st:]
                xs_ref[d, tt] = jnp.concatenate([nr, ni], axis=-1)
                hr[d], hi[d] = nr, ni
        if want_y:
            for d in range(2):
                rows = slice(ks[d] * sub, (ks[d] + 1) * sub)
                hs = xs_ref[d, rows].reshape(sub * S5_ROWS, 2 * st)
                yv = _dot(hs, cm_ref[d, 0]).reshape(sub, S5_ROWS, S5_TILE_CH)
                if ks[d] in written:
                    y_ref[rows] += yv
                else:
                    y_ref[rows] = yv
                    written.add(ks[d])
    for d in range(2):
        fr_ref[d] = hr[d]
        fi_ref[d] = hi[d]


def _s5_scan(u_t, bmat, cmat, amat, h0r, h0i, want_y=True):
    n, rows, _ = u_t.shape
    state = lambda: pl.BlockSpec((2, S5_ROWS, S5_TILE_ST), lambda r, j: (0, r, j))
    st_shape = jax.ShapeDtypeStruct((2, rows, S5_GROUPS * S5_STATE), F32)
    out_specs = [state(), state()]
    out_shape = [st_shape, st_shape]
    if want_y:
        out_specs.insert(0, pl.BlockSpec((n, S5_ROWS, S5_TILE_CH), lambda r, j: (0, r, j)))
        out_shape.insert(0, jax.ShapeDtypeStruct((n, rows, S5_WIDTH), F32))
    return pl.pallas_call(
        functools.partial(_s5_kernel, want_y=want_y),
        grid=(rows // S5_ROWS, S5_TILES),
        in_specs=[pl.BlockSpec((n, S5_ROWS, S5_TILE_CH), lambda r, j: (0, r, j)),
                  pl.BlockSpec((2, 1, S5_TILE_CH, 2 * S5_TILE_ST), lambda r, j: (0, j, 0, 0)),
                  pl.BlockSpec((2, 1, 2 * S5_TILE_ST, S5_TILE_CH), lambda r, j: (0, j, 0, 0)),
                  pl.BlockSpec((2, 1, 8, S5_TILE_ST), lambda r, j: (0, j, 0, 0)),
                  state(), state()],
        out_specs=out_specs,
        out_shape=out_shape,
        scratch_shapes=[pltpu.VMEM((2, n, S5_ROWS, 2 * S5_TILE_ST), F32)],
        compiler_params=_cparams(("parallel", "parallel")),
        name="s5_scan" if want_y else "s5_states",
    )(u_t, bmat, cmat, amat, h0r, h0i)


def _s5_params(p, j):
    lam_re, lam_im, log_dt = p['s5_lam_re'][j], p['s5_lam_im'][j], p['s5_log_dt'][j]
    dt = jnp.exp(log_dt)[..., None]
    mag = jnp.exp(lam_re * dt)
    ar, ai = mag * jnp.cos(lam_im * dt), mag * jnp.sin(lam_im * dt)
    den = lam_re * lam_re + lam_im * lam_im
    fr = ((ar - 1.0) * lam_re + ai * lam_im) / den
    fi = (ai * lam_re - (ar - 1.0) * lam_im) / den
    b_re, b_im = p['s5_b_re'][j], p['s5_b_im'][j]
    bbr = fr[..., None] * b_re - fi[..., None] * b_im
    bbi = fr[..., None] * b_im + fi[..., None] * b_re
    eye = jnp.eye(S5_TILE_GROUPS, dtype=F32)

    def in_blocks(t):
        t = t.reshape(2, S5_TILES, S5_TILE_GROUPS, S5_STATE, S5_GROUP)
        t = jnp.einsum('dtgpc,gh->dtgchp', t, eye)
        return t.reshape(2, S5_TILES, S5_TILE_CH, S5_TILE_ST)

    def out_blocks(t):
        t = t.reshape(2, S5_TILES, S5_TILE_GROUPS, S5_GROUP, S5_STATE)
        t = jnp.einsum('dtgcp,gh->dtgphc', t, eye)
        return t.reshape(2, S5_TILES, S5_TILE_ST, S5_TILE_CH)

    bmat = jnp.concatenate([in_blocks(bbr), in_blocks(bbi)], axis=-1).astype(BF16)
    cmat = jnp.concatenate([out_blocks(p['s5_c_re'][j]), -out_blocks(p['s5_c_im'][j])], axis=-2).astype(BF16)
    seg_mag = jnp.exp(lam_re * dt * S5_SEG)
    pr, pi = seg_mag * jnp.cos(lam_im * dt * S5_SEG), seg_mag * jnp.sin(lam_im * dt * S5_SEG)
    flat = lambda t: t.reshape(2, S5_TILES, 1, S5_TILE_ST)
    amat = jnp.concatenate([flat(ar), flat(ai), flat(pr), flat(pi),
                            jnp.zeros((2, S5_TILES, 4, S5_TILE_ST), F32)], axis=2)
    return bmat, cmat, amat


def _s5_glu_kernel(y_ref, u_ref, d_ref, w_ref, b_ref, o_ref):
    tm, ns, _ = y_ref.shape
    y = jnp.concatenate([y_ref[:, s, :] + d_ref[...] * u_ref[:, s, :] for s in range(ns)], axis=0)
    g = _gelu_tanh(y)
    out = g * _sigmoid(_dot(g, w_ref[...]) + b_ref[...])
    o_ref[...] = out.reshape(ns, tm, S5_WIDTH)


def _s5_glu(y_t, u_t, s5_d, w_glu, b_glu):
    n, rows, _ = y_t.shape
    tm = PROJ_AB_TOKENS
    tblk = pl.BlockSpec((tm, S5_ROWS, S5_WIDTH), lambda r, i: (i, r, 0))
    return pl.pallas_call(
        _s5_glu_kernel,
        grid=(rows // S5_ROWS, n // tm),
        in_specs=[tblk, tblk,
                  pl.BlockSpec((1, S5_WIDTH), lambda r, i: (0, 0)),
                  pl.BlockSpec((S5_WIDTH, S5_WIDTH), lambda r, i: (0, 0)),
                  pl.BlockSpec((1, S5_WIDTH), lambda r, i: (0, 0))],
        out_specs=pl.BlockSpec((S5_ROWS, tm, S5_WIDTH), lambda r, i: (r, i, 0)),
        out_shape=jax.ShapeDtypeStruct((rows, n, S5_WIDTH), F32),
        compiler_params=_cparams(("parallel", "parallel")),
        name="s5_glu",
    )(y_t, u_t, s5_d, w_glu, b_glu)


def _s5_mixer(u3, p, j, h0r, h0i, bsz, nseg):
    rows = bsz * nseg
    bmat, cmat, amat = _s5_params(p, j)
    if nseg == 1:
        y_t, fr, fi = _s5_scan(u3, bmat, cmat, amat, h0r, h0i)
    else:
        zero = jnp.zeros((2, bsz, nseg, S5_GROUPS * S5_STATE), F32)
        first = jnp.array([0, nseg - 1])
        seed = lambda h0: zero.at[jnp.arange(2), :, first].set(h0).reshape(2, rows, -1)
        fr, fi = _s5_scan(u3, bmat, cmat, amat, seed(h0r), seed(h0i), want_y=False)
        fr = fr.reshape(2, bsz, nseg, -1)
        fi = fi.reshape(2, bsz, nseg, -1)
        pr = amat[:, :, 2].reshape(2, 1, -1)
        pi = amat[:, :, 3].reshape(2, 1, -1)

        def chain(dr, order):
            hr, hi = (h0r[dr], h0i[dr])
            outs_r, outs_i = {}, {}
            for n_done, k in enumerate(order):
                outs_r[k], outs_i[k] = hr, hi
                if n_done == 0:
                    hr, hi = fr[dr, :, k], fi[dr, :, k]
                else:
                    hr, hi = (pr[dr] * hr - pi[dr] * hi + fr[dr, :, k],
                              pr[dr] * hi + pi[dr] * hr + fi[dr, :, k])
            st = lambda o: jnp.stack([o[k] for k in range(nseg)], axis=1)
            return st(outs_r), st(outs_i), hr, hi

        sr0, si0, er0, ei0 = chain(0, list(range(nseg)))
        sr1, si1, er1, ei1 = chain(1, list(range(nseg - 1, -1, -1)))
        start_r = jnp.stack([sr0, sr1]).reshape(2, rows, -1)
        start_i = jnp.stack([si0, si1]).reshape(2, rows, -1)
        y_t, _, _ = _s5_scan(u3, bmat, cmat, amat, start_r, start_i)
        fr = jnp.stack([er0, er1])
        fi = jnp.stack([ei0, ei1])
    ya = _s5_glu(y_t, u3, p['s5_d'][j][None], p['s5_w_glu'][j].astype(BF16), p['s5_b_glu'][j][None])
    return ya, fr, fi


def _gdn_kernel(qkv_ref, z_ref, ab_ref, cw_ref, gp_ref, ng_ref, *rest, zero_init):
    if zero_init:
        s0_ref = None
        o_ref, sf_ref, q_s, k_s, v_s, gate_s, of_s, ob_s, st_s, uw_s, a_s, kgt_s = rest
    else:
        s0_ref, o_ref, sf_ref, q_s, k_s, v_s, gate_s, of_s, ob_s, st_s, uw_s, a_s, kgt_s = rest
    ns, n = qkv_ref.shape[:2]
    nc = n // GDN_CHUNK
    row = lax.broadcasted_iota(jnp.int32, (n, 1), 0)

    for sq in range(ns):
        for blk in range(3 * GDN_HEADS):
            cols = slice(blk * GDN_DK, (blk + 1) * GDN_DK)
            hs = slice((blk % GDN_HEADS) * GDN_DK, (blk % GDN_HEADS + 1) * GDN_DK)
            x = qkv_ref[sq, :, cols]
            xm = jnp.where(row == 0, 0.0, pltpu.roll(x, 1, 0))
            xp = jnp.where(row == n - 1, 0.0, pltpu.roll(x, n - 1, 0))
            y = _silu(xm * cw_ref[0:1, cols] + x * cw_ref[1:2, cols] + xp * cw_ref[2:3, cols])
            if blk < GDN_HEADS:
                q_s[sq, :, hs] = y * lax.rsqrt(jnp.sum(y * y, axis=-1, keepdims=True) + EPS) * (GDN_DK ** -0.5)
            elif blk < 2 * GDN_HEADS:
                k_s[sq, :, hs] = y * lax.rsqrt(jnp.sum(y * y, axis=-1, keepdims=True) + EPS)
            else:
                v_s[sq, :, hs] = y

        ab = ab_ref[sq]
        beta = _sigmoid(ab)
        g = -jnp.exp(gp_ref[0:1, :]) * _softplus(ab + gp_ref[1:2, :])
        pos = row % GDN_CHUNK
        pre, suf = g, g
        sft = 1
        while sft < GDN_CHUNK:
            pre = pre + jnp.where(pos >= sft, pltpu.roll(pre, sft, 0), 0.0)
            suf = suf + jnp.where(pos < GDN_CHUNK - sft, pltpu.roll(suf, n - sft, 0), 0.0)
            sft *= 2
        gate_s[sq, 0] = beta
        gate_s[sq, 1] = pre
        gate_s[sq, 2] = suf

    st_s[...] = jnp.zeros_like(st_s) if zero_init else s0_ref[...]
    cs = GDN_CHUNK
    pk = GDN_HEADS * cs
    ri = lax.broadcasted_iota(jnp.int32, (cs, pk), 0)
    lane_pk = lax.broadcasted_iota(jnp.int32, (cs, pk), 1)
    ci = lane_pk % cs
    eye_pk = (ri == ci).astype(F32)
    head_pk = [(lax.broadcasted_iota(jnp.int32, (1, pk), 1) // cs) == h for h in range(GDN_HEADS)]
    head_w = [(lax.broadcasted_iota(jnp.int32, (1, GDN_WIDTH), 1) // GDN_DK) == h for h in range(GDN_HEADS)]

    def block_diag(p):
        return jnp.concatenate([jnp.where(m, p, jnp.zeros_like(p)) for m in head_pk], axis=0)

    def dot3_bd(a, p):
        a_hi, a_lo = _hi_lo(a)
        p_hi, p_lo = _hi_lo(p)
        b_hi, b_lo = block_diag(p_hi), block_diag(p_lo)
        mm = functools.partial(jnp.dot, preferred_element_type=F32)
        return mm(a_hi, b_hi) + (mm(a_lo, b_hi) + mm(a_hi, b_lo))

    def lanes(cols, width):
        return jnp.concatenate([jnp.broadcast_to(c, (cs, width)) for c in cols], axis=1)

    def phase_a(sq, it):
        st = []
        for cc in range(GDN_GROUP):
            rows = pl.ds(pl.multiple_of((it * GDN_GROUP + cc) * cs, cs), cs)
            q_all, k_all, v_all = q_s[sq, rows, :], k_s[sq, rows, :], v_s[sq, rows, :]
            beta_blk = gate_s[sq, 0, rows, :]
            for dr in range(2):
                gc_blk = gate_s[sq, 1 + dr, rows, :]
                lane0 = dr * GDN_HEADS
                bcols = [beta_blk[:, 8 + lane0 + h:9 + lane0 + h] for h in range(GDN_HEADS)]
                gcols = [gc_blk[:, lane0 + h:lane0 + h + 1] for h in range(GDN_HEADS)]
                st.append(dict(sq=sq, dr=dr, rows=rows, cidx=it * GDN_GROUP + cc, q=q_all, k=k_all, v=v_all, bcols=bcols, gcols=gcols,
                               incl=(ri >= ci) if dr == 0 else (ri <= ci),
                               strict=(ri > ci) if dr == 0 else (ri < ci)))
        for t in st:
            b_w = lanes(t['bcols'], GDN_DK)
            t['kb'] = t['k'] * b_w
            t['vb'] = t['v'] * b_w
            k_bd = jnp.concatenate([jnp.where(m, t['k'], 0.0) for m in head_w], axis=0)
            t['kq'] = _dot_nt(jnp.concatenate([t['kb'], t['q']], axis=0), k_bd)
        for t in st:
            gcol = lanes(t['gcols'], cs)
            grow = jnp.sum(eye_pk * gcol, axis=0, keepdims=True)
            decay = jnp.exp(jnp.where(t['incl'], gcol - grow, NEG_INF))
            t['pw'] = jnp.where(t['strict'], t['kq'][:cs] * decay, 0.0)
            a_s[t['sq'], t['dr'], t['rows'], :] =jnp.where(t['incl'], t['kq'][cs:] * decay, 0.0)
            t['tm'] = eye_pk - t['pw']
        for t in st:
            t['pw'] = dot3_bd(t['pw'], t['pw'])
        for rnd in range(5):
            for t in st:
                if rnd < 4:
                    r = dot3_bd(jnp.concatenate([t['pw'], t['tm']], axis=0), t['pw'])
                    t['pw'] = r[:cs]
                    t['tm'] = t['tm'] + r[cs:]
                else:
                    t['tm'] = t['tm'] + dot3_bd(t['tm'], t['pw'])
        for t in st:
            for h in range(GDN_HEADS):
                g = t['gcols'][h]
                gl = g[cs - 1:cs] if t['dr'] == 0 else g[0:1]
                kg = t['k'][:, h * GDN_DK:(h + 1) * GDN_DK] * jnp.exp(gl - g)
                kgt_s[t['sq'], t['dr'], t['cidx'], h] = kg.T.astype(BF16)
        for t in st:
            kbg = t['kb'] * lanes([jnp.exp(g) for g in t['gcols']], GDN_DK)
            for h in range(GDN_HEADS):
                hs = slice(h * GDN_DK, (h + 1) * GDN_DK)
                rhs = jnp.concatenate([t['vb'][:, hs], kbg[:, hs]], axis=1)
                uw_s[t['sq'], t['dr'], t['rows'], 2 * h * GDN_DK:2 * (h + 1) * GDN_DK] = _dot(
                    t['tm'][:, h * cs:(h + 1) * cs], rhs)

    for sq in range(ns):
        if nc // GDN_GROUP == 1:
            phase_a(sq, 0)
        else:
            lax.fori_loop(0, nc // GDN_GROUP, lambda it, c, sq=sq: (phase_a(sq, it), c)[1], 0)

    def phase_b(c, carry):
        ch = []
        for sq, dr in [(sq, dr) for sq in range(ns) for dr in range(2)]:
            cidx = c if dr == 0 else nc - 1 - c
            rows = pl.ds(pl.multiple_of(cidx * cs, cs), cs)
            gc_blk = gate_s[sq, 1 + dr, rows, :]
            for h in range(GDN_HEADS):
                hs = slice(h * GDN_DK, (h + 1) * GDN_DK)
                lane = dr * GDN_HEADS + h
                gcol = gc_blk[:, lane:lane + 1]
                ch.append(dict(sq=sq, dr=dr, h=h, rows=rows, hs=hs, gcol=gcol, cidx=cidx,
                               gl=gcol[cs - 1:cs] if dr == 0 else gcol[0:1],
                               uw=uw_s[sq, dr, rows, 2 * h * GDN_DK:2 * (h + 1) * GDN_DK],
                               amat=a_s[sq, dr, rows, h * cs:(h + 1) * cs]))
        for t in ch:
            t['s'] = st_s[t['sq'], t['dr'], t['h']]
            qg = q_s[t['sq'], t['rows'], t['hs']] * jnp.exp(t['gcol'])
            t['ws'] = _dot(jnp.concatenate([t['uw'][:, GDN_DV:], qg], axis=0), t['s'])
        for t in ch:
            vn = t['uw'][:, :GDN_DV] - t['ws'][:cs]
            o = t['ws'][cs:] + _dot(t['amat'], vn)
            kgt = kgt_s[t['sq'], t['dr'], t['cidx'], t['h']]
            st_s[t['sq'], t['dr'], t['h']] = t['s'] * jnp.exp(t['gl']) + _dot(kgt, vn)
            if t['dr'] == 0:
                of_s[t['sq'], t['rows'], t['hs']] = o
            else:
                ob_s[t['sq'], t['rows'], t['hs']] = o
        return carry

    lax.fori_loop(0, nc, phase_b, 0)
    sf_ref[...] = st_s[...]

    for sq in range(ns):
        z = z_ref[sq]
        for h in range(GDN_HEADS):
            hs = slice(h * GDN_DV, (h + 1) * GDN_DV)
            o = of_s[sq, :, hs] + ob_s[sq, :, hs]
            o = o * lax.rsqrt(jnp.mean(o * o, axis=-1, keepdims=True) + EPS) * ng_ref[...]
            o_ref[sq, :, hs] = o * _silu(z[:, hs])


def _gdn_mixer(qkv, z, ab, p, j, s0):
    bsz, n, _ = qkv.shape
    gp = jnp.zeros((8, 128), F32)
    gp = gp.at[0, :8].set(p['gdn_a_log'][j].reshape(8)).at[1, :8].set(p['gdn_dt_bias'][j].reshape(8))
    cw = jnp.zeros((8, 3 * GDN_WIDTH), F32).at[:3].set(p['gdn_conv_w'][j])
    ns = GDN_SEQS_SHORT if (n <= GDN_SHORT_LEN and bsz % GDN_SEQS_SHORT == 0) else 1
    blk = lambda w_: pl.BlockSpec((ns, n, w_), lambda b: (b, 0, 0))
    sblk = pl.BlockSpec((ns, 2, GDN_HEADS, GDN_DK, GDN_DV), lambda b: (b, 0, 0, 0, 0))
    tok = lambda w_: pltpu.VMEM((ns, n, w_), F32)
    return pl.pallas_call(
        functools.partial(_gdn_kernel, zero_init=s0 is None),
        grid=(bsz // ns,),
        in_specs=[blk(3 * GDN_WIDTH), blk(GDN_WIDTH), blk(128),
                  pl.BlockSpec((8, 3 * GDN_WIDTH), lambda b: (0, 0)),
                  pl.BlockSpec((8, 128), lambda b: (0, 0)),
                  pl.BlockSpec((1, GDN_DV), lambda b: (0, 0))] + ([] if s0 is None else [sblk]),
        out_specs=[blk(GDN_WIDTH), sblk],
        out_shape=[jax.ShapeDtypeStruct((bsz, n, GDN_WIDTH), F32),
                   jax.ShapeDtypeStruct((bsz, 2, GDN_HEADS, GDN_DK, GDN_DV), F32)],
        scratch_shapes=[tok(GDN_WIDTH), tok(GDN_WIDTH), tok(GDN_WIDTH),
                        pltpu.VMEM((ns, 3, n, 128), F32),
                        tok(GDN_WIDTH), tok(GDN_WIDTH),
                        pltpu.VMEM((ns, 2, GDN_HEADS, GDN_DK, GDN_DV), F32),
                        pltpu.VMEM((ns, 2, n, 2 * GDN_WIDTH), F32),
                        pltpu.VMEM((ns, 2, n, GDN_HEADS * GDN_CHUNK), F32),
                        pltpu.VMEM((ns, 2, n // GDN_CHUNK, GDN_HEADS, GDN_DK, GDN_CHUNK), BF16)],
        compiler_params=_cparams(("parallel",)),
        name="gdn_mixer",
    )(qkv, z, ab, cw, gp, p['gdn_norm_g'][j][None], *(() if s0 is None else (s0,)))


def _proj_cd_kernel(x_ref, g_ref, sh_ref, sc_ref, w_ref, gm_ref, gain_ref, cos_ref, sin_ref,
                    qc_ref, kc_ref, vc_ref, qd_ref, kd_ref, vd_ref, *, rope):
    h = _rms_mod(x_ref[0], g_ref[...], sh_ref[0], sc_ref[0])
    y = _dot(h, w_ref[...])
    lane = lax.broadcasted_iota(jnp.int32, (1, 512), 1)
    low = (lane % 32) < 16

    def head_norm(t, gain, scale):
        w = t.shape[1]
        ms = _dot(t * t, gm_ref[:w, :w])
        t = t * lax.rsqrt(ms + EPS) * gain
        if rope:
            part = jnp.where(low[:, :w], pltpu.roll(t, w - 16, 1), pltpu.roll(t, 16, 1))
            t = t * cos_ref[:, :w] + part * sin_ref[:, :w]
        return t * scale if scale != 1.0 else t

    qc_ref[0] = head_norm(y[:, 0:512], gain_ref[0:1, :], ATTN_SCALE)
    kc_ref[0] = head_norm(y[:, 512:640], gain_ref[1:2, :128], 1.0)
    vc_ref[0] = y[:, 640:768]
    qd_ref[0] = head_norm(y[:, 768:1280], gain_ref[2:3, :], ATTN_SCALE)
    kd_ref[0] = head_norm(y[:, 1280:1792], gain_ref[3:4, :], 1.0)
    vd_ref[0] = y[:, 1792:2304]


def _rope_tables(n):
    rows = n // GRID_W
    row = jnp.repeat(jnp.arange(rows), GRID_W).astype(F32)
    col = jnp.tile(jnp.arange(GRID_W), rows).astype(F32)
    quarter = HEAD_DIM // 4
    inv = ROPE_THETA ** (-jnp.arange(quarter, dtype=F32) / quarter)
    ang_r = row[:, None] * inv[None, :]
    ang_c = col[:, None] * inv[None, :]
    cos = jnp.concatenate([jnp.cos(ang_r), jnp.cos(ang_r), jnp.cos(ang_c), jnp.cos(ang_c)], axis=-1)
    sin = jnp.concatenate([-jnp.sin(ang_r), jnp.sin(ang_r), -jnp.sin(ang_c), jnp.sin(ang_c)], axis=-1)
    return jnp.tile(cos, (1, 8)), jnp.tile(sin, (1, 8))


def _proj_cd(x, g, shift, scale, w, p, j, rope):
    bsz, n, d = x.shape
    tm = 256
    per_seq = shift.shape[0] > 1
    midx = (lambda b, i: (b, 0, 0)) if per_seq else (lambda b, i: (0, 0, 0))
    lane = jnp.arange(512)
    gmat = ((lane[:, None] // HEAD_DIM) == (lane[None, :] // HEAD_DIM)).astype(F32) / HEAD_DIM
    gains = jnp.zeros((8, 512), F32)
    gains = gains.at[0].set(jnp.tile(p['c_qn'][j], 8)).at[1].set(jnp.tile(p['c_kn'][j], 8))
    gains = gains.at[2].set(jnp.tile(p['d_qn'][j], 8)).at[3].set(jnp.tile(p['d_kn'][j], 8))
    if rope:
        cos, sin = _rope_tables(n)
    else:
        cos, sin = jnp.ones((tm, 512), F32), jnp.zeros((tm, 512), F32)
    tidx = (lambda b, i: (i, 0)) if rope else (lambda b, i: (0, 0))
    blk = lambda w_: pl.BlockSpec((1, tm, w_), lambda b, i: (b, i, 0))
    return pl.pallas_call(
        functools.partial(_proj_cd_kernel, rope=rope),
        grid=(bsz, n // tm),
        in_specs=[blk(d),
                  pl.BlockSpec((1, d), lambda b, i: (0, 0)),
                  pl.BlockSpec((1, 1, d), midx),
                  pl.BlockSpec((1, 1, d), midx),
                  pl.BlockSpec((d, CD_IN), lambda b, i: (0, 0)),
                  pl.BlockSpec((512, 512), lambda b, i: (0, 0)),
                  pl.BlockSpec((8, 512), lambda b, i: (0, 0)),
                  pl.BlockSpec((tm, 512), tidx),
                  pl.BlockSpec((tm, 512), tidx)],
        out_specs=[blk(512), blk(128), blk(128), blk(512), blk(512), blk(512)],
        out_shape=[jax.ShapeDtypeStruct((bsz, n, w_), F32) for w_ in (512, 128, 128, 512, 512, 512)],
        compiler_params=_cparams(("parallel", "parallel")),
        name="proj_cd",
    )(x, g, shift, scale, w, gmat.astype(BF16), gains, cos, sin)


def _probs(scores, extra=None):
    m = scores[0].max(axis=-1, keepdims=True)
    for s in scores[1:]:
        m = jnp.maximum(m, s.max(axis=-1, keepdims=True))
    if extra is not None:
        m = jnp.maximum(m, extra)
    return [jnp.exp(s - m).astype(BF16) for s in scores], m


def _attn_kernel(qc_ref, kc_ref, vc_ref, qd_ref, kd_ref, vd_ref, *rest, windowed, n_ctx, lam_scale):
    if n_ctx:
        ck_ref, cv_ref, dk_ref, dv_ref, misc_ref, sub_ref, oc_ref, od_ref = rest
    else:
        misc_ref, sub_ref, oc_ref, od_ref = rest
    tq = qc_ref.shape[1]
    n = kc_ref.shape[1]
    start = pl.program_id(1) * tq
    if windowed:
        span = tq + 2 * WINDOW
        k0 = pl.multiple_of(jnp.clip(start - WINDOW, 0, n - span), 128)
        krows = pl.ds(k0, span)
        qpos = start + lax.broadcasted_iota(jnp.int32, (tq, span), 0)
        kpos = k0 + lax.broadcasted_iota(jnp.int32, (tq, span), 1)
        ok = jnp.abs(qpos - kpos) <= WINDOW
    else:
        krows = pl.ds(0, n)
    low = lax.broadcasted_iota(jnp.int32, (1, 2 * HEAD_DIM), 1) < HEAD_DIM

    def key_ops(k):
        kr = pltpu.roll(k, HEAD_DIM, 1)
        z = jnp.zeros_like(k)
        return {(0, 0): jnp.where(low, k, z).astype(BF16), (0, 1): jnp.where(low, z, kr).astype(BF16),
                (1, 0): jnp.where(low, kr, z).astype(BF16), (1, 1): jnp.where(low, z, k).astype(BF16)}

    def with_ones(v):
        return jnp.concatenate([v, jnp.ones_like(v)], axis=1)

    def val_ops(v):
        return {key: with_ones(op) for key, op in key_ops(v).items()}

    def half_ops(k):
        z = jnp.zeros_like(k)
        return [jnp.where(low, k, z).astype(BF16), jnp.where(low, z, k).astype(BF16)]

    seqs = range(qc_ref.shape[0])

    c_scores, d_scores = [[] for _ in seqs], [[] for _ in seqs]
    for sq in seqs:
        qc = qc_ref[sq].astype(BF16)
        qd = qd_ref[sq].astype(BF16)
        kc_ops = key_ops(kc_ref[sq, krows, :])
        ck_ops = key_ops(ck_ref[sq]) if n_ctx else None
        for h in range(C_HEADS):
            key = (h // C_GROUP, h % 2)
            q = qc[:, (h // 2) * 128:(h // 2 + 1) * 128]
            sc = [_dot_nt(q, kc_ops[key])]
            if n_ctx:
                sc.append(_dot_nt(q, ck_ops[key]))
            c_scores[sq].append(sc)
        for h in range(D_HEADS):
            q = qd[:, h * 128:(h + 1) * 128]
            kd_ops = half_ops(kd_ref[sq, :, h * 128:(h + 1) * 128])
            dk_ops = half_ops(dk_ref[sq, :, h * 128:(h + 1) * 128]) if n_ctx else None
            for c in range(2):
                sc = [_dot_nt(q, kd_ops[c])]
                if n_ctx:
                    sc.append(_dot_nt(q, dk_ops[c]))
                d_scores[sq].append(sc)

    c_probs, d_probs = [[] for _ in seqs], [[] for _ in seqs]
    for sq in seqs:
        for h in range(C_HEADS):
            sc = c_scores[sq][h]
            if windowed:
                sc[0] = jnp.where(ok, sc[0], NEG_INF)
            c_probs[sq].append(_probs(sc, misc_ref[0:1, h:h + 1]))
        d_probs[sq] = [_probs(sc)[0] for sc in d_scores[sq]]

    lam = misc_ref[1:2, 0:1]
    for sq in seqs:
        vc_ops = val_ops(vc_ref[sq, krows, :])
        cv_ops = val_ops(cv_ref[sq]) if n_ctx else None
        for j in range(C_HEADS // 2):
            pair = None
            for h in (2 * j, 2 * j + 1):
                key = (h // C_GROUP, h % 2)
                ps, m = c_probs[sq][h]
                r = jnp.dot(ps[0], vc_ops[key], preferred_element_type=F32)
                if n_ctx:
                    r = r + jnp.dot(ps[1], cv_ops[key], preferred_element_type=F32)
                o = r[:, :128] / (r[:, 128:] + jnp.exp(misc_ref[0:1, h:h + 1] - m))
                pair = o if pair is None else pair + o
            oc_ref[sq, :, j * 128:(j + 1) * 128] = pair
        for h in range(D_HEADS):
            vsl = slice(h * D_VDIM, (h + 1) * D_VDIM)
            v_op = with_ones(vd_ref[sq, :, vsl]).astype(BF16)
            dv_op = with_ones(dv_ref[sq, :, vsl]).astype(BF16) if n_ctx else None
            parts = []
            for c in range(2):
                ps = d_probs[sq][2 * h + c]
                r = jnp.dot(ps[0], v_op, preferred_element_type=F32)
                if n_ctx:
                    r = r + jnp.dot(ps[1], dv_op, preferred_element_type=F32)
                parts.append(r[:, :D_VDIM] / r[:, D_VDIM:])
            o = parts[0] - lam * parts[1]
            o = o * lax.rsqrt(jnp.mean(o * o, axis=-1, keepdims=True) + EPS) * sub_ref[...] * lam_scale
            od_ref[sq, :, vsl] = o


def _attention(qc, kc, vc, qd, kd, vd, caches, misc, subln, lam_init, tq, windowed):
    bsz, n, _ = qc.shape
    ns = ATTN_SEQS_SHORT if (not windowed and tq == n and bsz % ATTN_SEQS_SHORT == 0) else 1
    qblk = lambda w_: pl.BlockSpec((ns, tq, w_), lambda b, i: (b, i, 0))
    kblk = lambda rows, w_: pl.BlockSpec((ns, rows, w_), lambda b, i: (b, 0, 0))
    in_specs = [qblk(512), kblk(n, 128), kblk(n, 128), qblk(512), kblk(n, 512), kblk(n, 512)]
    args = [qc, kc, vc, qd, kd, vd]
    n_ctx = 0
    if caches is not None:
        n_ctx = caches[0].shape[1]
        in_specs += [kblk(n_ctx, 128), kblk(n_ctx, 128), kblk(n_ctx, 512), kblk(n_ctx, 512)]
        args += list(caches)
    in_specs += [pl.BlockSpec((8, 128), lambda b, i: (0, 0)), pl.BlockSpec((1, D_VDIM), lambda b, i: (0, 0))]
    args += [misc, subln]
    return pl.pallas_call(
        functools.partial(_attn_kernel, windowed=windowed, n_ctx=n_ctx, lam_scale=1.0 - lam_init),
        grid=(bsz // ns, n // tq),
        in_specs=in_specs,
        out_specs=[qblk(512), qblk(512)],
        out_shape=[jax.ShapeDtypeStruct((bsz, n, 512), F32), jax.ShapeDtypeStruct((bsz, n, 512), F32)],
        compiler_params=_cparams(("parallel", "parallel")),
        name="attn_win" if windowed else "attn_ctx",
    )(*args)


def _post_kernel(x_ref, xp_ref, xn_ref, ma_ref, map_ref, man_ref, mb_ref, mbp_ref, mbn_ref,
                 g1_ref, sh_ref, sc_ref, g2_ref, ng_ref, wo_ref, wup_ref, cw_ref, wdn_ref, o_ref, act_ref,
                 *, seq_len):
    rows = x_ref.shape[0]
    ext = rows + 2 * POST_HALO
    half = ma_ref.shape[1]
    xe = jnp.concatenate([xp_ref[...], x_ref[...], xn_ref[...]], axis=0)
    mae = jnp.concatenate([map_ref[...], ma_ref[...], man_ref[...]], axis=0)
    mbe = jnp.concatenate([mbp_ref[...], mb_ref[...], mbn_ref[...]], axis=0)
    x1 = xe + g1_ref[0] * (_dot(mae, wo_ref[:half, :]) + _dot(mbe, wo_ref[half:, :]))
    h = _rms_mod(x1, ng_ref[...], sh_ref[0], sc_ref[0]).astype(BF16)
    x1 = x1[POST_HALO:POST_HALO + rows]
    row0 = pl.program_id(0) * rows - POST_HALO
    pos = (row0 + lax.broadcasted_iota(jnp.int32, (ext, 1), 0)) % seq_len
    first = pos == 0
    last = pos == seq_len - 1
    for c in range(FF_CHUNKS):
        cs = slice(c * FF_CHUNK, (c + 1) * FF_CHUNK)
        a = jnp.dot(h, wup_ref[:, cs], preferred_element_type=F32)
        b = jnp.dot(h, wup_ref[:, D_FF + c * FF_CHUNK:D_FF + (c + 1) * FF_CHUNK], preferred_element_type=F32)
        am = jnp.where(first, 0.0, pltpu.roll(a, 1, 0))
        ap = jnp.where(last, 0.0, pltpu.roll(a, ext - 1, 0))
        a = am * cw_ref[0:1, cs] + a * cw_ref[1:2, cs] + ap * cw_ref[2:3, cs] + cw_ref[3:4, cs]
        act_ref[:, cs] = (_silu(a) * b)[POST_HALO:POST_HALO + rows].astype(BF16)
    ffn = jnp.dot(act_ref[...], wdn_ref[...], preferred_element_type=F32)
    o_ref[...] = x1 + g2_ref[0] * ffn


def _post(x, mix_a, mix_b, g1, sh2, sc2, g2, norm_g, w_out, wup, cw, wdn, layer):
    bsz, seq_len, d = x.shape
    half = mix_a.shape[-1]
    rows = POST_ROWS
    total = bsz * seq_len
    nhalo = total // POST_HALO
    per_seq = g1.shape[0] > 1
    midx = (lambda i: ((i * rows) // seq_len, 0, 0)) if per_seq else (lambda i: (0, 0, 0))
    mspec = pl.BlockSpec((1, 1, d), midx)
    pidx = lambda i: (jnp.maximum(i * (rows // POST_HALO) - 1, 0), 0)
    nidx = lambda i: (jnp.minimum((i + 1) * (rows // POST_HALO), nhalo - 1), 0)
    trio = lambda w_: [pl.BlockSpec((rows, w_), lambda i: (i, 0)), pl.BlockSpec((POST_HALO, w_), pidx),
                       pl.BlockSpec((POST_HALO, w_), nidx)]
    whole = lambda shape: pl.BlockSpec(shape, lambda i: (0,) * len(shape), pipeline_mode=pl.Buffered(1))
    x2 = x.reshape(total, d)
    a2 = mix_a.reshape(total, half)
    b2 = mix_b.reshape(total, half)
    out = pl.pallas_call(
        functools.partial(_post_kernel, seq_len=seq_len),
        grid=(total // rows,),
        in_specs=trio(d) + trio(half) + trio(half) + [
            mspec, mspec, mspec, mspec,
            pl.BlockSpec((1, d), lambda i: (0, 0)),
            whole((d, d)),
            pl.BlockSpec((None, d, 2 * D_FF), lambda i: (layer, 0, 0), pipeline_mode=pl.Buffered(1)),
            whole((8, D_FF)),
            pl.BlockSpec((None, D_FF, d), lambda i: (layer, 0, 0), pipeline_mode=pl.Buffered(1))],
        out_specs=pl.BlockSpec((rows, d), lambda i: (i, 0)),
        out_shape=jax.ShapeDtypeStruct((total, d), F32),
        scratch_shapes=[pltpu.VMEM((rows, D_FF), BF16)],
        compiler_params=_cparams(("parallel",)),
        name="post_ffn",
    )(x2, x2, x2, a2, a2, a2, b2, b2, b2, g1, sh2, sc2, g2, norm_g, w_out, wup, cw, wdn)
    return out.reshape(bsz, seq_len, d)


def _ffn_weights(p, l):
    cw = jnp.zeros((8, D_FF), F32).at[:3].set(p['ffn_conv_w'][l]).at[3].set(p['ffn_conv_b'][l])
    return p['ffn_up'].astype(BF16), cw, p['ffn_down'].astype(BF16)


def _lambda_init(layer):
    return 0.8 - 0.6 * math.exp(-0.3 * layer)


def _trunk(x, mods, p, states, caches):
    bsz, n, d = x.shape
    nseg = n // S5_SEG
    depth = p['w_mod'].shape[0]
    news = {k: [] for k in ('s5r', 's5i', 'gdn', 'ck', 'cv', 'dk', 'dv')}
    for l in range(depth):
        j = l // 2
        sh1, sc1, g1, sh2, sc2, g2 = mods[l]
        ng1 = p['norm1_g'][l][None]
        if l % 2 == 0:
            w_in = p['w_in_ab'][j]
            w_gate = jnp.zeros((d, 128), BF16).at[:, :w_in.shape[1] - AB_MAIN].set(
                w_in[:, AB_MAIN:].astype(BF16))
            rows = bsz * nseg
            per_row = lambda m: jnp.repeat(m, nseg, axis=0) if m.shape[0] > 1 else m
            u_t, qkv, z, ab = _proj_ab(x.reshape(rows, S5_SEG, d), ng1, per_row(sh1), per_row(sc1),
                                       w_in[:, :AB_MAIN].astype(BF16), w_gate)
            qkv, z, ab = (t.reshape(bsz, n, t.shape[-1]) for t in (qkv, z, ab))
            if states is None:
                h0r = jnp.zeros((2, bsz, S5_GROUPS * S5_STATE), F32)
                h0i = h0r
                s0 = None
            else:
                h0r = states[0][:, j].reshape(bsz, 2, -1).transpose(1, 0, 2)
                h0i = states[1][:, j].reshape(bsz, 2, -1).transpose(1, 0, 2)
                s0 = states[2][:, j]
            ya, fr, fi = _s5_mixer(u_t, p, j, h0r, h0i, bsz, nseg)
            yb, sg = _gdn_mixer(qkv, z, ab, p, j, s0)
            mix = (ya.reshape(bsz, n, S5_WIDTH), yb)
            w_out = p['w_out_ab'][j]
            news['s5r'].append(fr.transpose(1, 0, 2).reshape(bsz, 2, S5_GROUPS, S5_STATE))
            news['s5i'].append(fi.transpose(1, 0, 2).reshape(bsz, 2, S5_GROUPS, S5_STATE))
            news['gdn'].append(sg)
        else:
            lam_init = _lambda_init(l)
            f = lambda name: p[name][j]
            lam = (jnp.exp(jnp.sum(f('d_lq1') * f('d_lk1'))) - jnp.exp(jnp.sum(f('d_lq2') * f('d_lk2')))
                   + lam_init)
            misc = jnp.zeros((8, 128), F32).at[0, :C_HEADS].set(p['c_sink'][j]).at[1, :].set(lam)
            rope = caches is not None
            qc, kc, vc, qd, kd, vd = _proj_cd(x, ng1, sh1, sc1, p['w_in_cd'][j].astype(BF16), p, j, rope)
            if caches is None:
                mix = _attention(qc, kc, vc, qd, kd, vd, None, misc, p['d_subln'][j][None],
                                 lam_init, n, False)
            else:
                n_ctx = caches[0].shape[2]
                cc = (caches[0][:, j].reshape(bsz, n_ctx, 128), caches[1][:, j].reshape(bsz, n_ctx, 128),
                      caches[2][:, j].reshape(bsz, n_ctx, 512), caches[3][:, j].reshape(bsz, n_ctx, 512))
                mix = _attention(qc, kc, vc, qd, kd, vd, cc, misc, p['d_subln'][j][None],
                                 lam_init, Q_BLOCK, True)
            w_out = p['w_out_cd'][j]
            news['ck'].append(kc.reshape(bsz, n, C_KV_HEADS, HEAD_DIM))
            news['cv'].append(vc.reshape(bsz, n, C_KV_HEADS, HEAD_DIM))
            news['dk'].append(kd.reshape(bsz, n, D_HEADS, 2, HEAD_DIM))
            news['dv'].append(vd.reshape(bsz, n, D_HEADS, D_VDIM))
        wup, cw, wdn = _ffn_weights(p, l)
        x = _post(x, mix[0], mix[1], g1, sh2, sc2, g2, p['norm2_g'][l][None], w_out.astype(BF16),
                  wup, cw, wdn, l)
    return x, news


def kernel(x_prompt, x_sample, c, state_s5_re, state_s5_im, state_gdn, cache_c_k, cache_c_v, cache_d_k, cache_d_v, c_ctx, w_mod, b_mod, norm1_g, norm2_g, w_in_ab, w_out_ab, s5_lam_re, s5_lam_im, s5_log_dt, s5_b_re, s5_b_im, s5_c_re, s5_c_im, s5_d, s5_w_glu, s5_b_glu, gdn_conv_w, gdn_a_log, gdn_dt_bias, gdn_norm_g, w_in_cd, w_out_cd, c_qn, c_kn, c_sink, d_qn, d_kn, d_lq1, d_lk1, d_lq2, d_lk2, d_subln, ffn_up, ffn_conv_w, ffn_conv_b, ffn_down):
    p = dict(w_mod=w_mod, b_mod=b_mod, norm1_g=norm1_g, norm2_g=norm2_g, w_in_ab=w_in_ab, w_out_ab=w_out_ab,
             s5_lam_re=s5_lam_re, s5_lam_im=s5_lam_im, s5_log_dt=s5_log_dt, s5_b_re=s5_b_re, s5_b_im=s5_b_im,
             s5_c_re=s5_c_re, s5_c_im=s5_c_im, s5_d=s5_d, s5_w_glu=s5_w_glu, s5_b_glu=s5_b_glu,
             gdn_conv_w=gdn_conv_w, gdn_a_log=gdn_a_log, gdn_dt_bias=gdn_dt_bias, gdn_norm_g=gdn_norm_g,
             w_in_cd=w_in_cd, w_out_cd=w_out_cd, c_qn=c_qn, c_kn=c_kn, c_sink=c_sink, d_qn=d_qn, d_kn=d_kn,
             d_lq1=d_lq1, d_lk1=d_lk1, d_lq2=d_lq2, d_lk2=d_lk2, d_subln=d_subln,
             ffn_up=ffn_up, ffn_conv_w=ffn_conv_w, ffn_conv_b=ffn_conv_b, ffn_down=ffn_down)
    depth = w_mod.shape[0]
    n_dec = c.shape[0]
    mod = _modulation(jnp.concatenate([c_ctx[None], c], axis=0), w_mod, b_mod)
    split6 = lambda m: [m[:, None, k * D_MODEL:(k + 1) * D_MODEL] for k in range(6)]
    mods_ctx = [split6(mod[l, 0:1]) for l in range(depth)]
    mods_dec = [split6(mod[l, 1:1 + n_dec]) for l in range(depth)]

    y_prompt, nw = _trunk(x_prompt, mods_ctx, p, None, None)
    y_sample, _ = _trunk(x_sample, mods_dec, p, (state_s5_re, state_s5_im, state_gdn),
                         (cache_c_k, cache_c_v, cache_d_k, cache_d_v))
    st = lambda name: jnp.stack(nw[name], axis=1)
    return (y_prompt, y_sample, st('s5r'), st('s5i'), st('gdn'), st('ck'), st('cv'), st('dk'), st('dv'))
```

```python
import functools
import math

import jax
import jax.numpy as jnp
from jax import lax
from jax.experimental import pallas as pl
from jax.experimental.pallas import tpu as pltpu

F32 = jnp.float32
BF16 = jnp.bfloat16

D_MODEL = 1024
GRID_W = 64
EPS = 1e-6
NEG_INF = -1e30

S5_WIDTH = 512
S5_GROUP = 16
S5_GROUPS = 32
S5_STATE = 64
S5_TILE_GROUPS = 8
S5_TILE_CH = S5_TILE_GROUPS * S5_GROUP
S5_TILE_ST = S5_TILE_GROUPS * S5_STATE
S5_TILES = S5_GROUPS // S5_TILE_GROUPS
S5_SEG = 256
S5_SUB = 32
PROJ_AB_TOKENS = 64
S5_ROWS = 8

GDN_DK = 128
GDN_DV = 128
GDN_HEADS = 4
GDN_WIDTH = 512
GDN_CHUNK = 64
GDN_GROUP = 4
GDN_SHORT_LEN = 256
GDN_SEQS_SHORT = 2
GDN_EXACT_ROUNDS = 3

HEAD_DIM = 64
C_HEADS = 8
C_KV_HEADS = 2
C_GROUP = 4
WINDOW = 128
Q_BLOCK = 128
D_HEADS = 4
D_VDIM = 128
ATTN_SCALE = HEAD_DIM ** -0.5
ATTN_SEQS_SHORT = 2
ROPE_THETA = 10000.0

D_FF = 2816
FF_CHUNK = 256
FF_CHUNKS = D_FF // FF_CHUNK
POST_ROWS = 512
POST_HALO = 8

AB_MAIN = S5_WIDTH + 4 * GDN_WIDTH
AB_PAD = AB_MAIN + 128
CD_IN = 2304

MOD_TK = 128
MOD_LANES = 512
VMEM_LIMIT = 56 * 1024 * 1024


def _cparams(sem):
    return pltpu.CompilerParams(dimension_semantics=sem, vmem_limit_bytes=VMEM_LIMIT)


def _sigmoid(x):
    return 1.0 / (1.0 + jnp.exp(-x))


def _silu(x):
    return x * _sigmoid(x)


def _softplus(x):
    return jnp.maximum(x, 0.0) + jnp.log(1.0 + jnp.exp(-jnp.abs(x)))


def _gelu_tanh(x):
    return 0.5 * x * (1.0 + jnp.tanh(math.sqrt(2.0 / math.pi) * (x + 0.044715 * (x * x * x))))


def _rms_mod(x, g, shift, scale):
    y = x * lax.rsqrt(jnp.mean(x * x, axis=-1, keepdims=True) + EPS)
    return (y * g) * (1.0 + scale) + shift


def _dot(a, b):
    return jnp.dot(a.astype(BF16), b.astype(BF16), preferred_element_type=F32)


def _dot_nt(a, b):
    return lax.dot_general(a.astype(BF16), b.astype(BF16), (((1,), (1,)), ((), ())),
                           preferred_element_type=F32)


def _hi_lo(a):
    hi = a.astype(BF16)
    return hi, (a - hi.astype(F32)).astype(BF16)


def _mod_kernel(ct_ref, w_ref, b_ref, o_ref, acc_ref, *, n_rows):
    k = pl.program_id(1)

    @pl.when(k == 0)
    def _():
        acc_ref[...] = jnp.zeros_like(acc_ref)

    tk, n_out = w_ref.shape[1:]
    s = _silu(ct_ref[...])
    sb = [[jnp.broadcast_to(s[g * 8:(g + 1) * 8, m:m + 1], (8, MOD_LANES)) for g in range(tk // 8)]
          for m in range(n_rows)]
    for c in range(n_out // MOD_LANES):
        cols = slice(c * MOD_LANES, (c + 1) * MOD_LANES)
        accs = [acc_ref[m, :, cols] for m in range(n_rows)]
        for g in range(tk // 8):
            wg = w_ref[0, g * 8:(g + 1) * 8, cols]
            accs = [a + wg * sb[m][g] for m, a in enumerate(accs)]
        for m in range(n_rows):
            acc_ref[m, :, cols] = accs[m]

    @pl.when(k == pl.num_programs(1) - 1)
    def _():
        o_ref[0] = jnp.zeros(o_ref.shape[1:], F32)
        for m in range(n_rows):
            o_ref[0, m:m + 1, :] = jnp.sum(acc_ref[m], axis=0, keepdims=True) + b_ref[0]


def _modulation(cvecs, w_mod, b_mod):
    n, d = cvecs.shape
    depth, _, n_out = w_mod.shape
    ct = jnp.zeros((d, 8), F32).at[:, :n].set(cvecs.T)
    return pl.pallas_call(
        functools.partial(_mod_kernel, n_rows=n),
        grid=(depth, d // MOD_TK),
        in_specs=[pl.BlockSpec((MOD_TK, 8), lambda l, k: (k, 0)),
                  pl.BlockSpec((1, MOD_TK, n_out), lambda l, k: (l, k, 0)),
                  pl.BlockSpec((1, 1, n_out), lambda l, k: (l, 0, 0))],
        out_specs=pl.BlockSpec((1, 8, n_out), lambda l, k: (l, 0, 0)),
        out_shape=jax.ShapeDtypeStruct((depth, 8, n_out), F32),
        scratch_shapes=[pltpu.VMEM((n, 8, n_out), F32)],
        compiler_params=_cparams(("parallel", "arbitrary")),
        name="adaln_mod",
    )(ct, w_mod, b_mod.reshape(depth, 1, n_out))


def _proj_ab_kernel(x_ref, g_ref, sh_ref, sc_ref, w_ref, wg_ref, u_ref, qkv_ref, z_ref, ab_ref):
    ns, tm, d = x_ref.shape
    h = _rms_mod(x_ref[...], g_ref[...], sh_ref[...], sc_ref[...])
    h = h.reshape(ns * tm, d).astype(BF16)
    y = jnp.dot(h, w_ref[...], preferred_element_type=F32)
    for s in range(ns):
        u_ref[:, s, :] = y[s * tm:(s + 1) * tm, :S5_WIDTH]
    qkv_ref[...] = y[:, S5_WIDTH:S5_WIDTH + 3 * GDN_WIDTH].reshape(ns, tm, 3 * GDN_WIDTH)
    z_ref[...] = y[:, S5_WIDTH + 3 * GDN_WIDTH:AB_MAIN].reshape(ns, tm, GDN_WIDTH)
    ab_ref[...] = jnp.dot(h, wg_ref[...], preferred_element_type=F32).reshape(ns, tm, 128)


def _proj_ab(x, g, shift, scale, w, w_gate):
    rows, n, d = x.shape
    tm = PROJ_AB_TOKENS
    per_row = shift.shape[0] > 1
    mspec = pl.BlockSpec((S5_ROWS, 1, d), lambda r, i: (r, 0, 0)) if per_row else \
        pl.BlockSpec((1, 1, d), lambda r, i: (0, 0, 0))
    blk = lambda w_: pl.BlockSpec((S5_ROWS, tm, w_), lambda r, i: (r, i, 0))
    return pl.pallas_call(
        _proj_ab_kernel,
        grid=(rows // S5_ROWS, n // tm),
        in_specs=[blk(d),
                  pl.BlockSpec((1, d), lambda r, i: (0, 0)),
                  mspec, mspec,
                  pl.BlockSpec((d, AB_MAIN), lambda r, i: (0, 0)),
                  pl.BlockSpec((d, 128), lambda r, i: (0, 0))],
        out_specs=[pl.BlockSpec((tm, S5_ROWS, S5_WIDTH), lambda r, i: (i, r, 0)),
                   blk(3 * GDN_WIDTH), blk(GDN_WIDTH), blk(128)],
        out_shape=[jax.ShapeDtypeStruct((n, rows, S5_WIDTH), F32),
                   jax.ShapeDtypeStruct((rows, n, 3 * GDN_WIDTH), F32),
                   jax.ShapeDtypeStruct((rows, n, GDN_WIDTH), F32),
                   jax.ShapeDtypeStruct((rows, n, 128), F32)],
        compiler_params=_cparams(("parallel", "parallel")),
        name="proj_ab",
    )(x, g, shift, scale, w, w_gate)


def _s5_kernel(u_ref, bm_ref, cm_ref, a_ref, h0r_ref, h0i_ref, *rest, want_y):
    if want_y:
        y_ref, fr_ref, fi_ref, xs_ref = rest
    else:
        fr_ref, fi_ref, xs_ref = rest
    n = u_ref.shape[0]
    sub = S5_SUB
    nsub = n // sub
    st = S5_TILE_ST

    def x_proj(d, k):
        u2 = u_ref[k * sub:(k + 1) * sub].reshape(sub * S5_ROWS, S5_TILE_CH)
        xs_ref[d, k * sub:(k + 1) * sub] = _dot(u2, bm_ref[d, 0]).reshape(sub, S5_ROWS, 2 * st)

    ar = [jnp.broadcast_to(a_ref[d, 0, 0:1, :], (S5_ROWS, st)) for d in range(2)]
    ai = [jnp.broadcast_to(a_ref[d, 0, 1:2, :], (S5_ROWS, st)) for d in range(2)]
    hr = [h0r_ref[0], h0r_ref[1]]
    hi = [h0i_ref[0], h0i_ref[1]]
    x_proj(0, 0)
    x_proj(1, nsub - 1)
    written = set()
    for k in range(nsub):
        ks = (k, nsub - 1 - k)
        if k + 1 < nsub:
            x_proj(0, ks[0] + 1)
            x_proj(1, ks[1] - 1)
        for t in range(sub):
            for d in range(2):
                tt = ks[d] * sub + (t if d == 0 else sub - 1 - t)
                x = xs_ref[d, tt]
                nr = ar[d] * hr[d] - ai[d] * hi[d] + x[:, :st]
                ni = ar[d] * hi[d] + ai[d] * hr[d] + x[:, st:]
                xs_ref[d, tt] = jnp.concatenate([nr, ni], axis=-1)
                hr[d], hi[d] = nr, ni
        if want_y:
            for d in range(2):
                rows = slice(ks[d] * sub, (ks[d] + 1) * sub)
                hs = xs_ref[d, rows].reshape(sub * S5_ROWS, 2 * st)
                yv = _dot(hs, cm_ref[d, 0]).reshape(sub, S5_ROWS, S5_TILE_CH)
                if ks[d] in written:
                    y_ref[rows] += yv
                else:
                    y_ref[rows] = yv
                    written.add(ks[d])
    for d in range(2):
        fr_ref[d] = hr[d]
        fi_ref[d] = hi[d]


def _s5_scan(u_t, bmat, cmat, amat, h0r, h0i, want_y=True):
    n, rows, _ = u_t.shape
    state = lambda: pl.BlockSpec((2, S5_ROWS, S5_TILE_ST), lambda r, j: (0, r, j))
    st_shape = jax.ShapeDtypeStruct((2, rows, S5_GROUPS * S5_STATE), F32)
    out_specs = [state(), state()]
    out_shape = [st_shape, st_shape]
    if want_y:
        out_specs.insert(0, pl.BlockSpec((n, S5_ROWS, S5_TILE_CH), lambda r, j: (0, r, j)))
        out_shape.insert(0, jax.ShapeDtypeStruct((n, rows, S5_WIDTH), F32))
    return pl.pallas_call(
        functools.partial(_s5_kernel, want_y=want_y),
        grid=(rows // S5_ROWS, S5_TILES),
        in_specs=[pl.BlockSpec((n, S5_ROWS, S5_TILE_CH), lambda r, j: (0, r, j)),
                  pl.BlockSpec((2, 1, S5_TILE_CH, 2 * S5_TILE_ST), lambda r, j: (0, j, 0, 0)),
                  pl.BlockSpec((2, 1, 2 * S5_TILE_ST, S5_TILE_CH), lambda r, j: (0, j, 0, 0)),
                  pl.BlockSpec((2, 1, 8, S5_TILE_ST), lambda r, j: (0, j, 0, 0)),
                  state(), state()],
        out_specs=out_specs,
        out_shape=out_shape,
        scratch_shapes=[pltpu.VMEM((2, n, S5_ROWS, 2 * S5_TILE_ST), F32)],
        compiler_params=_cparams(("parallel", "parallel")),
        name="s5_scan" if want_y else "s5_states",
    )(u_t, bmat, cmat, amat, h0r, h0i)


def _s5_params(p, j):
    lam_re, lam_im, log_dt = p['s5_lam_re'][j], p['s5_lam_im'][j], p['s5_log_dt'][j]
    dt = jnp.exp(log_dt)[..., None]
    mag = jnp.exp(lam_re * dt)
    ar, ai = mag * jnp.cos(lam_im * dt), mag * jnp.sin(lam_im * dt)
    den = lam_re * lam_re + lam_im * lam_im
    fr = ((ar - 1.0) * lam_re + ai * lam_im) / den
    fi = (ai * lam_re - (ar - 1.0) * lam_im) / den
    b_re, b_im = p['s5_b_re'][j], p['s5_b_im'][j]
    bbr = fr[..., None] * b_re - fi[..., None] * b_im
    bbi = fr[..., None] * b_im + fi[..., None] * b_re
    eye = jnp.eye(S5_TILE_GROUPS, dtype=F32)

    def in_blocks(t):
        t = t.reshape(2, S5_TILES, S5_TILE_GROUPS, S5_STATE, S5_GROUP)
        t = jnp.einsum('dtgpc,gh->dtgchp', t, eye)
        return t.reshape(2, S5_TILES, S5_TILE_CH, S5_TILE_ST)

    def out_blocks(t):
        t = t.reshape(2, S5_TILES, S5_TILE_GROUPS, S5_GROUP, S5_STATE)
        t = jnp.einsum('dtgcp,gh->dtgphc', t, eye)
        return t.reshape(2, S5_TILES, S5_TILE_ST, S5_TILE_CH)

    bmat = jnp.concatenate([in_blocks(bbr), in_blocks(bbi)], axis=-1).astype(BF16)
    cmat = jnp.concatenate([out_blocks(p['s5_c_re'][j]), -out_blocks(p['s5_c_im'][j])], axis=-2).astype(BF16)
    seg_mag = jnp.exp(lam_re * dt * S5_SEG)
    pr, pi = seg_mag * jnp.cos(lam_im * dt * S5_SEG), seg_mag * jnp.sin(lam_im * dt * S5_SEG)
    flat = lambda t: t.reshape(2, S5_TILES, 1, S5_TILE_ST)
    amat = jnp.concatenate([flat(ar), flat(ai), flat(pr), flat(pi),
                            jnp.zeros((2, S5_TILES, 4, S5_TILE_ST), F32)], axis=2)
    return bmat, cmat, amat


def _s5_glu_kernel(y_ref, u_ref, d_ref, w_ref, b_ref, o_ref):
    tm, ns, _ = y_ref.shape
    y = jnp.concatenate([y_ref[:, s, :] + d_ref[...] * u_ref[:, s, :] for s in range(ns)], axis=0)
    g = _gelu_tanh(y)
    out = g * _sigmoid(_dot(g, w_ref[...]) + b_ref[...])
    o_ref[...] = out.reshape(ns, tm, S5_WIDTH)


def _s5_glu(y_t, u_t, s5_d, w_glu, b_glu):
    n, rows, _ = y_t.shape
    tm = PROJ_AB_TOKENS
    tblk = pl.BlockSpec((tm, S5_ROWS, S5_WIDTH), lambda r, i: (i, r, 0))
    return pl.pallas_call(
        _s5_glu_kernel,
        grid=(rows // S5_ROWS, n // tm),
        in_specs=[tblk, tblk,
                  pl.BlockSpec((1, S5_WIDTH), lambda r, i: (0, 0)),
                  pl.BlockSpec((S5_WIDTH, S5_WIDTH), lambda r, i: (0, 0)),
                  pl.BlockSpec((1, S5_WIDTH), lambda r, i: (0, 0))],
        out_specs=pl.BlockSpec((S5_ROWS, tm, S5_WIDTH), lambda r, i: (r, i, 0)),
        out_shape=jax.ShapeDtypeStruct((rows, n, S5_WIDTH), F32),
        compiler_params=_cparams(("parallel", "parallel")),
        name="s5_glu",
    )(y_t, u_t, s5_d, w_glu, b_glu)


def _s5_mixer(u3, p, j, h0r, h0i, bsz, nseg):
    rows = bsz * nseg
    bmat, cmat, amat = _s5_params(p, j)
    if nseg == 1:
        y_t, fr, fi = _s5_scan(u3, bmat, cmat, amat, h0r, h0i)
    else:
        zero = jnp.zeros((2, bsz, nseg, S5_GROUPS * S5_STATE), F32)
        first = jnp.array([0, nseg - 1])
        seed = lambda h0: zero.at[jnp.arange(2), :, first].set(h0).reshape(2, rows, -1)
        fr, fi = _s5_scan(u3, bmat, cmat, amat, seed(h0r), seed(h0i), want_y=False)
        fr = fr.reshape(2, bsz, nseg, -1)
        fi = fi.reshape(2, bsz, nseg, -1)
        pr = amat[:, :, 2].reshape(2, 1, -1)
        pi = amat[:, :, 3].reshape(2, 1, -1)

        def chain(dr, order):
            hr, hi = (h0r[dr], h0i[dr])
            outs_r, outs_i = {}, {}
            for n_done, k in enumerate(order):
                outs_r[k], outs_i[k] = hr, hi
                if n_done == 0:
                    hr, hi = fr[dr, :, k], fi[dr, :, k]
                else:
                    hr, hi = (pr[dr] * hr - pi[dr] * hi + fr[dr, :, k],
                              pr[dr] * hi + pi[dr] * hr + fi[dr, :, k])
            st = lambda o: jnp.stack([o[k] for k in range(nseg)], axis=1)
            return st(outs_r), st(outs_i), hr, hi

        sr0, si0, er0, ei0 = chain(0, list(range(nseg)))
        sr1, si1, er1, ei1 = chain(1, list(range(nseg - 1, -1, -1)))
        start_r = jnp.stack([sr0, sr1]).reshape(2, rows, -1)
        start_i = jnp.stack([si0, si1]).reshape(2, rows, -1)
        y_t, _, _ = _s5_scan(u3, bmat, cmat, amat, start_r, start_i)
        fr = jnp.stack([er0, er1])
        fi = jnp.stack([ei0, ei1])
    ya = _s5_glu(y_t, u3, p['s5_d'][j][None], p['s5_w_glu'][j].astype(BF16), p['s5_b_glu'][j][None])
    return ya, fr, fi


def _gdn_kernel(qkv_ref, z_ref, ab_ref, cw_ref, gp_ref, ng_ref, *rest, zero_init):
    if zero_init:
        s0_ref = None
        o_ref, sf_ref, q_s, k_s, v_s, gate_s, of_s, ob_s, st_s, uw_s, a_s, kgt_s = rest
    else:
        s0_ref, o_ref, sf_ref, q_s, k_s, v_s, gate_s, of_s, ob_s, st_s, uw_s, a_s, kgt_s = rest
    ns, n = qkv_ref.shape[:2]
    nc = n // GDN_CHUNK
    row = lax.broadcasted_iota(jnp.int32, (n, 1), 0)

    for sq in range(ns):
        for blk in range(3 * GDN_HEADS):
            cols = slice(blk * GDN_DK, (blk + 1) * GDN_DK)
            hs = slice((blk % GDN_HEADS) * GDN_DK, (blk % GDN_HEADS + 1) * GDN_DK)
            x = qkv_ref[sq, :, cols]
            xm = jnp.where(row == 0, 0.0, pltpu.roll(x, 1, 0))
            xp = jnp.where(row == n - 1, 0.0, pltpu.roll(x, n - 1, 0))
            y = _silu(xm * cw_ref[0:1, cols] + x * cw_ref[1:2, cols] + xp * cw_ref[2:3, cols])
            if blk < GDN_HEADS:
                q_s[sq, :, hs] = y * lax.rsqrt(jnp.sum(y * y, axis=-1, keepdims=True) + EPS) * (GDN_DK ** -0.5)
            elif blk < 2 * GDN_HEADS:
                k_s[sq, :, hs] = y * lax.rsqrt(jnp.sum(y * y, axis=-1, keepdims=True) + EPS)
            else:
                v_s[sq, :, hs] = y

        ab = ab_ref[sq]
        beta = _sigmoid(ab)
        g = -jnp.exp(gp_ref[0:1, :]) * _softplus(ab + gp_ref[1:2, :])
        pos = row % GDN_CHUNK
        pre, suf = g, g
        sft = 1
        while sft < GDN_CHUNK:
            pre = pre + jnp.where(pos >= sft, pltpu.roll(pre, sft, 0), 0.0)
            suf = suf + jnp.where(pos < GDN_CHUNK - sft, pltpu.roll(suf, n - sft, 0), 0.0)
            sft *= 2
        gate_s[sq, 0] = beta
        gate_s[sq, 1] = pre
        gate_s[sq, 2] = suf

    st_s[...] = jnp.zeros_like(st_s) if zero_init else s0_ref[...]
    cs = GDN_CHUNK
    pk = GDN_HEADS * cs
    ri = lax.broadcasted_iota(jnp.int32, (cs, pk), 0)
    lane_pk = lax.broadcasted_iota(jnp.int32, (cs, pk), 1)
    ci = lane_pk % cs
    eye_pk = (ri == ci).astype(F32)
    head_pk = [(lax.broadcasted_iota(jnp.int32, (1, pk), 1) // cs) == h for h in range(GDN_HEADS)]
    head_w = [(lax.broadcasted_iota(jnp.int32, (1, GDN_WIDTH), 1) // GDN_DK) == h for h in range(GDN_HEADS)]

    def block_diag(p):
        return jnp.concatenate([jnp.where(m, p, jnp.zeros_like(p)) for m in head_pk], axis=0)

    def dot3_bd(a, p):
        a_hi, a_lo = _hi_lo(a)
        p_hi, p_lo = _hi_lo(p)
        b_hi, b_lo = block_diag(p_hi), block_diag(p_lo)
        mm = functools.partial(jnp.dot, preferred_element_type=F32)
        return mm(a_hi, b_hi) + (mm(a_lo, b_hi) + mm(a_hi, b_lo))

    def dot1_bd(a, p):
        return jnp.dot(a.astype(BF16), block_diag(p.astype(BF16)), preferred_element_type=F32)

    def lanes(cols, width):
        return jnp.concatenate([jnp.broadcast_to(c, (cs, width)) for c in cols], axis=1)

    def phase_a(sq, it):
        st = []
        for cc in range(GDN_GROUP):
            rows = pl.ds(pl.multiple_of((it * GDN_GROUP + cc) * cs, cs), cs)
            q_all, k_all, v_all = q_s[sq, rows, :], k_s[sq, rows, :], v_s[sq, rows, :]
            beta_blk = gate_s[sq, 0, rows, :]
            for dr in range(2):
                gc_blk = gate_s[sq, 1 + dr, rows, :]
                lane0 = dr * GDN_HEADS
                bcols = [beta_blk[:, 8 + lane0 + h:9 + lane0 + h] for h in range(GDN_HEADS)]
                gcols = [gc_blk[:, lane0 + h:lane0 + h + 1] for h in range(GDN_HEADS)]
                st.append(dict(sq=sq, dr=dr, rows=rows, cidx=it * GDN_GROUP + cc, q=q_all, k=k_all, v=v_all, bcols=bcols, gcols=gcols,
                               incl=(ri >= ci) if dr == 0 else (ri <= ci),
                               strict=(ri > ci) if dr == 0 else (ri < ci)))
        for t in st:
            b_w = lanes(t['bcols'], GDN_DK)
            t['kb'] = t['k'] * b_w
            t['vb'] = t['v'] * b_w
            k_bd = jnp.concatenate([jnp.where(m, t['k'], 0.0) for m in head_w], axis=0)
            t['kq'] = _dot_nt(jnp.concatenate([t['kb'], t['q']], axis=0), k_bd)
        for t in st:
            gcol = lanes(t['gcols'], cs)
            grow = jnp.sum(eye_pk * gcol, axis=0, keepdims=True)
            decay = jnp.exp(jnp.where(t['incl'], gcol - grow, NEG_INF))
            t['pw'] = jnp.where(t['strict'], t['kq'][:cs] * decay, 0.0)
            a_s[t['sq'], t['dr'], t['rows'], :] =jnp.where(t['incl'], t['kq'][cs:] * decay, 0.0)
            t['tm'] = eye_pk - t['pw']
        for t in st:
            t['pw'] = dot3_bd(t['pw'], t['pw'])
        for rnd in range(5):
            mm_bd = dot3_bd if rnd < GDN_EXACT_ROUNDS else dot1_bd
            for t in st:
                if rnd < 4:
                    r = mm_bd(jnp.concatenate([t['pw'], t['tm']], axis=0), t['pw'])
                    t['pw'] = r[:cs]
                    t['tm'] = t['tm'] + r[cs:]
                else:
                    t['tm'] = t['tm'] + mm_bd(t['tm'], t['pw'])
        for t in st:
            for h in range(GDN_HEADS):
                g = t['gcols'][h]
                gl = g[cs - 1:cs] if t['dr'] == 0 else g[0:1]
                kg = t['k'][:, h * GDN_DK:(h + 1) * GDN_DK] * jnp.exp(gl - g)
                kgt_s[t['sq'], t['dr'], t['cidx'], h] = kg.T.astype(BF16)
        for t in st:
            kbg = t['kb'] * lanes([jnp.exp(g) for g in t['gcols']], GDN_DK)
            for h in range(GDN_HEADS):
                hs = slice(h * GDN_DK, (h + 1) * GDN_DK)
                rhs = jnp.concatenate([t['vb'][:, hs], kbg[:, hs]], axis=1)
                uw_s[t['sq'], t['dr'], t['rows'], 2 * h * GDN_DK:2 * (h + 1) * GDN_DK] = _dot(
                    t['tm'][:, h * cs:(h + 1) * cs], rhs)

    for sq in range(ns):
        if nc // GDN_GROUP == 1:
            phase_a(sq, 0)
        else:
            lax.fori_loop(0, nc // GDN_GROUP, lambda it, c, sq=sq: (phase_a(sq, it), c)[1], 0)

    def phase_b(c, carry):
        ch = []
        for sq, dr in [(sq, dr) for sq in range(ns) for dr in range(2)]:
            cidx = c if dr == 0 else nc - 1 - c
            rows = pl.ds(pl.multiple_of(cidx * cs, cs), cs)
            gc_blk = gate_s[sq, 1 + dr, rows, :]
            for h in range(GDN_HEADS):
                hs = slice(h * GDN_DK, (h + 1) * GDN_DK)
                lane = dr * GDN_HEADS + h
                gcol = gc_blk[:, lane:lane + 1]
                ch.append(dict(sq=sq, dr=dr, h=h, rows=rows, hs=hs, gcol=gcol, cidx=cidx,
                               gl=gcol[cs - 1:cs] if dr == 0 else gcol[0:1],
                               uw=uw_s[sq, dr, rows, 2 * h * GDN_DK:2 * (h + 1) * GDN_DK],
                               amat=a_s[sq, dr, rows, h * cs:(h + 1) * cs]))
        for t in ch:
            t['s'] = st_s[t['sq'], t['dr'], t['h']]
            qg = q_s[t['sq'], t['rows'], t['hs']] * jnp.exp(t['gcol'])
            t['ws'] = _dot(jnp.concatenate([t['uw'][:, GDN_DV:], qg], axis=0), t['s'])
        for t in ch:
            vn = t['uw'][:, :GDN_DV] - t['ws'][:cs]
            o = t['ws'][cs:] + _dot(t['amat'], vn)
            kgt = kgt_s[t['sq'], t['dr'], t['cidx'], t['h']]
            st_s[t['sq'], t['dr'], t['h']] = t['s'] * jnp.exp(t['gl']) + _dot(kgt, vn)
            if t['dr'] == 0:
                of_s[t['sq'], t['rows'], t['hs']] = o
            else:
                ob_s[t['sq'], t['rows'], t['hs']] = o
        return carry

    lax.fori_loop(0, nc, phase_b, 0)
    sf_ref[...] = st_s[...]

    for sq in range(ns):
        z = z_ref[sq]
        for h in range(GDN_HEADS):
            hs = slice(h * GDN_DV, (h + 1) * GDN_DV)
            o = of_s[sq, :, hs] + ob_s[sq, :, hs]
            o = o * lax.rsqrt(jnp.mean(o * o, axis=-1, keepdims=True) + EPS) * ng_ref[...]
            o_ref[sq, :, hs] = o * _silu(z[:, hs])


def _gdn_mixer(qkv, z, ab, p, j, s0):
    bsz, n, _ = qkv.shape
    gp = jnp.zeros((8, 128), F32)
    gp = gp.at[0, :8].set(p['gdn_a_log'][j].reshape(8)).at[1, :8].set(p['gdn_dt_bias'][j].reshape(8))
    cw = jnp.zeros((8, 3 * GDN_WIDTH), F32).at[:3].set(p['gdn_conv_w'][j])
    ns = GDN_SEQS_SHORT if (n <= GDN_SHORT_LEN and bsz % GDN_SEQS_SHORT == 0) else 1
    blk = lambda w_: pl.BlockSpec((ns, n, w_), lambda b: (b, 0, 0))
    sblk = pl.BlockSpec((ns, 2, GDN_HEADS, GDN_DK, GDN_DV), lambda b: (b, 0, 0, 0, 0))
    tok = lambda w_: pltpu.VMEM((ns, n, w_), F32)
    return pl.pallas_call(
        functools.partial(_gdn_kernel, zero_init=s0 is None),
        grid=(bsz // ns,),
        in_specs=[blk(3 * GDN_WIDTH), blk(GDN_WIDTH), blk(128),
                  pl.BlockSpec((8, 3 * GDN_WIDTH), lambda b: (0, 0)),
                  pl.BlockSpec((8, 128), lambda b: (0, 0)),
                  pl.BlockSpec((1, GDN_DV), lambda b: (0, 0))] + ([] if s0 is None else [sblk]),
        out_specs=[blk(GDN_WIDTH), sblk],
        out_shape=[jax.ShapeDtypeStruct((bsz, n, GDN_WIDTH), F32),
                   jax.ShapeDtypeStruct((bsz, 2, GDN_HEADS, GDN_DK, GDN_DV), F32)],
        scratch_shapes=[tok(GDN_WIDTH), tok(GDN_WIDTH), tok(GDN_WIDTH),
                        pltpu.VMEM((ns, 3, n, 128), F32),
                        tok(GDN_WIDTH), tok(GDN_WIDTH),
                        pltpu.VMEM((ns, 2, GDN_HEADS, GDN_DK, GDN_DV), F32),
                        pltpu.VMEM((ns, 2, n, 2 * GDN_WIDTH), F32),
                        pltpu.VMEM((ns, 2, n, GDN_HEADS * GDN_CHUNK), F32),
                        pltpu.VMEM((ns, 2, n // GDN_CHUNK, GDN_HEADS, GDN_DK, GDN_CHUNK), BF16)],
        compiler_params=_cparams(("parallel",)),
        name="gdn_mixer",
    )(qkv, z, ab, cw, gp, p['gdn_norm_g'][j][None], *(() if s0 is None else (s0,)))


def _proj_cd_kernel(x_ref, g_ref, sh_ref, sc_ref, w_ref, gm_ref, gain_ref, cos_ref, sin_ref,
                    qc_ref, kc_ref, vc_ref, qd_ref, kd_ref, vd_ref, *, rope):
    h = _rms_mod(x_ref[0], g_ref[...], sh_ref[0], sc_ref[0])
    y = _dot(h, w_ref[...])
    lane = lax.broadcasted_iota(jnp.int32, (1, 512), 1)
    low = (lane % 32) < 16

    def head_norm(t, gain, scale):
        w = t.shape[1]
        ms = _dot(t * t, gm_ref[:w, :w])
        t = t * lax.rsqrt(ms + EPS) * gain
        if rope:
            part = jnp.where(low[:, :w], pltpu.roll(t, w - 16, 1), pltpu.roll(t, 16, 1))
            t = t * cos_ref[:, :w] + part * sin_ref[:, :w]
        return t * scale if scale != 1.0 else t

    qc_ref[0] = head_norm(y[:, 0:512], gain_ref[0:1, :], ATTN_SCALE)
    kc_ref[0] = head_norm(y[:, 512:640], gain_ref[1:2, :128], 1.0)
    vc_ref[0] = y[:, 640:768]
    qd_ref[0] = head_norm(y[:, 768:1280], gain_ref[2:3, :], ATTN_SCALE)
    kd_ref[0] = head_norm(y[:, 1280:1792], gain_ref[3:4, :], 1.0)
    vd_ref[0] = y[:, 1792:2304]


def _rope_tables(n):
    rows = n // GRID_W
    row = jnp.repeat(jnp.arange(rows), GRID_W).astype(F32)
    col = jnp.tile(jnp.arange(GRID_W), rows).astype(F32)
    quarter = HEAD_DIM // 4
    inv = ROPE_THETA ** (-jnp.arange(quarter, dtype=F32) / quarter)
    ang_r = row[:, None] * inv[None, :]
    ang_c = col[:, None] * inv[None, :]
    cos = jnp.concatenate([jnp.cos(ang_r), jnp.cos(ang_r), jnp.cos(ang_c), jnp.cos(ang_c)], axis=-1)
    sin = jnp.concatenate([-jnp.sin(ang_r), jnp.sin(ang_r), -jnp.sin(ang_c), jnp.sin(ang_c)], axis=-1)
    return jnp.tile(cos, (1, 8)), jnp.tile(sin, (1, 8))


def _proj_cd(x, g, shift, scale, w, p, j, rope):
    bsz, n, d = x.shape
    tm = 256
    per_seq = shift.shape[0] > 1
    midx = (lambda b, i: (b, 0, 0)) if per_seq else (lambda b, i: (0, 0, 0))
    lane = jnp.arange(512)
    gmat = ((lane[:, None] // HEAD_DIM) == (lane[None, :] // HEAD_DIM)).astype(F32) / HEAD_DIM
    gains = jnp.zeros((8, 512), F32)
    gains = gains.at[0].set(jnp.tile(p['c_qn'][j], 8)).at[1].set(jnp.tile(p['c_kn'][j], 8))
    gains = gains.at[2].set(jnp.tile(p['d_qn'][j], 8)).at[3].set(jnp.tile(p['d_kn'][j], 8))
    if rope:
        cos, sin = _rope_tables(n)
    else:
        cos, sin = jnp.ones((tm, 512), F32), jnp.zeros((tm, 512), F32)
    tidx = (lambda b, i: (i, 0)) if rope else (lambda b, i: (0, 0))
    blk = lambda w_: pl.BlockSpec((1, tm, w_), lambda b, i: (b, i, 0))
    return pl.pallas_call(
        functools.partial(_proj_cd_kernel, rope=rope),
        grid=(bsz, n // tm),
        in_specs=[blk(d),
                  pl.BlockSpec((1, d), lambda b, i: (0, 0)),
                  pl.BlockSpec((1, 1, d), midx),
                  pl.BlockSpec((1, 1, d), midx),
                  pl.BlockSpec((d, CD_IN), lambda b, i: (0, 0)),
                  pl.BlockSpec((512, 512), lambda b, i: (0, 0)),
                  pl.BlockSpec((8, 512), lambda b, i: (0, 0)),
                  pl.BlockSpec((tm, 512), tidx),
                  pl.BlockSpec((tm, 512), tidx)],
        out_specs=[blk(512), blk(128), blk(128), blk(512), blk(512), blk(512)],
        out_shape=[jax.ShapeDtypeStruct((bsz, n, w_), F32) for w_ in (512, 128, 128, 512, 512, 512)],
        compiler_params=_cparams(("parallel", "parallel")),
        name="proj_cd",
    )(x, g, shift, scale, w, gmat.astype(BF16), gains, cos, sin)


def _probs(scores, extra=None):
    m = scores[0].max(axis=-1, keepdims=True)
    for s in scores[1:]:
        m = jnp.maximum(m, s.max(axis=-1, keepdims=True))
    if extra is not None:
        m = jnp.maximum(m, extra)
    return [jnp.exp(s - m).astype(BF16) for s in scores], m


def _attn_kernel(qc_ref, kc_ref, vc_ref, qd_ref, kd_ref, vd_ref, *rest, windowed, n_ctx, lam_scale):
    if n_ctx:
        ck_ref, cv_ref, dk_ref, dv_ref, misc_ref, sub_ref, oc_ref, od_ref = rest
    else:
        misc_ref, sub_ref, oc_ref, od_ref = rest
    tq = qc_ref.shape[1]
    n = kc_ref.shape[1]
    start = pl.program_id(1) * tq
    if windowed:
        span = tq + 2 * WINDOW
        k0 = pl.multiple_of(jnp.clip(start - WINDOW, 0, n - span), 128)
        krows = pl.ds(k0, span)
        qpos = start + lax.broadcasted_iota(jnp.int32, (tq, span), 0)
        kpos = k0 + lax.broadcasted_iota(jnp.int32, (tq, span), 1)
        ok = jnp.abs(qpos - kpos) <= WINDOW
    else:
        krows = pl.ds(0, n)
    low = lax.broadcasted_iota(jnp.int32, (1, 2 * HEAD_DIM), 1) < HEAD_DIM

    def key_ops(k):
        kr = pltpu.roll(k, HEAD_DIM, 1)
        z = jnp.zeros_like(k)
        return {(0, 0): jnp.where(low, k, z).astype(BF16), (0, 1): jnp.where(low, z, kr).astype(BF16),
                (1, 0): jnp.where(low, kr, z).astype(BF16), (1, 1): jnp.where(low, z, k).astype(BF16)}

    def with_ones(v):
        return jnp.concatenate([v, jnp.ones_like(v)], axis=1)

    def val_ops(v):
        return {key: with_ones(op) for key, op in key_ops(v).items()}

    def half_ops(k):
        z = jnp.zeros_like(k)
        return [jnp.where(low, k, z).astype(BF16), jnp.where(low, z, k).astype(BF16)]

    seqs = range(qc_ref.shape[0])

    c_scores, d_scores = [[] for _ in seqs], [[] for _ in seqs]
    for sq in seqs:
        qc = qc_ref[sq].astype(BF16)
        qd = qd_ref[sq].astype(BF16)
        kc_ops = key_ops(kc_ref[sq, krows, :])
        ck_ops = key_ops(ck_ref[sq]) if n_ctx else None
        for h in range(C_HEADS):
            key = (h // C_GROUP, h % 2)
            q = qc[:, (h // 2) * 128:(h // 2 + 1) * 128]
            sc = [_dot_nt(q, kc_ops[key])]
            if n_ctx:
                sc.append(_dot_nt(q, ck_ops[key]))
            c_scores[sq].append(sc)
        for h in range(D_HEADS):
            q = qd[:, h * 128:(h + 1) * 128]
            kd_ops = half_ops(kd_ref[sq, :, h * 128:(h + 1) * 128])
            dk_ops = half_ops(dk_ref[sq, :, h * 128:(h + 1) * 128]) if n_ctx else None
            for c in range(2):
                sc = [_dot_nt(q, kd_ops[c])]
                if n_ctx:
                    sc.append(_dot_nt(q, dk_ops[c]))
                d_scores[sq].append(sc)

    c_probs, d_probs = [[] for _ in seqs], [[] for _ in seqs]
    for sq in seqs:
        for h in range(C_HEADS):
            sc = c_scores[sq][h]
            if windowed:
                sc[0] = jnp.where(ok, sc[0], NEG_INF)
            c_probs[sq].append(_probs(sc, misc_ref[0:1, h:h + 1]))
        d_probs[sq] = [_probs(sc)[0] for sc in d_scores[sq]]

    lam = misc_ref[1:2, 0:1]
    for sq in seqs:
        vc_ops = val_ops(vc_ref[sq, krows, :])
        cv_ops = val_ops(cv_ref[sq]) if n_ctx else None
        for j in range(C_HEADS // 2):
            pair = None
            for h in (2 * j, 2 * j + 1):
                key = (h // C_GROUP, h % 2)
                ps, m = c_probs[sq][h]
                r = jnp.dot(ps[0], vc_ops[key], preferred_element_type=F32)
                if n_ctx:
                    r = r + jnp.dot(ps[1], cv_ops[key], preferred_element_type=F32)
                o = r[:, :128] / (r[:, 128:] + jnp.exp(misc_ref[0:1, h:h + 1] - m))
                pair = o if pair is None else pair + o
            oc_ref[sq, :, j * 128:(j + 1) * 128] = pair
        for h in range(D_HEADS):
            vsl = slice(h * D_VDIM, (h + 1) * D_VDIM)
            v_op = with_ones(vd_ref[sq, :, vsl]).astype(BF16)
            dv_op = with_ones(dv_ref[sq, :, vsl]).astype(BF16) if n_ctx else None
            parts = []
            for c in range(2):
                ps = d_probs[sq][2 * h + c]
                r = jnp.dot(ps[0], v_op, preferred_element_type=F32)
                if n_ctx:
                    r = r + jnp.dot(ps[1], dv_op, preferred_element_type=F32)
                parts.append(r[:, :D_VDIM] / r[:, D_VDIM:])
            o = parts[0] - lam * parts[1]
            o = o * lax.rsqrt(jnp.mean(o * o, axis=-1, keepdims=True) + EPS) * sub_ref[...] * lam_scale
            od_ref[sq, :, vsl] = o


def _attention(qc, kc, vc, qd, kd, vd, caches, misc, subln, lam_init, tq, windowed):
    bsz, n, _ = qc.shape
    ns = ATTN_SEQS_SHORT if (not windowed and tq == n and bsz % ATTN_SEQS_SHORT == 0) else 1
    qblk = lambda w_: pl.BlockSpec((ns, tq, w_), lambda b, i: (b, i, 0))
    kblk = lambda rows, w_: pl.BlockSpec((ns, rows, w_), lambda b, i: (b, 0, 0))
    in_specs = [qblk(512), kblk(n, 128), kblk(n, 128), qblk(512), kblk(n, 512), kblk(n, 512)]
    args = [qc, kc, vc, qd, kd, vd]
    n_ctx = 0
    if caches is not None:
        n_ctx = caches[0].shape[1]
        in_specs += [kblk(n_ctx, 128), kblk(n_ctx, 128), kblk(n_ctx, 512), kblk(n_ctx, 512)]
        args += list(caches)
    in_specs += [pl.BlockSpec((8, 128), lambda b, i: (0, 0)), pl.BlockSpec((1, D_VDIM), lambda b, i: (0, 0))]
    args += [misc, subln]
    return pl.pallas_call(
        functools.partial(_attn_kernel, windowed=windowed, n_ctx=n_ctx, lam_scale=1.0 - lam_init),
        grid=(bsz // ns, n // tq),
        in_specs=in_specs,
        out_specs=[qblk(512), qblk(512)],
        out_shape=[jax.ShapeDtypeStruct((bsz, n, 512), F32), jax.ShapeDtypeStruct((bsz, n, 512), F32)],
        compiler_params=_cparams(("parallel", "parallel")),
        name="attn_win" if windowed else "attn_ctx",
    )(*args)


def _post_kernel(x_ref, xp_ref, xn_ref, ma_ref, map_ref, man_ref, mb_ref, mbp_ref, mbn_ref,
                 g1_ref, sh_ref, sc_ref, g2_ref, ng_ref, wo_ref, wup_ref, cw_ref, wdn_ref, o_ref, act_ref,
                 *, seq_len):
    rows = x_ref.shape[0]
    ext = rows + 2 * POST_HALO
    half = ma_ref.shape[1]
    xe = jnp.concatenate([xp_ref[...], x_ref[...], xn_ref[...]], axis=0)
    mae = jnp.concatenate([map_ref[...], ma_ref[...], man_ref[...]], axis=0)
    mbe = jnp.concatenate([mbp_ref[...], mb_ref[...], mbn_ref[...]], axis=0)
    x1 = xe + g1_ref[0] * (_dot(mae, wo_ref[:half, :]) + _dot(mbe, wo_ref[half:, :]))
    h = _rms_mod(x1, ng_ref[...], sh_ref[0], sc_ref[0]).astype(BF16)
    x1 = x1[POST_HALO:POST_HALO + rows]
    row0 = pl.program_id(0) * rows - POST_HALO
    pos = (row0 + lax.broadcasted_iota(jnp.int32, (ext, 1), 0)) % seq_len
    first = pos == 0
    last = pos == seq_len - 1
    for c in range(FF_CHUNKS):
        cs = slice(c * FF_CHUNK, (c + 1) * FF_CHUNK)
        a = jnp.dot(h, wup_ref[:, cs], preferred_element_type=F32)
        b = jnp.dot(h, wup_ref[:, D_FF + c * FF_CHUNK:D_FF + (c + 1) * FF_CHUNK], preferred_element_type=F32)
        am = jnp.where(first, 0.0, pltpu.roll(a, 1, 0))
        ap = jnp.where(last, 0.0, pltpu.roll(a, ext - 1, 0))
        a = am * cw_ref[0:1, cs] + a * cw_ref[1:2, cs] + ap * cw_ref[2:3, cs] + cw_ref[3:4, cs]
        act_ref[:, cs] = (_silu(a) * b)[POST_HALO:POST_HALO + rows].astype(BF16)
    ffn = jnp.dot(act_ref[...], wdn_ref[...], preferred_element_type=F32)
    o_ref[...] = x1 + g2_ref[0] * ffn


def _post(x, mix_a, mix_b, g1, sh2, sc2, g2, norm_g, w_out, wup, cw, wdn, layer):
    bsz, seq_len, d = x.shape
    half = mix_a.shape[-1]
    rows = POST_ROWS
    total = bsz * seq_len
    nhalo = total // POST_HALO
    per_seq = g1.shape[0] > 1
    midx = (lambda i: ((i * rows) // seq_len, 0, 0)) if per_seq else (lambda i: (0, 0, 0))
    mspec = pl.BlockSpec((1, 1, d), midx)
    pidx = lambda i: (jnp.maximum(i * (rows // POST_HALO) - 1, 0), 0)
    nidx = lambda i: (jnp.minimum((i + 1) * (rows // POST_HALO), nhalo - 1), 0)
    trio = lambda w_: [pl.BlockSpec((rows, w_), lambda i: (i, 0)), pl.BlockSpec((POST_HALO, w_), pidx),
                       pl.BlockSpec((POST_HALO, w_), nidx)]
    whole = lambda shape: pl.BlockSpec(shape, lambda i: (0,) * len(shape), pipeline_mode=pl.Buffered(1))
    x2 = x.reshape(total, d)
    a2 = mix_a.reshape(total, half)
    b2 = mix_b.reshape(total, half)
    out = pl.pallas_call(
        functools.partial(_post_kernel, seq_len=seq_len),
        grid=(total // rows,),
        in_specs=trio(d) + trio(half) + trio(half) + [
            mspec, mspec, mspec, mspec,
            pl.BlockSpec((1, d), lambda i: (0, 0)),
            whole((d, d)),
            pl.BlockSpec((None, d, 2 * D_FF), lambda i: (layer, 0, 0), pipeline_mode=pl.Buffered(1)),
            whole((8, D_FF)),
            pl.BlockSpec((None, D_FF, d), lambda i: (layer, 0, 0), pipeline_mode=pl.Buffered(1))],
        out_specs=pl.BlockSpec((rows, d), lambda i: (i, 0)),
        out_shape=jax.ShapeDtypeStruct((total, d), F32),
        scratch_shapes=[pltpu.VMEM((rows, D_FF), BF16)],
        compiler_params=_cparams(("parallel",)),
        name="post_ffn",
    )(x2, x2, x2, a2, a2, a2, b2, b2, b2, g1, sh2, sc2, g2, norm_g, w_out, wup, cw, wdn)
    return out.reshape(bsz, seq_len, d)


def _ffn_weights(p, l):
    cw = jnp.zeros((8, D_FF), F32).at[:3].set(p['ffn_conv_w'][l]).at[3].set(p['ffn_conv_b'][l])
    return p['ffn_up'].astype(BF16), cw, p['ffn_down'].astype(BF16)


def _lambda_init(layer):
    return 0.8 - 0.6 * math.exp(-0.3 * layer)


def _trunk(x, mods, p, states, caches):
    bsz, n, d = x.shape
    nseg = n // S5_SEG
    depth = p['w_mod'].shape[0]
    news = {k: [] for k in ('s5r', 's5i', 'gdn', 'ck', 'cv', 'dk', 'dv')}
    for l in range(depth):
        j = l // 2
        sh1, sc1, g1, sh2, sc2, g2 = mods[l]
        ng1 = p['norm1_g'][l][None]
        if l % 2 == 0:
            w_in = p['w_in_ab'][j]
            w_gate = jnp.zeros((d, 128), BF16).at[:, :w_in.shape[1] - AB_MAIN].set(
                w_in[:, AB_MAIN:].astype(BF16))
            rows = bsz * nseg
            per_row = lambda m: jnp.repeat(m, nseg, axis=0) if m.shape[0] > 1 else m
            u_t, qkv, z, ab = _proj_ab(x.reshape(rows, S5_SEG, d), ng1, per_row(sh1), per_row(sc1),
                                       w_in[:, :AB_MAIN].astype(BF16), w_gate)
            qkv, z, ab = (t.reshape(bsz, n, t.shape[-1]) for t in (qkv, z, ab))
            if states is None:
                h0r = jnp.zeros((2, bsz, S5_GROUPS * S5_STATE), F32)
                h0i = h0r
                s0 = None
            else:
                h0r = states[0][:, j].reshape(bsz, 2, -1).transpose(1, 0, 2)
                h0i = states[1][:, j].reshape(bsz, 2, -1).transpose(1, 0, 2)
                s0 = states[2][:, j]
            ya, fr, fi = _s5_mixer(u_t, p, j, h0r, h0i, bsz, nseg)
            yb, sg = _gdn_mixer(qkv, z, ab, p, j, s0)
            mix = (ya.reshape(bsz, n, S5_WIDTH), yb)
            w_out = p['w_out_ab'][j]
            news['s5r'].append(fr.transpose(1, 0, 2).reshape(bsz, 2, S5_GROUPS, S5_STATE))
            news['s5i'].append(fi.transpose(1, 0, 2).reshape(bsz, 2, S5_GROUPS, S5_STATE))
            news['gdn'].append(sg)
        else:
            lam_init = _lambda_init(l)
            f = lambda name: p[name][j]
            lam = (jnp.exp(jnp.sum(f('d_lq1') * f('d_lk1'))) - jnp.exp(jnp.sum(f('d_lq2') * f('d_lk2')))
                   + lam_init)
            misc = jnp.zeros((8, 128), F32).at[0, :C_HEADS].set(p['c_sink'][j]).at[1, :].set(lam)
            rope = caches is not None
            qc, kc, vc, qd, kd, vd = _proj_cd(x, ng1, sh1, sc1, p['w_in_cd'][j].astype(BF16), p, j, rope)
            if caches is None:
                mix = _attention(qc, kc, vc, qd, kd, vd, None, misc, p['d_subln'][j][None],
                                 lam_init, n, False)
            else:
                n_ctx = caches[0].shape[2]
                cc = (caches[0][:, j].reshape(bsz, n_ctx, 128), caches[1][:, j].reshape(bsz, n_ctx, 128),
                      caches[2][:, j].reshape(bsz, n_ctx, 512), caches[3][:, j].reshape(bsz, n_ctx, 512))
                mix = _attention(qc, kc, vc, qd, kd, vd, cc, misc, p['d_subln'][j][None],
                                 lam_init, Q_BLOCK, True)
            w_out = p['w_out_cd'][j]
            news['ck'].append(kc.reshape(bsz, n, C_KV_HEADS, HEAD_DIM))
            news['cv'].append(vc.reshape(bsz, n, C_KV_HEADS, HEAD_DIM))
            news['dk'].append(kd.reshape(bsz, n, D_HEADS, 2, HEAD_DIM))
            news['dv'].append(vd.reshape(bsz, n, D_HEADS, D_VDIM))
        wup, cw, wdn = _ffn_weights(p, l)
        x = _post(x, mix[0], mix[1], g1, sh2, sc2, g2, p['norm2_g'][l][None], w_out.astype(BF16),
                  wup, cw, wdn, l)
    return x, news


def kernel(x_prompt, x_sample, c, state_s5_re, state_s5_im, state_gdn, cache_c_k, cache_c_v, cache_d_k, cache_d_v, c_ctx, w_mod, b_mod, norm1_g, norm2_g, w_in_ab, w_out_ab, s5_lam_re, s5_lam_im, s5_log_dt, s5_b_re, s5_b_im, s5_c_re, s5_c_im, s5_d, s5_w_glu, s5_b_glu, gdn_conv_w, gdn_a_log, gdn_dt_bias, gdn_norm_g, w_in_cd, w_out_cd, c_qn, c_kn, c_sink, d_qn, d_kn, d_lq1, d_lk1, d_lq2, d_lk2, d_subln, ffn_up, ffn_conv_w, ffn_conv_b, ffn_down):
    p = dict(w_mod=w_mod, b_mod=b_mod, norm1_g=norm1_g, norm2_g=norm2_g, w_in_ab=w_in_ab, w_out_ab=w_out_ab,
             s5_lam_re=s5_lam_re, s5_lam_im=s5_lam_im, s5_log_dt=s5_log_dt, s5_b_re=s5_b_re, s5_b_im=s5_b_im,
             s5_c_re=s5_c_re, s5_c_im=s5_c_im, s5_d=s5_d, s5_w_glu=s5_w_glu, s5_b_glu=s5_b_glu,
             gdn_conv_w=gdn_conv_w, gdn_a_log=gdn_a_log, gdn_dt_bias=gdn_dt_bias, gdn_norm_g=gdn_norm_g,
             w_in_cd=w_in_cd, w_out_cd=w_out_cd, c_qn=c_qn, c_kn=c_kn, c_sink=c_sink, d_qn=d_qn, d_kn=d_kn,
             d_lq1=d_lq1, d_lk1=d_lk1, d_lq2=d_lq2, d_lk2=d_lk2, d_subln=d_subln,
             ffn_up=ffn_up, ffn_conv_w=ffn_conv_w, ffn_conv_b=ffn_conv_b, ffn_down=ffn_down)
    depth = w_mod.shape[0]
    n_dec = c.shape[0]
    mod = _modulation(jnp.concatenate([c_ctx[None], c], axis=0), w_mod, b_mod)
    split6 = lambda m: [m[:, None, k * D_MODEL:(k + 1) * D_MODEL] for k in range(6)]
    mods_ctx = [split6(mod[l, 0:1]) for l in range(depth)]
    mods_dec = [split6(mod[l, 1:1 + n_dec]) for l in range(depth)]

    y_prompt, nw = _trunk(x_prompt, mods_ctx, p, None, None)
    y_sample, _ = _trunk(x_sample, mods_dec, p, (state_s5_re, state_s5_im, state_gdn),
                         (cache_c_k, cache_c_v, cache_d_k, cache_d_v))
    st = lambda name: jnp.stack(nw[name], axis=1)
    return (y_prompt, y_sample, st('s5r'), st('s5i'), st('gdn'), st('ck'), st('cv'), st('dk'), st('dv'))
```

```python
import functools
import math

import jax
import jax.numpy as jnp
from jax import lax
from jax.experimental import pallas as pl
from jax.experimental.pallas import tpu as pltpu

F32 = jnp.float32
BF16 = jnp.bfloat16

D_MODEL = 1024
GRID_W = 64
EPS = 1e-6
NEG_INF = -1e30

S5_WIDTH = 512
S5_GROUP = 16
S5_GROUPS = 32
S5_STATE = 64
S5_TILE_GROUPS = 8
S5_TILE_CH = S5_TILE_GROUPS * S5_GROUP
S5_TILE_ST = S5_TILE_GROUPS * S5_STATE
S5_TILES = S5_GROUPS // S5_TILE_GROUPS
S5_SEG = 256
S5_SUB = 32
PROJ_AB_TOKENS = 64
S5_ROWS = 8

GDN_DK = 128
GDN_DV = 128
GDN_HEADS = 4
GDN_WIDTH = 512
GDN_CHUNK = 64
GDN_GROUP = 4
GDN_SHORT_LEN = 256
GDN_SEQS_SHORT = 2
GDN_EXACT_ROUNDS = 3

HEAD_DIM = 64
C_HEADS = 8
C_KV_HEADS = 2
C_GROUP = 4
WINDOW = 128
Q_BLOCK = 128
D_HEADS = 4
D_VDIM = 128
ATTN_SCALE = HEAD_DIM ** -0.5
ATTN_SEQS_SHORT = 2
ROPE_THETA = 10000.0

D_FF = 2816
FF_CHUNK = 256
FF_CHUNKS = D_FF // FF_CHUNK
POST_ROWS = 512
POST_HALO = 8

AB_MAIN = S5_WIDTH + 4 * GDN_WIDTH
AB_PAD = AB_MAIN + 128
CD_IN = 2304

MOD_TK = 128
MOD_LANES = 512
VMEM_LIMIT = 56 * 1024 * 1024


def _cparams(sem):
    return pltpu.CompilerParams(dimension_semantics=sem, vmem_limit_bytes=VMEM_LIMIT)


def _sigmoid(x):
    return 1.0 / (1.0 + jnp.exp(-x))


def _silu(x):
    return x * _sigmoid(x)


def _softplus(x):
    return jnp.maximum(x, 0.0) + jnp.log(1.0 + jnp.exp(-jnp.abs(x)))


def _gelu_tanh(x):
    return 0.5 * x * (1.0 + jnp.tanh(math.sqrt(2.0 / math.pi) * (x + 0.044715 * (x * x * x))))


def _rms_mod(x, g, shift, scale):
    y = x * lax.rsqrt(jnp.mean(x * x, axis=-1, keepdims=True) + EPS)
    return (y * g) * (1.0 + scale) + shift


def _dot(a, b):
    return jnp.dot(a.astype(BF16), b.astype(BF16), preferred_element_type=F32)


def _dot_nt(a, b):
    return lax.dot_general(a.astype(BF16), b.astype(BF16), (((1,), (1,)), ((), ())),
                           preferred_element_type=F32)


def _hi_lo(a):
    hi = a.astype(BF16)
    return hi, (a - hi.astype(F32)).astype(BF16)


def _mod_kernel(ct_ref, w_ref, b_ref, o_ref, acc_ref, *, n_rows):
    k = pl.program_id(1)

    @pl.when(k == 0)
    def _():
        acc_ref[...] = jnp.zeros_like(acc_ref)

    tk, n_out = w_ref.shape[1:]
    s = _silu(ct_ref[...])
    sb = [[jnp.broadcast_to(s[g * 8:(g + 1) * 8, m:m + 1], (8, MOD_LANES)) for g in range(tk // 8)]
          for m in range(n_rows)]
    for c in range(n_out // MOD_LANES):
        cols = slice(c * MOD_LANES, (c + 1) * MOD_LANES)
        accs = [acc_ref[m, :, cols] for m in range(n_rows)]
        for g in range(tk // 8):
            wg = w_ref[0, g * 8:(g + 1) * 8, cols]
            accs = [a + wg * sb[m][g] for m, a in enumerate(accs)]
        for m in range(n_rows):
            acc_ref[m, :, cols] = accs[m]

    @pl.when(k == pl.num_programs(1) - 1)
    def _():
        o_ref[0] = jnp.zeros(o_ref.shape[1:], F32)
        for m in range(n_rows):
            o_ref[0, m:m + 1, :] = jnp.sum(acc_ref[m], axis=0, keepdims=True) + b_ref[0]


def _modulation(cvecs, w_mod, b_mod):
    n, d = cvecs.shape
    depth, _, n_out = w_mod.shape
    ct = jnp.zeros((d, 8), F32).at[:, :n].set(cvecs.T)
    return pl.pallas_call(
        functools.partial(_mod_kernel, n_rows=n),
        grid=(depth, d // MOD_TK),
        in_specs=[pl.BlockSpec((MOD_TK, 8), lambda l, k: (k, 0)),
                  pl.BlockSpec((1, MOD_TK, n_out), lambda l, k: (l, k, 0)),
                  pl.BlockSpec((1, 1, n_out), lambda l, k: (l, 0, 0))],
        out_specs=pl.BlockSpec((1, 8, n_out), lambda l, k: (l, 0, 0)),
        out_shape=jax.ShapeDtypeStruct((depth, 8, n_out), F32),
        scratch_shapes=[pltpu.VMEM((n, 8, n_out), F32)],
        compiler_params=_cparams(("parallel", "arbitrary")),
        name="adaln_mod",
    )(ct, w_mod, b_mod.reshape(depth, 1, n_out))


def _proj_ab_kernel(x_ref, g_ref, sh_ref, sc_ref, w_ref, wg_ref, u_ref, qkv_ref, z_ref, ab_ref):
    ns, tm, d = x_ref.shape
    h = _rms_mod(x_ref[...], g_ref[...], sh_ref[...], sc_ref[...])
    h = h.reshape(ns * tm, d).astype(BF16)
    y = jnp.dot(h, w_ref[...], preferred_element_type=F32)
    for s in range(ns):
        u_ref[:, s, :] = y[s * tm:(s + 1) * tm, :S5_WIDTH]
    qkv_ref[...] = y[:, S5_WIDTH:S5_WIDTH + 3 * GDN_WIDTH].reshape(ns, tm, 3 * GDN_WIDTH)
    z_ref[...] = y[:, S5_WIDTH + 3 * GDN_WIDTH:AB_MAIN].reshape(ns, tm, GDN_WIDTH)
    ab_ref[...] = jnp.dot(h, wg_ref[...], preferred_element_type=F32).reshape(ns, tm, 128)


def _proj_ab(x, g, shift, scale, w, w_gate):
    rows, n, d = x.shape
    tm = PROJ_AB_TOKENS
    per_row = shift.shape[0] > 1
    mspec = pl.BlockSpec((S5_ROWS, 1, d), lambda r, i: (r, 0, 0)) if per_row else \
        pl.BlockSpec((1, 1, d), lambda r, i: (0, 0, 0))
    blk = lambda w_: pl.BlockSpec((S5_ROWS, tm, w_), lambda r, i: (r, i, 0))
    return pl.pallas_call(
        _proj_ab_kernel,
        grid=(rows // S5_ROWS, n // tm),
        in_specs=[blk(d),
                  pl.BlockSpec((1, d), lambda r, i: (0, 0)),
                  mspec, mspec,
                  pl.BlockSpec((d, AB_MAIN), lambda r, i: (0, 0)),
                  pl.BlockSpec((d, 128), lambda r, i: (0, 0))],
        out_specs=[pl.BlockSpec((tm, S5_ROWS, S5_WIDTH), lambda r, i: (i, r, 0)),
                   blk(3 * GDN_WIDTH), blk(GDN_WIDTH), blk(128)],
        out_shape=[jax.ShapeDtypeStruct((n, rows, S5_WIDTH), F32),
                   jax.ShapeDtypeStruct((rows, n, 3 * GDN_WIDTH), F32),
                   jax.ShapeDtypeStruct((rows, n, GDN_WIDTH), F32),
                   jax.ShapeDtypeStruct((rows, n, 128), F32)],
        compiler_params=_cparams(("parallel", "parallel")),
        name="proj_ab",
    )(x, g, shift, scale, w, w_gate)


def _s5_kernel(u_ref, bm_ref, cm_ref, a_ref, h0r_ref, h0i_ref, *rest, want_y):
    if want_y:
        y_ref, fr_ref, fi_ref, xs_ref = rest
    else:
        fr_ref, fi_ref, xs_ref = rest
    n = u_ref.shape[0]
    sub = S5_SUB
    nsub = n // sub
    st = S5_TILE_ST

    def x_proj(d, k):
        u2 = u_ref[k * sub:(k + 1) * sub].reshape(sub * S5_ROWS, S5_TILE_CH)
        xs_ref[d, k * sub:(k + 1) * sub] = _dot(u2, bm_ref[d, 0]).reshape(sub, S5_ROWS, 2 * st)

    ar = [jnp.broadcast_to(a_ref[d, 0, 0:1, :], (S5_ROWS, st)) for d in range(2)]
    ai = [jnp.broadcast_to(a_ref[d, 0, 1:2, :], (S5_ROWS, st)) for d in range(2)]
    hr = [h0r_ref[0], h0r_ref[1]]
    hi = [h0i_ref[0], h0i_ref[1]]
    x_proj(0, 0)
    x_proj(1, nsub - 1)
    written = set()
    for k in range(nsub):
        ks = (k, nsub - 1 - k)
        if k + 1 < nsub:
            x_proj(0, ks[0] + 1)
            x_proj(1, ks[1] - 1)
        for t in range(sub):
            for d in range(2):
                tt = ks[d] * sub + (t if d == 0 else sub - 1 - t)
                x = xs_ref[d, tt]
                nr = ar[d] * hr[d] - ai[d] * hi[d] + x[:, :st]
                ni = ar[d] * hi[d] + ai[d] * hr[d] + x[:, st:]
                xs_ref[d, tt] = jnp.concatenate([nr, ni], axis=-1)
                hr[d], hi[d] = nr, ni
        if want_y:
            for d in range(2):
                rows = slice(ks[d] * sub, (ks[d] + 1) * sub)
                hs = xs_ref[d, rows].reshape(sub * S5_ROWS, 2 * st)
                yv = _dot(hs, cm_ref[d, 0]).reshape(sub, S5_ROWS, S5_TILE_CH)
                if ks[d] in written:
                    y_ref[rows] += yv
                else:
                    y_ref[rows] = yv
                    written.add(ks[d])
    for d in range(2):
        fr_ref[d] = hr[d]
        fi_ref[d] = hi[d]


def _s5_scan(u_t, bmat, cmat, amat, h0r, h0i, want_y=True):
    n, rows, _ = u_t.shape
    state = lambda: pl.BlockSpec((2, S5_ROWS, S5_TILE_ST), lambda r, j: (0, r, j))
    st_shape = jax.ShapeDtypeStruct((2, rows, S5_GROUPS * S5_STATE), F32)
    out_specs = [state(), state()]
    out_shape = [st_shape, st_shape]
    if want_y:
        out_specs.insert(0, pl.BlockSpec((n, S5_ROWS, S5_TILE_CH), lambda r, j: (0, r, j)))
        out_shape.insert(0, jax.ShapeDtypeStruct((n, rows, S5_WIDTH), F32))
    return pl.pallas_call(
        functools.partial(_s5_kernel, want_y=want_y),
        grid=(rows // S5_ROWS, S5_TILES),
        in_specs=[pl.BlockSpec((n, S5_ROWS, S5_TILE_CH), lambda r, j: (0, r, j)),
                  pl.BlockSpec((2, 1, S5_TILE_CH, 2 * S5_TILE_ST), lambda r, j: (0, j, 0, 0)),
                  pl.BlockSpec((2, 1, 2 * S5_TILE_ST, S5_TILE_CH), lambda r, j: (0, j, 0, 0)),
                  pl.BlockSpec((2, 1, 8, S5_TILE_ST), lambda r, j: (0, j, 0, 0)),
                  state(), state()],
        out_specs=out_specs,
        out_shape=out_shape,
        scratch_shapes=[pltpu.VMEM((2, n, S5_ROWS, 2 * S5_TILE_ST), F32)],
        compiler_params=_cparams(("parallel", "parallel")),
        name="s5_scan" if want_y else "s5_states",
    )(u_t, bmat, cmat, amat, h0r, h0i)


def _s5_params(p, j):
    lam_re, lam_im, log_dt = p['s5_lam_re'][j], p['s5_lam_im'][j], p['s5_log_dt'][j]
    dt = jnp.exp(log_dt)[..., None]
    mag = jnp.exp(lam_re * dt)
    ar, ai = mag * jnp.cos(lam_im * dt), mag * jnp.sin(lam_im * dt)
    den = lam_re * lam_re + lam_im * lam_im
    fr = ((ar - 1.0) * lam_re + ai * lam_im) / den
    fi = (ai * lam_re - (ar - 1.0) * lam_im) / den
    b_re, b_im = p['s5_b_re'][j], p['s5_b_im'][j]
    bbr = fr[..., None] * b_re - fi[..., None] * b_im
    bbi = fr[..., None] * b_im + fi[..., None] * b_re
    eye = jnp.eye(S5_TILE_GROUPS, dtype=F32)

    def in_blocks(t):
        t = t.reshape(2, S5_TILES, S5_TILE_GROUPS, S5_STATE, S5_GROUP)
        t = jnp.einsum('dtgpc,gh->dtgchp', t, eye)
        return t.reshape(2, S5_TILES, S5_TILE_CH, S5_TILE_ST)

    def out_blocks(t):
        t = t.reshape(2, S5_TILES, S5_TILE_GROUPS, S5_GROUP, S5_STATE)
        t = jnp.einsum('dtgcp,gh->dtgphc', t, eye)
        return t.reshape(2, S5_TILES, S5_TILE_ST, S5_TILE_CH)

    bmat = jnp.concatenate([in_blocks(bbr), in_blocks(bbi)], axis=-1).astype(BF16)
    cmat = jnp.concatenate([out_blocks(p['s5_c_re'][j]), -out_blocks(p['s5_c_im'][j])], axis=-2).astype(BF16)
    seg_mag = jnp.exp(lam_re * dt * S5_SEG)
    pr, pi = seg_mag * jnp.cos(lam_im * dt * S5_SEG), seg_mag * jnp.sin(lam_im * dt * S5_SEG)
    flat = lambda t: t.reshape(2, S5_TILES, 1, S5_TILE_ST)
    amat = jnp.concatenate([flat(ar), flat(ai), flat(pr), flat(pi),
                            jnp.zeros((2, S5_TILES, 4, S5_TILE_ST), F32)], axis=2)
    return bmat, cmat, amat


def _s5_glu_kernel(y_ref, u_ref, d_ref, w_ref, b_ref, o_ref):
    tm, ns, _ = y_ref.shape
    y = jnp.concatenate([y_ref[:, s, :] + d_ref[...] * u_ref[:, s, :] for s in range(ns)], axis=0)
    g = _gelu_tanh(y)
    out = g * _sigmoid(_dot(g, w_ref[...]) + b_ref[...])
    o_ref[...] = out.reshape(ns, tm, S5_WIDTH)


def _s5_glu(y_t, u_t, s5_d, w_glu, b_glu):
    n, rows, _ = y_t.shape
    tm = PROJ_AB_TOKENS
    tblk = pl.BlockSpec((tm, S5_ROWS, S5_WIDTH), lambda r, i: (i, r, 0))
    return pl.pallas_call(
        _s5_glu_kernel,
        grid=(rows // S5_ROWS, n // tm),
        in_specs=[tblk, tblk,
                  pl.BlockSpec((1, S5_WIDTH), lambda r, i: (0, 0)),
                  pl.BlockSpec((S5_WIDTH, S5_WIDTH), lambda r, i: (0, 0)),
                  pl.BlockSpec((1, S5_WIDTH), lambda r, i: (0, 0))],
        out_specs=pl.BlockSpec((S5_ROWS, tm, S5_WIDTH), lambda r, i: (r, i, 0)),
        out_shape=jax.ShapeDtypeStruct((rows, n, S5_WIDTH), F32),
        compiler_params=_cparams(("parallel", "parallel")),
        name="s5_glu",
    )(y_t, u_t, s5_d, w_glu, b_glu)


def _s5_mixer(u3, p, j, h0r, h0i, bsz, nseg):
    rows = bsz * nseg
    bmat, cmat, amat = _s5_params(p, j)
    if nseg == 1:
        y_t, fr, fi = _s5_scan(u3, bmat, cmat, amat, h0r, h0i)
    else:
        zero = jnp.zeros((2, bsz, nseg, S5_GROUPS * S5_STATE), F32)
        first = jnp.array([0, nseg - 1])
        seed = lambda h0: zero.at[jnp.arange(2), :, first].set(h0).reshape(2, rows, -1)
        fr, fi = _s5_scan(u3, bmat, cmat, amat, seed(h0r), seed(h0i), want_y=False)
        fr = fr.reshape(2, bsz, nseg, -1)
        fi = fi.reshape(2, bsz, nseg, -1)
        pr = amat[:, :, 2].reshape(2, 1, -1)
        pi = amat[:, :, 3].reshape(2, 1, -1)

        def chain(dr, order):
            hr, hi = (h0r[dr], h0i[dr])
            outs_r, outs_i = {}, {}
            for n_done, k in enumerate(order):
                outs_r[k], outs_i[k] = hr, hi
                if n_done == 0:
                    hr, hi = fr[dr, :, k], fi[dr, :, k]
                else:
                    hr, hi = (pr[dr] * hr - pi[dr] * hi + fr[dr, :, k],
                              pr[dr] * hi + pi[dr] * hr + fi[dr, :, k])
            st = lambda o: jnp.stack([o[k] for k in range(nseg)], axis=1)
            return st(outs_r), st(outs_i), hr, hi

        sr0, si0, er0, ei0 = chain(0, list(range(nseg)))
        sr1, si1, er1, ei1 = chain(1, list(range(nseg - 1, -1, -1)))
        start_r = jnp.stack([sr0, sr1]).reshape(2, rows, -1)
        start_i = jnp.stack([si0, si1]).reshape(2, rows, -1)
        y_t, _, _ = _s5_scan(u3, bmat, cmat, amat, start_r, start_i)
        fr = jnp.stack([er0, er1])
        fi = jnp.stack([ei0, ei1])
    ya = _s5_glu(y_t, u3, p['s5_d'][j][None], p['s5_w_glu'][j].astype(BF16), p['s5_b_glu'][j][None])
    return ya, fr, fi


def _gdn_kernel(qkv_ref, z_ref, ab_ref, cw_ref, gp_ref, ng_ref, *rest, zero_init):
    if zero_init:
        s0_ref = None
        o_ref, sf_ref, q_s, k_s, v_s, gate_s, of_s, ob_s, st_s, uw_s, a_s, kgt_s = rest
    else:
        s0_ref, o_ref, sf_ref, q_s, k_s, v_s, gate_s, of_s, ob_s, st_s, uw_s, a_s, kgt_s = rest
    ns, n = qkv_ref.shape[:2]
    nc = n // GDN_CHUNK
    row = lax.broadcasted_iota(jnp.int32, (n, 1), 0)

    for sq in range(ns):
        for blk in range(3 * GDN_HEADS):
            cols = slice(blk * GDN_DK, (blk + 1) * GDN_DK)
            hs = slice((blk % GDN_HEADS) * GDN_DK, (blk % GDN_HEADS + 1) * GDN_DK)
            x = qkv_ref[sq, :, cols]
            xm = jnp.where(row == 0, 0.0, pltpu.roll(x, 1, 0))
            xp = jnp.where(row == n - 1, 0.0, pltpu.roll(x, n - 1, 0))
            y = _silu(xm * cw_ref[0:1, cols] + x * cw_ref[1:2, cols] + xp * cw_ref[2:3, cols])
            if blk < GDN_HEADS:
                q_s[sq, :, hs] = y * lax.rsqrt(jnp.sum(y * y, axis=-1, keepdims=True) + EPS) * (GDN_DK ** -0.5)
            elif blk < 2 * GDN_HEADS:
                k_s[sq, :, hs] = y * lax.rsqrt(jnp.sum(y * y, axis=-1, keepdims=True) + EPS)
            else:
                v_s[sq, :, hs] = y

        ab = ab_ref[sq]
        beta = _sigmoid(ab)
        g = -jnp.exp(gp_ref[0:1, :]) * _softplus(ab + gp_ref[1:2, :])
        pos = row % GDN_CHUNK
        pre, suf = g, g
        sft = 1
        while sft < GDN_CHUNK:
            pre = pre + jnp.where(pos >= sft, pltpu.roll(pre, sft, 0), 0.0)
            suf = suf + jnp.where(pos < GDN_CHUNK - sft, pltpu.roll(suf, n - sft, 0), 0.0)
            sft *= 2
        gate_s[sq, 0] = beta
        gate_s[sq, 1] = pre
        gate_s[sq, 2] = suf

    st_s[...] = jnp.zeros_like(st_s) if zero_init else s0_ref[...]
    cs = GDN_CHUNK
    pk = GDN_HEADS * cs
    ri = lax.broadcasted_iota(jnp.int32, (cs, pk), 0)
    lane_pk = lax.broadcasted_iota(jnp.int32, (cs, pk), 1)
    ci = lane_pk % cs
    eye_pk = (ri == ci).astype(F32)
    head_pk = [(lax.broadcasted_iota(jnp.int32, (1, pk), 1) // cs) == h for h in range(GDN_HEADS)]
    head_w = [(lax.broadcasted_iota(jnp.int32, (1, GDN_WIDTH), 1) // GDN_DK) == h for h in range(GDN_HEADS)]

    def block_diag(p):
        return jnp.concatenate([jnp.where(m, p, jnp.zeros_like(p)) for m in head_pk], axis=0)

    def dot3_bd(a, p):
        a_hi, a_lo = _hi_lo(a)
        p_hi, p_lo = _hi_lo(p)
        b_hi, b_lo = block_diag(p_hi), block_diag(p_lo)
        mm = functools.partial(jnp.dot, preferred_element_type=F32)
        return mm(a_hi, b_hi) + (mm(a_lo, b_hi) + mm(a_hi, b_lo))

    def dot1_bd(a, p):
        return jnp.dot(a.astype(BF16), block_diag(p.astype(BF16)), preferred_element_type=F32)

    def lanes(cols, width):
        return jnp.concatenate([jnp.broadcast_to(c, (cs, width)) for c in cols], axis=1)

    def phase_a(sq, it):
        st = []
        for cc in range(GDN_GROUP):
            rows = pl.ds(pl.multiple_of((it * GDN_GROUP + cc) * cs, cs), cs)
            q_all, k_all, v_all = q_s[sq, rows, :], k_s[sq, rows, :], v_s[sq, rows, :]
            beta_blk = gate_s[sq, 0, rows, :]
            for dr in range(2):
                gc_blk = gate_s[sq, 1 + dr, rows, :]
                lane0 = dr * GDN_HEADS
                bcols = [beta_blk[:, 8 + lane0 + h:9 + lane0 + h] for h in range(GDN_HEADS)]
                gcols = [gc_blk[:, lane0 + h:lane0 + h + 1] for h in range(GDN_HEADS)]
                st.append(dict(sq=sq, dr=dr, rows=rows, cidx=it * GDN_GROUP + cc, q=q_all, k=k_all, v=v_all, bcols=bcols, gcols=gcols,
                               incl=(ri >= ci) if dr == 0 else (ri <= ci),
                               strict=(ri > ci) if dr == 0 else (ri < ci)))
        for t in st:
            b_w = lanes(t['bcols'], GDN_DK)
            t['kb'] = t['k'] * b_w
            t['vb'] = t['v'] * b_w
            k_bd = jnp.concatenate([jnp.where(m, t['k'], 0.0) for m in head_w], axis=0)
            t['kq'] = _dot_nt(jnp.concatenate([t['kb'], t['q']], axis=0), k_bd)
        for t in st:
            gcol = lanes(t['gcols'], cs)
            grow = jnp.sum(eye_pk * gcol, axis=0, keepdims=True)
            decay = jnp.exp(jnp.where(t['incl'], gcol - grow, NEG_INF))
            t['pw'] = jnp.where(t['strict'], t['kq'][:cs] * decay, 0.0)
            a_s[t['sq'], t['dr'], t['rows'], :] =jnp.where(t['incl'], t['kq'][cs:] * decay, 0.0)
            t['tm'] = eye_pk - t['pw']
        for t in st:
            t['pw'] = dot3_bd(t['pw'], t['pw'])
        for rnd in range(5):
            mm_bd = dot3_bd if rnd < GDN_EXACT_ROUNDS else dot1_bd
            for t in st:
                if rnd < 4:
                    r = mm_bd(jnp.concatenate([t['pw'], t['tm']], axis=0), t['pw'])
                    t['pw'] = r[:cs]
                    t['tm'] = t['tm'] + r[cs:]
                else:
                    t['tm'] = t['tm'] + mm_bd(t['tm'], t['pw'])
        for t in st:
            for h in range(GDN_HEADS):
                g = t['gcols'][h]
                gl = g[cs - 1:cs] if t['dr'] == 0 else g[0:1]
                kg = t['k'][:, h * GDN_DK:(h + 1) * GDN_DK] * jnp.exp(gl - g)
                kgt_s[t['sq'], t['dr'], t['cidx'], h] = kg.T.astype(BF16)
        for t in st:
            kbg = t['kb'] * lanes([jnp.exp(g) for g in t['gcols']], GDN_DK)
            for h in range(GDN_HEADS):
                hs = slice(h * GDN_DK, (h + 1) * GDN_DK)
                rhs = jnp.concatenate([t['vb'][:, hs], kbg[:, hs]], axis=1)
                uw_s[t['sq'], t['dr'], t['rows'], 2 * h * GDN_DK:2 * (h + 1) * GDN_DK] = _dot(
                    t['tm'][:, h * cs:(h + 1) * cs], rhs)

    for sq in range(ns):
        if nc // GDN_GROUP == 1:
            phase_a(sq, 0)
        else:
            lax.fori_loop(0, nc // GDN_GROUP, lambda it, c, sq=sq: (phase_a(sq, it), c)[1], 0)

    def phase_b(c, carry):
        ch = []
        for sq, dr in [(sq, dr) for sq in range(ns) for dr in range(2)]:
            cidx = c if dr == 0 else nc - 1 - c
            rows = pl.ds(pl.multiple_of(cidx * cs, cs), cs)
            gc_blk = gate_s[sq, 1 + dr, rows, :]
            for h in range(GDN_HEADS):
                hs = slice(h * GDN_DK, (h + 1) * GDN_DK)
                lane = dr * GDN_HEADS + h
                gcol = gc_blk[:, lane:lane + 1]
                ch.append(dict(sq=sq, dr=dr, h=h, rows=rows, hs=hs, gcol=gcol, cidx=cidx,
                               gl=gcol[cs - 1:cs] if dr == 0 else gcol[0:1],
                               uw=uw_s[sq, dr, rows, 2 * h * GDN_DK:2 * (h + 1) * GDN_DK],
                               amat=a_s[sq, dr, rows, h * cs:(h + 1) * cs]))
        for t in ch:
            t['s'] = st_s[t['sq'], t['dr'], t['h']]
            qg = q_s[t['sq'], t['rows'], t['hs']] * jnp.exp(t['gcol'])
            t['ws'] = _dot(jnp.concatenate([t['uw'][:, GDN_DV:], qg], axis=0), t['s'])
        for t in ch:
            vn = t['uw'][:, :GDN_DV] - t['ws'][:cs]
            o = t['ws'][cs:] + _dot(t['amat'], vn)
            kgt = kgt_s[t['sq'], t['dr'], t['cidx'], t['h']]
            st_s[t['sq'], t['dr'], t['h']] = t['s'] * jnp.exp(t['gl']) + _dot(kgt, vn)
            if t['dr'] == 0:
                of_s[t['sq'], t['rows'], t['hs']] = o
            else:
                ob_s[t['sq'], t['rows'], t['hs']] = o
        return carry

    lax.fori_loop(0, nc, phase_b, 0)
    sf_ref[...] = st_s[...]

    for sq in range(ns):
        z = z_ref[sq]
        for h in range(GDN_HEADS):
            hs = slice(h * GDN_DV, (h + 1) * GDN_DV)
            o = of_s[sq, :, hs] + ob_s[sq, :, hs]
            o = o * lax.rsqrt(jnp.mean(o * o, axis=-1, keepdims=True) + EPS) * ng_ref[...]
            o_ref[sq, :, hs] = o * _silu(z[:, hs])


def _gdn_mixer(qkv, z, ab, p, j, s0):
    bsz, n, _ = qkv.shape
    gp = jnp.zeros((8, 128), F32)
    gp = gp.at[0, :8].set(p['gdn_a_log'][j].reshape(8)).at[1, :8].set(p['gdn_dt_bias'][j].reshape(8))
    cw = jnp.zeros((8, 3 * GDN_WIDTH), F32).at[:3].set(p['gdn_conv_w'][j])
    ns = GDN_SEQS_SHORT if (n <= GDN_SHORT_LEN and bsz % GDN_SEQS_SHORT == 0) else 1
    blk = lambda w_: pl.BlockSpec((ns, n, w_), lambda b: (b, 0, 0))
    sblk = pl.BlockSpec((ns, 2, GDN_HEADS, GDN_DK, GDN_DV), lambda b: (b, 0, 0, 0, 0))
    tok = lambda w_: pltpu.VMEM((ns, n, w_), F32)
    return pl.pallas_call(
        functools.partial(_gdn_kernel, zero_init=s0 is None),
        grid=(bsz // ns,),
        in_specs=[blk(3 * GDN_WIDTH), blk(GDN_WIDTH), blk(128),
                  pl.BlockSpec((8, 3 * GDN_WIDTH), lambda b: (0, 0)),
                  pl.BlockSpec((8, 128), lambda b: (0, 0)),
                  pl.BlockSpec((1, GDN_DV), lambda b: (0, 0))] + ([] if s0 is None else [sblk]),
        out_specs=[blk(GDN_WIDTH), sblk],
        out_shape=[jax.ShapeDtypeStruct((bsz, n, GDN_WIDTH), F32),
                   jax.ShapeDtypeStruct((bsz, 2, GDN_HEADS, GDN_DK, GDN_DV), F32)],
        scratch_shapes=[tok(GDN_WIDTH), tok(GDN_WIDTH), tok(GDN_WIDTH),
                        pltpu.VMEM((ns, 3, n, 128), F32),
                        tok(GDN_WIDTH), tok(GDN_WIDTH),
                        pltpu.VMEM((ns, 2, GDN_HEADS, GDN_DK, GDN_DV), F32),
                        pltpu.VMEM((ns, 2, n, 2 * GDN_WIDTH), F32),
                        pltpu.VMEM((ns, 2, n, GDN_HEADS * GDN_CHUNK), F32),
                        pltpu.VMEM((ns, 2, n // GDN_CHUNK, GDN_HEADS, GDN_DK, GDN_CHUNK), BF16)],
        compiler_params=_cparams(("parallel",)),
        name="gdn_mixer",
    )(qkv, z, ab, cw, gp, p['gdn_norm_g'][j][None], *(() if s0 is None else (s0,)))


def _proj_cd_kernel(x_ref, g_ref, sh_ref, sc_ref, w_ref, gm_ref, gain_ref, cos_ref, sin_ref,
                    qc_ref, kc_ref, vc_ref, qd_ref, kd_ref, vd_ref, *cache_refs, rope):
    h = _rms_mod(x_ref[0], g_ref[...], sh_ref[0], sc_ref[0])
    y = _dot(h, w_ref[...])
    lane = lax.broadcasted_iota(jnp.int32, (1, 512), 1)
    low = (lane % 32) < 16

    def head_norm(t, gain, scale):
        w = t.shape[1]
        ms = _dot(t * t, gm_ref[:w, :w])
        t = t * lax.rsqrt(ms + EPS) * gain
        if rope:
            part = jnp.where(low[:, :w], pltpu.roll(t, w - 16, 1), pltpu.roll(t, 16, 1))
            t = t * cos_ref[:, :w] + part * sin_ref[:, :w]
        return t * scale if scale != 1.0 else t

    kc = head_norm(y[:, 512:640], gain_ref[1:2, :128], 1.0)
    vc = y[:, 640:768]
    kd = head_norm(y[:, 1280:1792], gain_ref[3:4, :], 1.0)
    vd = y[:, 1792:2304]
    qc_ref[0] = head_norm(y[:, 0:512], gain_ref[0:1, :], ATTN_SCALE)
    kc_ref[0] = kc
    vc_ref[0] = vc
    qd_ref[0] = head_norm(y[:, 768:1280], gain_ref[2:3, :], ATTN_SCALE)
    kd_ref[0] = kd
    vd_ref[0] = vd
    if cache_refs:
        ck_ref, cv_ref, dk_ref, dv_ref = cache_refs
        for g in range(C_KV_HEADS):
            ck_ref[0, :, g, :] = kc[:, g * HEAD_DIM:(g + 1) * HEAD_DIM]
            cv_ref[0, :, g, :] = vc[:, g * HEAD_DIM:(g + 1) * HEAD_DIM]
        for hd in range(D_HEADS):
            dv_ref[0, :, hd, :] = vd[:, hd * D_VDIM:(hd + 1) * D_VDIM]
            for c in range(2):
                dk_ref[0, :, hd, c, :] = kd[:, (2 * hd + c) * HEAD_DIM:(2 * hd + c + 1) * HEAD_DIM]


def _rope_tables(n):
    rows = n // GRID_W
    row = jnp.repeat(jnp.arange(rows), GRID_W).astype(F32)
    col = jnp.tile(jnp.arange(GRID_W), rows).astype(F32)
    quarter = HEAD_DIM // 4
    inv = ROPE_THETA ** (-jnp.arange(quarter, dtype=F32) / quarter)
    ang_r = row[:, None] * inv[None, :]
    ang_c = col[:, None] * inv[None, :]
    cos = jnp.concatenate([jnp.cos(ang_r), jnp.cos(ang_r), jnp.cos(ang_c), jnp.cos(ang_c)], axis=-1)
    sin = jnp.concatenate([-jnp.sin(ang_r), jnp.sin(ang_r), -jnp.sin(ang_c), jnp.sin(ang_c)], axis=-1)
    return jnp.tile(cos, (1, 8)), jnp.tile(sin, (1, 8))


def _proj_cd(x, g, shift, scale, w, p, j, rope, emit_caches):
    bsz, n, d = x.shape
    tm = 256
    per_seq = shift.shape[0] > 1
    midx = (lambda b, i: (b, 0, 0)) if per_seq else (lambda b, i: (0, 0, 0))
    lane = jnp.arange(512)
    gmat = ((lane[:, None] // HEAD_DIM) == (lane[None, :] // HEAD_DIM)).astype(F32) / HEAD_DIM
    gains = jnp.zeros((8, 512), F32)
    gains = gains.at[0].set(jnp.tile(p['c_qn'][j], 8)).at[1].set(jnp.tile(p['c_kn'][j], 8))
    gains = gains.at[2].set(jnp.tile(p['d_qn'][j], 8)).at[3].set(jnp.tile(p['d_kn'][j], 8))
    if rope:
        cos, sin = _rope_tables(n)
    else:
        cos, sin = jnp.ones((tm, 512), F32), jnp.zeros((tm, 512), F32)
    tidx = (lambda b, i: (i, 0)) if rope else (lambda b, i: (0, 0))
    blk = lambda w_: pl.BlockSpec((1, tm, w_), lambda b, i: (b, i, 0))
    out_specs = [blk(512), blk(128), blk(128), blk(512), blk(512), blk(512)]
    out_shape = [jax.ShapeDtypeStruct((bsz, n, w_), F32) for w_ in (512, 128, 128, 512, 512, 512)]
    if emit_caches:
        for tail in ((C_KV_HEADS, HEAD_DIM), (C_KV_HEADS, HEAD_DIM), (D_HEADS, 2, HEAD_DIM), (D_HEADS, D_VDIM)):
            out_specs.append(pl.BlockSpec((1, tm) + tail, lambda b, i, nd=len(tail): (b, i) + (0,) * nd))
            out_shape.append(jax.ShapeDtypeStruct((bsz, n) + tail, F32))
    return pl.pallas_call(
        functools.partial(_proj_cd_kernel, rope=rope),
        grid=(bsz, n // tm),
        in_specs=[blk(d),
                  pl.BlockSpec((1, d), lambda b, i: (0, 0)),
                  pl.BlockSpec((1, 1, d), midx),
                  pl.BlockSpec((1, 1, d), midx),
                  pl.BlockSpec((d, CD_IN), lambda b, i: (0, 0)),
                  pl.BlockSpec((512, 512), lambda b, i: (0, 0)),
                  pl.BlockSpec((8, 512), lambda b, i: (0, 0)),
                  pl.BlockSpec((tm, 512), tidx),
                  pl.BlockSpec((tm, 512), tidx)],
        out_specs=out_specs,
        out_shape=out_shape,
        compiler_params=_cparams(("parallel", "parallel")),
        name="proj_cd",
    )(x, g, shift, scale, w, gmat.astype(BF16), gains, cos, sin)


def _probs(scores, extra=None):
    m = scores[0].max(axis=-1, keepdims=True)
    for s in scores[1:]:
        m = jnp.maximum(m, s.max(axis=-1, keepdims=True))
    if extra is not None:
        m = jnp.maximum(m, extra)
    return [jnp.exp(s - m).astype(BF16) for s in scores], m


def _attn_kernel(qc_ref, kc_ref, vc_ref, qd_ref, kd_ref, vd_ref, *rest, windowed, n_ctx, lam_scale):
    if n_ctx:
        ck_ref, cv_ref, dk_ref, dv_ref, misc_ref, sub_ref, oc_ref, od_ref = rest
    else:
        misc_ref, sub_ref, oc_ref, od_ref = rest
    tq = qc_ref.shape[1]
    n = kc_ref.shape[1]
    start = pl.program_id(1) * tq
    if windowed:
        span = tq + 2 * WINDOW
        k0 = pl.multiple_of(jnp.clip(start - WINDOW, 0, n - span), 128)
        krows = pl.ds(k0, span)
        qpos = start + lax.broadcasted_iota(jnp.int32, (tq, span), 0)
        kpos = k0 + lax.broadcasted_iota(jnp.int32, (tq, span), 1)
        ok = jnp.abs(qpos - kpos) <= WINDOW
    else:
        krows = pl.ds(0, n)
    low = lax.broadcasted_iota(jnp.int32, (1, 2 * HEAD_DIM), 1) < HEAD_DIM

    def key_ops(k):
        kr = pltpu.roll(k, HEAD_DIM, 1)
        z = jnp.zeros_like(k)
        return {(0, 0): jnp.where(low, k, z).astype(BF16), (0, 1): jnp.where(low, z, kr).astype(BF16),
                (1, 0): jnp.where(low, kr, z).astype(BF16), (1, 1): jnp.where(low, z, k).astype(BF16)}

    def with_ones(v):
        return jnp.concatenate([v, jnp.ones_like(v)], axis=1)

    def val_ops(v):
        return {key: with_ones(op) for key, op in key_ops(v).items()}

    def half_ops(k):
        z = jnp.zeros_like(k)
        return [jnp.where(low, k, z).astype(BF16), jnp.where(low, z, k).astype(BF16)]

    seqs = range(qc_ref.shape[0])

    c_scores, d_scores = [[] for _ in seqs], [[] for _ in seqs]
    for sq in seqs:
        qc = qc_ref[sq].astype(BF16)
        qd = qd_ref[sq].astype(BF16)
        kc_ops = key_ops(kc_ref[sq, krows, :])
        ck_ops = key_ops(ck_ref[sq]) if n_ctx else None
        for h in range(C_HEADS):
            key = (h // C_GROUP, h % 2)
            q = qc[:, (h // 2) * 128:(h // 2 + 1) * 128]
            sc = [_dot_nt(q, kc_ops[key])]
            if n_ctx:
                sc.append(_dot_nt(q, ck_ops[key]))
            c_scores[sq].append(sc)
        for h in range(D_HEADS):
            q = qd[:, h * 128:(h + 1) * 128]
            kd_ops = half_ops(kd_ref[sq, :, h * 128:(h + 1) * 128])
            dk_ops = half_ops(dk_ref[sq, :, h * 128:(h + 1) * 128]) if n_ctx else None
            for c in range(2):
                sc = [_dot_nt(q, kd_ops[c])]
                if n_ctx:
                    sc.append(_dot_nt(q, dk_ops[c]))
                d_scores[sq].append(sc)

    c_probs, d_probs = [[] for _ in seqs], [[] for _ in seqs]
    for sq in seqs:
        for h in range(C_HEADS):
            sc = c_scores[sq][h]
            if windowed:
                sc[0] = jnp.where(ok, sc[0], NEG_INF)
            c_probs[sq].append(_probs(sc, misc_ref[0:1, h:h + 1]))
        d_probs[sq] = [_probs(sc)[0] for sc in d_scores[sq]]

    lam = misc_ref[1:2, 0:1]
    for sq in seqs:
        vc_ops = val_ops(vc_ref[sq, krows, :])
        cv_ops = val_ops(cv_ref[sq]) if n_ctx else None
        for j in range(C_HEADS // 2):
            pair = None
            for h in (2 * j, 2 * j + 1):
                key = (h // C_GROUP, h % 2)
                ps, m = c_probs[sq][h]
                r = jnp.dot(ps[0], vc_ops[key], preferred_element_type=F32)
                if n_ctx:
                    r = r + jnp.dot(ps[1], cv_ops[key], preferred_element_type=F32)
                o = r[:, :128] / (r[:, 128:] + jnp.exp(misc_ref[0:1, h:h + 1] - m))
                pair = o if pair is None else pair + o
            oc_ref[sq, :, j * 128:(j + 1) * 128] = pair
        for h in range(D_HEADS):
            vsl = slice(h * D_VDIM, (h + 1) * D_VDIM)
            v_op = with_ones(vd_ref[sq, :, vsl]).astype(BF16)
            dv_op = with_ones(dv_ref[sq, :, vsl]).astype(BF16) if n_ctx else None
            parts = []
            for c in range(2):
                ps = d_probs[sq][2 * h + c]
                r = jnp.dot(ps[0], v_op, preferred_element_type=F32)
                if n_ctx:
                    r = r + jnp.dot(ps[1], dv_op, preferred_element_type=F32)
                parts.append(r[:, :D_VDIM] / r[:, D_VDIM:])
            o = parts[0] - lam * parts[1]
            o = o * lax.rsqrt(jnp.mean(o * o, axis=-1, keepdims=True) + EPS) * sub_ref[...] * lam_scale
            od_ref[sq, :, vsl] = o


def _attention(qc, kc, vc, qd, kd, vd, caches, misc, subln, lam_init, tq, windowed):
    bsz, n, _ = qc.shape
    ns = ATTN_SEQS_SHORT if (not windowed and tq == n and bsz % ATTN_SEQS_SHORT == 0) else 1
    qblk = lambda w_: pl.BlockSpec((ns, tq, w_), lambda b, i: (b, i, 0))
    kblk = lambda rows, w_: pl.BlockSpec((ns, rows, w_), lambda b, i: (b, 0, 0))
    in_specs = [qblk(512), kblk(n, 128), kblk(n, 128), qblk(512), kblk(n, 512), kblk(n, 512)]
    args = [qc, kc, vc, qd, kd, vd]
    n_ctx = 0
    if caches is not None:
        n_ctx = caches[0].shape[1]
        in_specs += [kblk(n_ctx, 128), kblk(n_ctx, 128), kblk(n_ctx, 512), kblk(n_ctx, 512)]
        args += list(caches)
    in_specs += [pl.BlockSpec((8, 128), lambda b, i: (0, 0)), pl.BlockSpec((1, D_VDIM), lambda b, i: (0, 0))]
    args += [misc, subln]
    return pl.pallas_call(
        functools.partial(_attn_kernel, windowed=windowed, n_ctx=n_ctx, lam_scale=1.0 - lam_init),
        grid=(bsz // ns, n // tq),
        in_specs=in_specs,
        out_specs=[qblk(512), qblk(512)],
        out_shape=[jax.ShapeDtypeStruct((bsz, n, 512), F32), jax.ShapeDtypeStruct((bsz, n, 512), F32)],
        compiler_params=_cparams(("parallel", "parallel")),
        name="attn_win" if windowed else "attn_ctx",
    )(*args)


def _post_kernel(x_ref, xp_ref, xn_ref, ma_ref, map_ref, man_ref, mb_ref, mbp_ref, mbn_ref,
                 g1_ref, sh_ref, sc_ref, g2_ref, ng_ref, wo_ref, wup_ref, cw_ref, wdn_ref, o_ref, act_ref,
                 *, seq_len):
    rows = x_ref.shape[0]
    ext = rows + 2 * POST_HALO
    half = ma_ref.shape[1]
    xe = jnp.concatenate([xp_ref[...], x_ref[...], xn_ref[...]], axis=0)
    mae = jnp.concatenate([map_ref[...], ma_ref[...], man_ref[...]], axis=0)
    mbe = jnp.concatenate([mbp_ref[...], mb_ref[...], mbn_ref[...]], axis=0)
    x1 = xe + g1_ref[0] * (_dot(mae, wo_ref[:half, :]) + _dot(mbe, wo_ref[half:, :]))
    h = _rms_mod(x1, ng_ref[...], sh_ref[0], sc_ref[0]).astype(BF16)
    x1 = x1[POST_HALO:POST_HALO + rows]
    row0 = pl.program_id(0) * rows - POST_HALO
    pos = (row0 + lax.broadcasted_iota(jnp.int32, (ext, 1), 0)) % seq_len
    first = pos == 0
    last = pos == seq_len - 1
    for c in range(FF_CHUNKS):
        cs = slice(c * FF_CHUNK, (c + 1) * FF_CHUNK)
        a = jnp.dot(h, wup_ref[:, cs], preferred_element_type=F32)
        b = jnp.dot(h, wup_ref[:, D_FF + c * FF_CHUNK:D_FF + (c + 1) * FF_CHUNK], preferred_element_type=F32)
        am = jnp.where(first, 0.0, pltpu.roll(a, 1, 0))
        ap = jnp.where(last, 0.0, pltpu.roll(a, ext - 1, 0))
        a = am * cw_ref[0:1, cs] + a * cw_ref[1:2, cs] + ap * cw_ref[2:3, cs] + cw_ref[3:4, cs]
        act_ref[:, cs] = (_silu(a) * b)[POST_HALO:POST_HALO + rows].astype(BF16)
    ffn = jnp.dot(act_ref[...], wdn_ref[...], preferred_element_type=F32)
    o_ref[...] = x1 + g2_ref[0] * ffn


def _post(x, mix_a, mix_b, g1, sh2, sc2, g2, norm_g, w_out, wup, cw, wdn, layer):
    bsz, seq_len, d = x.shape
    half = mix_a.shape[-1]
    rows = POST_ROWS
    total = bsz * seq_len
    nhalo = total // POST_HALO
    per_seq = g1.shape[0] > 1
    midx = (lambda i: ((i * rows) // seq_len, 0, 0)) if per_seq else (lambda i: (0, 0, 0))
    mspec = pl.BlockSpec((1, 1, d), midx)
    pidx = lambda i: (jnp.maximum(i * (rows // POST_HALO) - 1, 0), 0)
    nidx = lambda i: (jnp.minimum((i + 1) * (rows // POST_HALO), nhalo - 1), 0)
    trio = lambda w_: [pl.BlockSpec((rows, w_), lambda i: (i, 0)), pl.BlockSpec((POST_HALO, w_), pidx),
                       pl.BlockSpec((POST_HALO, w_), nidx)]
    whole = lambda shape: pl.BlockSpec(shape, lambda i: (0,) * len(shape), pipeline_mode=pl.Buffered(1))
    x2 = x.reshape(total, d)
    a2 = mix_a.reshape(total, half)
    b2 = mix_b.reshape(total, half)
    out = pl.pallas_call(
        functools.partial(_post_kernel, seq_len=seq_len),
        grid=(total // rows,),
        in_specs=trio(d) + trio(half) + trio(half) + [
            mspec, mspec, mspec, mspec,
            pl.BlockSpec((1, d), lambda i: (0, 0)),
            whole((d, d)),
            pl.BlockSpec((None, d, 2 * D_FF), lambda i: (layer, 0, 0), pipeline_mode=pl.Buffered(1)),
            whole((8, D_FF)),
            pl.BlockSpec((None, D_FF, d), lambda i: (layer, 0, 0), pipeline_mode=pl.Buffered(1))],
        out_specs=pl.BlockSpec((rows, d), lambda i: (i, 0)),
        out_shape=jax.ShapeDtypeStruct((total, d), F32),
        scratch_shapes=[pltpu.VMEM((rows, D_FF), BF16)],
        compiler_params=_cparams(("parallel",)),
        name="post_ffn",
    )(x2, x2, x2, a2, a2, a2, b2, b2, b2, g1, sh2, sc2, g2, norm_g, w_out, wup, cw, wdn)
    return out.reshape(bsz, seq_len, d)


def _ffn_weights(p, l):
    cw = jnp.zeros((8, D_FF), F32).at[:3].set(p['ffn_conv_w'][l]).at[3].set(p['ffn_conv_b'][l])
    return p['ffn_up'].astype(BF16), cw, p['ffn_down'].astype(BF16)


def _lambda_init(layer):
    return 0.8 - 0.6 * math.exp(-0.3 * layer)


def _trunk(x, mods, p, states, caches):
    bsz, n, d = x.shape
    nseg = n // S5_SEG
    depth = p['w_mod'].shape[0]
    news = {k: [] for k in ('s5r', 's5i', 'gdn', 'ck', 'cv', 'dk', 'dv')}
    for l in range(depth):
        j = l // 2
        sh1, sc1, g1, sh2, sc2, g2 = mods[l]
        ng1 = p['norm1_g'][l][None]
        if l % 2 == 0:
            w_in = p['w_in_ab'][j]
            w_gate = jnp.zeros((d, 128), BF16).at[:, :w_in.shape[1] - AB_MAIN].set(
                w_in[:, AB_MAIN:].astype(BF16))
            rows = bsz * nseg
            per_row = lambda m: jnp.repeat(m, nseg, axis=0) if m.shape[0] > 1 else m
            u_t, qkv, z, ab = _proj_ab(x.reshape(rows, S5_SEG, d), ng1, per_row(sh1), per_row(sc1),
                                       w_in[:, :AB_MAIN].astype(BF16), w_gate)
            qkv, z, ab = (t.reshape(bsz, n, t.shape[-1]) for t in (qkv, z, ab))
            if states is None:
                h0r = jnp.zeros((2, bsz, S5_GROUPS * S5_STATE), F32)
                h0i = h0r
                s0 = None
            else:
                h0r = states[0][:, j].reshape(bsz, 2, -1).transpose(1, 0, 2)
                h0i = states[1][:, j].reshape(bsz, 2, -1).transpose(1, 0, 2)
                s0 = states[2][:, j]
            ya, fr, fi = _s5_mixer(u_t, p, j, h0r, h0i, bsz, nseg)
            yb, sg = _gdn_mixer(qkv, z, ab, p, j, s0)
            mix = (ya.reshape(bsz, n, S5_WIDTH), yb)
            w_out = p['w_out_ab'][j]
            news['s5r'].append(fr.transpose(1, 0, 2).reshape(bsz, 2, S5_GROUPS, S5_STATE))
            news['s5i'].append(fi.transpose(1, 0, 2).reshape(bsz, 2, S5_GROUPS, S5_STATE))
            news['gdn'].append(sg)
        else:
            lam_init = _lambda_init(l)
            f = lambda name: p[name][j]
            lam = (jnp.exp(jnp.sum(f('d_lq1') * f('d_lk1'))) - jnp.exp(jnp.sum(f('d_lq2') * f('d_lk2')))
                   + lam_init)
            misc = jnp.zeros((8, 128), F32).at[0, :C_HEADS].set(p['c_sink'][j]).at[1, :].set(lam)
            rope = caches is not None
            qc, kc, vc, qd, kd, vd, *new_caches = _proj_cd(
                x, ng1, sh1, sc1, p['w_in_cd'][j].astype(BF16), p, j, rope, emit_caches=caches is None)
            if caches is None:
                mix = _attention(qc, kc, vc, qd, kd, vd, None, misc, p['d_subln'][j][None],
                                 lam_init, n, False)
            else:
                n_ctx = caches[0].shape[2]
                cc = (caches[0][:, j].reshape(bsz, n_ctx, 128), caches[1][:, j].reshape(bsz, n_ctx, 128),
                      caches[2][:, j].reshape(bsz, n_ctx, 512), caches[3][:, j].reshape(bsz, n_ctx, 512))
                mix = _attention(qc, kc, vc, qd, kd, vd, cc, misc, p['d_subln'][j][None],
                                 lam_init, Q_BLOCK, True)
            w_out = p['w_out_cd'][j]
            for name, arr in zip(('ck', 'cv', 'dk', 'dv'), new_caches):
                news[name].append(arr)
        wup, cw, wdn = _ffn_weights(p, l)
        x = _post(x, mix[0], mix[1], g1, sh2, sc2, g2, p['norm2_g'][l][None], w_out.astype(BF16),
                  wup, cw, wdn, l)
    return x, news


def kernel(x_prompt, x_sample, c, state_s5_re, state_s5_im, state_gdn, cache_c_k, cache_c_v, cache_d_k, cache_d_v, c_ctx, w_mod, b_mod, norm1_g, norm2_g, w_in_ab, w_out_ab, s5_lam_re, s5_lam_im, s5_log_dt, s5_b_re, s5_b_im, s5_c_re, s5_c_im, s5_d, s5_w_glu, s5_b_glu, gdn_conv_w, gdn_a_log, gdn_dt_bias, gdn_norm_g, w_in_cd, w_out_cd, c_qn, c_kn, c_sink, d_qn, d_kn, d_lq1, d_lk1, d_lq2, d_lk2, d_subln, ffn_up, ffn_conv_w, ffn_conv_b, ffn_down):
    p = dict(w_mod=w_mod, b_mod=b_mod, norm1_g=norm1_g, norm2_g=norm2_g, w_in_ab=w_in_ab, w_out_ab=w_out_ab,
             s5_lam_re=s5_lam_re, s5_lam_im=s5_lam_im, s5_log_dt=s5_log_dt, s5_b_re=s5_b_re, s5_b_im=s5_b_im,
             s5_c_re=s5_c_re, s5_c_im=s5_c_im, s5_d=s5_d, s5_w_glu=s5_w_glu, s5_b_glu=s5_b_glu,
             gdn_conv_w=gdn_conv_w, gdn_a_log=gdn_a_log, gdn_dt_bias=gdn_dt_bias, gdn_norm_g=gdn_norm_g,
             w_in_cd=w_in_cd, w_out_cd=w_out_cd, c_qn=c_qn, c_kn=c_kn, c_sink=c_sink, d_qn=d_qn, d_kn=d_kn,
             d_lq1=d_lq1, d_lk1=d_lk1, d_lq2=d_lq2, d_lk2=d_lk2, d_subln=d_subln,
             ffn_up=ffn_up, ffn_conv_w=ffn_conv_w, ffn_conv_b=ffn_conv_b, ffn_down=ffn_down)
    depth = w_mod.shape[0]
    n_dec = c.shape[0]
    mod = _modulation(jnp.concatenate([c_ctx[None], c], axis=0), w_mod, b_mod)
    split6 = lambda m: [m[:, None, k * D_MODEL:(k + 1) * D_MODEL] for k in range(6)]
    mods_ctx = [split6(mod[l, 0:1]) for l in range(depth)]
    mods_dec = [split6(mod[l, 1:1 + n_dec]) for l in range(depth)]

    y_prompt, nw = _trunk(x_prompt, mods_ctx, p, None, None)
    y_sample, _ = _trunk(x_sample, mods_dec, p, (state_s5_re, state_s5_im, state_gdn),
                         (cache_c_k, cache_c_v, cache_d_k, cache_d_v))
    st = lambda name: jnp.stack(nw[name], axis=1)
    return (y_prompt, y_sample, st('s5r'), st('s5i'), st('gdn'), st('ck'), st('cv'), st('dk'), st('dv'))
```

```python
import functools
import math

import jax
import jax.numpy as jnp
from jax import lax
from jax.experimental import pallas as pl
from jax.experimental.pallas import tpu as pltpu

F32 = jnp.float32
BF16 = jnp.bfloat16

D_MODEL = 1024
GRID_W = 64
EPS = 1e-6
NEG_INF = -1e30

S5_WIDTH = 512
S5_GROUP = 16
S5_GROUPS = 32
S5_STATE = 64
S5_TILE_GROUPS = 8
S5_TILE_CH = S5_TILE_GROUPS * S5_GROUP
S5_TILE_ST = S5_TILE_GROUPS * S5_STATE
S5_TILES = S5_GROUPS // S5_TILE_GROUPS
S5_SEG = 256
S5_SUB = 32
PROJ_AB_TOKENS = 64
S5_ROWS = 8

GDN_DK = 128
GDN_DV = 128
GDN_HEADS = 4
GDN_WIDTH = 512
GDN_CHUNK = 64
GDN_GROUP = 4
GDN_SHORT_LEN = 256
GDN_SEQS_SHORT = 2
GDN_EXACT_ROUNDS = 3

HEAD_DIM = 64
C_HEADS = 8
C_KV_HEADS = 2
C_GROUP = 4
WINDOW = 128
Q_BLOCK = 128
D_HEADS = 4
D_VDIM = 128
ATTN_SCALE = HEAD_DIM ** -0.5
ATTN_SEQS_SHORT = 2
ROPE_THETA = 10000.0

D_FF = 2816
FF_CHUNK = 256
FF_CHUNKS = D_FF // FF_CHUNK
POST_ROWS = 512
POST_HALO = 8

AB_MAIN = S5_WIDTH + 4 * GDN_WIDTH
AB_PAD = AB_MAIN + 128
CD_IN = 2304

MOD_TK = 128
MOD_LANES = 512
VMEM_LIMIT = 56 * 1024 * 1024


def _cparams(sem):
    return pltpu.CompilerParams(dimension_semantics=sem, vmem_limit_bytes=VMEM_LIMIT)


def _sigmoid(x):
    return 1.0 / (1.0 + jnp.exp(-x))


def _silu(x):
    return x * _sigmoid(x)


def _softplus(x):
    return jnp.maximum(x, 0.0) + jnp.log(1.0 + jnp.exp(-jnp.abs(x)))


def _gelu_tanh(x):
    return 0.5 * x * (1.0 + jnp.tanh(math.sqrt(2.0 / math.pi) * (x + 0.044715 * (x * x * x))))


def _rms_mod(x, g, shift, scale):
    y = x * lax.rsqrt(jnp.mean(x * x, axis=-1, keepdims=True) + EPS)
    return (y * g) * (1.0 + scale) + shift


def _dot(a, b):
    return jnp.dot(a.astype(BF16), b.astype(BF16), preferred_element_type=F32)


def _dot_nt(a, b):
    return lax.dot_general(a.astype(BF16), b.astype(BF16), (((1,), (1,)), ((), ())),
                           preferred_element_type=F32)


def _hi_lo(a):
    hi = a.astype(BF16)
    return hi, (a - hi.astype(F32)).astype(BF16)


def _mod_kernel(ct_ref, w_ref, b_ref, o_ref, acc_ref, *, n_rows):
    k = pl.program_id(1)

    @pl.when(k == 0)
    def _():
        acc_ref[...] = jnp.zeros_like(acc_ref)

    tk, n_out = w_ref.shape[1:]
    s = _silu(ct_ref[...])
    sb = [[jnp.broadcast_to(s[g * 8:(g + 1) * 8, m:m + 1], (8, MOD_LANES)) for g in range(tk // 8)]
          for m in range(n_rows)]
    for c in range(n_out // MOD_LANES):
        cols = slice(c * MOD_LANES, (c + 1) * MOD_LANES)
        accs = [acc_ref[m, :, cols] for m in range(n_rows)]
        for g in range(tk // 8):
            wg = w_ref[0, g * 8:(g + 1) * 8, cols]
            accs = [a + wg * sb[m][g] for m, a in enumerate(accs)]
        for m in range(n_rows):
            acc_ref[m, :, cols] = accs[m]

    @pl.when(k == pl.num_programs(1) - 1)
    def _():
        o_ref[0] = jnp.zeros(o_ref.shape[1:], F32)
        for m in range(n_rows):
            o_ref[0, m:m + 1, :] = jnp.sum(acc_ref[m], axis=0, keepdims=True) + b_ref[0]


def _modulation(cvecs, w_mod, b_mod):
    n, d = cvecs.shape
    depth, _, n_out = w_mod.shape
    ct = jnp.zeros((d, 8), F32).at[:, :n].set(cvecs.T)
    return pl.pallas_call(
        functools.partial(_mod_kernel, n_rows=n),
        grid=(depth, d // MOD_TK),
        in_specs=[pl.BlockSpec((MOD_TK, 8), lambda l, k: (k, 0)),
                  pl.BlockSpec((1, MOD_TK, n_out), lambda l, k: (l, k, 0)),
                  pl.BlockSpec((1, 1, n_out), lambda l, k: (l, 0, 0))],
        out_specs=pl.BlockSpec((1, 8, n_out), lambda l, k: (l, 0, 0)),
        out_shape=jax.ShapeDtypeStruct((depth, 8, n_out), F32),
        scratch_shapes=[pltpu.VMEM((n, 8, n_out), F32)],
        compiler_params=_cparams(("parallel", "arbitrary")),
        name="adaln_mod",
    )(ct, w_mod, b_mod.reshape(depth, 1, n_out))


def _proj_ab_kernel(x_ref, g_ref, sh_ref, sc_ref, w_ref, wg_ref, u_ref, qkv_ref, z_ref, ab_ref):
    ns, tm, d = x_ref.shape
    h = _rms_mod(x_ref[...], g_ref[...], sh_ref[...], sc_ref[...])
    h = h.reshape(ns * tm, d).astype(BF16)
    y = jnp.dot(h, w_ref[...], preferred_element_type=F32)
    for s in range(ns):
        u_ref[:, s, :] = y[s * tm:(s + 1) * tm, :S5_WIDTH]
    qkv_ref[...] = y[:, S5_WIDTH:S5_WIDTH + 3 * GDN_WIDTH].reshape(ns, tm, 3 * GDN_WIDTH)
    z_ref[...] = y[:, S5_WIDTH + 3 * GDN_WIDTH:AB_MAIN].reshape(ns, tm, GDN_WIDTH)
    ab_ref[...] = jnp.dot(h, wg_ref[...], preferred_element_type=F32).reshape(ns, tm, 128)


def _proj_ab(x, g, shift, scale, w, w_gate):
    rows, n, d = x.shape
    tm = PROJ_AB_TOKENS
    per_row = shift.shape[0] > 1
    mspec = pl.BlockSpec((S5_ROWS, 1, d), lambda r, i: (r, 0, 0)) if per_row else \
        pl.BlockSpec((1, 1, d), lambda r, i: (0, 0, 0))
    blk = lambda w_: pl.BlockSpec((S5_ROWS, tm, w_), lambda r, i: (r, i, 0))
    return pl.pallas_call(
        _proj_ab_kernel,
        grid=(rows // S5_ROWS, n // tm),
        in_specs=[blk(d),
                  pl.BlockSpec((1, d), lambda r, i: (0, 0)),
                  mspec, mspec,
                  pl.BlockSpec((d, AB_MAIN), lambda r, i: (0, 0)),
                  pl.BlockSpec((d, 128), lambda r, i: (0, 0))],
        out_specs=[pl.BlockSpec((tm, S5_ROWS, S5_WIDTH), lambda r, i: (i, r, 0)),
                   blk(3 * GDN_WIDTH), blk(GDN_WIDTH), blk(128)],
        out_shape=[jax.ShapeDtypeStruct((n, rows, S5_WIDTH), F32),
                   jax.ShapeDtypeStruct((rows, n, 3 * GDN_WIDTH), F32),
                   jax.ShapeDtypeStruct((rows, n, GDN_WIDTH), F32),
                   jax.ShapeDtypeStruct((rows, n, 128), F32)],
        compiler_params=_cparams(("parallel", "parallel")),
        name="proj_ab",
    )(x, g, shift, scale, w, w_gate)


def _s5_kernel(u_ref, bm_ref, cm_ref, a_ref, h0r_ref, h0i_ref, *rest, want_y):
    if want_y:
        y_ref, fr_ref, fi_ref, xs_ref = rest
    else:
        fr_ref, fi_ref, xs_ref = rest
    n = u_ref.shape[0]
    sub = S5_SUB
    nsub = n // sub
    st = S5_TILE_ST

    def x_proj(d, k):
        u2 = u_ref[k * sub:(k + 1) * sub].reshape(sub * S5_ROWS, S5_TILE_CH)
        xs_ref[d, k * sub:(k + 1) * sub] = _dot(u2, bm_ref[d, 0]).reshape(sub, S5_ROWS, 2 * st)

    ar = [jnp.broadcast_to(a_ref[d, 0, 0:1, :], (S5_ROWS, st)) for d in range(2)]
    ai = [jnp.broadcast_to(a_ref[d, 0, 1:2, :], (S5_ROWS, st)) for d in range(2)]
    hr = [h0r_ref[0], h0r_ref[1]]
    hi = [h0i_ref[0], h0i_ref[1]]
    x_proj(0, 0)
    x_proj(1, nsub - 1)
    written = set()
    for k in range(nsub):
        ks = (k, nsub - 1 - k)
        if k + 1 < nsub:
            x_proj(0, ks[0] + 1)
            x_proj(1, ks[1] - 1)
        for t in range(sub):
            for d in range(2):
                tt = ks[d] * sub + (t if d == 0 else sub - 1 - t)
                x = xs_ref[d, tt]
                nr = ar[d] * hr[d] - ai[d] * hi[d] + x[:, :st]
                ni = ar[d] * hi[d] + ai[d] * hr[d] + x[:, st:]
                xs_ref[d, tt] = jnp.concatenate([nr, ni], axis=-1)
                hr[d], hi[d] = nr, ni
        if want_y:
            for d in range(2):
                rows = slice(ks[d] * sub, (ks[d] + 1) * sub)
                hs = xs_ref[d, rows].reshape(sub * S5_ROWS, 2 * st)
                yv = _dot(hs, cm_ref[d, 0]).reshape(sub, S5_ROWS, S5_TILE_CH)
                if ks[d] in written:
                    y_ref[rows] += yv
                else:
                    y_ref[rows] = yv
                    written.add(ks[d])
    for d in range(2):
        fr_ref[d] = hr[d]
        fi_ref[d] = hi[d]


def _s5_scan(u_t, bmat, cmat, amat, h0r, h0i, want_y=True):
    n, rows, _ = u_t.shape
    state = lambda: pl.BlockSpec((2, S5_ROWS, S5_TILE_ST), lambda r, j: (0, r, j))
    st_shape = jax.ShapeDtypeStruct((2, rows, S5_GROUPS * S5_STATE), F32)
    out_specs = [state(), state()]
    out_shape = [st_shape, st_shape]
    if want_y:
        out_specs.insert(0, pl.BlockSpec((n, S5_ROWS, S5_TILE_CH), lambda r, j: (0, r, j)))
        out_shape.insert(0, jax.ShapeDtypeStruct((n, rows, S5_WIDTH), F32))
    return pl.pallas_call(
        functools.partial(_s5_kernel, want_y=want_y),
        grid=(rows // S5_ROWS, S5_TILES),
        in_specs=[pl.BlockSpec((n, S5_ROWS, S5_TILE_CH), lambda r, j: (0, r, j)),
                  pl.BlockSpec((2, 1, S5_TILE_CH, 2 * S5_TILE_ST), lambda r, j: (0, j, 0, 0)),
                  pl.BlockSpec((2, 1, 2 * S5_TILE_ST, S5_TILE_CH), lambda r, j: (0, j, 0, 0)),
                  pl.BlockSpec((2, 1, 8, S5_TILE_ST), lambda r, j: (0, j, 0, 0)),
                  state(), state()],
        out_specs=out_specs,
        out_shape=out_shape,
        scratch_shapes=[pltpu.VMEM((2, n, S5_ROWS, 2 * S5_TILE_ST), F32)],
        compiler_params=_cparams(("parallel", "parallel")),
        name="s5_scan" if want_y else "s5_states",
    )(u_t, bmat, cmat, amat, h0r, h0i)


def _s5_params(p, j):
    lam_re, lam_im, log_dt = p['s5_lam_re'][j], p['s5_lam_im'][j], p['s5_log_dt'][j]
    dt = jnp.exp(log_dt)[..., None]
    mag = jnp.exp(lam_re * dt)
    ar, ai = mag * jnp.cos(lam_im * dt), mag * jnp.sin(lam_im * dt)
    den = lam_re * lam_re + lam_im * lam_im
    fr = ((ar - 1.0) * lam_re + ai * lam_im) / den
    fi = (ai * lam_re - (ar - 1.0) * lam_im) / den
    b_re, b_im = p['s5_b_re'][j], p['s5_b_im'][j]
    bbr = fr[..., None] * b_re - fi[..., None] * b_im
    bbi = fr[..., None] * b_im + fi[..., None] * b_re
    eye = jnp.eye(S5_TILE_GROUPS, dtype=F32)

    def in_blocks(t):
        t = t.reshape(2, S5_TILES, S5_TILE_GROUPS, S5_STATE, S5_GROUP)
        t = jnp.einsum('dtgpc,gh->dtgchp', t, eye)
        return t.reshape(2, S5_TILES, S5_TILE_CH, S5_TILE_ST)

    def out_blocks(t):
        t = t.reshape(2, S5_TILES, S5_TILE_GROUPS, S5_GROUP, S5_STATE)
        t = jnp.einsum('dtgcp,gh->dtgphc', t, eye)
        return t.reshape(2, S5_TILES, S5_TILE_ST, S5_TILE_CH)

    bmat = jnp.concatenate([in_blocks(bbr), in_blocks(bbi)], axis=-1).astype(BF16)
    cmat = jnp.concatenate([out_blocks(p['s5_c_re'][j]), -out_blocks(p['s5_c_im'][j])], axis=-2).astype(BF16)
    seg_mag = jnp.exp(lam_re * dt * S5_SEG)
    pr, pi = seg_mag * jnp.cos(lam_im * dt * S5_SEG), seg_mag * jnp.sin(lam_im * dt * S5_SEG)
    flat = lambda t: t.reshape(2, S5_TILES, 1, S5_TILE_ST)
    amat = jnp.concatenate([flat(ar), flat(ai), flat(pr), flat(pi),
                            jnp.zeros((2, S5_TILES, 4, S5_TILE_ST), F32)], axis=2)
    return bmat, cmat, amat


def _s5_glu_kernel(y_ref, u_ref, d_ref, w_ref, b_ref, o_ref):
    tm, ns, _ = y_ref.shape
    y = jnp.concatenate([y_ref[:, s, :] + d_ref[...] * u_ref[:, s, :] for s in range(ns)], axis=0)
    g = _gelu_tanh(y)
    out = g * _sigmoid(_dot(g, w_ref[...]) + b_ref[...])
    o_ref[...] = out.reshape(ns, tm, S5_WIDTH)


def _s5_glu(y_t, u_t, s5_d, w_glu, b_glu):
    n, rows, _ = y_t.shape
    tm = PROJ_AB_TOKENS
    tblk = pl.BlockSpec((tm, S5_ROWS, S5_WIDTH), lambda r, i: (i, r, 0))
    return pl.pallas_call(
        _s5_glu_kernel,
        grid=(rows // S5_ROWS, n // tm),
        in_specs=[tblk, tblk,
                  pl.BlockSpec((1, S5_WIDTH), lambda r, i: (0, 0)),
                  pl.BlockSpec((S5_WIDTH, S5_WIDTH), lambda r, i: (0, 0)),
                  pl.BlockSpec((1, S5_WIDTH), lambda r, i: (0, 0))],
        out_specs=pl.BlockSpec((S5_ROWS, tm, S5_WIDTH), lambda r, i: (r, i, 0)),
        out_shape=jax.ShapeDtypeStruct((rows, n, S5_WIDTH), F32),
        compiler_params=_cparams(("parallel", "parallel")),
        name="s5_glu",
    )(y_t, u_t, s5_d, w_glu, b_glu)


def _s5_mixer(u3, p, j, h0r, h0i, bsz, nseg):
    rows = bsz * nseg
    bmat, cmat, amat = _s5_params(p, j)
    if nseg == 1:
        y_t, fr, fi = _s5_scan(u3, bmat, cmat, amat, h0r, h0i)
    else:
        zero = jnp.zeros((2, bsz, nseg, S5_GROUPS * S5_STATE), F32)
        first = jnp.array([0, nseg - 1])
        seed = lambda h0: zero.at[jnp.arange(2), :, first].set(h0).reshape(2, rows, -1)
        fr, fi = _s5_scan(u3, bmat, cmat, amat, seed(h0r), seed(h0i), want_y=False)
        fr = fr.reshape(2, bsz, nseg, -1)
        fi = fi.reshape(2, bsz, nseg, -1)
        pr = amat[:, :, 2].reshape(2, 1, -1)
        pi = amat[:, :, 3].reshape(2, 1, -1)

        def chain(dr, order):
            hr, hi = (h0r[dr], h0i[dr])
            outs_r, outs_i = {}, {}
            for n_done, k in enumerate(order):
                outs_r[k], outs_i[k] = hr, hi
                if n_done == 0:
                    hr, hi = fr[dr, :, k], fi[dr, :, k]
                else:
                    hr, hi = (pr[dr] * hr - pi[dr] * hi + fr[dr, :, k],
                              pr[dr] * hi + pi[dr] * hr + fi[dr, :, k])
            st = lambda o: jnp.stack([o[k] for k in range(nseg)], axis=1)
            return st(outs_r), st(outs_i), hr, hi

        sr0, si0, er0, ei0 = chain(0, list(range(nseg)))
        sr1, si1, er1, ei1 = chain(1, list(range(nseg - 1, -1, -1)))
        start_r = jnp.stack([sr0, sr1]).reshape(2, rows, -1)
        start_i = jnp.stack([si0, si1]).reshape(2, rows, -1)
        y_t, _, _ = _s5_scan(u3, bmat, cmat, amat, start_r, start_i)
        fr = jnp.stack([er0, er1])
        fi = jnp.stack([ei0, ei1])
    ya = _s5_glu(y_t, u3, p['s5_d'][j][None], p['s5_w_glu'][j].astype(BF16), p['s5_b_glu'][j][None])
    return ya, fr, fi


def _gdn_kernel(qkv_ref, z_ref, ab_ref, cw_ref, gp_ref, ng_ref, *rest, zero_init):
    if zero_init:
        s0_ref = None
        o_ref, sf_ref, q_s, k_s, v_s, gate_s, of_s, ob_s, st_s, uw_s, a_s, kgt_s = rest
    else:
        s0_ref, o_ref, sf_ref, q_s, k_s, v_s, gate_s, of_s, ob_s, st_s, uw_s, a_s, kgt_s = rest
    ns, n = qkv_ref.shape[:2]
    nc = n // GDN_CHUNK
    row = lax.broadcasted_iota(jnp.int32, (n, 1), 0)

    for sq in range(ns):
        for blk in range(3 * GDN_HEADS):
            cols = slice(blk * GDN_DK, (blk + 1) * GDN_DK)
            hs = slice((blk % GDN_HEADS) * GDN_DK, (blk % GDN_HEADS + 1) * GDN_DK)
            x = qkv_ref[sq, :, cols]
            xm = jnp.where(row == 0, 0.0, pltpu.roll(x, 1, 0))
            xp = jnp.where(row == n - 1, 0.0, pltpu.roll(x, n - 1, 0))
            y = _silu(xm * cw_ref[0:1, cols] + x * cw_ref[1:2, cols] + xp * cw_ref[2:3, cols])
            if blk < GDN_HEADS:
                q_s[sq, :, hs] = y * lax.rsqrt(jnp.sum(y * y, axis=-1, keepdims=True) + EPS) * (GDN_DK ** -0.5)
            elif blk < 2 * GDN_HEADS:
                k_s[sq, :, hs] = y * lax.rsqrt(jnp.sum(y * y, axis=-1, keepdims=True) + EPS)
            else:
                v_s[sq, :, hs] = y

        ab = ab_ref[sq]
        beta = _sigmoid(ab)
        g = -jnp.exp(gp_ref[0:1, :]) * _softplus(ab + gp_ref[1:2, :])
        pos = row % GDN_CHUNK
        pre, suf = g, g
        sft = 1
        while sft < GDN_CHUNK:
            pre = pre + jnp.where(pos >= sft, pltpu.roll(pre, sft, 0), 0.0)
            suf = suf + jnp.where(pos < GDN_CHUNK - sft, pltpu.roll(suf, n - sft, 0), 0.0)
            sft *= 2
        gate_s[sq, 0] = beta
        gate_s[sq, 1] = pre
        gate_s[sq, 2] = suf

    st_s[...] = jnp.zeros_like(st_s) if zero_init else s0_ref[...]
    cs = GDN_CHUNK
    pk = GDN_HEADS * cs
    ri = lax.broadcasted_iota(jnp.int32, (cs, pk), 0)
    lane_pk = lax.broadcasted_iota(jnp.int32, (cs, pk), 1)
    ci = lane_pk % cs
    eye_pk = (ri == ci).astype(F32)
    head_pk = [(lax.broadcasted_iota(jnp.int32, (1, pk), 1) // cs) == h for h in range(GDN_HEADS)]
    head_w = [(lax.broadcasted_iota(jnp.int32, (1, GDN_WIDTH), 1) // GDN_DK) == h for h in range(GDN_HEADS)]

    def block_diag(p):
        return jnp.concatenate([jnp.where(m, p, jnp.zeros_like(p)) for m in head_pk], axis=0)

    def dot3_bd(a, p):
        a_hi, a_lo = _hi_lo(a)
        p_hi, p_lo = _hi_lo(p)
        b_hi, b_lo = block_diag(p_hi), block_diag(p_lo)
        mm = functools.partial(jnp.dot, preferred_element_type=F32)
        return mm(a_hi, b_hi) + (mm(a_lo, b_hi) + mm(a_hi, b_lo))

    def dot1_bd(a, p):
        return jnp.dot(a.astype(BF16), block_diag(p.astype(BF16)), preferred_element_type=F32)

    def lanes(cols, width):
        return jnp.concatenate([jnp.broadcast_to(c, (cs, width)) for c in cols], axis=1)

    def phase_a(sq, it):
        st = []
        for cc in range(GDN_GROUP):
            rows = pl.ds(pl.multiple_of((it * GDN_GROUP + cc) * cs, cs), cs)
            q_all, k_all, v_all = q_s[sq, rows, :], k_s[sq, rows, :], v_s[sq, rows, :]
            beta_blk = gate_s[sq, 0, rows, :]
            for dr in range(2):
                gc_blk = gate_s[sq, 1 + dr, rows, :]
                lane0 = dr * GDN_HEADS
                bcols = [beta_blk[:, 8 + lane0 + h:9 + lane0 + h] for h in range(GDN_HEADS)]
                gcols = [gc_blk[:, lane0 + h:lane0 + h + 1] for h in range(GDN_HEADS)]
                st.append(dict(sq=sq, dr=dr, rows=rows, cidx=it * GDN_GROUP + cc, q=q_all, k=k_all, v=v_all, bcols=bcols, gcols=gcols,
                               incl=(ri >= ci) if dr == 0 else (ri <= ci),
                               strict=(ri > ci) if dr == 0 else (ri < ci)))
        for t in st:
            b_w = lanes(t['bcols'], GDN_DK)
            t['kb'] = t['k'] * b_w
            t['vb'] = t['v'] * b_w
            k_bd = jnp.concatenate([jnp.where(m, t['k'], 0.0) for m in head_w], axis=0)
            t['kq'] = _dot_nt(jnp.concatenate([t['kb'], t['q']], axis=0), k_bd)
        for t in st:
            gcol = lanes(t['gcols'], cs)
            grow = jnp.sum(eye_pk * gcol, axis=0, keepdims=True)
            decay = jnp.exp(jnp.where(t['incl'], gcol - grow, NEG_INF))
            t['pw'] = jnp.where(t['strict'], t['kq'][:cs] * decay, 0.0)
            a_s[t['sq'], t['dr'], t['rows'], :] =jnp.where(t['incl'], t['kq'][cs:] * decay, 0.0)
            t['tm'] = eye_pk - t['pw']
        for t in st:
            t['pw'] = dot3_bd(t['pw'], t['pw'])
        for rnd in range(5):
            mm_bd = dot3_bd if rnd < GDN_EXACT_ROUNDS else dot1_bd
            for t in st:
                if rnd < 4:
                    r = mm_bd(jnp.concatenate([t['pw'], t['tm']], axis=0), t['pw'])
                    t['pw'] = r[:cs]
                    t['tm'] = t['tm'] + r[cs:]
                else:
                    t['tm'] = t['tm'] + mm_bd(t['tm'], t['pw'])
        for t in st:
            for h in range(GDN_HEADS):
                g = t['gcols'][h]
                gl = g[cs - 1:cs] if t['dr'] == 0 else g[0:1]
                kg = t['k'][:, h * GDN_DK:(h + 1) * GDN_DK] * jnp.exp(gl - g)
                kgt_s[t['sq'], t['dr'], t['cidx'], h] = kg.T.astype(BF16)
        for t in st:
            kbg = t['kb'] * lanes([jnp.exp(g) for g in t['gcols']], GDN_DK)
            for h in range(GDN_HEADS):
                hs = slice(h * GDN_DK, (h + 1) * GDN_DK)
                rhs = jnp.concatenate([t['vb'][:, hs], kbg[:, hs]], axis=1)
                uw_s[t['sq'], t['dr'], t['rows'], 2 * h * GDN_DK:2 * (h + 1) * GDN_DK] = _dot(
                    t['tm'][:, h * cs:(h + 1) * cs], rhs)

    for sq in range(ns):
        if nc // GDN_GROUP == 1:
            phase_a(sq, 0)
        else:
            lax.fori_loop(0, nc // GDN_GROUP, lambda it, c, sq=sq: (phase_a(sq, it), c)[1], 0)

    def phase_b(c, carry):
        ch = []
        for sq, dr in [(sq, dr) for sq in range(ns) for dr in range(2)]:
            cidx = c if dr == 0 else nc - 1 - c
            rows = pl.ds(pl.multiple_of(cidx * cs, cs), cs)
            gc_blk = gate_s[sq, 1 + dr, rows, :]
            for h in range(GDN_HEADS):
                hs = slice(h * GDN_DK, (h + 1) * GDN_DK)
                lane = dr * GDN_HEADS + h
                gcol = gc_blk[:, lane:lane + 1]
                ch.append(dict(sq=sq, dr=dr, h=h, rows=rows, hs=hs, gcol=gcol, cidx=cidx,
                               gl=gcol[cs - 1:cs] if dr == 0 else gcol[0:1],
                               uw=uw_s[sq, dr, rows, 2 * h * GDN_DK:2 * (h + 1) * GDN_DK],
                               amat=a_s[sq, dr, rows, h * cs:(h + 1) * cs]))
        for t in ch:
            t['s'] = st_s[t['sq'], t['dr'], t['h']]
            qg = q_s[t['sq'], t['rows'], t['hs']] * jnp.exp(t['gcol'])
            t['ws'] = _dot(jnp.concatenate([t['uw'][:, GDN_DV:], qg], axis=0), t['s'])
        for t in ch:
            vn = t['uw'][:, :GDN_DV] - t['ws'][:cs]
            o = t['ws'][cs:] + _dot(t['amat'], vn)
            kgt = kgt_s[t['sq'], t['dr'], t['cidx'], t['h']]
            st_s[t['sq'], t['dr'], t['h']] = t['s'] * jnp.exp(t['gl']) + _dot(kgt, vn)
            if t['dr'] == 0:
                of_s[t['sq'], t['rows'], t['hs']] = o
            else:
                ob_s[t['sq'], t['rows'], t['hs']] = o
        return carry

    lax.fori_loop(0, nc, phase_b, 0)
    sf_ref[...] = st_s[...]

    for sq in range(ns):
        z = z_ref[sq]
        for h in range(GDN_HEADS):
            hs = slice(h * GDN_DV, (h + 1) * GDN_DV)
            o = of_s[sq, :, hs] + ob_s[sq, :, hs]
            o = o * lax.rsqrt(jnp.mean(o * o, axis=-1, keepdims=True) + EPS) * ng_ref[...]
            o_ref[sq, :, hs] = o * _silu(z[:, hs])


def _gdn_mixer(qkv, z, ab, p, j, s0):
    bsz, n, _ = qkv.shape
    gp = jnp.zeros((8, 128), F32)
    gp = gp.at[0, :8].set(p['gdn_a_log'][j].reshape(8)).at[1, :8].set(p['gdn_dt_bias'][j].reshape(8))
    cw = jnp.zeros((8, 3 * GDN_WIDTH), F32).at[:3].set(p['gdn_conv_w'][j])
    ns = GDN_SEQS_SHORT if (n <= GDN_SHORT_LEN and bsz % GDN_SEQS_SHORT == 0) else 1
    blk = lambda w_: pl.BlockSpec((ns, n, w_), lambda b: (b, 0, 0))
    sblk = pl.BlockSpec((ns, 2, GDN_HEADS, GDN_DK, GDN_DV), lambda b: (b, 0, 0, 0, 0))
    tok = lambda w_: pltpu.VMEM((ns, n, w_), F32)
    return pl.pallas_call(
        functools.partial(_gdn_kernel, zero_init=s0 is None),
        grid=(bsz // ns,),
        in_specs=[blk(3 * GDN_WIDTH), blk(GDN_WIDTH), blk(128),
                  pl.BlockSpec((8, 3 * GDN_WIDTH), lambda b: (0, 0)),
                  pl.BlockSpec((8, 128), lambda b: (0, 0)),
                  pl.BlockSpec((1, GDN_DV), lambda b: (0, 0))] + ([] if s0 is None else [sblk]),
        out_specs=[blk(GDN_WIDTH), sblk],
        out_shape=[jax.ShapeDtypeStruct((bsz, n, GDN_WIDTH), F32),
                   jax.ShapeDtypeStruct((bsz, 2, GDN_HEADS, GDN_DK, GDN_DV), F32)],
        scratch_shapes=[tok(GDN_WIDTH), tok(GDN_WIDTH), tok(GDN_WIDTH),
                        pltpu.VMEM((ns, 3, n, 128), F32),
                        tok(GDN_WIDTH), tok(GDN_WIDTH),
                        pltpu.VMEM((ns, 2, GDN_HEADS, GDN_DK, GDN_DV), F32),
                        pltpu.VMEM((ns, 2, n, 2 * GDN_WIDTH), F32),
                        pltpu.VMEM((ns, 2, n, GDN_HEADS * GDN_CHUNK), F32),
                        pltpu.VMEM((ns, 2, n // GDN_CHUNK, GDN_HEADS, GDN_DK, GDN_CHUNK), BF16)],
        compiler_params=_cparams(("parallel",)),
        name="gdn_mixer",
    )(qkv, z, ab, cw, gp, p['gdn_norm_g'][j][None], *(() if s0 is None else (s0,)))


def _proj_cd_kernel(x_ref, g_ref, sh_ref, sc_ref, w_ref, gm_ref, gain_ref, cos_ref, sin_ref,
                    qc_ref, kc_ref, vc_ref, qd_ref, kd_ref, vd_ref, *, rope):
    h = _rms_mod(x_ref[0], g_ref[...], sh_ref[0], sc_ref[0])
    y = _dot(h, w_ref[...])
    lane = lax.broadcasted_iota(jnp.int32, (1, 512), 1)
    low = (lane % 32) < 16

    def head_norm(t, gain, scale):
        w = t.shape[1]
        sq = t * t
        ms = jnp.concatenate([_dot(sq[:, c:c + 128], gm_ref[...]) for c in range(0, w, 128)], axis=1)
        t = t * lax.rsqrt(ms + EPS) * gain
        if rope:
            part = jnp.where(low[:, :w], pltpu.roll(t, w - 16, 1), pltpu.roll(t, 16, 1))
            t = t * cos_ref[:, :w] + part * sin_ref[:, :w]
        return t * scale if scale != 1.0 else t

    qc_ref[0] = head_norm(y[:, 0:512], gain_ref[0:1, :], ATTN_SCALE)
    kc_ref[0] = head_norm(y[:, 512:640], gain_ref[1:2, :128], 1.0)
    vc_ref[0] = y[:, 640:768]
    qd_ref[0] = head_norm(y[:, 768:1280], gain_ref[2:3, :], ATTN_SCALE)
    kd_ref[0] = head_norm(y[:, 1280:1792], gain_ref[3:4, :], 1.0)
    vd_ref[0] = y[:, 1792:2304]


def _rope_tables(n):
    rows = n // GRID_W
    row = jnp.repeat(jnp.arange(rows), GRID_W).astype(F32)
    col = jnp.tile(jnp.arange(GRID_W), rows).astype(F32)
    quarter = HEAD_DIM // 4
    inv = ROPE_THETA ** (-jnp.arange(quarter, dtype=F32) / quarter)
    ang_r = row[:, None] * inv[None, :]
    ang_c = col[:, None] * inv[None, :]
    cos = jnp.concatenate([jnp.cos(ang_r), jnp.cos(ang_r), jnp.cos(ang_c), jnp.cos(ang_c)], axis=-1)
    sin = jnp.concatenate([-jnp.sin(ang_r), jnp.sin(ang_r), -jnp.sin(ang_c), jnp.sin(ang_c)], axis=-1)
    return jnp.tile(cos, (1, 8)), jnp.tile(sin, (1, 8))


def _proj_cd(x, g, shift, scale, w, p, j, rope):
    bsz, n, d = x.shape
    tm = 256
    per_seq = shift.shape[0] > 1
    midx = (lambda b, i: (b, 0, 0)) if per_seq else (lambda b, i: (0, 0, 0))
    lane = jnp.arange(128)
    gmat =((lane[:, None] // HEAD_DIM) == (lane[None, :] // HEAD_DIM)).astype(F32) / HEAD_DIM
    gains = jnp.zeros((8, 512), F32)
    gains = gains.at[0].set(jnp.tile(p['c_qn'][j], 8)).at[1].set(jnp.tile(p['c_kn'][j], 8))
    gains = gains.at[2].set(jnp.tile(p['d_qn'][j], 8)).at[3].set(jnp.tile(p['d_kn'][j], 8))
    if rope:
        cos, sin = _rope_tables(n)
    else:
        cos, sin = jnp.ones((tm, 512), F32), jnp.zeros((tm, 512), F32)
    tidx = (lambda b, i: (i, 0)) if rope else (lambda b, i: (0, 0))
    blk = lambda w_: pl.BlockSpec((1, tm, w_), lambda b, i: (b, i, 0))
    return pl.pallas_call(
        functools.partial(_proj_cd_kernel, rope=rope),
        grid=(bsz, n // tm),
        in_specs=[blk(d),
                  pl.BlockSpec((1, d), lambda b, i: (0, 0)),
                  pl.BlockSpec((1, 1, d), midx),
                  pl.BlockSpec((1, 1, d), midx),
                  pl.BlockSpec((d, CD_IN), lambda b, i: (0, 0)),
                  pl.BlockSpec((128, 128), lambda b, i: (0, 0)),
                  pl.BlockSpec((8, 512), lambda b, i: (0, 0)),
                  pl.BlockSpec((tm, 512), tidx),
                  pl.BlockSpec((tm, 512), tidx)],
        out_specs=[blk(512), blk(128), blk(128), blk(512), blk(512), blk(512)],
        out_shape=[jax.ShapeDtypeStruct((bsz, n, w_), F32) for w_ in (512, 128, 128, 512, 512, 512)],
        compiler_params=_cparams(("parallel", "parallel")),
        name="proj_cd",
    )(x, g, shift, scale, w, gmat.astype(BF16), gains, cos, sin)


def _probs(scores, extra=None):
    m = scores[0].max(axis=-1, keepdims=True)
    for s in scores[1:]:
        m = jnp.maximum(m, s.max(axis=-1, keepdims=True))
    if extra is not None:
        m = jnp.maximum(m, extra)
    return [jnp.exp(s - m).astype(BF16) for s in scores], m


def _attn_kernel(qc_ref, kc_ref, vc_ref, qd_ref, kd_ref, vd_ref, *rest, windowed, n_ctx, lam_scale):
    if n_ctx:
        ck_ref, cv_ref, dk_ref, dv_ref, misc_ref, sub_ref, oc_ref, od_ref = rest
    else:
        misc_ref, sub_ref, oc_ref, od_ref = rest
    tq = qc_ref.shape[1]
    n = kc_ref.shape[1]
    start = pl.program_id(1) * tq
    if windowed:
        span = tq + 2 * WINDOW
        k0 = pl.multiple_of(jnp.clip(start - WINDOW, 0, n - span), 128)
        krows = pl.ds(k0, span)
        qpos = start + lax.broadcasted_iota(jnp.int32, (tq, span), 0)
        kpos = k0 + lax.broadcasted_iota(jnp.int32, (tq, span), 1)
        ok = jnp.abs(qpos - kpos) <= WINDOW
    else:
        krows = pl.ds(0, n)
    low = lax.broadcasted_iota(jnp.int32, (1, 2 * HEAD_DIM), 1) < HEAD_DIM

    def key_ops(k):
        kr = pltpu.roll(k, HEAD_DIM, 1)
        z = jnp.zeros_like(k)
        return {(0, 0): jnp.where(low, k, z).astype(BF16), (0, 1): jnp.where(low, z, kr).astype(BF16),
                (1, 0): jnp.where(low, kr, z).astype(BF16), (1, 1): jnp.where(low, z, k).astype(BF16)}

    def with_ones(v):
        return jnp.concatenate([v, jnp.ones_like(v)], axis=1)

    def val_ops(v):
        return {key: with_ones(op) for key, op in key_ops(v).items()}

    def half_ops(k):
        z = jnp.zeros_like(k)
        return [jnp.where(low, k, z).astype(BF16), jnp.where(low, z, k).astype(BF16)]

    seqs = range(qc_ref.shape[0])

    c_scores, d_scores = [[] for _ in seqs], [[] for _ in seqs]
    for sq in seqs:
        qc = qc_ref[sq].astype(BF16)
        qd = qd_ref[sq].astype(BF16)
        kc_ops = key_ops(kc_ref[sq, krows, :])
        ck_ops = key_ops(ck_ref[sq]) if n_ctx else None
        for h in range(C_HEADS):
            key = (h // C_GROUP, h % 2)
            q = qc[:, (h // 2) * 128:(h // 2 + 1) * 128]
            sc = [_dot_nt(q, kc_ops[key])]
            if n_ctx:
                sc.append(_dot_nt(q, ck_ops[key]))
            c_scores[sq].append(sc)
        for h in range(D_HEADS):
            q = qd[:, h * 128:(h + 1) * 128]
            kd_ops = half_ops(kd_ref[sq, :, h * 128:(h + 1) * 128])
            dk_ops = half_ops(dk_ref[sq, :, h * 128:(h + 1) * 128]) if n_ctx else None
            for c in range(2):
                sc = [_dot_nt(q, kd_ops[c])]
                if n_ctx:
                    sc.append(_dot_nt(q, dk_ops[c]))
                d_scores[sq].append(sc)

    c_probs, d_probs = [[] for _ in seqs], [[] for _ in seqs]
    for sq in seqs:
        for h in range(C_HEADS):
            sc = c_scores[sq][h]
            if windowed:
                sc[0] = jnp.where(ok, sc[0], NEG_INF)
            c_probs[sq].append(_probs(sc, misc_ref[0:1, h:h + 1]))
        d_probs[sq] = [_probs(sc)[0] for sc in d_scores[sq]]

    lam = misc_ref[1:2, 0:1]
    for sq in seqs:
        vc_ops = val_ops(vc_ref[sq, krows, :])
        cv_ops = val_ops(cv_ref[sq]) if n_ctx else None
        for j in range(C_HEADS // 2):
            pair = None
            for h in (2 * j, 2 * j + 1):
                key = (h // C_GROUP, h % 2)
                ps, m = c_probs[sq][h]
                r = jnp.dot(ps[0], vc_ops[key], preferred_element_type=F32)
                if n_ctx:
                    r = r + jnp.dot(ps[1], cv_ops[key], preferred_element_type=F32)
                o = r[:, :128] / (r[:, 128:] + jnp.exp(misc_ref[0:1, h:h + 1] - m))
                pair = o if pair is None else pair + o
            oc_ref[sq, :, j * 128:(j + 1) * 128] = pair
        for h in range(D_HEADS):
            vsl = slice(h * D_VDIM, (h + 1) * D_VDIM)
            v_op = with_ones(vd_ref[sq, :, vsl]).astype(BF16)
            dv_op = with_ones(dv_ref[sq, :, vsl]).astype(BF16) if n_ctx else None
            parts = []
            for c in range(2):
                ps = d_probs[sq][2 * h + c]
                r = jnp.dot(ps[0], v_op, preferred_element_type=F32)
                if n_ctx:
                    r = r + jnp.dot(ps[1], dv_op, preferred_element_type=F32)
                parts.append(r[:, :D_VDIM] / r[:, D_VDIM:])
            o = parts[0] - lam * parts[1]
            o = o * lax.rsqrt(jnp.mean(o * o, axis=-1, keepdims=True) + EPS) * sub_ref[...] * lam_scale
            od_ref[sq, :, vsl] = o


def _attention(qc, kc, vc, qd, kd, vd, caches, misc, subln, lam_init, tq, windowed):
    bsz, n, _ = qc.shape
    ns = ATTN_SEQS_SHORT if (not windowed and tq == n and bsz % ATTN_SEQS_SHORT == 0) else 1
    qblk = lambda w_: pl.BlockSpec((ns, tq, w_), lambda b, i: (b, i, 0))
    kblk = lambda rows, w_: pl.BlockSpec((ns, rows, w_), lambda b, i: (b, 0, 0))
    in_specs = [qblk(512), kblk(n, 128), kblk(n, 128), qblk(512), kblk(n, 512), kblk(n, 512)]
    args = [qc, kc, vc, qd, kd, vd]
    n_ctx = 0
    if caches is not None:
        n_ctx = caches[0].shape[1]
        in_specs += [kblk(n_ctx, 128), kblk(n_ctx, 128), kblk(n_ctx, 512), kblk(n_ctx, 512)]
        args += list(caches)
    in_specs += [pl.BlockSpec((8, 128), lambda b, i: (0, 0)), pl.BlockSpec((1, D_VDIM), lambda b, i: (0, 0))]
    args += [misc, subln]
    return pl.pallas_call(
        functools.partial(_attn_kernel, windowed=windowed, n_ctx=n_ctx, lam_scale=1.0 - lam_init),
        grid=(bsz // ns, n // tq),
        in_specs=in_specs,
        out_specs=[qblk(512), qblk(512)],
        out_shape=[jax.ShapeDtypeStruct((bsz, n, 512), F32), jax.ShapeDtypeStruct((bsz, n, 512), F32)],
        compiler_params=_cparams(("parallel", "parallel")),
        name="attn_win" if windowed else "attn_ctx",
    )(*args)


def _post_kernel(x_ref, xp_ref, xn_ref, ma_ref, map_ref, man_ref, mb_ref, mbp_ref, mbn_ref,
                 g1_ref, sh_ref, sc_ref, g2_ref, ng_ref, wo_ref, wup_ref, cw_ref, wdn_ref, o_ref, act_ref,
                 *, seq_len):
    rows = x_ref.shape[0]
    ext = rows + 2 * POST_HALO
    half = ma_ref.shape[1]
    xe = jnp.concatenate([xp_ref[...], x_ref[...], xn_ref[...]], axis=0)
    mae = jnp.concatenate([map_ref[...], ma_ref[...], man_ref[...]], axis=0)
    mbe = jnp.concatenate([mbp_ref[...], mb_ref[...], mbn_ref[...]], axis=0)
    x1 = xe + g1_ref[0] * (_dot(mae, wo_ref[:half, :]) + _dot(mbe, wo_ref[half:, :]))
    h = _rms_mod(x1, ng_ref[...], sh_ref[0], sc_ref[0]).astype(BF16)
    x1 = x1[POST_HALO:POST_HALO + rows]
    row0 = pl.program_id(0) * rows - POST_HALO
    pos = (row0 + lax.broadcasted_iota(jnp.int32, (ext, 1), 0)) % seq_len
    first = pos == 0
    last = pos == seq_len - 1
    for c in range(FF_CHUNKS):
        cs = slice(c * FF_CHUNK, (c + 1) * FF_CHUNK)
        a = jnp.dot(h, wup_ref[:, cs], preferred_element_type=F32)
        b = jnp.dot(h, wup_ref[:, D_FF + c * FF_CHUNK:D_FF + (c + 1) * FF_CHUNK], preferred_element_type=F32)
        am = jnp.where(first, 0.0, pltpu.roll(a, 1, 0))
        ap = jnp.where(last, 0.0, pltpu.roll(a, ext - 1, 0))
        a = am * cw_ref[0:1, cs] + a * cw_ref[1:2, cs] + ap * cw_ref[2:3, cs] + cw_ref[3:4, cs]
        act_ref[:, cs] = (_silu(a) * b)[POST_HALO:POST_HALO + rows].astype(BF16)
    ffn = jnp.dot(act_ref[...], wdn_ref[...], preferred_element_type=F32)
    o_ref[...] = x1 + g2_ref[0] * ffn


def _post(x, mix_a, mix_b, g1, sh2, sc2, g2, norm_g, w_out, wup, cw, wdn, layer):
    bsz, seq_len, d = x.shape
    half = mix_a.shape[-1]
    rows = POST_ROWS
    total = bsz * seq_len
    nhalo = total // POST_HALO
    per_seq = g1.shape[0] > 1
    midx = (lambda i: ((i * rows) // seq_len, 0, 0)) if per_seq else (lambda i: (0, 0, 0))
    mspec = pl.BlockSpec((1, 1, d), midx)
    pidx = lambda i: (jnp.maximum(i * (rows // POST_HALO) - 1, 0), 0)
    nidx = lambda i: (jnp.minimum((i + 1) * (rows // POST_HALO), nhalo - 1), 0)
    trio = lambda w_: [pl.BlockSpec((rows, w_), lambda i: (i, 0)), pl.BlockSpec((POST_HALO, w_), pidx),
                       pl.BlockSpec((POST_HALO, w_), nidx)]
    whole = lambda shape: pl.BlockSpec(shape, lambda i: (0,) * len(shape), pipeline_mode=pl.Buffered(1))
    x2 = x.reshape(total, d)
    a2 = mix_a.reshape(total, half)
    b2 = mix_b.reshape(total, half)
    out = pl.pallas_call(
        functools.partial(_post_kernel, seq_len=seq_len),
        grid=(total // rows,),
        in_specs=trio(d) + trio(half) + trio(half) + [
            mspec, mspec, mspec, mspec,
            pl.BlockSpec((1, d), lambda i: (0, 0)),
            whole((d, d)),
            pl.BlockSpec((None, d, 2 * D_FF), lambda i: (layer, 0, 0), pipeline_mode=pl.Buffered(1)),
            whole((8, D_FF)),
            pl.BlockSpec((None, D_FF, d), lambda i: (layer, 0, 0), pipeline_mode=pl.Buffered(1))],
        out_specs=pl.BlockSpec((rows, d), lambda i: (i, 0)),
        out_shape=jax.ShapeDtypeStruct((total, d), F32),
        scratch_shapes=[pltpu.VMEM((rows, D_FF), BF16)],
        compiler_params=_cparams(("parallel",)),
        name="post_ffn",
    )(x2, x2, x2, a2, a2, a2, b2, b2, b2, g1, sh2, sc2, g2, norm_g, w_out, wup, cw, wdn)
    return out.reshape(bsz, seq_len, d)


def _ffn_weights(p, l):
    cw = jnp.zeros((8, D_FF), F32).at[:3].set(p['ffn_conv_w'][l]).at[3].set(p['ffn_conv_b'][l])
    return p['ffn_up'].astype(BF16), cw, p['ffn_down'].astype(BF16)


def _lambda_init(layer):
    return 0.8 - 0.6 * math.exp(-0.3 * layer)


def _trunk(x, mods, p, states, caches):
    bsz, n, d = x.shape
    nseg = n // S5_SEG
    depth = p['w_mod'].shape[0]
    news = {k: [] for k in ('s5r', 's5i', 'gdn', 'ck', 'cv', 'dk', 'dv')}
    for l in range(depth):
        j = l // 2
        sh1, sc1, g1, sh2, sc2, g2 = mods[l]
        ng1 = p['norm1_g'][l][None]
        if l % 2 == 0:
            w_in = p['w_in_ab'][j]
            w_gate = jnp.zeros((d, 128), BF16).at[:, :w_in.shape[1] - AB_MAIN].set(
                w_in[:, AB_MAIN:].astype(BF16))
            rows = bsz * nseg
            per_row = lambda m: jnp.repeat(m, nseg, axis=0) if m.shape[0] > 1 else m
            u_t, qkv, z, ab = _proj_ab(x.reshape(rows, S5_SEG, d), ng1, per_row(sh1), per_row(sc1),
                                       w_in[:, :AB_MAIN].astype(BF16), w_gate)
            qkv, z, ab = (t.reshape(bsz, n, t.shape[-1]) for t in (qkv, z, ab))
            if states is None:
                h0r = jnp.zeros((2, bsz, S5_GROUPS * S5_STATE), F32)
                h0i = h0r
                s0 = None
            else:
                h0r = states[0][:, j].reshape(bsz, 2, -1).transpose(1, 0, 2)
                h0i = states[1][:, j].reshape(bsz, 2, -1).transpose(1, 0, 2)
                s0 = states[2][:, j]
            ya, fr, fi = _s5_mixer(u_t, p, j, h0r, h0i, bsz, nseg)
            yb, sg = _gdn_mixer(qkv, z, ab, p, j, s0)
            mix = (ya.reshape(bsz, n, S5_WIDTH), yb)
            w_out = p['w_out_ab'][j]
            news['s5r'].append(fr.transpose(1, 0, 2).reshape(bsz, 2, S5_GROUPS, S5_STATE))
            news['s5i'].append(fi.transpose(1, 0, 2).reshape(bsz, 2, S5_GROUPS, S5_STATE))
            news['gdn'].append(sg)
        else:
            lam_init = _lambda_init(l)
            f = lambda name: p[name][j]
            lam = (jnp.exp(jnp.sum(f('d_lq1') * f('d_lk1'))) - jnp.exp(jnp.sum(f('d_lq2') * f('d_lk2')))
                   + lam_init)
            misc = jnp.zeros((8, 128), F32).at[0, :C_HEADS].set(p['c_sink'][j]).at[1, :].set(lam)
            rope = caches is not None
            qc, kc, vc, qd, kd, vd = _proj_cd(x, ng1, sh1, sc1, p['w_in_cd'][j].astype(BF16), p, j, rope)
            if caches is None:
                mix = _attention(qc, kc, vc, qd, kd, vd, None, misc, p['d_subln'][j][None],
                                 lam_init, n, False)
            else:
                n_ctx = caches[0].shape[2]
                cc = (caches[0][:, j].reshape(bsz, n_ctx, 128), caches[1][:, j].reshape(bsz, n_ctx, 128),
                      caches[2][:, j].reshape(bsz, n_ctx, 512), caches[3][:, j].reshape(bsz, n_ctx, 512))
                mix = _attention(qc, kc, vc, qd, kd, vd, cc, misc, p['d_subln'][j][None],
                                 lam_init, Q_BLOCK, True)
            w_out = p['w_out_cd'][j]
            news['ck'].append(kc.reshape(bsz, n, C_KV_HEADS, HEAD_DIM))
            news['cv'].append(vc.reshape(bsz, n, C_KV_HEADS, HEAD_DIM))
            news['dk'].append(kd.reshape(bsz, n, D_HEADS, 2, HEAD_DIM))
            news['dv'].append(vd.reshape(bsz, n, D_HEADS, D_VDIM))
        wup, cw, wdn = _ffn_weights(p, l)
        x = _post(x, mix[0], mix[1], g1, sh2, sc2, g2, p['norm2_g'][l][None], w_out.astype(BF16),
                  wup, cw, wdn, l)
    return x, news


def kernel(x_prompt, x_sample, c, state_s5_re, state_s5_im, state_gdn, cache_c_k, cache_c_v, cache_d_k, cache_d_v, c_ctx, w_mod, b_mod, norm1_g, norm2_g, w_in_ab, w_out_ab, s5_lam_re, s5_lam_im, s5_log_dt, s5_b_re, s5_b_im, s5_c_re, s5_c_im, s5_d, s5_w_glu, s5_b_glu, gdn_conv_w, gdn_a_log, gdn_dt_bias, gdn_norm_g, w_in_cd, w_out_cd, c_qn, c_kn, c_sink, d_qn, d_kn, d_lq1, d_lk1, d_lq2, d_lk2, d_subln, ffn_up, ffn_conv_w, ffn_conv_b, ffn_down):
    p = dict(w_mod=w_mod, b_mod=b_mod, norm1_g=norm1_g, norm2_g=norm2_g, w_in_ab=w_in_ab, w_out_ab=w_out_ab,
             s5_lam_re=s5_lam_re, s5_lam_im=s5_lam_im, s5_log_dt=s5_log_dt, s5_b_re=s5_b_re, s5_b_im=s5_b_im,
             s5_c_re=s5_c_re, s5_c_im=s5_c_im, s5_d=s5_d, s5_w_glu=s5_w_glu, s5_b_glu=s5_b_glu,
             gdn_conv_w=gdn_conv_w, gdn_a_log=gdn_a_log, gdn_dt_bias=gdn_dt_bias, gdn_norm_g=gdn_norm_g,
             w_in_cd=w_in_cd, w_out_cd=w_out_cd, c_qn=c_qn, c_kn=c_kn, c_sink=c_sink, d_qn=d_qn, d_kn=d_kn,
             d_lq1=d_lq1, d_lk1=d_lk1, d_lq2=d_lq2, d_lk2=d_lk2, d_subln=d_subln,
             ffn_up=ffn_up, ffn_conv_w=ffn_conv_w, ffn_conv_b=ffn_conv_b, ffn_down=ffn_down)
    depth = w_mod.shape[0]
    n_dec = c.shape[0]
    mod = _modulation(jnp.concatenate([c_ctx[None], c], axis=0), w_mod, b_mod)
    split6 = lambda m: [m[:, None, k * D_MODEL:(k + 1) * D_MODEL] for k in range(6)]
    mods_ctx = [split6(mod[l, 0:1]) for l in range(depth)]
    mods_dec = [split6(mod[l, 1:1 + n_dec]) for l in range(depth)]

    y_prompt, nw = _trunk(x_prompt, mods_ctx, p, None, None)
    y_sample, _ = _trunk(x_sample, mods_dec, p, (state_s5_re, state_s5_im, state_gdn),
                         (cache_c_k, cache_c_v, cache_d_k, cache_d_v))
    st = lambda name: jnp.stack(nw[name], axis=1)
    return (y_prompt, y_sample, st('s5r'), st('s5i'), st('gdn'), st('ck'), st('cv'), st('dk'), st('dv'))
```

```python
import functools
import math

import jax
import jax.numpy as jnp
from jax import lax
from jax.experimental import pallas as pl
from jax.experimental.pallas import tpu as pltpu

F32 = jnp.float32
BF16 = jnp.bfloat16

D_MODEL = 1024
GRID_W = 64
EPS = 1e-6
NEG_INF = -1e30

S5_WIDTH = 512
S5_GROUP = 16
S5_GROUPS = 32
S5_STATE = 64
S5_TILE_GROUPS = 8
S5_TILE_CH = S5_TILE_GROUPS * S5_GROUP
S5_TILE_ST = S5_TILE_GROUPS * S5_STATE
S5_TILES = S5_GROUPS // S5_TILE_GROUPS
S5_SEG = 256
S5_SUB = 32
PROJ_AB_TOKENS = 64
S5_ROWS = 8

GDN_DK = 128
GDN_DV = 128
GDN_HEADS = 4
GDN_WIDTH = 512
GDN_CHUNK = 64
GDN_GROUP = 4
GDN_SHORT_LEN = 256
GDN_SEQS_SHORT = 2
GDN_EXACT_ROUNDS = 3

HEAD_DIM = 64
C_HEADS = 8
C_KV_HEADS = 2
C_GROUP = 4
WINDOW = 128
Q_BLOCK = 128
D_HEADS = 4
D_VDIM = 128
ATTN_SCALE = HEAD_DIM ** -0.5
ATTN_SEQS_SHORT = 2
ROPE_THETA = 10000.0

D_FF = 2816
FF_CHUNK = 256
FF_CHUNKS = D_FF // FF_CHUNK
POST_ROWS = 512
POST_HALO = 8

AB_MAIN = S5_WIDTH + 4 * GDN_WIDTH
AB_PAD = AB_MAIN + 128
CD_IN = 2304

MOD_TK = 128
MOD_LANES = 512
VMEM_LIMIT = 56 * 1024 * 1024


def _cparams(sem):
    return pltpu.CompilerParams(dimension_semantics=sem, vmem_limit_bytes=VMEM_LIMIT)


def _sigmoid(x):
    return 1.0 / (1.0 + jnp.exp(-x))


def _silu(x):
    return x * _sigmoid(x)


def _softplus(x):
    return jnp.maximum(x, 0.0) + jnp.log(1.0 + jnp.exp(-jnp.abs(x)))


def _gelu_tanh(x):
    return 0.5 * x * (1.0 + jnp.tanh(math.sqrt(2.0 / math.pi) * (x + 0.044715 * (x * x * x))))


def _rms_mod(x, g, shift, scale):
    y = x * lax.rsqrt(jnp.mean(x * x, axis=-1, keepdims=True) + EPS)
    return (y * g) * (1.0 + scale) + shift


def _dot(a, b):
    return jnp.dot(a.astype(BF16), b.astype(BF16), preferred_element_type=F32)


def _dot_nt(a, b):
    return lax.dot_general(a.astype(BF16), b.astype(BF16), (((1,), (1,)), ((), ())),
                           preferred_element_type=F32)


def _hi_lo(a):
    hi = a.astype(BF16)
    return hi, (a - hi.astype(F32)).astype(BF16)


def _mod_kernel(ct_ref, w_ref, b_ref, o_ref, acc_ref, *, n_rows):
    k = pl.program_id(1)

    @pl.when(k == 0)
    def _():
        acc_ref[...] = jnp.zeros_like(acc_ref)

    tk, n_out = w_ref.shape[1:]
    s = _silu(ct_ref[...])
    sb = [[jnp.broadcast_to(s[g * 8:(g + 1) * 8, m:m + 1], (8, MOD_LANES)) for g in range(tk // 8)]
          for m in range(n_rows)]
    for c in range(n_out // MOD_LANES):
        cols = slice(c * MOD_LANES, (c + 1) * MOD_LANES)
        accs = [acc_ref[m, :, cols] for m in range(n_rows)]
        for g in range(tk // 8):
            wg = w_ref[0, g * 8:(g + 1) * 8, cols]
            accs = [a + wg * sb[m][g] for m, a in enumerate(accs)]
        for m in range(n_rows):
            acc_ref[m, :, cols] = accs[m]

    @pl.when(k == pl.num_programs(1) - 1)
    def _():
        o_ref[0] = jnp.zeros(o_ref.shape[1:], F32)
        for m in range(n_rows):
            o_ref[0, m:m + 1, :] = jnp.sum(acc_ref[m], axis=0, keepdims=True) + b_ref[0]


def _modulation(cvecs, w_mod, b_mod):
    n, d = cvecs.shape
    depth, _, n_out = w_mod.shape
    ct = jnp.zeros((d, 8), F32).at[:, :n].set(cvecs.T)
    return pl.pallas_call(
        functools.partial(_mod_kernel, n_rows=n),
        grid=(depth, d // MOD_TK),
        in_specs=[pl.BlockSpec((MOD_TK, 8), lambda l, k: (k, 0)),
                  pl.BlockSpec((1, MOD_TK, n_out), lambda l, k: (l, k, 0)),
                  pl.BlockSpec((1, 1, n_out), lambda l, k: (l, 0, 0))],
        out_specs=pl.BlockSpec((1, 8, n_out), lambda l, k: (l, 0, 0)),
        out_shape=jax.ShapeDtypeStruct((depth, 8, n_out), F32),
        scratch_shapes=[pltpu.VMEM((n, 8, n_out), F32)],
        compiler_params=_cparams(("parallel", "arbitrary")),
        name="adaln_mod",
    )(ct, w_mod, b_mod.reshape(depth, 1, n_out))


def _proj_ab_kernel(x_ref, g_ref, sh_ref, sc_ref, w_ref, wg_ref, u_ref, qkv_ref, z_ref, ab_ref):
    ns, tm, d = x_ref.shape
    h = _rms_mod(x_ref[...], g_ref[...], sh_ref[...], sc_ref[...])
    h = h.reshape(ns * tm, d).astype(BF16)
    y = jnp.dot(h, w_ref[...], preferred_element_type=F32)
    for s in range(ns):
        u_ref[:, s, :] = y[s * tm:(s + 1) * tm, :S5_WIDTH]
    qkv_ref[...] = y[:, S5_WIDTH:S5_WIDTH + 3 * GDN_WIDTH].reshape(ns, tm, 3 * GDN_WIDTH)
    z_ref[...] = y[:, S5_WIDTH + 3 * GDN_WIDTH:AB_MAIN].reshape(ns, tm, GDN_WIDTH)
    ab_ref[...] = jnp.dot(h, wg_ref[...], preferred_element_type=F32).reshape(ns, tm, 128)


def _proj_ab(x, g, shift, scale, w, w_gate):
    rows, n, d = x.shape
    tm = PROJ_AB_TOKENS
    per_row = shift.shape[0] > 1
    mspec = pl.BlockSpec((S5_ROWS, 1, d), lambda r, i: (r, 0, 0)) if per_row else \
        pl.BlockSpec((1, 1, d), lambda r, i: (0, 0, 0))
    blk = lambda w_: pl.BlockSpec((S5_ROWS, tm, w_), lambda r, i: (r, i, 0))
    return pl.pallas_call(
        _proj_ab_kernel,
        grid=(rows // S5_ROWS, n // tm),
        in_specs=[blk(d),
                  pl.BlockSpec((1, d), lambda r, i: (0, 0)),
                  mspec, mspec,
                  pl.BlockSpec((d, AB_MAIN), lambda r, i: (0, 0)),
                  pl.BlockSpec((d, 128), lambda r, i: (0, 0))],
        out_specs=[pl.BlockSpec((tm, S5_ROWS, S5_WIDTH), lambda r, i: (i, r, 0)),
                   blk(3 * GDN_WIDTH), blk(GDN_WIDTH), blk(128)],
        out_shape=[jax.ShapeDtypeStruct((n, rows, S5_WIDTH), F32),
                   jax.ShapeDtypeStruct((rows, n, 3 * GDN_WIDTH), F32),
                   jax.ShapeDtypeStruct((rows, n, GDN_WIDTH), F32),
                   jax.ShapeDtypeStruct((rows, n, 128), F32)],
        compiler_params=_cparams(("parallel", "parallel")),
        name="proj_ab",
    )(x, g, shift, scale, w, w_gate)


def _s5_kernel(u_ref, bm_ref, cm_ref, a_ref, h0r_ref, h0i_ref, *rest, want_y):
    if want_y:
        y_ref, fr_ref, fi_ref, xs_ref = rest
    else:
        fr_ref, fi_ref, xs_ref = rest
    n = u_ref.shape[0]
    sub = S5_SUB
    nsub = n // sub
    st = S5_TILE_ST

    def x_proj(d, k):
        u2 = u_ref[k * sub:(k + 1) * sub].reshape(sub * S5_ROWS, S5_TILE_CH)
        xs_ref[d, k * sub:(k + 1) * sub] = _dot(u2, bm_ref[d, 0]).reshape(sub, S5_ROWS, 2 * st)

    ar = [jnp.broadcast_to(a_ref[d, 0, 0:1, :], (S5_ROWS, st)) for d in range(2)]
    ai = [jnp.broadcast_to(a_ref[d, 0, 1:2, :], (S5_ROWS, st)) for d in range(2)]
    hr = [h0r_ref[0], h0r_ref[1]]
    hi = [h0i_ref[0], h0i_ref[1]]
    x_proj(0, 0)
    x_proj(1, nsub - 1)
    written = set()
    for k in range(nsub):
        ks = (k, nsub - 1 - k)
        if k + 1 < nsub:
            x_proj(0, ks[0] + 1)
            x_proj(1, ks[1] - 1)
        for t in range(sub):
            for d in range(2):
                tt = ks[d] * sub + (t if d == 0 else sub - 1 - t)
                x = xs_ref[d, tt]
                nr = ar[d] * hr[d] - ai[d] * hi[d] + x[:, :st]
                ni = ar[d] * hi[d] + ai[d] * hr[d] + x[:, st:]
                xs_ref[d, tt] = jnp.concatenate([nr, ni], axis=-1)
                hr[d], hi[d] = nr, ni
        if want_y:
            for d in range(2):
                rows = slice(ks[d] * sub, (ks[d] + 1) * sub)
                hs = xs_ref[d, rows].reshape(sub * S5_ROWS, 2 * st)
                yv = _dot(hs, cm_ref[d, 0]).reshape(sub, S5_ROWS, S5_TILE_CH)
                if ks[d] in written:
                    y_ref[rows] += yv
                else:
                    y_ref[rows] = yv
                    written.add(ks[d])
    for d in range(2):
        fr_ref[d] = hr[d]
        fi_ref[d] = hi[d]


def _s5_scan(u_t, bmat, cmat, amat, h0r, h0i, want_y=True):
    n, rows, _ = u_t.shape
    state = lambda: pl.BlockSpec((2, S5_ROWS, S5_TILE_ST), lambda r, j: (0, r, j))
    st_shape = jax.ShapeDtypeStruct((2, rows, S5_GROUPS * S5_STATE), F32)
    out_specs = [state(), state()]
    out_shape = [st_shape, st_shape]
    if want_y:
        out_specs.insert(0, pl.BlockSpec((n, S5_ROWS, S5_TILE_CH), lambda r, j: (0, r, j)))
        out_shape.insert(0, jax.ShapeDtypeStruct((n, rows, S5_WIDTH), F32))
    return pl.pallas_call(
        functools.partial(_s5_kernel, want_y=want_y),
        grid=(rows // S5_ROWS, S5_TILES),
        in_specs=[pl.BlockSpec((n, S5_ROWS, S5_TILE_CH), lambda r, j: (0, r, j)),
                  pl.BlockSpec((2, 1, S5_TILE_CH, 2 * S5_TILE_ST), lambda r, j: (0, j, 0, 0)),
                  pl.BlockSpec((2, 1, 2 * S5_TILE_ST, S5_TILE_CH), lambda r, j: (0, j, 0, 0)),
                  pl.BlockSpec((2, 1, 8, S5_TILE_ST), lambda r, j: (0, j, 0, 0)),
                  state(), state()],
        out_specs=out_specs,
        out_shape=out_shape,
        scratch_shapes=[pltpu.VMEM((2, n, S5_ROWS, 2 * S5_TILE_ST), F32)],
        compiler_params=_cparams(("parallel", "parallel")),
        name="s5_scan" if want_y else "s5_states",
    )(u_t, bmat, cmat, amat, h0r, h0i)


def _s5_params(p, j):
    lam_re, lam_im, log_dt = p['s5_lam_re'][j], p['s5_lam_im'][j], p['s5_log_dt'][j]
    dt = jnp.exp(log_dt)[..., None]
    mag = jnp.exp(lam_re * dt)
    ar, ai = mag * jnp.cos(lam_im * dt), mag * jnp.sin(lam_im * dt)
    den = lam_re * lam_re + lam_im * lam_im
    fr = ((ar - 1.0) * lam_re + ai * lam_im) / den
    fi = (ai * lam_re - (ar - 1.0) * lam_im) / den
    b_re, b_im = p['s5_b_re'][j], p['s5_b_im'][j]
    bbr = fr[..., None] * b_re - fi[..., None] * b_im
    bbi = fr[..., None] * b_im + fi[..., None] * b_re
    eye = jnp.eye(S5_TILE_GROUPS, dtype=F32)

    def in_blocks(t):
        t = t.reshape(2, S5_TILES, S5_TILE_GROUPS, S5_STATE, S5_GROUP)
        t = jnp.einsum('dtgpc,gh->dtgchp', t, eye)
        return t.reshape(2, S5_TILES, S5_TILE_CH, S5_TILE_ST)

    def out_blocks(t):
        t = t.reshape(2, S5_TILES, S5_TILE_GROUPS, S5_GROUP, S5_STATE)
        t = jnp.einsum('dtgcp,gh->dtgphc', t, eye)
        return t.reshape(2, S5_TILES, S5_TILE_ST, S5_TILE_CH)

    bmat = jnp.concatenate([in_blocks(bbr), in_blocks(bbi)], axis=-1).astype(BF16)
    cmat = jnp.concatenate([out_blocks(p['s5_c_re'][j]), -out_blocks(p['s5_c_im'][j])], axis=-2).astype(BF16)
    seg_mag = jnp.exp(lam_re * dt * S5_SEG)
    pr, pi = seg_mag * jnp.cos(lam_im * dt * S5_SEG), seg_mag * jnp.sin(lam_im * dt * S5_SEG)
    flat = lambda t: t.reshape(2, S5_TILES, 1, S5_TILE_ST)
    amat = jnp.concatenate([flat(ar), flat(ai), flat(pr), flat(pi),
                            jnp.zeros((2, S5_TILES, 4, S5_TILE_ST), F32)], axis=2)
    return bmat, cmat, amat


def _s5_glu_kernel(y_ref, u_ref, d_ref, w_ref, b_ref, o_ref):
    tm, ns, _ = y_ref.shape
    y = jnp.concatenate([y_ref[:, s, :] + d_ref[...] * u_ref[:, s, :] for s in range(ns)], axis=0)
    g = _gelu_tanh(y)
    out = g * _sigmoid(_dot(g, w_ref[...]) + b_ref[...])
    o_ref[...] = out.reshape(ns, tm, S5_WIDTH)


def _s5_glu(y_t, u_t, s5_d, w_glu, b_glu):
    n, rows, _ = y_t.shape
    tm = PROJ_AB_TOKENS
    tblk = pl.BlockSpec((tm, S5_ROWS, S5_WIDTH), lambda r, i: (i, r, 0))
    return pl.pallas_call(
        _s5_glu_kernel,
        grid=(rows // S5_ROWS, n // tm),
        in_specs=[tblk, tblk,
                  pl.BlockSpec((1, S5_WIDTH), lambda r, i: (0, 0)),
                  pl.BlockSpec((S5_WIDTH, S5_WIDTH), lambda r, i: (0, 0)),
                  pl.BlockSpec((1, S5_WIDTH), lambda r, i: (0, 0))],
        out_specs=pl.BlockSpec((S5_ROWS, tm, S5_WIDTH), lambda r, i: (r, i, 0)),
        out_shape=jax.ShapeDtypeStruct((rows, n, S5_WIDTH), F32),
        compiler_params=_cparams(("parallel", "parallel")),
        name="s5_glu",
    )(y_t, u_t, s5_d, w_glu, b_glu)


def _s5_mixer(u3, p, j, h0r, h0i, bsz, nseg):
    rows = bsz * nseg
    bmat, cmat, amat = _s5_params(p, j)
    if nseg == 1:
        y_t, fr, fi = _s5_scan(u3, bmat, cmat, amat, h0r, h0i)
    else:
        zero = jnp.zeros((2, bsz, nseg, S5_GROUPS * S5_STATE), F32)
        first = jnp.array([0, nseg - 1])
        seed = lambda h0: zero.at[jnp.arange(2), :, first].set(h0).reshape(2, rows, -1)
        fr, fi = _s5_scan(u3, bmat, cmat, amat, seed(h0r), seed(h0i), want_y=False)
        fr = fr.reshape(2, bsz, nseg, -1)
        fi = fi.reshape(2, bsz, nseg, -1)
        pr = amat[:, :, 2].reshape(2, 1, -1)
        pi = amat[:, :, 3].reshape(2, 1, -1)

        def chain(dr, order):
            hr, hi = (h0r[dr], h0i[dr])
            outs_r, outs_i = {}, {}
            for n_done, k in enumerate(order):
                outs_r[k], outs_i[k] = hr, hi
                if n_done == 0:
                    hr, hi = fr[dr, :, k], fi[dr, :, k]
                else:
                    hr, hi = (pr[dr] * hr - pi[dr] * hi + fr[dr, :, k],
                              pr[dr] * hi + pi[dr] * hr + fi[dr, :, k])
            st = lambda o: jnp.stack([o[k] for k in range(nseg)], axis=1)
            return st(outs_r), st(outs_i), hr, hi

        sr0, si0, er0, ei0 = chain(0, list(range(nseg)))
        sr1, si1, er1, ei1 = chain(1, list(range(nseg - 1, -1, -1)))
        start_r = jnp.stack([sr0, sr1]).reshape(2, rows, -1)
        start_i = jnp.stack([si0, si1]).reshape(2, rows, -1)
        y_t, _, _ = _s5_scan(u3, bmat, cmat, amat, start_r, start_i)
        fr = jnp.stack([er0, er1])
        fi = jnp.stack([ei0, ei1])
    ya = _s5_glu(y_t, u3, p['s5_d'][j][None], p['s5_w_glu'][j].astype(BF16), p['s5_b_glu'][j][None])
    return ya, fr, fi


def _gdn_kernel(qkv_ref, z_ref, ab_ref, cw_ref, gp_ref, ng_ref, *rest, zero_init):
    if zero_init:
        s0_ref = None
        o_ref, sf_ref, q_s, k_s, v_s, gate_s, of_s, ob_s, st_s, uw_s, a_s, kgt_s = rest
    else:
        s0_ref, o_ref, sf_ref, q_s, k_s, v_s, gate_s, of_s, ob_s, st_s, uw_s, a_s, kgt_s = rest
    ns, n = qkv_ref.shape[:2]
    nc = n // GDN_CHUNK
    row = lax.broadcasted_iota(jnp.int32, (n, 1), 0)

    for sq in range(ns):
        for blk in range(3 * GDN_HEADS):
            cols = slice(blk * GDN_DK, (blk + 1) * GDN_DK)
            hs = slice((blk % GDN_HEADS) * GDN_DK, (blk % GDN_HEADS + 1) * GDN_DK)
            x = qkv_ref[sq, :, cols]
            xm = jnp.where(row == 0, 0.0, pltpu.roll(x, 1, 0))
            xp = jnp.where(row == n - 1, 0.0, pltpu.roll(x, n - 1, 0))
            y = _silu(xm * cw_ref[0:1, cols] + x * cw_ref[1:2, cols] + xp * cw_ref[2:3, cols])
            if blk < GDN_HEADS:
                q_s[sq, :, hs] = y * lax.rsqrt(jnp.sum(y * y, axis=-1, keepdims=True) + EPS) * (GDN_DK ** -0.5)
            elif blk < 2 * GDN_HEADS:
                k_s[sq, :, hs] = y * lax.rsqrt(jnp.sum(y * y, axis=-1, keepdims=True) + EPS)
            else:
                v_s[sq, :, hs] = y

        ab = ab_ref[sq]
        beta = _sigmoid(ab)
        g = -jnp.exp(gp_ref[0:1, :]) * _softplus(ab + gp_ref[1:2, :])
        pos = row % GDN_CHUNK
        pre, suf = g, g
        sft = 1
        while sft < GDN_CHUNK:
            pre = pre + jnp.where(pos >= sft, pltpu.roll(pre, sft, 0), 0.0)
            suf = suf + jnp.where(pos < GDN_CHUNK - sft, pltpu.roll(suf, n - sft, 0), 0.0)
            sft *= 2
        gate_s[sq, 0] = beta
        gate_s[sq, 1] = pre
        gate_s[sq, 2] = suf

    st_s[...] = jnp.zeros_like(st_s) if zero_init else s0_ref[...]
    cs = GDN_CHUNK
    pk = GDN_HEADS * cs
    ri = lax.broadcasted_iota(jnp.int32, (cs, pk), 0)
    lane_pk = lax.broadcasted_iota(jnp.int32, (cs, pk), 1)
    ci = lane_pk % cs
    eye_pk = (ri == ci).astype(F32)
    head_pk = [(lax.broadcasted_iota(jnp.int32, (1, pk), 1) // cs) == h for h in range(GDN_HEADS)]
    head_w = [(lax.broadcasted_iota(jnp.int32, (1, GDN_WIDTH), 1) // GDN_DK) == h for h in range(GDN_HEADS)]

    def block_diag(p):
        return jnp.concatenate([jnp.where(m, p, jnp.zeros_like(p)) for m in head_pk], axis=0)

    def dot3_bd(a, p):
        a_hi, a_lo = _hi_lo(a)
        p_hi, p_lo = _hi_lo(p)
        b_hi, b_lo = block_diag(p_hi), block_diag(p_lo)
        mm = functools.partial(jnp.dot, preferred_element_type=F32)
        return mm(a_hi, b_hi) + (mm(a_lo, b_hi) + mm(a_hi, b_lo))

    def dot1_bd(a, p):
        return jnp.dot(a.astype(BF16), block_diag(p.astype(BF16)), preferred_element_type=F32)

    def lanes(cols, width):
        return jnp.concatenate([jnp.broadcast_to(c, (cs, width)) for c in cols], axis=1)

    def phase_a(sq, it):
        st = []
        for cc in range(GDN_GROUP):
            rows = pl.ds(pl.multiple_of((it * GDN_GROUP + cc) * cs, cs), cs)
            q_all, k_all, v_all = q_s[sq, rows, :], k_s[sq, rows, :], v_s[sq, rows, :]
            beta_blk = gate_s[sq, 0, rows, :]
            for dr in range(2):
                gc_blk = gate_s[sq, 1 + dr, rows, :]
                lane0 = dr * GDN_HEADS
                bcols = [beta_blk[:, 8 + lane0 + h:9 + lane0 + h] for h in range(GDN_HEADS)]
                gcols = [gc_blk[:, lane0 + h:lane0 + h + 1] for h in range(GDN_HEADS)]
                st.append(dict(sq=sq, dr=dr, rows=rows, cidx=it * GDN_GROUP + cc, q=q_all, k=k_all, v=v_all, bcols=bcols, gcols=gcols,
                               incl=(ri >= ci) if dr == 0 else (ri <= ci),
                               strict=(ri > ci) if dr == 0 else (ri < ci)))
        for t in st:
            b_w = lanes(t['bcols'], GDN_DK)
            t['kb'] = t['k'] * b_w
            t['vb'] = t['v'] * b_w
            k_bd = jnp.concatenate([jnp.where(m, t['k'], 0.0) for m in head_w], axis=0)
            t['kq'] = _dot_nt(jnp.concatenate([t['kb'], t['q']], axis=0), k_bd)
        for t in st:
            gcol = lanes(t['gcols'], cs)
            grow = jnp.sum(eye_pk * gcol, axis=0, keepdims=True)
            decay = jnp.exp(jnp.where(t['incl'], gcol - grow, NEG_INF))
            t['pw'] = jnp.where(t['strict'], t['kq'][:cs] * decay, 0.0)
            a_s[t['sq'], t['dr'], t['rows'], :] =jnp.where(t['incl'], t['kq'][cs:] * decay, 0.0)
            t['tm'] = eye_pk - t['pw']
        for t in st:
            t['pw'] = dot3_bd(t['pw'], t['pw'])
        for rnd in range(5):
            mm_bd = dot3_bd if rnd < GDN_EXACT_ROUNDS else dot1_bd
            for t in st:
                if rnd < 4:
                    r = mm_bd(jnp.concatenate([t['pw'], t['tm']], axis=0), t['pw'])
                    t['pw'] = r[:cs]
                    t['tm'] = t['tm'] + r[cs:]
                else:
                    t['tm'] = t['tm'] + mm_bd(t['tm'], t['pw'])
        for t in st:
            for h in range(GDN_HEADS):
                g = t['gcols'][h]
                gl = g[cs - 1:cs] if t['dr'] == 0 else g[0:1]
                kg = t['k'][:, h * GDN_DK:(h + 1) * GDN_DK] * jnp.exp(gl - g)
                kgt_s[t['sq'], t['dr'], t['cidx'], h] = kg.T.astype(BF16)
        for t in st:
            kbg = t['kb'] * lanes([jnp.exp(g) for g in t['gcols']], GDN_DK)
            for h in range(GDN_HEADS):
                hs = slice(h * GDN_DK, (h + 1) * GDN_DK)
                rhs = jnp.concatenate([t['vb'][:, hs], kbg[:, hs]], axis=1)
                uw_s[t['sq'], t['dr'], t['rows'], 2 * h * GDN_DK:2 * (h + 1) * GDN_DK] = _dot(
                    t['tm'][:, h * cs:(h + 1) * cs], rhs)

    for sq in range(ns):
        if nc // GDN_GROUP == 1:
            phase_a(sq, 0)
        else:
            lax.fori_loop(0, nc // GDN_GROUP, lambda it, c, sq=sq: (phase_a(sq, it), c)[1], 0)

    def phase_b(c, carry):
        ch = []
        for sq, dr in [(sq, dr) for sq in range(ns) for dr in range(2)]:
            cidx = c if dr == 0 else nc - 1 - c
            rows = pl.ds(pl.multiple_of(cidx * cs, cs), cs)
            gc_blk = gate_s[sq, 1 + dr, rows, :]
            for h in range(GDN_HEADS):
                hs = slice(h * GDN_DK, (h + 1) * GDN_DK)
                lane = dr * GDN_HEADS + h
                gcol = gc_blk[:, lane:lane + 1]
                ch.append(dict(sq=sq, dr=dr, h=h, rows=rows, hs=hs, gcol=gcol, cidx=cidx,
                               gl=gcol[cs - 1:cs] if dr == 0 else gcol[0:1],
                               uw=uw_s[sq, dr, rows, 2 * h * GDN_DK:2 * (h + 1) * GDN_DK],
                               amat=a_s[sq, dr, rows, h * cs:(h + 1) * cs]))
        for t in ch:
            t['s'] = st_s[t['sq'], t['dr'], t['h']]
            qg = q_s[t['sq'], t['rows'], t['hs']] * jnp.exp(t['gcol'])
            t['ws'] = _dot(jnp.concatenate([t['uw'][:, GDN_DV:], qg], axis=0), t['s'])
        for t in ch:
            vn = t['uw'][:, :GDN_DV] - t['ws'][:cs]
            o = t['ws'][cs:] + _dot(t['amat'], vn)
            kgt = kgt_s[t['sq'], t['dr'], t['cidx'], t['h']]
            st_s[t['sq'], t['dr'], t['h']] = t['s'] * jnp.exp(t['gl']) + _dot(kgt, vn)
            if t['dr'] == 0:
                of_s[t['sq'], t['rows'], t['hs']] = o
            else:
                ob_s[t['sq'], t['rows'], t['hs']] = o
        return carry

    lax.fori_loop(0, nc, phase_b, 0)
    sf_ref[...] = st_s[...]

    for sq in range(ns):
        z = z_ref[sq]
        for h in range(GDN_HEADS):
            hs = slice(h * GDN_DV, (h + 1) * GDN_DV)
            o = of_s[sq, :, hs] + ob_s[sq, :, hs]
            o = o * lax.rsqrt(jnp.mean(o * o, axis=-1, keepdims=True) + EPS) * ng_ref[...]
            o_ref[sq, :, hs] = o * _silu(z[:, hs])


def _gdn_mixer(qkv, z, ab, p, j, s0):
    bsz, n, _ = qkv.shape
    gp = jnp.zeros((8, 128), F32)
    gp = gp.at[0, :8].set(p['gdn_a_log'][j].reshape(8)).at[1, :8].set(p['gdn_dt_bias'][j].reshape(8))
    cw = jnp.zeros((8, 3 * GDN_WIDTH), F32).at[:3].set(p['gdn_conv_w'][j])
    ns = GDN_SEQS_SHORT if (n <= GDN_SHORT_LEN and bsz % GDN_SEQS_SHORT == 0) else 1
    blk = lambda w_: pl.BlockSpec((ns, n, w_), lambda b: (b, 0, 0))
    sblk = pl.BlockSpec((ns, 2, GDN_HEADS, GDN_DK, GDN_DV), lambda b: (b, 0, 0, 0, 0))
    tok = lambda w_: pltpu.VMEM((ns, n, w_), F32)
    return pl.pallas_call(
        functools.partial(_gdn_kernel, zero_init=s0 is None),
        grid=(bsz // ns,),
        in_specs=[blk(3 * GDN_WIDTH), blk(GDN_WIDTH), blk(128),
                  pl.BlockSpec((8, 3 * GDN_WIDTH), lambda b: (0, 0)),
                  pl.BlockSpec((8, 128), lambda b: (0, 0)),
                  pl.BlockSpec((1, GDN_DV), lambda b: (0, 0))] + ([] if s0 is None else [sblk]),
        out_specs=[blk(GDN_WIDTH), sblk],
        out_shape=[jax.ShapeDtypeStruct((bsz, n, GDN_WIDTH), F32),
                   jax.ShapeDtypeStruct((bsz, 2, GDN_HEADS, GDN_DK, GDN_DV), F32)],
        scratch_shapes=[tok(GDN_WIDTH), tok(GDN_WIDTH), tok(GDN_WIDTH),
                        pltpu.VMEM((ns, 3, n, 128), F32),
                        tok(GDN_WIDTH), tok(GDN_WIDTH),
                        pltpu.VMEM((ns, 2, GDN_HEADS, GDN_DK, GDN_DV), F32),
                        pltpu.VMEM((ns, 2, n, 2 * GDN_WIDTH), F32),
                        pltpu.VMEM((ns, 2, n, GDN_HEADS * GDN_CHUNK), F32),
                        pltpu.VMEM((ns, 2, n // GDN_CHUNK, GDN_HEADS, GDN_DK, GDN_CHUNK), BF16)],
        compiler_params=_cparams(("parallel",)),
        name="gdn_mixer",
    )(qkv, z, ab, cw, gp, p['gdn_norm_g'][j][None], *(() if s0 is None else (s0,)))


def _proj_cd_kernel(x_ref, g_ref, sh_ref, sc_ref, w_ref, gm_ref, gain_ref, cos_ref, sin_ref,
                    qc_ref, kc_ref, vc_ref, qd_ref, kd_ref, vd_ref, *, rope):
    ns, tm, d = x_ref.shape
    h = _rms_mod(x_ref[...], g_ref[...], sh_ref[...], sc_ref[...]).reshape(ns * tm, d)
    y = _dot(h, w_ref[...])
    lane = lax.broadcasted_iota(jnp.int32, (1, 512), 1)
    low = (lane % 32) < 16

    def head_norm(t, gain, scale):
        w = t.shape[1]
        sq = t * t
        ms = jnp.concatenate([_dot(sq[:, c:c + 128], gm_ref[...]) for c in range(0, w, 128)], axis=1)
        t = t * lax.rsqrt(ms + EPS) * gain
        if rope:
            part = jnp.where(low[:, :w], pltpu.roll(t, w - 16, 1), pltpu.roll(t, 16, 1))
            t = t * cos_ref[:, :w] + part * sin_ref[:, :w]
        return t * scale if scale != 1.0 else t

    def put(ref, v):
        ref[...] = v.reshape(ns, tm, v.shape[-1])

    put(qc_ref, head_norm(y[:, 0:512], gain_ref[0:1, :], ATTN_SCALE))
    put(kc_ref, head_norm(y[:, 512:640], gain_ref[1:2, :128], 1.0))
    put(vc_ref, y[:, 640:768])
    put(qd_ref, head_norm(y[:, 768:1280], gain_ref[2:3, :], ATTN_SCALE))
    put(kd_ref, head_norm(y[:, 1280:1792], gain_ref[3:4, :], 1.0))
    put(vd_ref, y[:, 1792:2304])


def _rope_tables(n):
    rows = n // GRID_W
    row = jnp.repeat(jnp.arange(rows), GRID_W).astype(F32)
    col = jnp.tile(jnp.arange(GRID_W), rows).astype(F32)
    quarter = HEAD_DIM // 4
    inv = ROPE_THETA ** (-jnp.arange(quarter, dtype=F32) / quarter)
    ang_r = row[:, None] * inv[None, :]
    ang_c = col[:, None] * inv[None, :]
    cos = jnp.concatenate([jnp.cos(ang_r), jnp.cos(ang_r), jnp.cos(ang_c), jnp.cos(ang_c)], axis=-1)
    sin = jnp.concatenate([-jnp.sin(ang_r), jnp.sin(ang_r), -jnp.sin(ang_c), jnp.sin(ang_c)], axis=-1)
    return jnp.tile(cos, (1, 8)), jnp.tile(sin, (1, 8))


def _proj_cd(x, g, shift, scale, w, p, j, rope):
    bsz, n, d = x.shape
    tm = 256
    per_seq = shift.shape[0] > 1
    midx = (lambda b, i: (b, 0, 0)) if per_seq else (lambda b, i: (0, 0, 0))
    lane = jnp.arange(128)
    gmat = ((lane[:, None] // HEAD_DIM) == (lane[None, :] // HEAD_DIM)).astype(F32) / HEAD_DIM
    gains = jnp.zeros((8, 512), F32)
    gains = gains.at[0].set(jnp.tile(p['c_qn'][j], 8)).at[1].set(jnp.tile(p['c_kn'][j], 8))
    gains = gains.at[2].set(jnp.tile(p['d_qn'][j], 8)).at[3].set(jnp.tile(p['d_kn'][j], 8))
    if rope:
        cos, sin = _rope_tables(n)
    else:
        cos, sin = jnp.ones((tm, 512), F32), jnp.zeros((tm, 512), F32)
    tidx = (lambda b, i: (i, 0)) if rope else (lambda b, i: (0, 0))
    ns = 2 if (not per_seq and not rope and bsz % 2 == 0) else 1
    blk = lambda w_: pl.BlockSpec((ns, tm, w_), lambda b, i: (b, i, 0))
    return pl.pallas_call(
        functools.partial(_proj_cd_kernel, rope=rope),
        grid=(bsz // ns, n // tm),
        in_specs=[blk(d),
                  pl.BlockSpec((1, d), lambda b, i: (0, 0)),
                  pl.BlockSpec((1, 1, d), midx),
                  pl.BlockSpec((1, 1, d), midx),
                  pl.BlockSpec((d, CD_IN), lambda b, i: (0, 0)),
                  pl.BlockSpec((128, 128), lambda b, i: (0, 0)),
                  pl.BlockSpec((8, 512), lambda b, i: (0, 0)),
                  pl.BlockSpec((tm, 512), tidx),
                  pl.BlockSpec((tm, 512), tidx)],
        out_specs=[blk(512), blk(128), blk(128), blk(512), blk(512), blk(512)],
        out_shape=[jax.ShapeDtypeStruct((bsz, n, w_), F32) for w_ in (512, 128, 128, 512, 512, 512)],
        compiler_params=_cparams(("parallel", "parallel")),
        name="proj_cd",
    )(x, g, shift, scale, w, gmat.astype(BF16), gains, cos, sin)


def _probs(scores, extra=None):
    m = scores[0].max(axis=-1, keepdims=True)
    for s in scores[1:]:
        m = jnp.maximum(m, s.max(axis=-1, keepdims=True))
    if extra is not None:
        m = jnp.maximum(m, extra)
    return [jnp.exp(s - m).astype(BF16) for s in scores], m


def _attn_kernel(qc_ref, kc_ref, vc_ref, qd_ref, kd_ref, vd_ref, *rest, windowed, n_ctx, lam_scale):
    if n_ctx:
        ck_ref, cv_ref, dk_ref, dv_ref, misc_ref, sub_ref, oc_ref, od_ref = rest
    else:
        misc_ref, sub_ref, oc_ref, od_ref = rest
    tq = qc_ref.shape[1]
    n = kc_ref.shape[1]
    start = pl.program_id(1) * tq
    if windowed:
        span = tq + 2 * WINDOW
        k0 = pl.multiple_of(jnp.clip(start - WINDOW, 0, n - span), 128)
        krows = pl.ds(k0, span)
        qpos = start + lax.broadcasted_iota(jnp.int32, (tq, span), 0)
        kpos = k0 + lax.broadcasted_iota(jnp.int32, (tq, span), 1)
        ok = jnp.abs(qpos - kpos) <= WINDOW
    else:
        krows = pl.ds(0, n)
    low = lax.broadcasted_iota(jnp.int32, (1, 2 * HEAD_DIM), 1) < HEAD_DIM

    def key_ops(k):
        kr = pltpu.roll(k, HEAD_DIM, 1)
        z = jnp.zeros_like(k)
        return {(0, 0): jnp.where(low, k, z).astype(BF16), (0, 1): jnp.where(low, z, kr).astype(BF16),
                (1, 0): jnp.where(low, kr, z).astype(BF16), (1, 1): jnp.where(low, z, k).astype(BF16)}

    def with_ones(v):
        return jnp.concatenate([v, jnp.ones_like(v)], axis=1)

    def val_ops(v):
        return {key: with_ones(op) for key, op in key_ops(v).items()}

    def half_ops(k):
        z = jnp.zeros_like(k)
        return [jnp.where(low, k, z).astype(BF16), jnp.where(low, z, k).astype(BF16)]

    seqs = range(qc_ref.shape[0])

    c_scores, d_scores = [[] for _ in seqs], [[] for _ in seqs]
    for sq in seqs:
        qc = qc_ref[sq].astype(BF16)
        qd = qd_ref[sq].astype(BF16)
        kc_ops = key_ops(kc_ref[sq, krows, :])
        ck_ops = key_ops(ck_ref[sq]) if n_ctx else None
        for h in range(C_HEADS):
            key = (h // C_GROUP, h % 2)
            q = qc[:, (h // 2) * 128:(h // 2 + 1) * 128]
            sc = [_dot_nt(q, kc_ops[key])]
            if n_ctx:
                sc.append(_dot_nt(q, ck_ops[key]))
            c_scores[sq].append(sc)
        for h in range(D_HEADS):
            q = qd[:, h * 128:(h + 1) * 128]
            kd_ops = half_ops(kd_ref[sq, :, h * 128:(h + 1) * 128])
            dk_ops = half_ops(dk_ref[sq, :, h * 128:(h + 1) * 128]) if n_ctx else None
            for c in range(2):
                sc = [_dot_nt(q, kd_ops[c])]
                if n_ctx:
                    sc.append(_dot_nt(q, dk_ops[c]))
                d_scores[sq].append(sc)

    c_probs, d_probs = [[] for _ in seqs], [[] for _ in seqs]
    for sq in seqs:
        for h in range(C_HEADS):
            sc = c_scores[sq][h]
            if windowed:
                sc[0] = jnp.where(ok, sc[0], NEG_INF)
            c_probs[sq].append(_probs(sc, misc_ref[0:1, h:h + 1]))
        d_probs[sq] = [_probs(sc)[0] for sc in d_scores[sq]]

    lam = misc_ref[1:2, 0:1]
    for sq in seqs:
        vc_ops = val_ops(vc_ref[sq, krows, :])
        cv_ops = val_ops(cv_ref[sq]) if n_ctx else None
        for j in range(C_HEADS // 2):
            pair = None
            for h in (2 * j, 2 * j + 1):
                key = (h // C_GROUP, h % 2)
                ps, m = c_probs[sq][h]
                r = jnp.dot(ps[0], vc_ops[key], preferred_element_type=F32)
                if n_ctx:
                    r = r + jnp.dot(ps[1], cv_ops[key], preferred_element_type=F32)
                o = r[:, :128] / (r[:, 128:] + jnp.exp(misc_ref[0:1, h:h + 1] - m))
                pair = o if pair is None else pair + o
            oc_ref[sq, :, j * 128:(j + 1) * 128] = pair
        for h in range(D_HEADS):
            vsl = slice(h * D_VDIM, (h + 1) * D_VDIM)
            v_op = with_ones(vd_ref[sq, :, vsl]).astype(BF16)
            dv_op = with_ones(dv_ref[sq, :, vsl]).astype(BF16) if n_ctx else None
            parts = []
            for c in range(2):
                ps = d_probs[sq][2 * h + c]
                r = jnp.dot(ps[0], v_op, preferred_element_type=F32)
                if n_ctx:
                    r = r + jnp.dot(ps[1], dv_op, preferred_element_type=F32)
                parts.append(r[:, :D_VDIM] / r[:, D_VDIM:])
            o = parts[0] - lam * parts[1]
            o = o * lax.rsqrt(jnp.mean(o * o, axis=-1, keepdims=True) + EPS) * sub_ref[...] * lam_scale
            od_ref[sq, :, vsl] = o


def _attention(qc, kc, vc, qd, kd, vd, caches, misc, subln, lam_init, tq, windowed):
    bsz, n, _ = qc.shape
    ns = ATTN_SEQS_SHORT if (not windowed and tq == n and bsz % ATTN_SEQS_SHORT == 0) else 1
    qblk = lambda w_: pl.BlockSpec((ns, tq, w_), lambda b, i: (b, i, 0))
    kblk = lambda rows, w_: pl.BlockSpec((ns, rows, w_), lambda b, i: (b, 0, 0))
    in_specs = [qblk(512), kblk(n, 128), kblk(n, 128), qblk(512), kblk(n, 512), kblk(n, 512)]
    args = [qc, kc, vc, qd, kd, vd]
    n_ctx = 0
    if caches is not None:
        n_ctx = caches[0].shape[1]
        in_specs += [kblk(n_ctx, 128), kblk(n_ctx, 128), kblk(n_ctx, 512), kblk(n_ctx, 512)]
        args += list(caches)
    in_specs += [pl.BlockSpec((8, 128), lambda b, i: (0, 0)), pl.BlockSpec((1, D_VDIM), lambda b, i: (0, 0))]
    args += [misc, subln]
    return pl.pallas_call(
        functools.partial(_attn_kernel, windowed=windowed, n_ctx=n_ctx, lam_scale=1.0 - lam_init),
        grid=(bsz // ns, n // tq),
        in_specs=in_specs,
        out_specs=[qblk(512), qblk(512)],
        out_shape=[jax.ShapeDtypeStruct((bsz, n, 512), F32), jax.ShapeDtypeStruct((bsz, n, 512), F32)],
        compiler_params=_cparams(("parallel", "parallel")),
        name="attn_win" if windowed else "attn_ctx",
    )(*args)


def _post_kernel(x_ref, xp_ref, xn_ref, ma_ref, map_ref, man_ref, mb_ref, mbp_ref, mbn_ref,
                 g1_ref, sh_ref, sc_ref, g2_ref, ng_ref, wo_ref, wup_ref, cw_ref, wdn_ref, o_ref, act_ref,
                 *, seq_len):
    rows = x_ref.shape[0]
    ext = rows + 2 * POST_HALO
    half = ma_ref.shape[1]
    xe = jnp.concatenate([xp_ref[...], x_ref[...], xn_ref[...]], axis=0)
    mae = jnp.concatenate([map_ref[...], ma_ref[...], man_ref[...]], axis=0)
    mbe = jnp.concatenate([mbp_ref[...], mb_ref[...], mbn_ref[...]], axis=0)
    x1 = xe + g1_ref[0] * (_dot(mae, wo_ref[:half, :]) + _dot(mbe, wo_ref[half:, :]))
    h = _rms_mod(x1, ng_ref[...], sh_ref[0], sc_ref[0]).astype(BF16)
    x1 = x1[POST_HALO:POST_HALO + rows]
    row0 = pl.program_id(0) * rows - POST_HALO
    pos = (row0 + lax.broadcasted_iota(jnp.int32, (ext, 1), 0)) % seq_len
    first = pos == 0
    last = pos == seq_len - 1
    for c in range(FF_CHUNKS):
        cs = slice(c * FF_CHUNK, (c + 1) * FF_CHUNK)
        a = jnp.dot(h, wup_ref[:, cs], preferred_element_type=F32)
        b = jnp.dot(h, wup_ref[:, D_FF + c * FF_CHUNK:D_FF + (c + 1) * FF_CHUNK], preferred_element_type=F32)
        am = jnp.where(first, 0.0, pltpu.roll(a, 1, 0))
        ap = jnp.where(last, 0.0, pltpu.roll(a, ext - 1, 0))
        a = am * cw_ref[0:1, cs] + a * cw_ref[1:2, cs] + ap * cw_ref[2:3, cs] + cw_ref[3:4, cs]
        act_ref[:, cs] = (_silu(a) * b)[POST_HALO:POST_HALO + rows].astype(BF16)
    ffn = jnp.dot(act_ref[...], wdn_ref[...], preferred_element_type=F32)
    o_ref[...] = x1 + g2_ref[0] * ffn


def _post(x, mix_a, mix_b, g1, sh2, sc2, g2, norm_g, w_out, wup, cw, wdn, layer):
    bsz, seq_len, d = x.shape
    half = mix_a.shape[-1]
    rows = POST_ROWS
    total = bsz * seq_len
    nhalo = total // POST_HALO
    per_seq = g1.shape[0] > 1
    midx = (lambda i: ((i * rows) // seq_len, 0, 0)) if per_seq else (lambda i: (0, 0, 0))
    mspec = pl.BlockSpec((1, 1, d), midx)
    pidx = lambda i: (jnp.maximum(i * (rows // POST_HALO) - 1, 0), 0)
    nidx = lambda i: (jnp.minimum((i + 1) * (rows // POST_HALO), nhalo - 1), 0)
    trio = lambda w_: [pl.BlockSpec((rows, w_), lambda i: (i, 0)), pl.BlockSpec((POST_HALO, w_), pidx),
                       pl.BlockSpec((POST_HALO, w_), nidx)]
    whole = lambda shape: pl.BlockSpec(shape, lambda i: (0,) * len(shape), pipeline_mode=pl.Buffered(1))
    x2 = x.reshape(total, d)
    a2 = mix_a.reshape(total, half)
    b2 = mix_b.reshape(total, half)
    out = pl.pallas_call(
        functools.partial(_post_kernel, seq_len=seq_len),
        grid=(total // rows,),
        in_specs=trio(d) + trio(half) + trio(half) + [
            mspec, mspec, mspec, mspec,
            pl.BlockSpec((1, d), lambda i: (0, 0)),
            whole((d, d)),
            pl.BlockSpec((None, d, 2 * D_FF), lambda i: (layer, 0, 0), pipeline_mode=pl.Buffered(1)),
            whole((8, D_FF)),
            pl.BlockSpec((None, D_FF, d), lambda i: (layer, 0, 0), pipeline_mode=pl.Buffered(1))],
        out_specs=pl.BlockSpec((rows, d), lambda i: (i, 0)),
        out_shape=jax.ShapeDtypeStruct((total, d), F32),
        scratch_shapes=[pltpu.VMEM((rows, D_FF), BF16)],
        compiler_params=_cparams(("parallel",)),
        name="post_ffn",
    )(x2, x2, x2, a2, a2, a2, b2, b2, b2, g1, sh2, sc2, g2, norm_g, w_out, wup, cw, wdn)
    return out.reshape(bsz, seq_len, d)


def _ffn_weights(p, l):
    cw = jnp.zeros((8, D_FF), F32).at[:3].set(p['ffn_conv_w'][l]).at[3].set(p['ffn_conv_b'][l])
    return p['ffn_up'].astype(BF16), cw, p['ffn_down'].astype(BF16)


def _lambda_init(layer):
    return 0.8 - 0.6 * math.exp(-0.3 * layer)


def _trunk(x, mods, p, states, caches):
    bsz, n, d = x.shape
    nseg = n // S5_SEG
    depth = p['w_mod'].shape[0]
    news = {k: [] for k in ('s5r', 's5i', 'gdn', 'ck', 'cv', 'dk', 'dv')}
    for l in range(depth):
        j = l // 2
        sh1, sc1, g1, sh2, sc2, g2 = mods[l]
        ng1 = p['norm1_g'][l][None]
        if l % 2 == 0:
            w_in = p['w_in_ab'][j]
            w_gate = jnp.zeros((d, 128), BF16).at[:, :w_in.shape[1] - AB_MAIN].set(
                w_in[:, AB_MAIN:].astype(BF16))
            rows = bsz * nseg
            per_row = lambda m: jnp.repeat(m, nseg, axis=0) if m.shape[0] > 1 else m
            u_t, qkv, z, ab = _proj_ab(x.reshape(rows, S5_SEG, d), ng1, per_row(sh1), per_row(sc1),
                                       w_in[:, :AB_MAIN].astype(BF16), w_gate)
            qkv, z, ab = (t.reshape(bsz, n, t.shape[-1]) for t in (qkv, z, ab))
            if states is None:
                h0r = jnp.zeros((2, bsz, S5_GROUPS * S5_STATE), F32)
                h0i = h0r
                s0 = None
            else:
                h0r = states[0][:, j].reshape(bsz, 2, -1).transpose(1, 0, 2)
                h0i = states[1][:, j].reshape(bsz, 2, -1).transpose(1, 0, 2)
                s0 = states[2][:, j]
            ya, fr, fi = _s5_mixer(u_t, p, j, h0r, h0i, bsz, nseg)
            yb, sg = _gdn_mixer(qkv, z, ab, p, j, s0)
            mix = (ya.reshape(bsz, n, S5_WIDTH), yb)
            w_out = p['w_out_ab'][j]
            news['s5r'].append(fr.transpose(1, 0, 2).reshape(bsz, 2, S5_GROUPS, S5_STATE))
            news['s5i'].append(fi.transpose(1, 0, 2).reshape(bsz, 2, S5_GROUPS, S5_STATE))
            news['gdn'].append(sg)
        else:
            lam_init = _lambda_init(l)
            f = lambda name: p[name][j]
            lam = (jnp.exp(jnp.sum(f('d_lq1') * f('d_lk1'))) - jnp.exp(jnp.sum(f('d_lq2') * f('d_lk2')))
                   + lam_init)
            misc = jnp.zeros((8, 128), F32).at[0, :C_HEADS].set(p['c_sink'][j]).at[1, :].set(lam)
            rope = caches is not None
            qc, kc, vc, qd, kd, vd = _proj_cd(x, ng1, sh1, sc1, p['w_in_cd'][j].astype(BF16), p, j, rope)
            if caches is None:
                mix = _attention(qc, kc, vc, qd, kd, vd, None, misc, p['d_subln'][j][None],
                                 lam_init, n, False)
            else:
                n_ctx = caches[0].shape[2]
                cc = (caches[0][:, j].reshape(bsz, n_ctx, 128), caches[1][:, j].reshape(bsz, n_ctx, 128),
                      caches[2][:, j].reshape(bsz, n_ctx, 512), caches[3][:, j].reshape(bsz, n_ctx, 512))
                mix = _attention(qc, kc, vc, qd, kd, vd, cc, misc, p['d_subln'][j][None],
                                 lam_init, Q_BLOCK, True)
            w_out = p['w_out_cd'][j]
            news['ck'].append(kc.reshape(bsz, n, C_KV_HEADS, HEAD_DIM))
            news['cv'].append(vc.reshape(bsz, n, C_KV_HEADS, HEAD_DIM))
            news['dk'].append(kd.reshape(bsz, n, D_HEADS, 2, HEAD_DIM))
            news['dv'].append(vd.reshape(bsz, n, D_HEADS, D_VDIM))
        wup, cw, wdn = _ffn_weights(p, l)
        x = _post(x, mix[0], mix[1], g1, sh2, sc2, g2, p['norm2_g'][l][None], w_out.astype(BF16),
                  wup, cw, wdn, l)
    return x, news


def kernel(x_prompt, x_sample, c, state_s5_re, state_s5_im, state_gdn, cache_c_k, cache_c_v, cache_d_k, cache_d_v, c_ctx, w_mod, b_mod, norm1_g, norm2_g, w_in_ab, w_out_ab, s5_lam_re, s5_lam_im, s5_log_dt, s5_b_re, s5_b_im, s5_c_re, s5_c_im, s5_d, s5_w_glu, s5_b_glu, gdn_conv_w, gdn_a_log, gdn_dt_bias, gdn_norm_g, w_in_cd, w_out_cd, c_qn, c_kn, c_sink, d_qn, d_kn, d_lq1, d_lk1, d_lq2, d_lk2, d_subln, ffn_up, ffn_conv_w, ffn_conv_b, ffn_down):
    p = dict(w_mod=w_mod, b_mod=b_mod, norm1_g=norm1_g, norm2_g=norm2_g, w_in_ab=w_in_ab, w_out_ab=w_out_ab,
             s5_lam_re=s5_lam_re, s5_lam_im=s5_lam_im, s5_log_dt=s5_log_dt, s5_b_re=s5_b_re, s5_b_im=s5_b_im,
             s5_c_re=s5_c_re, s5_c_im=s5_c_im, s5_d=s5_d, s5_w_glu=s5_w_glu, s5_b_glu=s5_b_glu,
             gdn_conv_w=gdn_conv_w, gdn_a_log=gdn_a_log, gdn_dt_bias=gdn_dt_bias, gdn_norm_g=gdn_norm_g,
             w_in_cd=w_in_cd, w_out_cd=w_out_cd, c_qn=c_qn, c_kn=c_kn, c_sink=c_sink, d_qn=d_qn, d_kn=d_kn,
             d_lq1=d_lq1, d_lk1=d_lk1, d_lq2=d_lq2, d_lk2=d_lk2, d_subln=d_subln,
             ffn_up=ffn_up, ffn_conv_w=ffn_conv_w, ffn_conv_b=ffn_conv_b, ffn_down=ffn_down)
    depth = w_mod.shape[0]
    n_dec = c.shape[0]
    mod = _modulation(jnp.concatenate([c_ctx[None], c], axis=0), w_mod, b_mod)
    split6 = lambda m: [m[:, None, k * D_MODEL:(k + 1) * D_MODEL] for k in range(6)]
    mods_ctx = [split6(mod[l, 0:1]) for l in range(depth)]
    mods_dec = [split6(mod[l, 1:1 + n_dec]) for l in range(depth)]

    y_prompt, nw = _trunk(x_prompt, mods_ctx, p, None, None)
    y_sample, _ = _trunk(x_sample, mods_dec, p, (state_s5_re, state_s5_im, state_gdn),
                         (cache_c_k, cache_c_v, cache_d_k, cache_d_v))
    st = lambda name: jnp.stack(nw[name], axis=1)
    return (y_prompt, y_sample, st('s5r'), st('s5i'), st('gdn'), st('ck'), st('cv'), st('dk'), st('dv'))
```

```python
import functools
import math

import jax
import jax.numpy as jnp
from jax import lax
from jax.experimental import pallas as pl
from jax.experimental.pallas import tpu as pltpu

F32 = jnp.float32
BF16 = jnp.bfloat16

D_MODEL = 1024
GRID_W = 64
EPS = 1e-6
NEG_INF = -1e30

S5_WIDTH = 512
S5_GROUP = 16
S5_GROUPS = 32
S5_STATE = 64
S5_TILE_GROUPS = 8
S5_TILE_CH = S5_TILE_GROUPS * S5_GROUP
S5_TILE_ST = S5_TILE_GROUPS * S5_STATE
S5_TILES = S5_GROUPS // S5_TILE_GROUPS
S5_SEG = 256
S5_SUB = 32
PROJ_AB_TOKENS = 64
S5_ROWS = 8

GDN_DK = 128
GDN_DV = 128
GDN_HEADS = 4
GDN_WIDTH = 512
GDN_CHUNK = 64
GDN_GROUP = 4
GDN_SHORT_LEN = 256
GDN_SEQS_SHORT = 2
GDN_EXACT_ROUNDS = 3

HEAD_DIM = 64
C_HEADS = 8
C_KV_HEADS = 2
C_GROUP = 4
WINDOW = 128
Q_BLOCK = 128
D_HEADS = 4
D_VDIM = 128
ATTN_SCALE = HEAD_DIM ** -0.5
ATTN_SEQS_SHORT = 2
ROPE_THETA = 10000.0

D_FF = 2816
FF_CHUNK = 256
FF_CHUNKS = D_FF // FF_CHUNK
POST_ROWS = 512
POST_HALO = 8

AB_MAIN = S5_WIDTH + 4 * GDN_WIDTH
AB_PAD = AB_MAIN + 128
CD_IN = 2304

MOD_TK = 128
MOD_LANES = 512
VMEM_LIMIT = 56 * 1024 * 1024


def _cparams(sem):
    return pltpu.CompilerParams(dimension_semantics=sem, vmem_limit_bytes=VMEM_LIMIT)


def _sigmoid(x):
    return 1.0 / (1.0 + jnp.exp(-x))


def _silu(x):
    return x * _sigmoid(x)


def _softplus(x):
    return jnp.maximum(x, 0.0) + jnp.log(1.0 + jnp.exp(-jnp.abs(x)))


def _gelu_tanh(x):
    return 0.5 * x * (1.0 + jnp.tanh(math.sqrt(2.0 / math.pi) * (x + 0.044715 * (x * x * x))))


def _rms_mod(x, g, shift, scale):
    y = x * lax.rsqrt(jnp.mean(x * x, axis=-1, keepdims=True) + EPS)
    return (y * g) * (1.0 + scale) + shift


def _dot(a, b):
    return jnp.dot(a.astype(BF16), b.astype(BF16), preferred_element_type=F32)


def _dot_nt(a, b):
    return lax.dot_general(a.astype(BF16), b.astype(BF16), (((1,), (1,)), ((), ())),
                           preferred_element_type=F32)


def _hi_lo(a):
    hi = a.astype(BF16)
    return hi, (a - hi.astype(F32)).astype(BF16)


def _mod_kernel(ct_ref, w_ref, b_ref, o_ref, acc_ref, *, n_rows):
    k = pl.program_id(1)

    @pl.when(k == 0)
    def _():
        acc_ref[...] = jnp.zeros_like(acc_ref)

    tk, n_out = w_ref.shape[1:]
    s = _silu(ct_ref[...])
    sb = [[jnp.broadcast_to(s[g * 8:(g + 1) * 8, m:m + 1], (8, MOD_LANES)) for g in range(tk // 8)]
          for m in range(n_rows)]
    for c in range(n_out // MOD_LANES):
        cols = slice(c * MOD_LANES, (c + 1) * MOD_LANES)
        accs = [acc_ref[m, :, cols] for m in range(n_rows)]
        for g in range(tk // 8):
            wg = w_ref[0, g * 8:(g + 1) * 8, cols]
            accs = [a + wg * sb[m][g] for m, a in enumerate(accs)]
        for m in range(n_rows):
            acc_ref[m, :, cols] = accs[m]

    @pl.when(k == pl.num_programs(1) - 1)
    def _():
        o_ref[0] = jnp.zeros(o_ref.shape[1:], F32)
        for m in range(n_rows):
            o_ref[0, m:m + 1, :] = jnp.sum(acc_ref[m], axis=0, keepdims=True) + b_ref[0]


def _modulation(cvecs, w_mod, b_mod):
    n, d = cvecs.shape
    depth, _, n_out = w_mod.shape
    ct = jnp.zeros((d, 8), F32).at[:, :n].set(cvecs.T)
    return pl.pallas_call(
        functools.partial(_mod_kernel, n_rows=n),
        grid=(depth, d // MOD_TK),
        in_specs=[pl.BlockSpec((MOD_TK, 8), lambda l, k: (k, 0)),
                  pl.BlockSpec((1, MOD_TK, n_out), lambda l, k: (l, k, 0)),
                  pl.BlockSpec((1, 1, n_out), lambda l, k: (l, 0, 0))],
        out_specs=pl.BlockSpec((1, 8, n_out), lambda l, k: (l, 0, 0)),
        out_shape=jax.ShapeDtypeStruct((depth, 8, n_out), F32),
        scratch_shapes=[pltpu.VMEM((n, 8, n_out), F32)],
        compiler_params=_cparams(("parallel", "arbitrary")),
        name="adaln_mod",
    )(ct, w_mod, b_mod.reshape(depth, 1, n_out))


def _proj_ab_kernel(x_ref, g_ref, sh_ref, sc_ref, w_ref, wg_ref, u_ref, qkv_ref, z_ref, ab_ref):
    ns, tm, d = x_ref.shape
    h = _rms_mod(x_ref[...], g_ref[...], sh_ref[...], sc_ref[...])
    h = h.reshape(ns * tm, d).astype(BF16)
    y = jnp.dot(h, w_ref[...], preferred_element_type=F32)
    for s in range(ns):
        u_ref[:, s, :] = y[s * tm:(s + 1) * tm, :S5_WIDTH]
    qkv_ref[...] = y[:, S5_WIDTH:S5_WIDTH + 3 * GDN_WIDTH].reshape(ns, tm, 3 * GDN_WIDTH)
    z_ref[...] = y[:, S5_WIDTH + 3 * GDN_WIDTH:AB_MAIN].reshape(ns, tm, GDN_WIDTH)
    ab_ref[...] = jnp.dot(h, wg_ref[...], preferred_element_type=F32).reshape(ns, tm, 128)


def _proj_ab(x, g, shift, scale, w, w_gate):
    rows, n, d = x.shape
    tm = PROJ_AB_TOKENS
    per_row = shift.shape[0] > 1
    mspec = pl.BlockSpec((S5_ROWS, 1, d), lambda r, i: (r, 0, 0)) if per_row else \
        pl.BlockSpec((1, 1, d), lambda r, i: (0, 0, 0))
    blk = lambda w_: pl.BlockSpec((S5_ROWS, tm, w_), lambda r, i: (r, i, 0))
    return pl.pallas_call(
        _proj_ab_kernel,
        grid=(rows // S5_ROWS, n // tm),
        in_specs=[blk(d),
                  pl.BlockSpec((1, d), lambda r, i: (0, 0)),
                  mspec, mspec,
                  pl.BlockSpec((d, AB_MAIN), lambda r, i: (0, 0)),
                  pl.BlockSpec((d, 128), lambda r, i: (0, 0))],
        out_specs=[pl.BlockSpec((tm, S5_ROWS, S5_WIDTH), lambda r, i: (i, r, 0)),
                   blk(3 * GDN_WIDTH), blk(GDN_WIDTH), blk(128)],
        out_shape=[jax.ShapeDtypeStruct((n, rows, S5_WIDTH), F32),
                   jax.ShapeDtypeStruct((rows, n, 3 * GDN_WIDTH), F32),
                   jax.ShapeDtypeStruct((rows, n, GDN_WIDTH), F32),
                   jax.ShapeDtypeStruct((rows, n, 128), F32)],
        compiler_params=_cparams(("parallel", "parallel")),
        name="proj_ab",
    )(x, g, shift, scale, w, w_gate)


def _s5_kernel(u_ref, bm_ref, cm_ref, a_ref, h0r_ref, h0i_ref, *rest, want_y):
    if want_y:
        y_ref, fr_ref, fi_ref, xs_ref = rest
    else:
        fr_ref, fi_ref, xs_ref = rest
    n = u_ref.shape[0]
    sub = S5_SUB
    nsub = n // sub
    st = S5_TILE_ST

    def x_proj(d, k):
        u2 = u_ref[k * sub:(k + 1) * sub].reshape(sub * S5_ROWS, S5_TILE_CH)
        xs_ref[d, k * sub:(k + 1) * sub] = _dot(u2, bm_ref[d, 0]).reshape(sub, S5_ROWS, 2 * st)

    ar = [jnp.broadcast_to(a_ref[d, 0, 0:1, :], (S5_ROWS, st)) for d in range(2)]
    ai = [jnp.broadcast_to(a_ref[d, 0, 1:2, :], (S5_ROWS, st)) for d in range(2)]
    hr = [h0r_ref[0], h0r_ref[1]]
    hi = [h0i_ref[0], h0i_ref[1]]
    x_proj(0, 0)
    x_proj(1, nsub - 1)
    written = set()
    for k in range(nsub):
        ks = (k, nsub - 1 - k)
        if k + 1 < nsub:
            x_proj(0, ks[0] + 1)
            x_proj(1, ks[1] - 1)
        for t in range(sub):
            for d in range(2):
                tt = ks[d] * sub + (t if d == 0 else sub - 1 - t)
                x = xs_ref[d, tt]
                nr = ar[d] * hr[d] - ai[d] * hi[d] + x[:, :st]
                ni = ar[d] * hi[d] + ai[d] * hr[d] + x[:, st:]
                xs_ref[d, tt] = jnp.concatenate([nr, ni], axis=-1)
                hr[d], hi[d] = nr, ni
        if want_y:
            for d in range(2):
                rows = slice(ks[d] * sub, (ks[d] + 1) * sub)
                hs = xs_ref[d, rows].reshape(sub * S5_ROWS, 2 * st)
                yv = _dot(hs, cm_ref[d, 0]).reshape(sub, S5_ROWS, S5_TILE_CH)
                if ks[d] in written:
                    y_ref[rows] += yv
                else:
                    y_ref[rows] = yv
                    written.add(ks[d])
    for d in range(2):
        fr_ref[d] = hr[d]
        fi_ref[d] = hi[d]


def _s5_scan(u_t, bmat, cmat, amat, h0r, h0i, want_y=True):
    n, rows, _ = u_t.shape
    state = lambda: pl.BlockSpec((2, S5_ROWS, S5_TILE_ST), lambda r, j: (0, r, j))
    st_shape = jax.ShapeDtypeStruct((2, rows, S5_GROUPS * S5_STATE), F32)
    out_specs = [state(), state()]
    out_shape = [st_shape, st_shape]
    if want_y:
        out_specs.insert(0, pl.BlockSpec((n, S5_ROWS, S5_TILE_CH), lambda r, j: (0, r, j)))
        out_shape.insert(0, jax.ShapeDtypeStruct((n, rows, S5_WIDTH), F32))
    return pl.pallas_call(
        functools.partial(_s5_kernel, want_y=want_y),
        grid=(rows // S5_ROWS, S5_TILES),
        in_specs=[pl.BlockSpec((n, S5_ROWS, S5_TILE_CH), lambda r, j: (0, r, j)),
                  pl.BlockSpec((2, 1, S5_TILE_CH, 2 * S5_TILE_ST), lambda r, j: (0, j, 0, 0)),
                  pl.BlockSpec((2, 1, 2 * S5_TILE_ST, S5_TILE_CH), lambda r, j: (0, j, 0, 0)),
                  pl.BlockSpec((2, 1, 8, S5_TILE_ST), lambda r, j: (0, j, 0, 0)),
                  state(), state()],
        out_specs=out_specs,
        out_shape=out_shape,
        scratch_shapes=[pltpu.VMEM((2, n, S5_ROWS, 2 * S5_TILE_ST), F32)],
        compiler_params=_cparams(("parallel", "parallel")),
        name="s5_scan" if want_y else "s5_states",
    )(u_t, bmat, cmat, amat, h0r, h0i)


def _s5_params(p, j):
    lam_re, lam_im, log_dt = p['s5_lam_re'][j], p['s5_lam_im'][j], p['s5_log_dt'][j]
    dt = jnp.exp(log_dt)[..., None]
    mag = jnp.exp(lam_re * dt)
    ar, ai = mag * jnp.cos(lam_im * dt), mag * jnp.sin(lam_im * dt)
    den = lam_re * lam_re + lam_im * lam_im
    fr = ((ar - 1.0) * lam_re + ai * lam_im) / den
    fi = (ai * lam_re - (ar - 1.0) * lam_im) / den
    b_re, b_im = p['s5_b_re'][j], p['s5_b_im'][j]
    bbr = fr[..., None] * b_re - fi[..., None] * b_im
    bbi = fr[..., None] * b_im + fi[..., None] * b_re
    eye = jnp.eye(S5_TILE_GROUPS, dtype=F32)

    def in_blocks(t):
        t = t.reshape(2, S5_TILES, S5_TILE_GROUPS, S5_STATE, S5_GROUP)
        t = jnp.einsum('dtgpc,gh->dtgchp', t, eye)
        return t.reshape(2, S5_TILES, S5_TILE_CH, S5_TILE_ST)

    def out_blocks(t):
        t = t.reshape(2, S5_TILES, S5_TILE_GROUPS, S5_GROUP, S5_STATE)
        t = jnp.einsum('dtgcp,gh->dtgphc', t, eye)
        return t.reshape(2, S5_TILES, S5_TILE_ST, S5_TILE_CH)

    bmat = jnp.concatenate([in_blocks(bbr), in_blocks(bbi)], axis=-1).astype(BF16)
    cmat = jnp.concatenate([out_blocks(p['s5_c_re'][j]), -out_blocks(p['s5_c_im'][j])], axis=-2).astype(BF16)
    seg_mag = jnp.exp(lam_re * dt * S5_SEG)
    pr, pi = seg_mag * jnp.cos(lam_im * dt * S5_SEG), seg_mag * jnp.sin(lam_im * dt * S5_SEG)
    flat = lambda t: t.reshape(2, S5_TILES, 1, S5_TILE_ST)
    amat = jnp.concatenate([flat(ar), flat(ai), flat(pr), flat(pi),
                            jnp.zeros((2, S5_TILES, 4, S5_TILE_ST), F32)], axis=2)
    return bmat, cmat, amat


def _s5_glu_kernel(y_ref, u_ref, d_ref, w_ref, b_ref, o_ref):
    tm, ns, _ = y_ref.shape
    y = jnp.concatenate([y_ref[:, s, :] + d_ref[...] * u_ref[:, s, :] for s in range(ns)], axis=0)
    g = _gelu_tanh(y)
    out = g * _sigmoid(_dot(g, w_ref[...]) + b_ref[...])
    o_ref[...] = out.reshape(ns, tm, S5_WIDTH)


def _s5_glu(y_t, u_t, s5_d, w_glu, b_glu):
    n, rows, _ = y_t.shape
    tm = PROJ_AB_TOKENS
    tblk = pl.BlockSpec((tm, S5_ROWS, S5_WIDTH), lambda r, i: (i, r, 0))
    return pl.pallas_call(
        _s5_glu_kernel,
        grid=(rows // S5_ROWS, n // tm),
        in_specs=[tblk, tblk,
                  pl.BlockSpec((1, S5_WIDTH), lambda r, i: (0, 0)),
                  pl.BlockSpec((S5_WIDTH, S5_WIDTH), lambda r, i: (0, 0)),
                  pl.BlockSpec((1, S5_WIDTH), lambda r, i: (0, 0))],
        out_specs=pl.BlockSpec((S5_ROWS, tm, S5_WIDTH), lambda r, i: (r, i, 0)),
        out_shape=jax.ShapeDtypeStruct((rows, n, S5_WIDTH), F32),
        compiler_params=_cparams(("parallel", "parallel")),
        name="s5_glu",
    )(y_t, u_t, s5_d, w_glu, b_glu)


def _s5_mixer(u3, p, j, h0r, h0i, bsz, nseg):
    rows = bsz * nseg
    bmat, cmat, amat = _s5_params(p, j)
    if nseg == 1:
        y_t, fr, fi = _s5_scan(u3, bmat, cmat, amat, h0r, h0i)
    else:
        zero = jnp.zeros((2, bsz, nseg, S5_GROUPS * S5_STATE), F32)
        first = jnp.array([0, nseg - 1])
        seed = lambda h0: zero.at[jnp.arange(2), :, first].set(h0).reshape(2, rows, -1)
        fr, fi = _s5_scan(u3, bmat, cmat, amat, seed(h0r), seed(h0i), want_y=False)
        fr = fr.reshape(2, bsz, nseg, -1)
        fi = fi.reshape(2, bsz, nseg, -1)
        pr = amat[:, :, 2].reshape(2, 1, -1)
        pi = amat[:, :, 3].reshape(2, 1, -1)

        def chain(dr, order):
            hr, hi = (h0r[dr], h0i[dr])
            outs_r, outs_i = {}, {}
            for n_done, k in enumerate(order):
                outs_r[k], outs_i[k] = hr, hi
                if n_done == 0:
                    hr, hi = fr[dr, :, k], fi[dr, :, k]
                else:
                    hr, hi = (pr[dr] * hr - pi[dr] * hi + fr[dr, :, k],
                              pr[dr] * hi + pi[dr] * hr + fi[dr, :, k])
            st = lambda o: jnp.stack([o[k] for k in range(nseg)], axis=1)
            return st(outs_r), st(outs_i), hr, hi

        sr0, si0, er0, ei0 = chain(0, list(range(nseg)))
        sr1, si1, er1, ei1 = chain(1, list(range(nseg - 1, -1, -1)))
        start_r = jnp.stack([sr0, sr1]).reshape(2, rows, -1)
        start_i = jnp.stack([si0, si1]).reshape(2, rows, -1)
        y_t, _, _ = _s5_scan(u3, bmat, cmat, amat, start_r, start_i)
        fr = jnp.stack([er0, er1])
        fi = jnp.stack([ei0, ei1])
    ya = _s5_glu(y_t, u3, p['s5_d'][j][None], p['s5_w_glu'][j].astype(BF16), p['s5_b_glu'][j][None])
    return ya, fr, fi


def _gdn_kernel(qkv_ref, z_ref, ab_ref, cw_ref, gp_ref, ng_ref, *rest, zero_init):
    if zero_init:
        s0_ref = None
        o_ref, sf_ref, q_s, k_s, v_s, gate_s, of_s, ob_s, st_s, uw_s, a_s, kgt_s = rest
    else:
        s0_ref, o_ref, sf_ref, q_s, k_s, v_s, gate_s, of_s, ob_s, st_s, uw_s, a_s, kgt_s = rest
    ns, n = qkv_ref.shape[:2]
    nc = n // GDN_CHUNK
    row = lax.broadcasted_iota(jnp.int32, (n, 1), 0)

    for sq in range(ns):
        for blk in range(3 * GDN_HEADS):
            cols = slice(blk * GDN_DK, (blk + 1) * GDN_DK)
            hs = slice((blk % GDN_HEADS) * GDN_DK, (blk % GDN_HEADS + 1) * GDN_DK)
            x = qkv_ref[sq, :, cols]
            xm = jnp.where(row == 0, 0.0, pltpu.roll(x, 1, 0))
            xp = jnp.where(row == n - 1, 0.0, pltpu.roll(x, n - 1, 0))
            y = _silu(xm * cw_ref[0:1, cols] + x * cw_ref[1:2, cols] + xp * cw_ref[2:3, cols])
            if blk < GDN_HEADS:
                q_s[sq, :, hs] = y * lax.rsqrt(jnp.sum(y * y, axis=-1, keepdims=True) + EPS) * (GDN_DK ** -0.5)
            elif blk < 2 * GDN_HEADS:
                k_s[sq, :, hs] = y * lax.rsqrt(jnp.sum(y * y, axis=-1, keepdims=True) + EPS)
            else:
                v_s[sq, :, hs] = y

        ab = ab_ref[sq]
        beta = _sigmoid(ab)
        g = -jnp.exp(gp_ref[0:1, :]) * _softplus(ab + gp_ref[1:2, :])
        pos = row % GDN_CHUNK
        pre, suf = g, g
        sft = 1
        while sft < GDN_CHUNK:
            pre = pre + jnp.where(pos >= sft, pltpu.roll(pre, sft, 0), 0.0)
            suf = suf + jnp.where(pos < GDN_CHUNK - sft, pltpu.roll(suf, n - sft, 0), 0.0)
            sft *= 2
        gate_s[sq, 0] = beta
        gate_s[sq, 1] = pre
        gate_s[sq, 2] = suf

    st_s[...] = jnp.zeros_like(st_s) if zero_init else s0_ref[...]
    cs = GDN_CHUNK
    pk = GDN_HEADS * cs
    ri = lax.broadcasted_iota(jnp.int32, (cs, pk), 0)
    lane_pk = lax.broadcasted_iota(jnp.int32, (cs, pk), 1)
    ci = lane_pk % cs
    eye_pk = (ri == ci).astype(F32)
    head_pk = [(lax.broadcasted_iota(jnp.int32, (1, pk), 1) // cs) == h for h in range(GDN_HEADS)]
    head_w = [(lax.broadcasted_iota(jnp.int32, (1, GDN_WIDTH), 1) // GDN_DK) == h for h in range(GDN_HEADS)]

    def block_diag(p):
        return jnp.concatenate([jnp.where(m, p, jnp.zeros_like(p)) for m in head_pk], axis=0)

    def dot3_bd(a, p):
        a_hi, a_lo = _hi_lo(a)
        p_hi, p_lo = _hi_lo(p)
        b_hi, b_lo = block_diag(p_hi), block_diag(p_lo)
        mm = functools.partial(jnp.dot, preferred_element_type=F32)
        return mm(a_hi, b_hi) + (mm(a_lo, b_hi) + mm(a_hi, b_lo))

    def dot1_bd(a, p):
        return jnp.dot(a.astype(BF16), block_diag(p.astype(BF16)), preferred_element_type=F32)

    def lanes(cols, width):
        return jnp.concatenate([jnp.broadcast_to(c, (cs, width)) for c in cols], axis=1)

    def phase_a(sq, it):
        st = []
        for cc in range(GDN_GROUP):
            rows = pl.ds(pl.multiple_of((it * GDN_GROUP + cc) * cs, cs), cs)
            q_all, k_all, v_all = q_s[sq, rows, :], k_s[sq, rows, :], v_s[sq, rows, :]
            beta_blk = gate_s[sq, 0, rows, :]
            for dr in range(2):
                gc_blk = gate_s[sq, 1 + dr, rows, :]
                lane0 = dr * GDN_HEADS
                bcols = [beta_blk[:, 8 + lane0 + h:9 + lane0 + h] for h in range(GDN_HEADS)]
                gcols = [gc_blk[:, lane0 + h:lane0 + h + 1] for h in range(GDN_HEADS)]
                st.append(dict(sq=sq, dr=dr, rows=rows, cidx=it * GDN_GROUP + cc, q=q_all, k=k_all, v=v_all, bcols=bcols, gcols=gcols,
                               incl=(ri >= ci) if dr == 0 else (ri <= ci),
                               strict=(ri > ci) if dr == 0 else (ri < ci)))
        for t in st:
            b_w = lanes(t['bcols'], GDN_DK)
            t['kb'] = t['k'] * b_w
            t['vb'] = t['v'] * b_w
            k_bd = jnp.concatenate([jnp.where(m, t['k'], 0.0) for m in head_w], axis=0)
            t['kq'] = _dot_nt(jnp.concatenate([t['kb'], t['q']], axis=0), k_bd)
        for t in st:
            gcol = lanes(t['gcols'], cs)
            grow = jnp.sum(eye_pk * gcol, axis=0, keepdims=True)
            decay = jnp.exp(jnp.where(t['incl'], gcol - grow, NEG_INF))
            t['pw'] = jnp.where(t['strict'], t['kq'][:cs] * decay, 0.0)
            a_s[t['sq'], t['dr'], t['rows'], :] =jnp.where(t['incl'], t['kq'][cs:] * decay, 0.0)
            t['tm'] = eye_pk - t['pw']
        for t in st:
            t['pw'] = dot3_bd(t['pw'], t['pw'])
        for rnd in range(5):
            mm_bd = dot3_bd if rnd < GDN_EXACT_ROUNDS else dot1_bd
            for t in st:
                if rnd < 4:
                    r = mm_bd(jnp.concatenate([t['pw'], t['tm']], axis=0), t['pw'])
                    t['pw'] = r[:cs]
                    t['tm'] = t['tm'] + r[cs:]
                else:
                    t['tm'] = t['tm'] + mm_bd(t['tm'], t['pw'])
        for t in st:
            for h in range(GDN_HEADS):
                g = t['gcols'][h]
                gl = g[cs - 1:cs] if t['dr'] == 0 else g[0:1]
                kg = t['k'][:, h * GDN_DK:(h + 1) * GDN_DK] * jnp.exp(gl - g)
                kgt_s[t['sq'], t['dr'], t['cidx'], h] = kg.T.astype(BF16)
        for t in st:
            kbg = t['kb'] * lanes([jnp.exp(g) for g in t['gcols']], GDN_DK)
            for h in range(GDN_HEADS):
                hs = slice(h * GDN_DK, (h + 1) * GDN_DK)
                rhs = jnp.concatenate([t['vb'][:, hs], kbg[:, hs]], axis=1)
                uw_s[t['sq'], t['dr'], t['rows'], 2 * h * GDN_DK:2 * (h + 1) * GDN_DK] = _dot(
                    t['tm'][:, h * cs:(h + 1) * cs], rhs)

    for sq in range(ns):
        if nc // GDN_GROUP == 1:
            phase_a(sq, 0)
        else:
            lax.fori_loop(0, nc // GDN_GROUP, lambda it, c, sq=sq: (phase_a(sq, it), c)[1], 0)

    def phase_b(c, carry):
        ch = []
        for sq, dr in [(sq, dr) for sq in range(ns) for dr in range(2)]:
            cidx = c if dr == 0 else nc - 1 - c
            rows = pl.ds(pl.multiple_of(cidx * cs, cs), cs)
            gc_blk = gate_s[sq, 1 + dr, rows, :]
            for h in range(GDN_HEADS):
                hs = slice(h * GDN_DK, (h + 1) * GDN_DK)
                lane = dr * GDN_HEADS + h
                gcol = gc_blk[:, lane:lane + 1]
                ch.append(dict(sq=sq, dr=dr, h=h, rows=rows, hs=hs, gcol=gcol, cidx=cidx,
                               gl=gcol[cs - 1:cs] if dr == 0 else gcol[0:1],
                               uw=uw_s[sq, dr, rows, 2 * h * GDN_DK:2 * (h + 1) * GDN_DK],
                               amat=a_s[sq, dr, rows, h * cs:(h + 1) * cs]))
        for t in ch:
            t['s'] = st_s[t['sq'], t['dr'], t['h']]
            qg = q_s[t['sq'], t['rows'], t['hs']] * jnp.exp(t['gcol'])
            t['ws'] = _dot(jnp.concatenate([t['uw'][:, GDN_DV:], qg], axis=0), t['s'])
        for t in ch:
            vn = t['uw'][:, :GDN_DV] - t['ws'][:cs]
            o = t['ws'][cs:] + _dot(t['amat'], vn)
            kgt = kgt_s[t['sq'], t['dr'], t['cidx'], t['h']]
            st_s[t['sq'], t['dr'], t['h']] = t['s'] * jnp.exp(t['gl']) + _dot(kgt, vn)
            if t['dr'] == 0:
                of_s[t['sq'], t['rows'], t['hs']] = o
            else:
                ob_s[t['sq'], t['rows'], t['hs']] = o
        return carry

    lax.fori_loop(0, nc, phase_b, 0)
    sf_ref[...] = st_s[...]

    for sq in range(ns):
        z = z_ref[sq]
        for h in range(GDN_HEADS):
            hs = slice(h * GDN_DV, (h + 1) * GDN_DV)
            o = of_s[sq, :, hs] + ob_s[sq, :, hs]
            o = o * lax.rsqrt(jnp.mean(o * o, axis=-1, keepdims=True) + EPS) * ng_ref[...]
            o_ref[sq, :, hs] = o * _silu(z[:, hs])


def _gdn_mixer(qkv, z, ab, p, j, s0):
    bsz, n, _ = qkv.shape
    gp = jnp.zeros((8, 128), F32)
    gp = gp.at[0, :8].set(p['gdn_a_log'][j].reshape(8)).at[1, :8].set(p['gdn_dt_bias'][j].reshape(8))
    cw = jnp.zeros((8, 3 * GDN_WIDTH), F32).at[:3].set(p['gdn_conv_w'][j])
    ns = GDN_SEQS_SHORT if (n <= GDN_SHORT_LEN and bsz % GDN_SEQS_SHORT == 0) else 1
    blk = lambda w_: pl.BlockSpec((ns, n, w_), lambda b: (b, 0, 0))
    sblk = pl.BlockSpec((ns, 2, GDN_HEADS, GDN_DK, GDN_DV), lambda b: (b, 0, 0, 0, 0))
    tok = lambda w_: pltpu.VMEM((ns, n, w_), F32)
    return pl.pallas_call(
        functools.partial(_gdn_kernel, zero_init=s0 is None),
        grid=(bsz // ns,),
        in_specs=[blk(3 * GDN_WIDTH), blk(GDN_WIDTH), blk(128),
                  pl.BlockSpec((8, 3 * GDN_WIDTH), lambda b: (0, 0)),
                  pl.BlockSpec((8, 128), lambda b: (0, 0)),
                  pl.BlockSpec((1, GDN_DV), lambda b: (0, 0))] + ([] if s0 is None else [sblk]),
        out_specs=[blk(GDN_WIDTH), sblk],
        out_shape=[jax.ShapeDtypeStruct((bsz, n, GDN_WIDTH), F32),
                   jax.ShapeDtypeStruct((bsz, 2, GDN_HEADS, GDN_DK, GDN_DV), F32)],
        scratch_shapes=[tok(GDN_WIDTH), tok(GDN_WIDTH), tok(GDN_WIDTH),
                        pltpu.VMEM((ns, 3, n, 128), F32),
                        tok(GDN_WIDTH), tok(GDN_WIDTH),
                        pltpu.VMEM((ns, 2, GDN_HEADS, GDN_DK, GDN_DV), F32),
                        pltpu.VMEM((ns, 2, n, 2 * GDN_WIDTH), F32),
                        pltpu.VMEM((ns, 2, n, GDN_HEADS * GDN_CHUNK), F32),
                        pltpu.VMEM((ns, 2, n // GDN_CHUNK, GDN_HEADS, GDN_DK, GDN_CHUNK), BF16)],
        compiler_params=_cparams(("parallel",)),
        name="gdn_mixer",
    )(qkv, z, ab, cw, gp, p['gdn_norm_g'][j][None], *(() if s0 is None else (s0,)))


def _proj_cd_kernel(x_ref, g_ref, sh_ref, sc_ref, w_ref, gm_ref, gain_ref, cos_ref, sin_ref,
                    qc_ref, kc_ref, vc_ref, qd_ref, kd_ref, vd_ref, *, rope):
    ns, tm, d = x_ref.shape
    h = _rms_mod(x_ref[...], g_ref[...], sh_ref[...], sc_ref[...]).reshape(ns * tm, d)
    y = _dot(h, w_ref[...])
    lane = lax.broadcasted_iota(jnp.int32, (1, 512), 1)
    low = (lane % 32) < 16

    def head_norm(t, gain, scale):
        w = t.shape[1]
        sq = t * t
        ms = jnp.concatenate([_dot(sq[:, c:c + 128], gm_ref[...]) for c in range(0, w, 128)], axis=1)
        t = t * lax.rsqrt(ms + EPS) * gain
        if rope:
            part = jnp.where(low[:, :w], pltpu.roll(t, w - 16, 1), pltpu.roll(t, 16, 1))
            t = t * cos_ref[:, :w] + part * sin_ref[:, :w]
        return t * scale if scale != 1.0 else t

    def put(ref, v):
        ref[...] = v.reshape(ns, tm, v.shape[-1])

    put(qc_ref, head_norm(y[:, 0:512], gain_ref[0:1, :], ATTN_SCALE))
    put(kc_ref, head_norm(y[:, 512:640], gain_ref[1:2, :128], 1.0))
    put(vc_ref, y[:, 640:768])
    put(qd_ref, head_norm(y[:, 768:1280], gain_ref[2:3, :], ATTN_SCALE))
    put(kd_ref, head_norm(y[:, 1280:1792], gain_ref[3:4, :], 1.0))
    put(vd_ref, y[:, 1792:2304])


def _rope_tables(n):
    rows = n // GRID_W
    row = jnp.repeat(jnp.arange(rows), GRID_W).astype(F32)
    col = jnp.tile(jnp.arange(GRID_W), rows).astype(F32)
    quarter = HEAD_DIM // 4
    inv = ROPE_THETA ** (-jnp.arange(quarter, dtype=F32) / quarter)
    ang_r = row[:, None] * inv[None, :]
    ang_c = col[:, None] * inv[None, :]
    cos = jnp.concatenate([jnp.cos(ang_r), jnp.cos(ang_r), jnp.cos(ang_c), jnp.cos(ang_c)], axis=-1)
    sin = jnp.concatenate([-jnp.sin(ang_r), jnp.sin(ang_r), -jnp.sin(ang_c), jnp.sin(ang_c)], axis=-1)
    return jnp.tile(cos, (1, 8)), jnp.tile(sin, (1, 8))


def _proj_cd(x, g, shift, scale, w, p, j, rope):
    bsz, n, d = x.shape
    tm = 256
    per_seq = shift.shape[0] > 1
    midx = (lambda b, i: (b, 0, 0)) if per_seq else (lambda b, i: (0, 0, 0))
    lane = jnp.arange(128)
    gmat = ((lane[:, None] // HEAD_DIM) == (lane[None, :] // HEAD_DIM)).astype(F32) / HEAD_DIM
    gains = jnp.zeros((8, 512), F32)
    gains = gains.at[0].set(jnp.tile(p['c_qn'][j], 8)).at[1].set(jnp.tile(p['c_kn'][j], 8))
    gains = gains.at[2].set(jnp.tile(p['d_qn'][j], 8)).at[3].set(jnp.tile(p['d_kn'][j], 8))
    if rope:
        cos, sin = _rope_tables(n)
    else:
        cos, sin = jnp.ones((tm, 512), F32), jnp.zeros((tm, 512), F32)
    tidx = (lambda b, i: (i, 0)) if rope else (lambda b, i: (0, 0))
    ns = 2 if (not per_seq and not rope and bsz % 2 == 0) else 1
    blk = lambda w_: pl.BlockSpec((ns, tm, w_), lambda b, i: (b, i, 0))
    return pl.pallas_call(
        functools.partial(_proj_cd_kernel, rope=rope),
        grid=(bsz // ns, n // tm),
        in_specs=[blk(d),
                  pl.BlockSpec((1, d), lambda b, i: (0, 0)),
                  pl.BlockSpec((1, 1, d), midx),
                  pl.BlockSpec((1, 1, d), midx),
                  pl.BlockSpec((d, CD_IN), lambda b, i: (0, 0)),
                  pl.BlockSpec((128, 128), lambda b, i: (0, 0)),
                  pl.BlockSpec((8, 512), lambda b, i: (0, 0)),
                  pl.BlockSpec((tm, 512), tidx),
                  pl.BlockSpec((tm, 512), tidx)],
        out_specs=[blk(512), blk(128), blk(128), blk(512), blk(512), blk(512)],
        out_shape=[jax.ShapeDtypeStruct((bsz, n, w_), F32) for w_ in (512, 128, 128, 512, 512, 512)],
        compiler_params=_cparams(("parallel", "parallel")),
        name="proj_cd",
    )(x, g, shift, scale, w, gmat.astype(BF16), gains, cos, sin)


def _probs(scores, extra=None):
    m = scores[0].max(axis=-1, keepdims=True)
    for s in scores[1:]:
        m = jnp.maximum(m, s.max(axis=-1, keepdims=True))
    if extra is not None:
        m = jnp.maximum(m, extra)
    return [jnp.exp(s - m).astype(BF16) for s in scores], m


def _attn_kernel(qc_ref, kc_ref, vc_ref, qd_ref, kd_ref, vd_ref, *rest, windowed, n_ctx, lam_scale):
    if n_ctx:
        ck_ref, cv_ref, dk_ref, dv_ref, misc_ref, sub_ref, oc_ref, od_ref = rest
    else:
        misc_ref, sub_ref, oc_ref, od_ref = rest
    tq = qc_ref.shape[1]
    n = kc_ref.shape[1]
    start = pl.program_id(1) * tq
    if windowed:
        span = tq + 2 * WINDOW
        k0 = pl.multiple_of(jnp.clip(start - WINDOW, 0, n - span), 128)
        krows = pl.ds(k0, span)
        qpos = start + lax.broadcasted_iota(jnp.int32, (tq, span), 0)
        kpos = k0 + lax.broadcasted_iota(jnp.int32, (tq, span), 1)
        ok = jnp.abs(qpos - kpos) <= WINDOW
    else:
        krows = pl.ds(0, n)
    low = lax.broadcasted_iota(jnp.int32, (1, 2 * HEAD_DIM), 1) < HEAD_DIM

    def key_ops(k):
        kr = pltpu.roll(k, HEAD_DIM, 1)
        z = jnp.zeros_like(k)
        return {(0, 0): jnp.where(low, k, z).astype(BF16), (0, 1): jnp.where(low, z, kr).astype(BF16),
                (1, 0): jnp.where(low, kr, z).astype(BF16), (1, 1): jnp.where(low, z, k).astype(BF16)}

    def with_ones(v):
        return jnp.concatenate([v, jnp.ones_like(v)], axis=1)

    def val_ops(v):
        return {key: with_ones(op) for key, op in key_ops(v).items()}

    def half_ops(k):
        z = jnp.zeros_like(k)
        return [jnp.where(low, k, z).astype(BF16), jnp.where(low, z, k).astype(BF16)]

    seqs = range(qc_ref.shape[0])

    c_scores, d_scores = [[] for _ in seqs], [[] for _ in seqs]
    for sq in seqs:
        qc = qc_ref[sq].astype(BF16)
        qd = qd_ref[sq].astype(BF16)
        kc_ops = key_ops(kc_ref[sq, krows, :])
        ck_ops = key_ops(ck_ref[sq]) if n_ctx else None
        for h in range(C_HEADS):
            key = (h // C_GROUP, h % 2)
            q = qc[:, (h // 2) * 128:(h // 2 + 1) * 128]
            sc = [_dot_nt(q, kc_ops[key])]
            if n_ctx:
                sc.append(_dot_nt(q, ck_ops[key]))
            c_scores[sq].append(sc)
        for h in range(D_HEADS):
            q = qd[:, h * 128:(h + 1) * 128]
            kd_ops = half_ops(kd_ref[sq, :, h * 128:(h + 1) * 128])
            dk_ops = half_ops(dk_ref[sq, :, h * 128:(h + 1) * 128]) if n_ctx else None
            for c in range(2):
                sc = [_dot_nt(q, kd_ops[c])]
                if n_ctx:
                    sc.append(_dot_nt(q, dk_ops[c]))
                d_scores[sq].append(sc)

    c_probs, d_probs = [[] for _ in seqs], [[] for _ in seqs]
    for sq in seqs:
        for h in range(C_HEADS):
            sc = c_scores[sq][h]
            if windowed:
                sc[0] = jnp.where(ok, sc[0], NEG_INF)
            c_probs[sq].append(_probs(sc, misc_ref[0:1, h:h + 1]))
        d_probs[sq] = [_probs(sc)[0] for sc in d_scores[sq]]

    lam = misc_ref[1:2, 0:1]
    for sq in seqs:
        vc_ops = val_ops(vc_ref[sq, krows, :])
        cv_ops = val_ops(cv_ref[sq]) if n_ctx else None
        for j in range(C_HEADS // 2):
            pair = None
            for h in (2 * j, 2 * j + 1):
                key = (h // C_GROUP, h % 2)
                ps, m = c_probs[sq][h]
                r = jnp.dot(ps[0], vc_ops[key], preferred_element_type=F32)
                if n_ctx:
                    r = r + jnp.dot(ps[1], cv_ops[key], preferred_element_type=F32)
                o = r[:, :128] / (r[:, 128:] + jnp.exp(misc_ref[0:1, h:h + 1] - m))
                pair = o if pair is None else pair + o
            oc_ref[sq, :, j * 128:(j + 1) * 128] = pair
        for h in range(D_HEADS):
            vsl = slice(h * D_VDIM, (h + 1) * D_VDIM)
            v_op = with_ones(vd_ref[sq, :, vsl]).astype(BF16)
            dv_op = with_ones(dv_ref[sq, :, vsl]).astype(BF16) if n_ctx else None
            parts = []
            for c in range(2):
                ps = d_probs[sq][2 * h + c]
                r = jnp.dot(ps[0], v_op, preferred_element_type=F32)
                if n_ctx:
                    r = r + jnp.dot(ps[1], dv_op, preferred_element_type=F32)
                parts.append(r[:, :D_VDIM] / r[:, D_VDIM:])
            o = parts[0] - lam * parts[1]
            o = o * lax.rsqrt(jnp.mean(o * o, axis=-1, keepdims=True) + EPS) * sub_ref[...] * lam_scale
            od_ref[sq, :, vsl] = o


def _attention(qc, kc, vc, qd, kd, vd, caches, misc, subln, lam_init, tq, windowed):
    bsz, n, _ = qc.shape
    ns = ATTN_SEQS_SHORT if (not windowed and tq == n and bsz % ATTN_SEQS_SHORT == 0) else 1
    qblk = lambda w_: pl.BlockSpec((ns, tq, w_), lambda b, i: (b, i, 0))
    kblk = lambda rows, w_: pl.BlockSpec((ns, rows, w_), lambda b, i: (b, 0, 0))
    in_specs = [qblk(512), kblk(n, 128), kblk(n, 128), qblk(512), kblk(n, 512), kblk(n, 512)]
    args = [qc, kc, vc, qd, kd, vd]
    n_ctx = 0
    if caches is not None:
        n_ctx = caches[0].shape[1]
        in_specs += [kblk(n_ctx, 128), kblk(n_ctx, 128), kblk(n_ctx, 512), kblk(n_ctx, 512)]
        args += list(caches)
    in_specs += [pl.BlockSpec((8, 128), lambda b, i: (0, 0)), pl.BlockSpec((1, D_VDIM), lambda b, i: (0, 0))]
    args += [misc, subln]
    return pl.pallas_call(
        functools.partial(_attn_kernel, windowed=windowed, n_ctx=n_ctx, lam_scale=1.0 - lam_init),
        grid=(bsz // ns, n // tq),
        in_specs=in_specs,
        out_specs=[qblk(512), qblk(512)],
        out_shape=[jax.ShapeDtypeStruct((bsz, n, 512), F32), jax.ShapeDtypeStruct((bsz, n, 512), F32)],
        compiler_params=_cparams(("parallel", "parallel")),
        name="attn_win" if windowed else "attn_ctx",
    )(*args)


def _post_kernel(x_ref, xp_ref, xn_ref, ma_ref, map_ref, man_ref, mb_ref, mbp_ref, mbn_ref,
                 g1_ref, sh_ref, sc_ref, g2_ref, ng_ref, wo_ref, wup_ref, cw_ref, wdn_ref, o_ref, act_ref,
                 *, seq_len):
    rows = x_ref.shape[0]
    ext = rows + 2 * POST_HALO
    half = ma_ref.shape[1]
    xe = jnp.concatenate([xp_ref[...], x_ref[...], xn_ref[...]], axis=0)
    mae = jnp.concatenate([map_ref[...], ma_ref[...], man_ref[...]], axis=0)
    mbe = jnp.concatenate([mbp_ref[...], mb_ref[...], mbn_ref[...]], axis=0)
    x1 = xe + g1_ref[0] * (_dot(mae, wo_ref[:half, :]) + _dot(mbe, wo_ref[half:, :]))
    h = _rms_mod(x1, ng_ref[...], sh_ref[0], sc_ref[0]).astype(BF16)
    x1 = x1[POST_HALO:POST_HALO + rows]
    row0 = pl.program_id(0) * rows - POST_HALO
    pos = (row0 + lax.broadcasted_iota(jnp.int32, (ext, 1), 0)) % seq_len
    first = pos == 0
    last = pos == seq_len - 1
    for c in range(FF_CHUNKS):
        cs = slice(c * FF_CHUNK, (c + 1) * FF_CHUNK)
        a = jnp.dot(h, wup_ref[:, cs], preferred_element_type=F32)
        b = jnp.dot(h, wup_ref[:, D_FF + c * FF_CHUNK:D_FF + (c + 1) * FF_CHUNK], preferred_element_type=F32)
        am = jnp.where(first, 0.0, pltpu.roll(a, 1, 0))
        ap = jnp.where(last, 0.0, pltpu.roll(a, ext - 1, 0))
        a = am * cw_ref[0:1, cs] + a * cw_ref[1:2, cs] + ap * cw_ref[2:3, cs] + cw_ref[3:4, cs]
        act_ref[:, cs] = (_silu(a) * b)[POST_HALO:POST_HALO + rows].astype(BF16)
    ffn = jnp.dot(act_ref[...], wdn_ref[...], preferred_element_type=F32)
    o_ref[...] = x1 + g2_ref[0] * ffn


def _post(x, mix_a, mix_b, g1, sh2, sc2, g2, norm_g, w_out, wup, cw, wdn, layer):
    bsz, seq_len, d = x.shape
    half = mix_a.shape[-1]
    rows = POST_ROWS
    total = bsz * seq_len
    nhalo = total // POST_HALO
    per_seq = g1.shape[0] > 1
    midx = (lambda i: ((i * rows) // seq_len, 0, 0)) if per_seq else (lambda i: (0, 0, 0))
    mspec = pl.BlockSpec((1, 1, d), midx)
    pidx = lambda i: (jnp.maximum(i * (rows // POST_HALO) - 1, 0), 0)
    nidx = lambda i: (jnp.minimum((i + 1) * (rows // POST_HALO), nhalo - 1), 0)
    trio = lambda w_: [pl.BlockSpec((rows, w_), lambda i: (i, 0)), pl.BlockSpec((POST_HALO, w_), pidx),
                       pl.BlockSpec((POST_HALO, w_), nidx)]
    whole = lambda shape: pl.BlockSpec(shape, lambda i: (0,) * len(shape), pipeline_mode=pl.Buffered(1))
    x2 = x.reshape(total, d)
    a2 = mix_a.reshape(total, half)
    b2 = mix_b.reshape(total, half)
    out = pl.pallas_call(
        functools.partial(_post_kernel, seq_len=seq_len),
        grid=(total // rows,),
        in_specs=trio(d) + trio(half) + trio(half) + [
            mspec, mspec, mspec, mspec,
            pl.BlockSpec((1, d), lambda i: (0, 0)),
            whole((d, d)),
            pl.BlockSpec((None, d, 2 * D_FF), lambda i: (layer, 0, 0), pipeline_mode=pl.Buffered(1)),
            whole((8, D_FF)),
            pl.BlockSpec((None, D_FF, d), lambda i: (layer, 0, 0), pipeline_mode=pl.Buffered(1))],
        out_specs=pl.BlockSpec((rows, d), lambda i: (i, 0)),
        out_shape=jax.ShapeDtypeStruct((total, d), F32),
        scratch_shapes=[pltpu.VMEM((rows, D_FF), BF16)],
        compiler_params=_cparams(("parallel",)),
        name="post_ffn",
    )(x2, x2, x2, a2, a2, a2, b2, b2, b2, g1, sh2, sc2, g2, norm_g, w_out, wup, cw, wdn)
    return out.reshape(bsz, seq_len, d)


def _ffn_weights(p, l):
    cw = jnp.zeros((8, D_FF), F32).at[:3].set(p['ffn_conv_w'][l]).at[3].set(p['ffn_conv_b'][l])
    return p['ffn_up'].astype(BF16), cw, p['ffn_down'].astype(BF16)


def _lambda_init(layer):
    return 0.8 - 0.6 * math.exp(-0.3 * layer)


def _trunk(x, mods, p, states, caches):
    bsz, n, d = x.shape
    nseg = n // S5_SEG
    depth = p['w_mod'].shape[0]
    news = {k: [] for k in ('s5r', 's5i', 'gdn', 'ck', 'cv', 'dk', 'dv')}
    for l in range(depth):
        j = l // 2
        sh1, sc1, g1, sh2, sc2, g2 = mods[l]
        ng1 = p['norm1_g'][l][None]
        if l % 2 == 0:
            w_in = p['w_in_ab'][j]
            w_gate = jnp.zeros((d, 128), BF16).at[:, :w_in.shape[1] - AB_MAIN].set(
                w_in[:, AB_MAIN:].astype(BF16))
            rows = bsz * nseg
            per_row = lambda m: jnp.repeat(m, nseg, axis=0) if m.shape[0] > 1 else m
            u_t, qkv, z, ab = _proj_ab(x.reshape(rows, S5_SEG, d), ng1, per_row(sh1), per_row(sc1),
                                       w_in[:, :AB_MAIN].astype(BF16), w_gate)
            qkv, z, ab = (t.reshape(bsz, n, t.shape[-1]) for t in (qkv, z, ab))
            if states is None:
                h0r = jnp.zeros((2, bsz, S5_GROUPS * S5_STATE), F32)
                h0i = h0r
                s0 = None
            else:
                h0r = states[0][:, j].reshape(bsz, 2, -1).transpose(1, 0, 2)
                h0i = states[1][:, j].reshape(bsz, 2, -1).transpose(1, 0, 2)
                s0 = states[2][:, j]
            ya, fr, fi = _s5_mixer(u_t, p, j, h0r, h0i, bsz, nseg)
            yb, sg = _gdn_mixer(qkv, z, ab, p, j, s0)
            mix = (ya.reshape(bsz, n, S5_WIDTH), yb)
            w_out = p['w_out_ab'][j]
            news['s5r'].append(fr.transpose(1, 0, 2).reshape(bsz, 2, S5_GROUPS, S5_STATE))
            news['s5i'].append(fi.transpose(1, 0, 2).reshape(bsz, 2, S5_GROUPS, S5_STATE))
            news['gdn'].append(sg)
        else:
            lam_init = _lambda_init(l)
            f = lambda name: p[name][j]
            lam = (jnp.exp(jnp.sum(f('d_lq1') * f('d_lk1'))) - jnp.exp(jnp.sum(f('d_lq2') * f('d_lk2')))
                   + lam_init)
            misc = jnp.zeros((8, 128), F32).at[0, :C_HEADS].set(p['c_sink'][j]).at[1, :].set(lam)
            rope = caches is not None
            qc, kc, vc, qd, kd, vd = _proj_cd(x, ng1, sh1, sc1, p['w_in_cd'][j].astype(BF16), p, j, rope)
            if caches is None:
                mix = _attention(qc, kc, vc, qd, kd, vd, None, misc, p['d_subln'][j][None],
                                 lam_init, n, False)
            else:
                n_ctx = caches[0].shape[2]
                cc = (caches[0][:, j].reshape(bsz, n_ctx, 128), caches[1][:, j].reshape(bsz, n_ctx, 128),
                      caches[2][:, j].reshape(bsz, n_ctx, 512), caches[3][:, j].reshape(bsz, n_ctx, 512))
                mix = _attention(qc, kc, vc, qd, kd, vd, cc, misc, p['d_subln'][j][None],
                                 lam_init, 2 * Q_BLOCK, True)
            w_out = p['w_out_cd'][j]
            news['ck'].append(kc.reshape(bsz, n, C_KV_HEADS, HEAD_DIM))
            news['cv'].append(vc.reshape(bsz, n, C_KV_HEADS, HEAD_DIM))
            news['dk'].append(kd.reshape(bsz, n, D_HEADS, 2, HEAD_DIM))
            news['dv'].append(vd.reshape(bsz, n, D_HEADS, D_VDIM))
        wup, cw, wdn = _ffn_weights(p, l)
        x = _post(x, mix[0], mix[1], g1, sh2, sc2, g2, p['norm2_g'][l][None], w_out.astype(BF16),
                  wup, cw, wdn, l)
    return x, news


def kernel(x_prompt, x_sample, c, state_s5_re, state_s5_im, state_gdn, cache_c_k, cache_c_v, cache_d_k, cache_d_v, c_ctx, w_mod, b_mod, norm1_g, norm2_g, w_in_ab, w_out_ab, s5_lam_re, s5_lam_im, s5_log_dt, s5_b_re, s5_b_im, s5_c_re, s5_c_im, s5_d, s5_w_glu, s5_b_glu, gdn_conv_w, gdn_a_log, gdn_dt_bias, gdn_norm_g, w_in_cd, w_out_cd, c_qn, c_kn, c_sink, d_qn, d_kn, d_lq1, d_lk1, d_lq2, d_lk2, d_subln, ffn_up, ffn_conv_w, ffn_conv_b, ffn_down):
    p = dict(w_mod=w_mod, b_mod=b_mod, norm1_g=norm1_g, norm2_g=norm2_g, w_in_ab=w_in_ab, w_out_ab=w_out_ab,
             s5_lam_re=s5_lam_re, s5_lam_im=s5_lam_im, s5_log_dt=s5_log_dt, s5_b_re=s5_b_re, s5_b_im=s5_b_im,
             s5_c_re=s5_c_re, s5_c_im=s5_c_im, s5_d=s5_d, s5_w_glu=s5_w_glu, s5_b_glu=s5_b_glu,
             gdn_conv_w=gdn_conv_w, gdn_a_log=gdn_a_log, gdn_dt_bias=gdn_dt_bias, gdn_norm_g=gdn_norm_g,
             w_in_cd=w_in_cd, w_out_cd=w_out_cd, c_qn=c_qn, c_kn=c_kn, c_sink=c_sink, d_qn=d_qn, d_kn=d_kn,
             d_lq1=d_lq1, d_lk1=d_lk1, d_lq2=d_lq2, d_lk2=d_lk2, d_subln=d_subln,
             ffn_up=ffn_up, ffn_conv_w=ffn_conv_w, ffn_conv_b=ffn_conv_b, ffn_down=ffn_down)
    depth = w_mod.shape[0]
    n_dec = c.shape[0]
    mod = _modulation(jnp.concatenate([c_ctx[None], c], axis=0), w_mod, b_mod)
    split6 = lambda m: [m[:, None, k * D_MODEL:(k + 1) * D_MODEL] for k in range(6)]
    mods_ctx = [split6(mod[l, 0:1]) for l in range(depth)]
    mods_dec = [split6(mod[l, 1:1 + n_dec]) for l in range(depth)]

    y_prompt, nw = _trunk(x_prompt, mods_ctx, p, None, None)
    y_sample, _ = _trunk(x_sample, mods_dec, p, (state_s5_re, state_s5_im, state_gdn),
                         (cache_c_k, cache_c_v, cache_d_k, cache_d_v))
    st = lambda name: jnp.stack(nw[name], axis=1)
    return (y_prompt, y_sample, st('s5r'), st('s5i'), st('gdn'), st('ck'), st('cv'), st('dk'), st('dv'))
```
